```python
import math
import jax, jax.numpy as jnp
from jax import lax
import numpy as np

D_MODEL = 2048
BATCH = 1
SEQ = 16384
DEPTH = 4
DEC_BATCH = 2
DEC_SEQ = 4096
PAST_LEN = 128

D_HY = D_MODEL // 4
D_S5 = D_MODEL // 4
D_ML = D_MODEL - D_HY - D_S5
MIX_W = D_HY + D_S5 + D_ML
HY_ORDER = 2
HY_EMB = 33
HY_BANDS = (HY_EMB - 1) // 2
HY_FILT = 64
HY_FAST_DECAY = 0.3
HY_SLOW_DECAY = 1.5
HY_TARGET = 1e-2
S5_GROUP = 16
S5_G = D_S5 // S5_GROUP
S5_P = 64
S5_DT_MIN = 1e-3
S5_DT_MAX = 1e-1
ML_HEADS = 4
ML_DH = D_ML // ML_HEADS
ML_CHUNK = 64
D_FF = 4 * D_MODEL
CONV_W = 3
N_CONV = 3 * D_HY + 2 * D_ML
P_IN = N_CONV + D_S5 + 2 * D_ML + 4 * ML_HEADS
EPS = 1e-6
F32 = jnp.float32

kernel_name = "hymba_style_hyena_s5_mlstm_encoder"


def rms_norm(x, g):
    xf = x.astype(F32)
    y = xf * lax.rsqrt(jnp.mean(xf * xf, axis=-1, keepdims=True) + EPS)
    return (y * g.astype(F32)).astype(x.dtype)


def short_conv(x, w, b):
    L = x.shape[1]
    half = CONV_W // 2
    xp = jnp.pad(x, ((0, 0), (half, CONV_W - 1 - half), (0, 0)))
    out = b
    for j in range(CONV_W):
        out = out + xp[:, j:j + L] * w[j]
    return out


def hyena_filters(L, w1, b1, freq, w2, b2, w3):
    t = jnp.linspace(0.0, 1.0, L, dtype=F32)[:, None]
    w = 2.0 * math.pi * jnp.arange(L, dtype=F32)[:, None] / L
    f = jnp.linspace(1e-4, HY_BANDS - 1, HY_BANDS, dtype=F32)[None, :]
    z = jnp.concatenate([t, jnp.cos(f * w), -jnp.sin(f * w)], axis=-1)
    fr = freq.astype(F32)
    h = jnp.sin(fr * (z @ w1.astype(F32) + b1.astype(F32)))
    h = jnp.sin(fr * (h @ w2.astype(F32) + b2.astype(F32)))
    h = (h @ w3.astype(F32)).reshape(L, HY_ORDER, 2, D_HY)
    deltas = jnp.abs(jnp.linspace(math.log(HY_TARGET) / HY_SLOW_DECAY,
                                  math.log(HY_TARGET) / HY_FAST_DECAY, D_HY, dtype=F32))
    h = h * jnp.exp(-t * deltas)[:, None, None, :]
    fwd = h[:, :, 0]
    bwd = h[1:, :, 1][::-1]
    k = jnp.concatenate([fwd, jnp.zeros((1, HY_ORDER, D_HY), F32), bwd], axis=0)
    k = k * lax.rsqrt(jnp.sum(k * k, axis=0, keepdims=True) + EPS)
    return jnp.fft.rfft(k, axis=0)


def hyena_mixer(hy, kf, skip):
    L = hy.shape[1]
    v, x1, x2 = jnp.split(hy.astype(F32), 3, axis=-1)
    z = v
    for o, gate in enumerate((x1, x2)):
        zf = jnp.fft.rfft(z, n=2 * L, axis=1)
        conv = jnp.fft.irfft(zf * kf[None, :, o], n=2 * L, axis=1)[:, :L]
        z = gate * (conv + skip[o].astype(F32) * z)
    return z


def _lin_combine(e1, e2):
    a1, b1 = e1
    a2, b2 = e2
    return a1 * a2, a2 * b1 + b2


def s5_direction(u, lam_re, lam_im, log_dt, b_re, b_im, c_re, c_im, reverse):
    lam = lax.complex(jnp.minimum(lam_re.astype(F32), -1e-4), lam_im.astype(F32))
    dt = jnp.exp(log_dt.astype(F32))[:, None]
    a_bar = jnp.exp(lam * dt)
    b_bar = ((a_bar - 1.0) / lam)[..., None] * lax.complex(b_re.astype(F32), b_im.astype(F32))
    bu = jnp.einsum('blgh,gph->blgp', u.astype(jnp.complex64), b_bar)
    a = jnp.broadcast_to(a_bar, bu.shape)
    _, xs = lax.associative_scan(_lin_combine, (a, bu), axis=1, reverse=reverse)
    c = lax.complex(c_re.astype(F32), c_im.astype(F32))
    return jnp.real(jnp.einsum('blgp,ghp->blgh', xs, c))


def s5_mixer(u, lam_re, lam_im, log_dt, b_re, b_im, c_re, c_im, d, glu_w, glu_b):
    B_, L, _ = u.shape
    uf = u.astype(F32)
    ug = uf.reshape(B_, L, S5_G, S5_GROUP)
    y = s5_direction(ug, lam_re[0], lam_im[0], log_dt[0], b_re[0], b_im[0], c_re[0], c_im[0], False)
    y = y + s5_direction(ug, lam_re[1], lam_im[1], log_dt[1], b_re[1], b_im[1], c_re[1], c_im[1], True)
    y = y.reshape(B_, L, D_S5) + d.astype(F32) * uf
    g = jax.nn.gelu(y)
    val, gate = jnp.split(g @ glu_w.astype(F32) + glu_b.astype(F32), 2, axis=-1)
    return val * jax.nn.sigmoid(gate)


def mlstm_direction(q, k, v, ig, fg):
    B_, NH, L, DH = q.shape
    nc = L // ML_CHUNK

    def chunks(t):
        return jnp.moveaxis(t.reshape((B_, NH, nc, ML_CHUNK) + t.shape[3:]), 2, 0)

    b = jnp.cumsum(chunks(jax.nn.log_sigmoid(fg)), axis=-1)
    causal = jnp.tril(jnp.ones((ML_CHUNK, ML_CHUNK), dtype=bool))

    def step(carry, inp):
        c, n, m = carry
        qc, kc, vc, ic, bc = inp
        g = bc + m[..., None]
        dmat = bc[..., :, None] - bc[..., None, :] + ic[..., None, :]
        dmat = jnp.where(causal, dmat, -jnp.inf)
        mt = jnp.maximum(g, jnp.max(dmat, axis=-1))
        inter = jnp.exp(g - mt)
        s = jnp.einsum('bhtd,bhsd->bhts', qc, kc) * jnp.exp(dmat - mt[..., None])
        num = inter[..., None] * jnp.einsum('bhtk,bhkv->bhtv', qc, c) + jnp.einsum('bhts,bhsv->bhtv', s, vc)
        den = inter * jnp.einsum('bhtk,bhk->bht', qc, n) + jnp.sum(s, axis=-1)
        h = num / jnp.maximum(jnp.abs(den), jnp.exp(-mt))[..., None]
        b_last = bc[..., -1]
        a = b_last[..., None] - bc + ic
        m_new = jnp.maximum(b_last + m, jnp.max(a, axis=-1))
        decay = jnp.exp(b_last + m - m_new)
        wgt = jnp.exp(a - m_new[..., None])
        c = decay[..., None, None] * c + jnp.einsum('bhtk,bhtv->bhkv', kc * wgt[..., None], vc)
        n = decay[..., None] * n + jnp.einsum('bht,bhtk->bhk', wgt, kc)
        return (c, n, m_new), h

    init = (jnp.zeros((B_, NH, DH, DH), F32), jnp.zeros((B_, NH, DH), F32), jnp.zeros((B_, NH), F32))
    _, h = lax.scan(step, init, (chunks(q), chunks(k), chunks(v), chunks(ig), b))
    return jnp.moveaxis(h, 0, 2).reshape(B_, NH, L, DH)


def mlstm_mixer(qk, v, o, gates, gate_b, norm_g):
    B_, L, _ = v.shape
    qk = jax.nn.silu(qk.astype(F32))

    def heads(t):
        return t.reshape(B_, L, ML_HEADS, ML_DH).transpose(0, 2, 1, 3)

    q = heads(qk[..., :D_ML])
    k = heads(qk[..., D_ML:]) * (ML_DH ** -0.5)
    vv = heads(v.astype(F32))
    g = (gates.astype(F32) + gate_b.astype(F32).reshape(-1)).reshape(B_, L, 4, ML_HEADS).transpose(2, 0, 3, 1)
    h_f = mlstm_direction(q, k, vv, g[0], g[1])
    flip = lambda t: jnp.flip(t, axis=2)
    h_b = flip(mlstm_direction(flip(q), flip(k), flip(vv), flip(g[2]), flip(g[3])))
    h = (h_f + h_b).transpose(0, 2, 1, 3)
    h = rms_norm(h, norm_g.reshape(ML_HEADS, ML_DH)).reshape(B_, L, D_ML)
    return h * jax.nn.sigmoid(o.astype(F32))


def encoder_layer(x, ln_g, w_in, conv_w, conv_b, hy_w1, hy_b1, hy_freq, hy_w2, hy_b2, hy_w3, hy_skip,
                  s5_lam_re, s5_lam_im, s5_log_dt, s5_b_re, s5_b_im, s5_c_re, s5_c_im, s5_d,
                  s5_glu_w, s5_glu_b, ml_gate_b, group_norm, w_out, mlp_w1, mlp_w2):
    L = x.shape[1]
    h = rms_norm(x, ln_g[0])
    proj = h @ w_in
    conv = short_conv(proj[..., :N_CONV], conv_w, conv_b)
    hy_in = conv[..., :3 * D_HY]
    qk = conv[..., 3 * D_HY:]
    s = N_CONV
    u_s5 = proj[..., s:s + D_S5]
    s += D_S5
    v_ml = proj[..., s:s + D_ML]
    s += D_ML
    o_ml = proj[..., s:s + D_ML]
    s += D_ML
    g_ml = proj[..., s:]
    kf = hyena_filters(L, hy_w1, hy_b1, hy_freq, hy_w2, hy_b2, hy_w3)
    y_hy = rms_norm(hyena_mixer(hy_in, kf, hy_skip), group_norm[:D_HY])
    y_s5 = rms_norm(s5_mixer(u_s5, s5_lam_re, s5_lam_im, s5_log_dt, s5_b_re, s5_b_im, s5_c_re, s5_c_im,
                             s5_d, s5_glu_w, s5_glu_b), group_norm[D_HY:D_HY + D_S5])
    y_ml = mlstm_mixer(qk, v_ml, o_ml, g_ml, ml_gate_b, group_norm[D_HY + D_S5:])
    mix = jnp.concatenate([y_hy, y_s5, y_ml], axis=-1).astype(x.dtype) @ w_out
    x = x + rms_norm(mix, ln_g[1])
    h = rms_norm(x, ln_g[2])
    f = jnp.square(jax.nn.relu(h @ mlp_w1)) @ mlp_w2
    return x + rms_norm(f, ln_g[3])


def setup_inputs(seed: int = 0) -> dict:
    key = jax.random.key(seed)
    ks = jax.random.split(key, 32)

    def nrm(k, shape, scale):
        return scale * jax.random.normal(k, shape, F32)

    n_idx = jnp.arange(S5_P, dtype=F32)
    gb = nrm(ks[23], (DEPTH, 4, ML_HEADS), 0.1)
    gb = gb + jnp.array([0.0, 1.0, 0.0, 1.0], F32)[:, None] * jnp.linspace(3.0, 6.0, ML_HEADS, dtype=F32)[None, :]
    return {
        "x_prompt": nrm(ks[0], (BATCH, SEQ, D_MODEL), 1.0),
        "x_sample": nrm(ks[1], (DEC_BATCH, DEC_SEQ, D_MODEL), 1.0),
        "ln_g": 1.0 + nrm(ks[2], (DEPTH, 4, D_MODEL), 0.05),
        "w_in": nrm(ks[3], (DEPTH, D_MODEL, P_IN), D_MODEL ** -0.5),
        "conv_w": nrm(ks[4], (DEPTH, CONV_W, N_CONV), CONV_W ** -0.5),
        "conv_b": nrm(ks[5], (DEPTH, N_CONV), 0.02),
        "hy_w1": nrm(ks[6], (DEPTH, HY_EMB, HY_FILT), HY_EMB ** -0.5),
        "hy_b1": nrm(ks[7], (DEPTH, HY_FILT), 0.1),
        "hy_freq": 1.0 + nrm(ks[8], (DEPTH, HY_FILT), 0.05),
        "hy_w2": nrm(ks[9], (DEPTH, HY_FILT, HY_FILT), HY_FILT ** -0.5),
        "hy_b2": nrm(ks[10], (DEPTH, HY_FILT), 0.1),
        "hy_w3": nrm(ks[11], (DEPTH, HY_FILT, HY_ORDER * 2 * D_HY), HY_FILT ** -0.5),
        "hy_skip": nrm(ks[12], (DEPTH, HY_ORDER, D_HY), 0.3),
        "s5_lam_re": -0.5 + nrm(ks[13], (DEPTH, 2, S5_G, S5_P), 0.01),
        "s5_lam_im": math.pi * n_idx + nrm(ks[14], (DEPTH, 2, S5_G, S5_P), 0.01),
        "s5_log_dt": jax.random.uniform(ks[15], (DEPTH, 2, S5_G), F32, math.log(S5_DT_MIN), math.log(S5_DT_MAX)),
        "s5_b_re": nrm(ks[16], (DEPTH, 2, S5_G, S5_P, S5_GROUP), (2 * S5_GROUP) ** -0.5),
        "s5_b_im": nrm(ks[17], (DEPTH, 2, S5_G, S5_P, S5_GROUP), (2 * S5_GROUP) ** -0.5),
        "s5_c_re": nrm(ks[18], (DEPTH, 2, S5_G, S5_GROUP, S5_P), S5_P ** -0.5),
        "s5_c_im": nrm(ks[19], (DEPTH, 2, S5_G, S5_GROUP, S5_P), S5_P ** -0.5),
        "s5_d": nrm(ks[20], (DEPTH, D_S5), 1.0),
        "s5_glu_w": nrm(ks[21], (DEPTH, D_S5, 2 * D_S5), D_S5 ** -0.5),
        "s5_glu_b": nrm(ks[22], (DEPTH, 2 * D_S5), 0.02),
        "ml_gate_b": gb,
        "group_norm": 1.0 + nrm(ks[24], (DEPTH, MIX_W), 0.05),
        "w_out": nrm(ks[25], (DEPTH, MIX_W, D_MODEL), MIX_W ** -0.5),
        "mlp_w1": nrm(ks[26], (DEPTH, D_MODEL, D_FF), D_MODEL ** -0.5),
        "mlp_w2": nrm(ks[27], (DEPTH, D_FF, D_MODEL), D_FF ** -0.5),
    }


def reference(x_prompt, x_sample, ln_g, w_in, conv_w, conv_b, hy_w1, hy_b1, hy_freq, hy_w2, hy_b2, hy_w3,
              hy_skip, s5_lam_re, s5_lam_im, s5_log_dt, s5_b_re, s5_b_im, s5_c_re, s5_c_im, s5_d,
              s5_glu_w, s5_glu_b, ml_gate_b, group_norm, w_out, mlp_w1, mlp_w2):
    params = (ln_g, w_in, conv_w, conv_b, hy_w1, hy_b1, hy_freq, hy_w2, hy_b2, hy_w3, hy_skip,
              s5_lam_re, s5_lam_im, s5_log_dt, s5_b_re, s5_b_im, s5_c_re, s5_c_im, s5_d,
              s5_glu_w, s5_glu_b, ml_gate_b, group_norm, w_out, mlp_w1, mlp_w2)
    y_prompt = x_prompt
    y_sample = x_sample
    for l in range(DEPTH):
        lp = [p[l] for p in params]
        y_prompt = encoder_layer(y_prompt, *lp)
        y_sample = encoder_layer(y_sample, *lp)
    return (y_prompt, y_sample)
```

```python
import functools
import math

import jax
import jax.numpy as jnp
from jax import lax
from jax.experimental import pallas as pl
from jax.experimental.pallas import tpu as pltpu

F32 = jnp.float32
BF16 = jnp.bfloat16

D_MODEL = 2048
DEPTH = 4
D_HY = 512
D_S5 = 512
D_ML = 1024
HY_EMB = 33
HY_BANDS = 16
HY_FILT = 64
HY_FAST_DECAY = 0.3
HY_SLOW_DECAY = 1.5
HY_TARGET = 1e-2
S5_GROUP = 16
S5_G = 32
S5_P = 64
S5_STATE = S5_G * S5_P
ML_HEADS = 4
ML_DH = 256
D_FF = 4 * D_MODEL
N_CONV = 3 * D_HY + 2 * D_ML
P_IN = N_CONV + D_S5 + 2 * D_ML + 4 * ML_HEADS
P_PAD = 6272
EPS = 1e-6

LANE = 128
SUBLANE = 8
VMEM_LIMIT_BYTES = 48 * 1024 * 1024

DFT_N2 = LANE
S5_STREAMS = SUBLANE
S5_TSTEP = 32
ML_CHUNK = 256
COL_U = N_CONV // D_S5
COL_V = (N_CONV + D_S5) // ML_DH
COL_O = (N_CONV + D_S5 + D_ML) // ML_DH
COL_G = (N_CONV + D_S5 + 2 * D_ML) // LANE


def _cp(*sem):
    return pltpu.CompilerParams(dimension_semantics=sem, vmem_limit_bytes=VMEM_LIMIT_BYTES)


def _tile(n, pref):
    t = min(n, pref)
    while n % t:
        t //= 2
    return t


def _norm_mm_body(x_ref, g_ref, w_ref, o_ref, xn_ref, *, relu2):
    @pl.when(pl.program_id(1) == 0)
    def _():
        x = x_ref[...]
        ms = jnp.mean(x * x, axis=-1, keepdims=True)
        xn_ref[...] = (x * lax.rsqrt(ms + EPS) * g_ref[...]).astype(BF16)

    acc = jnp.dot(xn_ref[...], w_ref[...], preferred_element_type=F32)
    if relu2:
        acc = jnp.square(jnp.maximum(acc, 0.0))
    o_ref[...] = acc.astype(o_ref.dtype)


def _norm_matmul(x, g, w, *, relu2, out_dtype, tn):
    m, k = x.shape
    n = w.shape[1]
    tm = _tile(m, 1024)
    return pl.pallas_call(
        functools.partial(_norm_mm_body, relu2=relu2),
        grid=(m // tm, n // tn),
        in_specs=[pl.BlockSpec((tm, k), lambda i, j: (i, 0)),
                  pl.BlockSpec((1, k), lambda i, j: (0, 0)),
                  pl.BlockSpec((k, tn), lambda i, j: (0, j))],
        out_specs=pl.BlockSpec((tm, tn), lambda i, j: (i, j)),
        out_shape=jax.ShapeDtypeStruct((m, n), out_dtype),
        scratch_shapes=[pltpu.VMEM((tm, k), BF16)],
        compiler_params=_cp("arbitrary", "arbitrary"),
        name="norm_mlp_up" if relu2 else "norm_proj_in",
    )(x, g, w)


def _mm_resnorm_body(*refs, npairs, nk):
    a_refs = refs[:npairs]
    w_refs = refs[npairs:2 * npairs]
    r_ref, g_ref, o_ref, acc_ref = refs[2 * npairs:]
    kk = pl.program_id(1)
    part = None
    for a_ref, w_ref in zip(a_refs, w_refs):
        d = jnp.dot(a_ref[...].astype(BF16), w_ref[...], preferred_element_type=F32)
        part = d if part is None else part + d

    @pl.when(kk == 0)
    def _():
        acc_ref[...] = part

    @pl.when(kk > 0)
    def _():
        acc_ref[...] += part

    @pl.when(kk == nk - 1)
    def _():
        f = acc_ref[...]
        ms = jnp.mean(f * f, axis=-1, keepdims=True)
        o_ref[...] = r_ref[...] + f * lax.rsqrt(ms + EPS) * g_ref[...]


def _matmul_resnorm(pairs, resid, g, *, nk, tm):
    m, n = resid.shape
    tm = _tile(m, tm)
    npairs = len(pairs)
    a_specs, w_specs = [], []
    for a, w in pairs:
        ka = a.shape[1]
        a_specs.append(pl.BlockSpec((tm, ka // nk), lambda i, kk: (i, kk)))
        w_specs.append(pl.BlockSpec((ka // nk, n), lambda i, kk: (kk, 0)))
    return pl.pallas_call(
        functools.partial(_mm_resnorm_body, npairs=npairs, nk=nk),
        grid=(m // tm, nk),
        in_specs=a_specs + w_specs + [pl.BlockSpec((tm, n), lambda i, kk: (i, 0)),
                                      pl.BlockSpec((1, n), lambda i, kk: (0, 0))],
        out_specs=pl.BlockSpec((tm, n), lambda i, kk: (i, 0)),
        out_shape=jax.ShapeDtypeStruct((m, n), F32),
        scratch_shapes=[pltpu.VMEM((tm, n), F32)],
        compiler_params=_cp("arbitrary", "arbitrary"),
        name="proj_out_resnorm" if nk == 1 else "mlp_down_resnorm",
    )(*[a for a, _ in pairs], *[w for _, w in pairs], resid, g)


def _conv_body(x_ref, p_ref, n_ref, w_ref, b_ref, o_ref, *, nblk):
    i = pl.program_id(1)
    j = pl.program_id(2)
    x = x_ref[...]
    tm = x.shape[0]
    row = lax.broadcasted_iota(jnp.int32, x.shape, 0)
    prev_row = jnp.where(i == 0, 0.0, p_ref[SUBLANE - 1:SUBLANE, :])
    next_row = jnp.where(i == nblk - 1, 0.0, n_ref[0:1, :])
    xm = jnp.where(row == 0, prev_row, pltpu.roll(x, 1, 0))
    xp = jnp.where(row == tm - 1, next_row, pltpu.roll(x, tm - 1, 0))
    w = w_ref[...]
    y = b_ref[...] + xm * w[0:1] + x * w[1:2] + xp * w[2:3]
    act = y * jax.nn.sigmoid(y)
    scale = jnp.where(j >= 5, ML_DH ** -0.5, 1.0)
    o_ref[...] = jnp.where(j >= 3, act * scale, y)


def _short_conv(proj, conv_w, conv_b, r0, nb, seq):
    mtot = proj.shape[0]
    tm = _tile(seq, 512)
    nblk = seq // tm
    ncol = N_CONV // D_HY
    last8 = mtot // SUBLANE - 1

    tm8 = tm // SUBLANE

    def x_map(b, i, j):
        return (r0 // tm + b * nblk + i, j)

    def p_map(b, i, j):
        return (jnp.maximum(r0 // SUBLANE + (b * nblk + i) * tm8 - 1, 0), j)

    def n_map(b, i, j):
        return (jnp.minimum(r0 // SUBLANE + (b * nblk + i + 1) * tm8, last8), j)

    return pl.pallas_call(
        functools.partial(_conv_body, nblk=nblk),
        grid=(nb, nblk, ncol),
        in_specs=[pl.BlockSpec((tm, D_HY), x_map),
                  pl.BlockSpec((SUBLANE, D_HY), p_map),
                  pl.BlockSpec((SUBLANE, D_HY), n_map),
                  pl.BlockSpec((3, D_HY), lambda b, i, j: (0, j)),
                  pl.BlockSpec((1, D_HY), lambda b, i, j: (0, j))],
        out_specs=pl.BlockSpec((None, tm, D_HY), lambda b, i, j: (j, b * nblk + i, 0)),
        out_shape=jax.ShapeDtypeStruct((ncol, nb * seq, D_HY), F32),
        compiler_params=_cp("arbitrary", "arbitrary", "arbitrary"),
        name="short_conv",
    )(proj, proj, proj, conv_w, conv_b)


def _dft_tables(seq):
    n = 2 * seq
    n1 = n // DFT_N2
    n1h = n1 // 2
    k1 = jnp.arange(n1, dtype=jnp.int32)
    t1 = jnp.arange(n1h, dtype=jnp.int32)
    ang = ((k1[:, None] * t1[None, :]) % n1).astype(F32) * (2.0 * math.pi / n1)
    lead_f = jnp.concatenate([jnp.cos(ang), -jnp.sin(ang)], axis=0).astype(BF16)
    lead_ic = jnp.cos(ang).T.astype(BF16)
    lead_is = (-jnp.sin(ang)).T.astype(BF16)
    t2 = jnp.arange(DFT_N2, dtype=jnp.int32)
    k2 = jnp.arange(DFT_N2, dtype=jnp.int32)
    idx = (t2[None, None, :] * (k2[None, :, None] * n1 + k1[:, None, None])) % n
    ang2 = idx.astype(F32) * (2.0 * math.pi / n)
    f_re = jnp.cos(ang2)
    f_im = -jnp.sin(ang2)
    mid_f = jnp.concatenate([f_re, f_im], axis=1).astype(BF16)
    mid_g = jnp.concatenate([jnp.swapaxes(f_re, 1, 2), jnp.swapaxes(f_im, 1, 2)], axis=1).astype(BF16)
    return dict(n=n, n1=n1, n1h=n1h, lead_f=lead_f, lead_ic=lead_ic, lead_is=lead_is,
                mid_f=mid_f, mid_g=mid_g)


def _hy_features(seq):
    t = jnp.linspace(0.0, 1.0, seq, dtype=F32)[:, None]
    w = 2.0 * math.pi * jnp.arange(seq, dtype=F32)[:, None] / seq
    f = jnp.linspace(1e-4, HY_BANDS - 1, HY_BANDS, dtype=F32)[None, :]
    z = jnp.concatenate([t, jnp.cos(f * w), -jnp.sin(f * w)], axis=-1)
    return jnp.pad(z, ((0, 0), (0, LANE - HY_EMB)))


def _hy_deltas():
    d = jnp.abs(jnp.linspace(math.log(HY_TARGET) / HY_SLOW_DECAY,
                             math.log(HY_TARGET) / HY_FAST_DECAY, D_HY, dtype=F32))
    return jnp.tile(d, 4)[None, :]


def _hy_filter_body(z_ref, w1_ref, b1_ref, fr_ref, w2_ref, b2_ref, w3_ref, dl_ref, h_ref, ss_ref):
    i = pl.program_id(0)
    hi = lax.Precision.HIGHEST
    z = z_ref[...]
    fr = fr_ref[...]
    h = jnp.sin(fr * (jnp.dot(z, w1_ref[...], precision=hi, preferred_element_type=F32) + b1_ref[...]))
    h = jnp.sin(fr * (jnp.dot(h, w2_ref[...], precision=hi, preferred_element_type=F32) + b2_ref[...]))
    h = jnp.dot(h, w3_ref[...], precision=hi, preferred_element_type=F32)
    h = h * jnp.exp(-z[:, 0:1] * dl_ref[...])
    row = lax.broadcasted_iota(jnp.int32, h.shape, 0)
    col = lax.broadcasted_iota(jnp.int32, h.shape, 1)
    bwd = (col // D_HY) % 2 == 1
    h = jnp.where((row + i * h.shape[0] == 0) & bwd, 0.0, h)
    h_ref[...] = h
    part = jnp.broadcast_to(jnp.sum(h * h, axis=0, keepdims=True), ss_ref.shape)

    @pl.when(i == 0)
    def _():
        ss_ref[...] = part

    @pl.when(i > 0)
    def _():
        ss_ref[...] += part


def _hy_filter(seq, z, w1, b1, fr, w2, b2, w3, deltas):
    tm = _tile(seq, 512)
    nc = 4 * D_HY
    full = lambda shape: pl.BlockSpec(shape, lambda i: (0, 0))
    return pl.pallas_call(
        _hy_filter_body,
        grid=(seq // tm,),
        in_specs=[pl.BlockSpec((tm, LANE), lambda i: (i, 0)),
                  full((LANE, LANE)), full((1, LANE)), full((1, LANE)),
                  full((LANE, LANE)), full((1, LANE)), full((LANE, nc)), full((1, nc))],
        out_specs=[pl.BlockSpec((tm, nc), lambda i: (i, 0)), full((SUBLANE, nc))],
        out_shape=[jax.ShapeDtypeStruct((seq, nc), F32), jax.ShapeDtypeStruct((SUBLANE, nc), F32)],
        compiler_params=_cp("arbitrary"),
        name="hyena_filter",
    )(z, w1, b1, fr, w2, b2, w3, deltas)


def _lead_fwd_body(x_ref, f_ref, re_ref, im_ref):
    n1 = re_ref.shape[0]
    r = jnp.dot(f_ref[...], x_ref[...].astype(BF16), preferred_element_type=F32)
    re_ref[...] = r[:n1].astype(BF16)
    im_ref[...] = r[n1:].astype(BF16)


def _lead_fwd(x, tb, plane, row_blk, nb):
    n1, n1h = tb["n1"], tb["n1h"]
    cols = x.shape[-1]
    tn = _tile(cols, 4096)
    return pl.pallas_call(
        _lead_fwd_body,
        grid=(nb, cols // tn),
        in_specs=[pl.BlockSpec((None, n1h, tn), lambda b, j: (plane, row_blk + b, j)),
                  pl.BlockSpec((2 * n1, n1h), lambda b, j: (0, 0))],
        out_specs=[pl.BlockSpec((None, n1, tn), lambda b, j: (b, 0, j))] * 2,
        out_shape=[jax.ShapeDtypeStruct((nb, n1, cols), BF16)] * 2,
        compiler_params=_cp("arbitrary", "arbitrary"),
        name="hyena_lead_fwd",
    )(x, tb["lead_f"])


def _mid_spec_body(are_ref, aim_ref, f_ref, ss_ref, kr_ref, ki_ref, *, bk):
    n2 = DFT_N2
    ss = ss_ref[0:1, :]
    scale = lax.rsqrt(ss[:, :D_HY] + ss[:, D_HY:] + EPS)

    def body(i, c):
        f = f_ref[i]
        pa = jnp.dot(f, are_ref[i], preferred_element_type=F32)
        pb = jnp.dot(f, aim_ref[i], preferred_element_type=F32)
        xr = pa[:n2] - pb[n2:]
        xi = pb[:n2] + pa[n2:]
        kr_ref[i] = (xr[:, :D_HY] + xr[:, D_HY:]) * scale
        ki_ref[i] = (xi[:, :D_HY] - xi[:, D_HY:]) * scale
        return c

    lax.fori_loop(0, bk, body, 0)


def _mid_spec(a_re, a_im, sumsq, tb):
    n1 = tb["n1"]
    bk = _tile(n1, 8)
    a_re = a_re.reshape(n1, DFT_N2, 4 * D_HY)
    a_im = a_im.reshape(n1, DFT_N2, 4 * D_HY)
    a_spec = pl.BlockSpec((bk, DFT_N2, 2 * D_HY), lambda o, i: (i, 0, o))
    k_spec = pl.BlockSpec((None, bk, DFT_N2, D_HY), lambda o, i: (o, i, 0, 0))
    return pl.pallas_call(
        functools.partial(_mid_spec_body, bk=bk),
        grid=(2, n1 // bk),
        in_specs=[a_spec, a_spec,
                  pl.BlockSpec((bk, 2 * DFT_N2, DFT_N2), lambda o, i: (i, 0, 0)),
                  pl.BlockSpec((SUBLANE, 2 * D_HY), lambda o, i: (0, o))],
        out_specs=[k_spec, k_spec],
        out_shape=[jax.ShapeDtypeStruct((2, n1, DFT_N2, D_HY), F32)] * 2,
        compiler_params=_cp("arbitrary", "arbitrary"),
        name="hyena_mid_spectrum",
    )(a_re, a_im, tb["mid_f"], sumsq)


def _mid_conv_body(are_ref, aim_ref, f_ref, g_ref, kr_ref, ki_ref, bre_ref, bim_ref, *, bk):
    n2 = DFT_N2

    def body(i, c):
        f = f_ref[i]
        pa = jnp.dot(f, are_ref[i], preferred_element_type=F32)
        pb = jnp.dot(f, aim_ref[i], preferred_element_type=F32)
        xr = pa[:n2] - pb[n2:]
        xi = pb[:n2] + pa[n2:]
        kr = kr_ref[i]
        ki = ki_ref[i]
        pr = (xr * kr - xi * ki).astype(BF16)
        pi = (xr * ki + xi * kr).astype(BF16)
        g = g_ref[i]
        qa = jnp.dot(g, pr, preferred_element_type=F32)
        qb = jnp.dot(g, pi, preferred_element_type=F32)
        bre_ref[i] = (qa[:n2] + qb[n2:]).astype(BF16)
        bim_ref[i] = (qb[:n2] - qa[n2:]).astype(BF16)
        return c

    lax.fori_loop(0, bk, body, 0)


def _mid_conv(a_re, a_im, kf_re, kf_im, order, tb):
    n1 = tb["n1"]
    nb = a_re.shape[0]
    bk = _tile(n1, 8)
    a_re = a_re.reshape(nb, n1, DFT_N2, D_HY)
    a_im = a_im.reshape(nb, n1, DFT_N2, D_HY)
    a_spec = pl.BlockSpec((None, bk, DFT_N2, D_HY), lambda b, i: (b, i, 0, 0))
    t_spec = pl.BlockSpec((bk, 2 * DFT_N2, DFT_N2), lambda b, i: (i, 0, 0))
    k_spec = pl.BlockSpec((None, bk, DFT_N2, D_HY), lambda b, i: (order, i, 0, 0))
    b_re, b_im = pl.pallas_call(
        functools.partial(_mid_conv_body, bk=bk),
        grid=(nb, n1 // bk),
        in_specs=[a_spec, a_spec, t_spec, t_spec, k_spec, k_spec],
        out_specs=[a_spec, a_spec],
        out_shape=[jax.ShapeDtypeStruct((nb, n1, DFT_N2, D_HY), BF16)] * 2,
        compiler_params=_cp("arbitrary", "arbitrary"),
        name="hyena_mid_conv",
    )(a_re, a_im, tb["mid_f"], tb["mid_g"], kf_re, kf_im)
    return b_re.reshape(nb, n1, DFT_N2 * D_HY), b_im.reshape(nb, n1, DFT_N2 * D_HY)


def _lead_inv_body(bre_ref, bim_ref, c_ref, s_ref, z_ref, gate_ref, skip_ref, gn_ref, o_ref, *, inv_n, last):
    y = (jnp.dot(c_ref[...], bre_ref[...], preferred_element_type=F32)
         + jnp.dot(s_ref[...], bim_ref[...], preferred_element_type=F32)) * inv_n
    out = gate_ref[...] * (y + skip_ref[...] * z_ref[...])
    if not last:
        o_ref[...] = out
    else:
        gn = gn_ref[...]
        for c in range(out.shape[1] // D_HY):
            blk = out[:, c * D_HY:(c + 1) * D_HY]
            ms = jnp.mean(blk * blk, axis=-1, keepdims=True)
            o_ref[:, c * D_HY:(c + 1) * D_HY] = blk * lax.rsqrt(ms + EPS) * gn


def _lead_inv(b_re, b_im, z, z_plane, gates, gate_plane, skip, gn, tb, last):
    n1, n1h = tb["n1"], tb["n1h"]
    nb, _, cols = b_re.shape
    tn = _tile(cols, 4096)
    skip_t = jnp.tile(skip[None, :], (1, tn // D_HY))
    b_spec = pl.BlockSpec((None, n1, tn), lambda b, j: (b, 0, j))
    t_spec = pl.BlockSpec((n1h, n1), lambda b, j: (0, 0))
    return pl.pallas_call(
        functools.partial(_lead_inv_body, inv_n=1.0 / tb["n"], last=last),
        grid=(nb, cols // tn),
        in_specs=[b_spec, b_spec, t_spec, t_spec,
                  pl.BlockSpec((None, n1h, tn), lambda b, j: (z_plane, b, j)),
                  pl.BlockSpec((None, n1h, tn), lambda b, j: (gate_plane, b, j)),
                  pl.BlockSpec((1, tn), lambda b, j: (0, 0)),
                  pl.BlockSpec((1, D_HY), lambda b, j: (0, 0))],
        out_specs=pl.BlockSpec((None, n1h, tn), lambda b, j: (0, b, j)),
        out_shape=jax.ShapeDtypeStruct((1, nb * n1h, cols), F32),
        compiler_params=_cp("arbitrary", "arbitrary"),
        name="hyena_lead_inv",
    )(b_re, b_im, tb["lead_ic"], tb["lead_is"], z, gates, skip_t, gn)


def _hyena(conv_out, nb, seq, tb, kf_re, kf_im, skip, gn):
    n1h = tb["n1h"]
    planes = conv_out.reshape(conv_out.shape[0], nb * n1h, DFT_N2 * D_HY)
    z, z_plane = planes, 0
    for order in range(2):
        a_re, a_im = _lead_fwd(z, tb, z_plane, 0, nb)
        b_re, b_im = _mid_conv(a_re, a_im, kf_re, kf_im, order, tb)
        z = _lead_inv(b_re, b_im, z, z_plane, planes, 1 + order, skip[order], gn, tb, order == 1)
        z_plane = 0
    return z.reshape(nb * seq, D_HY)


def _hyena_spectra(seq, tb, w1, b1, fr, w2, b2, w3):
    pad_r = LANE - HY_EMB
    pad_c = LANE - HY_FILT
    hfilt, sumsq = _hy_filter(
        seq, _hy_features(seq),
        jnp.pad(w1, ((0, pad_r), (0, pad_c))), jnp.pad(b1, (0, pad_c))[None, :],
        jnp.pad(fr, (0, pad_c))[None, :], jnp.pad(w2, ((0, pad_c), (0, pad_c))),
        jnp.pad(b2, (0, pad_c))[None, :], jnp.pad(w3, ((0, pad_c), (0, 0))), _hy_deltas())
    x = hfilt.reshape(1, tb["n1h"], DFT_N2 * 4 * D_HY)
    a_re, a_im = _lead_fwd(x, tb, 0, 0, 1)
    return _mid_spec(a_re[0], a_im[0], sumsq, tb)


def _s5_scan_body(u_ref, bd_ref, a_ref, cd_ref, x0_ref, o_ref, bu_ref, st_ref, *, tstep, final_state):
    tb = pl.program_id(1)
    ns = S5_STREAMS
    half = S5_STATE

    @pl.when(tb == 0)
    def _():
        st_ref[...] = x0_ref[...]

    u = u_ref[...].reshape(tstep * ns, D_S5).astype(BF16)
    bu_ref[...] = jnp.dot(u, bd_ref[...], preferred_element_type=F32)

    cw = 4 * LANE
    for c in range(half // cw):
        re = pl.ds(c * cw, cw)
        im = pl.ds(half + c * cw, cw)
        ar = a_ref[:, re]
        ai = a_ref[:, im]

        def step(t, carry, re=re, im=im, ar=ar, ai=ai):
            xr, xi = carry
            rows = pl.ds(pl.multiple_of(t * ns, ns), ns)
            nxr = ar * xr - ai * xi + bu_ref[rows, re]
            nxi = ar * xi + ai * xr + bu_ref[rows, im]
            bu_ref[rows, re] = nxr
            bu_ref[rows, im] = nxi
            return nxr, nxi

        xr, xi = lax.fori_loop(0, tstep, step, (st_ref[:, re], st_ref[:, im]), unroll=2)
        st_ref[:, re] = xr
        st_ref[:, im] = xi

    if final_state:
        @pl.when(tb == pl.num_programs(1) - 1)
        def _():
            o_ref[...] = st_ref[...]
    else:
        y = jnp.dot(bu_ref[...].astype(BF16), cd_ref[...], preferred_element_type=F32)
        o_ref[...] = y.reshape(tstep, ns, D_S5)


def _s5_scan(u_t, bd, a_b, cd, x0, final_state):
    _, tlen, ns, _ = u_t.shape
    tstep = _tile(tlen, S5_TSTEP)
    if final_state:
        out_spec = pl.BlockSpec((None, ns, 2 * S5_STATE), lambda d, t: (d, 0, 0))
        out_shape = jax.ShapeDtypeStruct((2, ns, 2 * S5_STATE), F32)
    else:
        out_spec = pl.BlockSpec((None, tstep, ns, D_S5), lambda d, t: (d, t, 0, 0))
        out_shape = jax.ShapeDtypeStruct((2, tlen, ns, D_S5), F32)
    return pl.pallas_call(
        functools.partial(_s5_scan_body, tstep=tstep, final_state=final_state),
        grid=(2, tlen // tstep),
        in_specs=[pl.BlockSpec((None, tstep, ns, D_S5), lambda d, t: (d, t, 0, 0)),
                  pl.BlockSpec((None, D_S5, 2 * S5_STATE), lambda d, t: (d, 0, 0)),
                  pl.BlockSpec((None, ns, 2 * S5_STATE), lambda d, t: (d, 0, 0)),
                  pl.BlockSpec((None, 2 * S5_STATE, D_S5), lambda d, t: (d, 0, 0)),
                  pl.BlockSpec((None, ns, 2 * S5_STATE), lambda d, t: (d, 0, 0))],
        out_specs=out_spec,
        out_shape=out_shape,
        scratch_shapes=[pltpu.VMEM((tstep * ns, 2 * S5_STATE), F32),
                        pltpu.VMEM((ns, 2 * S5_STATE), F32)],
        compiler_params=_cp("arbitrary", "arbitrary"),
        name="s5_end_state" if final_state else "s5_scan",
    )(u_t, bd, a_b, cd, x0)


def _s5_params(lam_re, lam_im, log_dt, b_re, b_im, c_re, c_im):
    lam = lax.complex(jnp.minimum(lam_re, -1e-4), lam_im)
    dt = jnp.exp(log_dt)[..., None]
    a_bar = jnp.exp(lam * dt)
    b_bar = ((a_bar - 1.0) / lam)[..., None] * lax.complex(b_re, b_im)
    eye = jnp.eye(S5_G, dtype=F32)

    def in_mat(x):
        return jnp.einsum("dgph,gk->dghkp", x, eye).reshape(2, D_S5, S5_STATE)

    def out_mat(x):
        return jnp.einsum("dghp,gk->dgpkh", x, eye).reshape(2, S5_STATE, D_S5)

    bd = jnp.concatenate([in_mat(jnp.real(b_bar)), in_mat(jnp.imag(b_bar))], axis=2).astype(BF16)
    cd = jnp.concatenate([out_mat(c_re), out_mat(-c_im)], axis=1).astype(BF16)
    return lam * dt, a_bar.reshape(2, S5_STATE), bd, cd


def _s5_mixer(u, nb, seq, log_a, a_bar, bd, cd):
    ns = S5_STREAMS
    ncs = ns // nb
    tlen = seq // ncs
    ub = u.reshape(nb, seq, D_S5)

    def to_streams(x):
        return x.reshape(nb, ncs, tlen, D_S5).transpose(2, 0, 1, 3).reshape(tlen, ns, D_S5)

    def from_streams(y):
        return y.reshape(tlen, nb, ncs, D_S5).transpose(1, 2, 0, 3).reshape(nb, seq, D_S5)

    u_t = jnp.stack([to_streams(ub), to_streams(jnp.flip(ub, axis=1))])
    a_b = jnp.broadcast_to(jnp.concatenate([jnp.real(a_bar), jnp.imag(a_bar)], axis=1)[:, None, :],
                           (2, ns, 2 * S5_STATE))
    zeros = jnp.zeros((2, ns, 2 * S5_STATE), F32)
    x_end = _s5_scan(u_t, bd, a_b, cd, zeros, True)
    a_t = jnp.exp(log_a * tlen).reshape(2, 1, S5_STATE)
    xe = lax.complex(x_end[..., :S5_STATE], x_end[..., S5_STATE:]).reshape(2, nb, ncs, S5_STATE)
    carry = jnp.zeros((2, nb, S5_STATE), jnp.complex64)
    x_in = []
    for c in range(ncs):
        x_in.append(carry)
        carry = a_t * carry + xe[:, :, c]
    x_in = jnp.stack(x_in, axis=2).reshape(2, ns, S5_STATE)
    x0 = jnp.concatenate([jnp.real(x_in), jnp.imag(x_in)], axis=-1)
    y = _s5_scan(u_t, bd, a_b, cd, x0, False)
    y_f = from_streams(y[0])
    y_b = jnp.flip(from_streams(y[1]), axis=1)
    return jnp.stack([y_f, y_b]).reshape(2, nb * seq, D_S5)


def _s5_post_body(y_ref, u_ref, d_ref, w_ref, b_ref, gn_ref, o_ref):
    y = y_ref[0] + y_ref[1] + d_ref[...] * u_ref[...]
    g = jax.nn.gelu(y)
    r = jnp.dot(g.astype(BF16), w_ref[...], preferred_element_type=F32) + b_ref[...]
    out = r[:, :D_S5] * jax.nn.sigmoid(r[:, D_S5:])
    ms = jnp.mean(out * out, axis=-1, keepdims=True)
    o_ref[...] = out * lax.rsqrt(ms + EPS) * gn_ref[...]


def _s5_post(y2, proj, d, glu_w, glu_b, gn):
    m = y2.shape[1]
    tm = _tile(m, 512)
    full = lambda shape: pl.BlockSpec(shape, lambda i: (0, 0))
    return pl.pallas_call(
        _s5_post_body,
        grid=(m // tm,),
        in_specs=[pl.BlockSpec((2, tm, D_S5), lambda i: (0, i, 0)),
                  pl.BlockSpec((tm, D_S5), lambda i: (i, COL_U)),
                  full((1, D_S5)), full((D_S5, 2 * D_S5)), full((1, 2 * D_S5)), full((1, D_S5))],
        out_specs=pl.BlockSpec((tm, D_S5), lambda i: (i, 0)),
        out_shape=jax.ShapeDtypeStruct((m, D_S5), F32),
        compiler_params=_cp("arbitrary"),
        name="s5_glu_norm",
    )(y2, proj, d, glu_w, glu_b, gn)


def _mlstm_body(*refs, reverse, epilogue):
    if epilogue:
        (q_ref, k_ref, v_ref, g_ref, gt_ref, bias_ref, biast_ref, hf_ref, o_gate_ref, gn_ref,
         out_ref, c_ref, n_ref, m_ref) = refs
    else:
        (q_ref, k_ref, v_ref, g_ref, gt_ref, bias_ref, biast_ref,
         out_ref, c_ref, n_ref, m_ref) = refs
    head = pl.program_id(1)
    tc = ML_CHUNK

    @pl.when(pl.program_id(2) == 0)
    def _():
        c_ref[...] = jnp.zeros_like(c_ref)
        n_ref[...] = jnp.zeros_like(n_ref)
        m_ref[...] = jnp.zeros_like(m_ref)

    gate_i = 2 if reverse else 0
    idx_i = gate_i * ML_HEADS + head
    idx_f = idx_i + ML_HEADS
    g = g_ref[...] + bias_ref[...]
    lane = lax.broadcasted_iota(jnp.int32, g.shape, 1)
    ig_col = jnp.sum(jnp.where(lane == idx_i, g, 0.0), axis=1, keepdims=True)
    fg_col = jnp.sum(jnp.where(lane == idx_f, g, 0.0), axis=1, keepdims=True)
    gt = gt_ref[...] + biast_ref[:, 0:1]
    sub = lax.broadcasted_iota(jnp.int32, gt.shape, 0)
    ig_row = jnp.sum(jnp.where(sub == idx_i, gt, 0.0), axis=0, keepdims=True)
    fg_row = jnp.sum(jnp.where(sub == idx_f, gt, 0.0), axis=0, keepdims=True)
    lf_col = jax.nn.log_sigmoid(fg_col)
    lf_row = jax.nn.log_sigmoid(fg_row)

    r = lax.broadcasted_iota(jnp.int32, (tc, tc), 0)
    s = lax.broadcasted_iota(jnp.int32, (tc, tc), 1)
    seen = (s >= r) if reverse else (s <= r)
    seen_t = (r >= s) if reverse else (r <= s)
    b_col = jnp.sum(jnp.where(seen, lf_row, 0.0), axis=1, keepdims=True)
    b_row = jnp.sum(jnp.where(seen_t, lf_col, 0.0), axis=0, keepdims=True)
    b_last = jnp.sum(lf_row, axis=1, keepdims=True)

    m_prev = m_ref[...]
    dmat = jnp.where(seen, b_col - b_row + ig_row, -jnp.inf)
    g_car = b_col + m_prev
    mt = jnp.maximum(g_car, jnp.max(dmat, axis=1, keepdims=True))
    inter = jnp.exp(g_car - mt)
    q = q_ref[...]
    k = k_ref[...]
    v = v_ref[...]
    qb = q.astype(BF16)
    kb = k.astype(BF16)
    vb = v.astype(BF16)
    qk = lax.dot_general(qb, kb, (((1,), (1,)), ((), ())), preferred_element_type=F32)
    sc = qk * jnp.exp(dmat - mt)
    c_state = c_ref[...]
    num = (inter * jnp.dot(qb, c_state.astype(BF16), preferred_element_type=F32)
           + jnp.dot(sc.astype(BF16), vb, preferred_element_type=F32))
    n_state = n_ref[...]
    den = inter * jnp.sum(q * n_state, axis=1, keepdims=True) + jnp.sum(sc, axis=1, keepdims=True)
    h = num / jnp.maximum(jnp.abs(den), jnp.exp(-mt))

    a_row = b_last - b_row + ig_row
    a_col = b_last - b_col + ig_col
    m_new = jnp.maximum(b_last + m_prev, jnp.max(a_row, axis=1, keepdims=True))
    decay = jnp.exp(b_last + m_prev - m_new)
    kw = k * jnp.exp(a_col - m_new)
    c_ref[...] = decay * c_state + lax.dot_general(
        kw.astype(BF16), vb, (((0,), (0,)), ((), ())), preferred_element_type=F32)
    n_ref[...] = decay * n_state + jnp.sum(kw, axis=0, keepdims=True)
    m_ref[...] = m_new

    if epilogue:
        h = h + hf_ref[...]
        ms = jnp.mean(h * h, axis=-1, keepdims=True)
        hn = h * lax.rsqrt(ms + EPS) * gn_ref[...]
        out_ref[...] = hn * jax.nn.sigmoid(o_gate_ref[...])
    else:
        out_ref[...] = h


def _mlstm_dir(conv_out, proj, gates_t, bias, bias_t, r0, nb, seq, reverse, h_fwd=None, gn=None):
    tc = ML_CHUNK
    nch = seq // tc
    rb0 = r0 // tc

    def chunk(b, ci):
        return b * nch + (nch - 1 - ci if reverse else ci)

    in_specs = [
        pl.BlockSpec((None, tc, ML_DH), lambda b, h, ci: (3 + h // 2, chunk(b, ci), h % 2)),
        pl.BlockSpec((None, tc, ML_DH), lambda b, h, ci: (5 + h // 2, chunk(b, ci), h % 2)),
        pl.BlockSpec((tc, ML_DH), lambda b, h, ci: (rb0 + chunk(b, ci), COL_V + h)),
        pl.BlockSpec((tc, LANE), lambda b, h, ci: (rb0 + chunk(b, ci), COL_G)),
        pl.BlockSpec((4 * ML_HEADS, tc), lambda b, h, ci: (0, rb0 + chunk(b, ci))),
        pl.BlockSpec((1, LANE), lambda b, h, ci: (0, 0)),
        pl.BlockSpec((4 * ML_HEADS, LANE), lambda b, h, ci: (0, 0)),
    ]
    args = [conv_out, conv_out, proj, proj, gates_t, bias, bias_t]
    epilogue = h_fwd is not None
    if epilogue:
        in_specs += [
            pl.BlockSpec((tc, ML_DH), lambda b, h, ci: (chunk(b, ci), h)),
            pl.BlockSpec((tc, ML_DH), lambda b, h, ci: (rb0 + chunk(b, ci), COL_O + h)),
            pl.BlockSpec((1, ML_DH), lambda b, h, ci: (0, h)),
        ]
        args += [h_fwd, proj, gn]
    return pl.pallas_call(
        functools.partial(_mlstm_body, reverse=reverse, epilogue=epilogue),
        grid=(nb, ML_HEADS, nch),
        in_specs=in_specs,
        out_specs=pl.BlockSpec((tc, ML_DH), lambda b, h, ci: (chunk(b, ci), h)),
        out_shape=jax.ShapeDtypeStruct((nb * seq, D_ML), F32),
        scratch_shapes=[pltpu.VMEM((ML_DH, ML_DH), F32), pltpu.VMEM((1, ML_DH), F32),
                        pltpu.VMEM((1, 1), F32)],
        compiler_params=_cp("arbitrary", "arbitrary", "arbitrary"),
        name="mlstm_bwd_norm" if reverse else "mlstm_fwd",
    )(*args)


def _layer(x, sets, tables, p):
    w_in = jnp.pad(p["w_in"], ((0, 0), (0, P_PAD - P_IN))).astype(BF16)
    proj = _norm_matmul(x, p["ln_g"][0:1], w_in, relu2=False, out_dtype=F32, tn=P_PAD // 7)

    gates_t = proj[:, COL_G * LANE:COL_G * LANE + 4 * ML_HEADS].T
    bias = jnp.pad(p["ml_gate_b"].reshape(1, -1), ((0, 0), (0, LANE - 4 * ML_HEADS)))
    bias_t = jnp.broadcast_to(p["ml_gate_b"].reshape(-1, 1), (4 * ML_HEADS, LANE))
    gn = p["group_norm"]
    gn_hy = gn[None, :D_HY]
    gn_s5 = gn[None, D_HY:D_HY + D_S5]
    gn_ml = gn[None, D_HY + D_S5:]
    log_a, a_bar, bd, cd = _s5_params(p["s5_lam_re"], p["s5_lam_im"], p["s5_log_dt"], p["s5_b_re"],
                                      p["s5_b_im"], p["s5_c_re"], p["s5_c_im"])

    y_hy, y_s5d, y_ml = [], [], []
    for (r0, nb, seq) in sets:
        tb = tables[seq]
        conv_out = _short_conv(proj, p["conv_w"], p["conv_b"][None, :], r0, nb, seq)
        kf_re, kf_im = _hyena_spectra(seq, tb, p["hy_w1"], p["hy_b1"], p["hy_freq"], p["hy_w2"],
                                      p["hy_b2"], p["hy_w3"])
        y_hy.append(_hyena(conv_out, nb, seq, tb, kf_re, kf_im, p["hy_skip"], gn_hy))
        u = proj[r0:r0 + nb * seq, N_CONV:N_CONV + D_S5]
        y_s5d.append(_s5_mixer(u, nb, seq, log_a, a_bar, bd, cd))
        h_f = _mlstm_dir(conv_out, proj, gates_t, bias, bias_t, r0, nb, seq, False)
        y_ml.append(_mlstm_dir(conv_out, proj, gates_t, bias, bias_t, r0, nb, seq, True, h_f, gn_ml))
    y_hy = jnp.concatenate(y_hy, axis=0)
    y_ml = jnp.concatenate(y_ml, axis=0)
    y_s5 = _s5_post(jnp.concatenate(y_s5d, axis=1), proj, p["s5_d"][None, :],
                    p["s5_glu_w"].astype(BF16), p["s5_glu_b"][None, :], gn_s5)

    w_out = p["w_out"].astype(BF16)
    x = _matmul_resnorm([(y_hy, w_out[:D_HY]), (y_s5, w_out[D_HY:D_HY + D_S5]),
                         (y_ml, w_out[D_HY + D_S5:])], x, p["ln_g"][1:2], nk=1, tm=256)
    hid = _norm_matmul(x, p["ln_g"][2:3], p["mlp_w1"].astype(BF16), relu2=True, out_dtype=BF16, tn=1024)
    return _matmul_resnorm([(hid, p["mlp_w2"].astype(BF16))], x, p["ln_g"][3:4], nk=D_FF // 1024, tm=512)


_PARAM_NAMES = ("ln_g", "w_in", "conv_w", "conv_b", "hy_w1", "hy_b1", "hy_freq", "hy_w2", "hy_b2",
                "hy_w3", "hy_skip", "s5_lam_re", "s5_lam_im", "s5_log_dt", "s5_b_re", "s5_b_im",
                "s5_c_re", "s5_c_im", "s5_d", "s5_glu_w", "s5_glu_b", "ml_gate_b", "group_norm",
                "w_out", "mlp_w1", "mlp_w2")


def kernel(x_prompt, x_sample, ln_g, w_in, conv_w, conv_b, hy_w1, hy_b1, hy_freq, hy_w2, hy_b2, hy_w3,
           hy_skip, s5_lam_re, s5_lam_im, s5_log_dt, s5_b_re, s5_b_im, s5_c_re, s5_c_im, s5_d,
           s5_glu_w, s5_glu_b, ml_gate_b, group_norm, w_out, mlp_w1, mlp_w2):
    params = (ln_g, w_in, conv_w, conv_b, hy_w1, hy_b1, hy_freq, hy_w2, hy_b2, hy_w3, hy_skip,
              s5_lam_re, s5_lam_im, s5_log_dt, s5_b_re, s5_b_im, s5_c_re, s5_c_im, s5_d,
              s5_glu_w, s5_glu_b, ml_gate_b, group_norm, w_out, mlp_w1, mlp_w2)
    bp, lp, _ = x_prompt.shape
    bs, ls, _ = x_sample.shape
    sets = [(0, bp, lp), (bp * lp, bs, ls)]
    tables = {seq: _dft_tables(seq) for seq in {lp, ls}}
    x = jnp.concatenate([x_prompt.reshape(bp * lp, D_MODEL), x_sample.reshape(bs * ls, D_MODEL)], axis=0)
    for layer in range(DEPTH):
        p = {name: arr[layer] for name, arr in zip(_PARAM_NAMES, params)}
        x = _layer(x, sets, tables, p)
    return (x[:bp * lp].reshape(bp, lp, D_MODEL), x[bp * lp:].reshape(bs, ls, D_MODEL))
```

```python
import functools
import math

import jax
import jax.numpy as jnp
from jax import lax
from jax.experimental import pallas as pl
from jax.experimental.pallas import tpu as pltpu

F32 = jnp.float32
BF16 = jnp.bfloat16

D_MODEL = 2048
DEPTH = 4
D_HY = 512
D_S5 = 512
D_ML = 1024
HY_EMB = 33
HY_BANDS = 16
HY_FILT = 64
HY_FAST_DECAY = 0.3
HY_SLOW_DECAY = 1.5
HY_TARGET = 1e-2
S5_GROUP = 16
S5_G = 32
S5_P = 64
S5_STATE = S5_G * S5_P
ML_HEADS = 4
ML_DH = 256
D_FF = 4 * D_MODEL
N_CONV = 3 * D_HY + 2 * D_ML
P_IN = N_CONV + D_S5 + 2 * D_ML + 4 * ML_HEADS
P_PAD = 6272
EPS = 1e-6

LANE = 128
SUBLANE = 8
VMEM_LIMIT_BYTES = 48 * 1024 * 1024

DFT_N2 = LANE
S5_STREAMS = SUBLANE
S5_TSTEP = 64
ML_CHUNK = 256
COL_U = N_CONV // D_S5
COL_V = (N_CONV + D_S5) // ML_DH
COL_O = (N_CONV + D_S5 + D_ML) // ML_DH
COL_G = (N_CONV + D_S5 + 2 * D_ML) // LANE


def _cp(*sem):
    return pltpu.CompilerParams(dimension_semantics=sem, vmem_limit_bytes=VMEM_LIMIT_BYTES)


def _tile(n, pref):
    t = min(n, pref)
    while n % t:
        t //= 2
    return t


def _norm_mm_body(x_ref, g_ref, w_ref, o_ref, xn_ref, *, relu2):
    @pl.when(pl.program_id(1) == 0)
    def _():
        x = x_ref[...]
        ms = jnp.mean(x * x, axis=-1, keepdims=True)
        xn_ref[...] = (x * lax.rsqrt(ms + EPS) * g_ref[...]).astype(BF16)

    acc = jnp.dot(xn_ref[...], w_ref[...], preferred_element_type=F32)
    if relu2:
        acc = jnp.square(jnp.maximum(acc, 0.0))
    o_ref[...] = acc.astype(o_ref.dtype)


def _norm_matmul(x, g, w, *, relu2, out_dtype, tn):
    m, k = x.shape
    n = w.shape[1]
    tm = _tile(m, 1024)
    return pl.pallas_call(
        functools.partial(_norm_mm_body, relu2=relu2),
        grid=(m // tm, n // tn),
        in_specs=[pl.BlockSpec((tm, k), lambda i, j: (i, 0)),
                  pl.BlockSpec((1, k), lambda i, j: (0, 0)),
                  pl.BlockSpec((k, tn), lambda i, j: (0, j))],
        out_specs=pl.BlockSpec((tm, tn), lambda i, j: (i, j)),
        out_shape=jax.ShapeDtypeStruct((m, n), out_dtype),
        scratch_shapes=[pltpu.VMEM((tm, k), BF16)],
        compiler_params=_cp("arbitrary", "arbitrary"),
        name="norm_mlp_up" if relu2 else "norm_proj_in",
    )(x, g, w)


def _mm_resnorm_body(*refs, npairs, nk):
    a_refs = refs[:npairs]
    w_refs = refs[npairs:2 * npairs]
    r_ref, g_ref, o_ref, acc_ref = refs[2 * npairs:]
    kk = pl.program_id(1)
    part = None
    for a_ref, w_ref in zip(a_refs, w_refs):
        d = jnp.dot(a_ref[...].astype(BF16), w_ref[...], preferred_element_type=F32)
        part = d if part is None else part + d

    @pl.when(kk == 0)
    def _():
        acc_ref[...] = part

    @pl.when(kk > 0)
    def _():
        acc_ref[...] += part

    @pl.when(kk == nk - 1)
    def _():
        f = acc_ref[...]
        ms = jnp.mean(f * f, axis=-1, keepdims=True)
        o_ref[...] = r_ref[...] + f * lax.rsqrt(ms + EPS) * g_ref[...]


def _matmul_resnorm(pairs, resid, g, *, nk, tm):
    m, n = resid.shape
    tm = _tile(m, tm)
    npairs = len(pairs)
    a_specs, w_specs = [], []
    for a, w in pairs:
        ka = a.shape[1]
        a_specs.append(pl.BlockSpec((tm, ka // nk), lambda i, kk: (i, kk)))
        w_specs.append(pl.BlockSpec((ka // nk, n), lambda i, kk: (kk, 0)))
    return pl.pallas_call(
        functools.partial(_mm_resnorm_body, npairs=npairs, nk=nk),
        grid=(m // tm, nk),
        in_specs=a_specs + w_specs + [pl.BlockSpec((tm, n), lambda i, kk: (i, 0)),
                                      pl.BlockSpec((1, n), lambda i, kk: (0, 0))],
        out_specs=pl.BlockSpec((tm, n), lambda i, kk: (i, 0)),
        out_shape=jax.ShapeDtypeStruct((m, n), F32),
        scratch_shapes=[pltpu.VMEM((tm, n), F32)],
        compiler_params=_cp("arbitrary", "arbitrary"),
        name="proj_out_resnorm" if nk == 1 else "mlp_down_resnorm",
    )(*[a for a, _ in pairs], *[w for _, w in pairs], resid, g)


def _conv_body(x_ref, p_ref, n_ref, w_ref, b_ref, o_ref, *, nblk):
    i = pl.program_id(1)
    j = pl.program_id(2)
    x = x_ref[...]
    tm = x.shape[0]
    row = lax.broadcasted_iota(jnp.int32, x.shape, 0)
    prev_row = jnp.where(i == 0, 0.0, p_ref[SUBLANE - 1:SUBLANE, :])
    next_row = jnp.where(i == nblk - 1, 0.0, n_ref[0:1, :])
    xm = jnp.where(row == 0, prev_row, pltpu.roll(x, 1, 0))
    xp = jnp.where(row == tm - 1, next_row, pltpu.roll(x, tm - 1, 0))
    w = w_ref[...]
    y = b_ref[...] + xm * w[0:1] + x * w[1:2] + xp * w[2:3]
    act = y * jax.nn.sigmoid(y)
    scale = jnp.where(j >= 5, ML_DH ** -0.5, 1.0)
    o_ref[...] = jnp.where(j >= 3, act * scale, y)


def _short_conv(proj, conv_w, conv_b, nb, seq):
    mtot = proj.shape[0]
    tm = _tile(seq, 512)
    nblk = seq // tm
    ncol = N_CONV // D_HY
    last8 = mtot // SUBLANE - 1
    tm8 = tm // SUBLANE

    def x_map(b, i, j):
        return (b * nblk + i, j)

    def p_map(b, i, j):
        return (jnp.maximum((b * nblk + i) * tm8 - 1, 0), j)

    def n_map(b, i, j):
        return (jnp.minimum((b * nblk + i + 1) * tm8, last8), j)

    return pl.pallas_call(
        functools.partial(_conv_body, nblk=nblk),
        grid=(nb, nblk, ncol),
        in_specs=[pl.BlockSpec((tm, D_HY), x_map),
                  pl.BlockSpec((SUBLANE, D_HY), p_map),
                  pl.BlockSpec((SUBLANE, D_HY), n_map),
                  pl.BlockSpec((3, D_HY), lambda b, i, j: (0, j)),
                  pl.BlockSpec((1, D_HY), lambda b, i, j: (0, j))],
        out_specs=pl.BlockSpec((None, tm, D_HY), lambda b, i, j: (j, b * nblk + i, 0)),
        out_shape=jax.ShapeDtypeStruct((ncol, nb * seq, D_HY), F32),
        compiler_params=_cp("arbitrary", "arbitrary", "arbitrary"),
        name="short_conv",
    )(proj, proj, proj, conv_w, conv_b)


def _dft_tables(seq):
    n = 2 * seq
    n1 = n // DFT_N2
    n1h = n1 // 2
    k1 = jnp.arange(n1, dtype=jnp.int32)
    t1 = jnp.arange(n1h, dtype=jnp.int32)
    ang = ((k1[:, None] * t1[None, :]) % n1).astype(F32) * (2.0 * math.pi / n1)
    lead_f = jnp.concatenate([jnp.cos(ang), -jnp.sin(ang)], axis=0).astype(BF16)
    lead_ic = jnp.cos(ang).T.astype(BF16)
    lead_is = (-jnp.sin(ang)).T.astype(BF16)
    t2 = jnp.arange(DFT_N2, dtype=jnp.int32)
    k2 = jnp.arange(DFT_N2, dtype=jnp.int32)
    idx = (t2[None, None, :] * (k2[None, :, None] * n1 + k1[:, None, None])) % n
    ang2 = idx.astype(F32) * (2.0 * math.pi / n)
    f_re = jnp.cos(ang2)
    f_im = -jnp.sin(ang2)
    mid_f = jnp.concatenate([jnp.concatenate([f_re, -f_im], axis=2),
                             jnp.concatenate([f_im, f_re], axis=2)], axis=1).astype(BF16)
    g_re = jnp.swapaxes(f_re, 1, 2)
    g_im = jnp.swapaxes(f_im, 1, 2)
    mid_g = jnp.concatenate([jnp.concatenate([g_re, g_im], axis=2),
                             jnp.concatenate([-g_im, g_re], axis=2)], axis=1).astype(BF16)
    return dict(n=n, n1=n1, n1h=n1h, lead_f=lead_f, lead_ic=lead_ic, lead_is=lead_is,
                mid_f=mid_f, mid_g=mid_g)


def _hy_features(seq):
    t = jnp.linspace(0.0, 1.0, seq, dtype=F32)[:, None]
    w = 2.0 * math.pi * jnp.arange(seq, dtype=F32)[:, None] / seq
    f = jnp.linspace(1e-4, HY_BANDS - 1, HY_BANDS, dtype=F32)[None, :]
    z = jnp.concatenate([t, jnp.cos(f * w), -jnp.sin(f * w)], axis=-1)
    return jnp.pad(z, ((0, 0), (0, LANE - HY_EMB)))


def _hy_deltas():
    d = jnp.abs(jnp.linspace(math.log(HY_TARGET) / HY_SLOW_DECAY,
                             math.log(HY_TARGET) / HY_FAST_DECAY, D_HY, dtype=F32))
    return jnp.tile(d, 4)[None, :]


def _hy_filter_body(z_ref, w1_ref, b1_ref, fr_ref, w2_ref, b2_ref, w3_ref, dl_ref, h_ref, ss_ref):
    i = pl.program_id(0)
    hi = lax.Precision.HIGHEST
    z = z_ref[...]
    fr = fr_ref[...]
    h = jnp.sin(fr * (jnp.dot(z, w1_ref[...], precision=hi, preferred_element_type=F32) + b1_ref[...]))
    h = jnp.sin(fr * (jnp.dot(h, w2_ref[...], precision=hi, preferred_element_type=F32) + b2_ref[...]))
    h = jnp.dot(h.astype(BF16), w3_ref[...], preferred_element_type=F32)
    h = h * jnp.exp(-z[:, 0:1] * dl_ref[...])
    row = lax.broadcasted_iota(jnp.int32, h.shape, 0)
    col = lax.broadcasted_iota(jnp.int32, h.shape, 1)
    bwd = (col // D_HY) % 2 == 1
    h = jnp.where((row + i * h.shape[0] == 0) & bwd, 0.0, h)
    h_ref[...] = h
    part = jnp.broadcast_to(jnp.sum(h * h, axis=0, keepdims=True), ss_ref.shape)

    @pl.when(i == 0)
    def _():
        ss_ref[...] = part

    @pl.when(i > 0)
    def _():
        ss_ref[...] += part


def _hy_filter(seq, z, w1, b1, fr, w2, b2, w3, deltas):
    tm = _tile(seq, 512)
    nc = 4 * D_HY
    full = lambda shape: pl.BlockSpec(shape, lambda i: (0, 0))
    return pl.pallas_call(
        _hy_filter_body,
        grid=(seq // tm,),
        in_specs=[pl.BlockSpec((tm, LANE), lambda i: (i, 0)),
                  full((LANE, LANE)), full((1, LANE)), full((1, LANE)),
                  full((LANE, LANE)), full((1, LANE)), full((LANE, nc)), full((1, nc))],
        out_specs=[pl.BlockSpec((tm, nc), lambda i: (i, 0)), full((SUBLANE, nc))],
        out_shape=[jax.ShapeDtypeStruct((seq, nc), F32), jax.ShapeDtypeStruct((SUBLANE, nc), F32)],
        compiler_params=_cp("arbitrary"),
        name="hyena_filter",
    )(z, w1, b1, fr, w2, b2, w3, deltas)


def _lead_fwd_body(x_ref, f_ref, re_ref, im_ref):
    n1 = re_ref.shape[0]
    r = jnp.dot(f_ref[...], x_ref[...].astype(BF16), preferred_element_type=F32)
    re_ref[...] = r[:n1].astype(BF16)
    im_ref[...] = r[n1:].astype(BF16)


def _lead_fwd(x, tb, plane, nb):
    n1, n1h = tb["n1"], tb["n1h"]
    cols = x.shape[-1]
    tn = _tile(cols, 4096)
    return pl.pallas_call(
        _lead_fwd_body,
        grid=(nb, cols // tn),
        in_specs=[pl.BlockSpec((None, n1h, tn), lambda b, j: (plane, b, j)),
                  pl.BlockSpec((2 * n1, n1h), lambda b, j: (0, 0))],
        out_specs=[pl.BlockSpec((None, n1, tn), lambda b, j: (b, 0, j))] * 2,
        out_shape=[jax.ShapeDtypeStruct((nb, n1, cols), BF16)] * 2,
        compiler_params=_cp("arbitrary", "arbitrary"),
        name="hyena_lead_fwd",
    )(x, tb["lead_f"])


def _mid_spec_body(are_ref, aim_ref, f_ref, ss_ref, kr_ref, ki_ref, *, bk):
    n2 = DFT_N2
    ss = ss_ref[0:1, :]
    scale = lax.rsqrt(ss[:, :D_HY] + ss[:, D_HY:] + EPS)

    def body(i, c):
        a = jnp.concatenate([are_ref[i], aim_ref[i]], axis=0)
        x = jnp.dot(f_ref[i], a, preferred_element_type=F32)
        xr = x[:n2]
        xi = x[n2:]
        kr_ref[i] = (xr[:, :D_HY] + xr[:, D_HY:]) * scale
        ki_ref[i] = (xi[:, :D_HY] - xi[:, D_HY:]) * scale
        return c

    lax.fori_loop(0, bk, body, 0)


def _mid_spec(a_re, a_im, sumsq, tb):
    n1 = tb["n1"]
    bk = _tile(n1, 8)
    a_re = a_re.reshape(n1, DFT_N2, 4 * D_HY)
    a_im = a_im.reshape(n1, DFT_N2, 4 * D_HY)
    a_spec = pl.BlockSpec((bk, DFT_N2, 2 * D_HY), lambda o, i: (i, 0, o))
    k_spec = pl.BlockSpec((None, bk, DFT_N2, D_HY), lambda o, i: (o, i, 0, 0))
    return pl.pallas_call(
        functools.partial(_mid_spec_body, bk=bk),
        grid=(2, n1 // bk),
        in_specs=[a_spec, a_spec,
                  pl.BlockSpec((bk, 2 * DFT_N2, 2 * DFT_N2), lambda o, i: (i, 0, 0)),
                  pl.BlockSpec((SUBLANE, 2 * D_HY), lambda o, i: (0, o))],
        out_specs=[k_spec, k_spec],
        out_shape=[jax.ShapeDtypeStruct((2, n1, DFT_N2, D_HY), F32)] * 2,
        compiler_params=_cp("arbitrary", "arbitrary"),
        name="hyena_mid_spectrum",
    )(a_re, a_im, tb["mid_f"], sumsq)


def _mid_conv_body(are_ref, aim_ref, f_ref, g_ref, kr_ref, ki_ref, bre_ref, bim_ref, *, bk):
    n2 = DFT_N2

    def body(i, c):
        a = jnp.concatenate([are_ref[i], aim_ref[i]], axis=0)
        x = jnp.dot(f_ref[i], a, preferred_element_type=F32)
        xr = x[:n2]
        xi = x[n2:]
        kr = kr_ref[i]
        ki = ki_ref[i]
        p = jnp.concatenate([xr * kr - xi * ki, xr * ki + xi * kr], axis=0).astype(BF16)
        q = jnp.dot(g_ref[i], p, preferred_element_type=F32)
        bre_ref[i] = q[:n2].astype(BF16)
        bim_ref[i] = q[n2:].astype(BF16)
        return c

    lax.fori_loop(0, bk, body, 0)


def _mid_conv(a_re, a_im, kf_re, kf_im, order, tb):
    n1 = tb["n1"]
    nb = a_re.shape[0]
    bk = _tile(n1, 8)
    a_re = a_re.reshape(nb, n1, DFT_N2, D_HY)
    a_im = a_im.reshape(nb, n1, DFT_N2, D_HY)
    a_spec = pl.BlockSpec((None, bk, DFT_N2, D_HY), lambda b, i: (b, i, 0, 0))
    t_spec = pl.BlockSpec((bk, 2 * DFT_N2, 2 * DFT_N2), lambda b, i: (i, 0, 0))
    k_spec = pl.BlockSpec((None, bk, DFT_N2, D_HY), lambda b, i: (order, i, 0, 0))
    b_re, b_im = pl.pallas_call(
        functools.partial(_mid_conv_body, bk=bk),
        grid=(nb, n1 // bk),
        in_specs=[a_spec, a_spec, t_spec, t_spec, k_spec, k_spec],
        out_specs=[a_spec, a_spec],
        out_shape=[jax.ShapeDtypeStruct((nb, n1, DFT_N2, D_HY), BF16)] * 2,
        compiler_params=_cp("arbitrary", "arbitrary"),
        name="hyena_mid_conv",
    )(a_re, a_im, tb["mid_f"], tb["mid_g"], kf_re, kf_im)
    return b_re.reshape(nb, n1, DFT_N2 * D_HY), b_im.reshape(nb, n1, DFT_N2 * D_HY)


def _lead_inv_body(bre_ref, bim_ref, c_ref, s_ref, z_ref, gate_ref, skip_ref, gn_ref, o_ref, *, inv_n, last):
    y = (jnp.dot(c_ref[...], bre_ref[...], preferred_element_type=F32)
         + jnp.dot(s_ref[...], bim_ref[...], preferred_element_type=F32)) * inv_n
    out = gate_ref[...] * (y + skip_ref[...] * z_ref[...])
    if not last:
        o_ref[...] = out
    else:
        gn = gn_ref[...]
        for c in range(out.shape[1] // D_HY):
            blk = out[:, c * D_HY:(c + 1) * D_HY]
            ms = jnp.mean(blk * blk, axis=-1, keepdims=True)
            o_ref[:, c * D_HY:(c + 1) * D_HY] = blk * lax.rsqrt(ms + EPS) * gn


def _lead_inv(b_re, b_im, z, z_plane, gates, gate_plane, skip, gn, tb, last):
    n1, n1h = tb["n1"], tb["n1h"]
    nb, _, cols = b_re.shape
    tn = _tile(cols, 4096)
    skip_t = jnp.tile(skip[None, :], (1, tn // D_HY))
    b_spec = pl.BlockSpec((None, n1, tn), lambda b, j: (b, 0, j))
    t_spec = pl.BlockSpec((n1h, n1), lambda b, j: (0, 0))
    return pl.pallas_call(
        functools.partial(_lead_inv_body, inv_n=1.0 / tb["n"], last=last),
        grid=(nb, cols // tn),
        in_specs=[b_spec, b_spec, t_spec, t_spec,
                  pl.BlockSpec((None, n1h, tn), lambda b, j: (z_plane, b, j)),
                  pl.BlockSpec((None, n1h, tn), lambda b, j: (gate_plane, b, j)),
                  pl.BlockSpec((1, tn), lambda b, j: (0, 0)),
                  pl.BlockSpec((1, D_HY), lambda b, j: (0, 0))],
        out_specs=pl.BlockSpec((None, n1h, tn), lambda b, j: (0, b, j)),
        out_shape=jax.ShapeDtypeStruct((1, nb * n1h, cols), F32),
        compiler_params=_cp("arbitrary", "arbitrary"),
        name="hyena_lead_inv",
    )(b_re, b_im, tb["lead_ic"], tb["lead_is"], z, gates, skip_t, gn)


def _hyena(conv_out, nb, seq, tb, kf_re, kf_im, skip, gn):
    n1h = tb["n1h"]
    planes = conv_out.reshape(conv_out.shape[0], nb * n1h, DFT_N2 * D_HY)
    z, z_plane = planes, 0
    for order in range(2):
        a_re, a_im = _lead_fwd(z, tb, z_plane, nb)
        b_re, b_im = _mid_conv(a_re, a_im, kf_re, kf_im, order, tb)
        z = _lead_inv(b_re, b_im, z, z_plane, planes, 1 + order, skip[order], gn, tb, order == 1)
        z_plane = 0
    return z.reshape(nb * seq, D_HY)


def _hyena_spectra(seq, tb, w1, b1, fr, w2, b2, w3):
    pad_r = LANE - HY_EMB
    pad_c = LANE - HY_FILT
    hfilt, sumsq = _hy_filter(
        seq, _hy_features(seq),
        jnp.pad(w1, ((0, pad_r), (0, pad_c))), jnp.pad(b1, (0, pad_c))[None, :],
        jnp.pad(fr, (0, pad_c))[None, :], jnp.pad(w2, ((0, pad_c), (0, pad_c))),
        jnp.pad(b2, (0, pad_c))[None, :], jnp.pad(w3, ((0, pad_c), (0, 0))).astype(BF16), _hy_deltas())
    x = hfilt.reshape(1, tb["n1h"], DFT_N2 * 4 * D_HY)
    a_re, a_im = _lead_fwd(x, tb, 0, 1)
    return _mid_spec(a_re[0], a_im[0], sumsq, tb)


def _s5_scan_body(u_ref, pm_ref, bd_ref, a_ref, cd_ref, x0_ref, o_ref, bu_ref, st_ref, y_ref,
                  *, tstep, final_state):
    d = pl.program_id(0)
    tb = pl.program_id(1)
    ns = S5_STREAMS
    half = S5_STATE

    @pl.when(tb == 0)
    def _():
        st_ref[...] = x0_ref[...]

    u = u_ref[...].reshape(ns * tstep, D_S5).astype(BF16)
    u_tm = jnp.dot(pm_ref[...], u, preferred_element_type=F32).astype(BF16)
    bu_ref[...] = jnp.dot(u_tm, bd_ref[...], preferred_element_type=F32)

    cw = 4 * LANE
    for c in range(half // cw):
        re = pl.ds(c * cw, cw)
        im = pl.ds(half + c * cw, cw)
        ar = a_ref[:, re]
        ai = a_ref[:, im]

        def step(i, carry, re=re, im=im, ar=ar, ai=ai):
            xr, xi = carry
            t = i + d * (tstep - 1 - 2 * i)
            rows = pl.ds(pl.multiple_of(t * ns, ns), ns)
            nxr = ar * xr - ai * xi + bu_ref[rows, re]
            nxi = ar * xi + ai * xr + bu_ref[rows, im]
            bu_ref[rows, re] = nxr
            bu_ref[rows, im] = nxi
            return nxr, nxi

        xr, xi = lax.fori_loop(0, tstep, step, (st_ref[:, re], st_ref[:, im]), unroll=2)
        st_ref[:, re] = xr
        st_ref[:, im] = xi

    if final_state:
        @pl.when(tb == pl.num_programs(1) - 1)
        def _():
            o_ref[...] = st_ref[...]
    else:
        y = jnp.dot(bu_ref[...].astype(BF16), cd_ref[...], preferred_element_type=F32)
        nlb = D_S5 // LANE
        for c in range(nlb):
            y_ref[c] = y[:, c * LANE:(c + 1) * LANE]
        for s in range(ns):
            for c in range(nlb):
                o_ref[s, :, c * LANE:(c + 1) * LANE] = y_ref[c, pl.ds(s, tstep, stride=ns), :]


def _s5_scan(proj, pmat, bd, a_b, cd, x0, tlen, final_state):
    ns = S5_STREAMS
    tstep = pmat.shape[0] // ns
    nt = tlen // tstep
    proj3 = proj.reshape(ns, tlen, P_PAD)

    def window(d, t):
        return t + d * (nt - 1 - 2 * t)

    if final_state:
        out_spec = pl.BlockSpec((None, ns, 2 * S5_STATE), lambda d, t: (d, 0, 0))
        out_shape = jax.ShapeDtypeStruct((2, ns, 2 * S5_STATE), F32)
    else:
        out_spec = pl.BlockSpec((None, ns, tstep, D_S5), lambda d, t: (d, 0, window(d, t), 0))
        out_shape = jax.ShapeDtypeStruct((2, ns, tlen, D_S5), F32)
    return pl.pallas_call(
        functools.partial(_s5_scan_body, tstep=tstep, final_state=final_state),
        grid=(2, nt),
        in_specs=[pl.BlockSpec((ns, tstep, D_S5), lambda d, t: (0, window(d, t), COL_U)),
                  pl.BlockSpec((ns * tstep, ns * tstep), lambda d, t: (0, 0)),
                  pl.BlockSpec((None, D_S5, 2 * S5_STATE), lambda d, t: (d, 0, 0)),
                  pl.BlockSpec((None, ns, 2 * S5_STATE), lambda d, t: (d, 0, 0)),
                  pl.BlockSpec((None, 2 * S5_STATE, D_S5), lambda d, t: (d, 0, 0)),
                  pl.BlockSpec((None, ns, 2 * S5_STATE), lambda d, t: (d, 0, 0))],
        out_specs=out_spec,
        out_shape=out_shape,
        scratch_shapes=[pltpu.VMEM((tstep * ns, 2 * S5_STATE), F32),
                        pltpu.VMEM((ns, 2 * S5_STATE), F32),
                        pltpu.VMEM((D_S5 // LANE, tstep * ns, LANE), F32)],
        compiler_params=_cp("arbitrary", "arbitrary"),
        name="s5_end_state" if final_state else "s5_scan",
    )(proj3, pmat, bd, a_b, cd, x0)


def _s5_params(lam_re, lam_im, log_dt, b_re, b_im, c_re, c_im):
    lr = jnp.minimum(lam_re, -1e-4)
    li = lam_im
    dt = jnp.exp(log_dt)[..., None]
    er = jnp.exp(lr * dt)
    a_re = er * jnp.cos(li * dt)
    a_im = er * jnp.sin(li * dt)
    den = lr * lr + li * li
    q_re = ((a_re - 1.0) * lr + a_im * li) / den
    q_im = (a_im * lr - (a_re - 1.0) * li) / den
    bb_re = q_re[..., None] * b_re - q_im[..., None] * b_im
    bb_im = q_re[..., None] * b_im + q_im[..., None] * b_re
    eye = jnp.eye(S5_G, dtype=F32)

    def in_mat(x):
        return jnp.einsum("dgph,gk->dghkp", x, eye).reshape(2, D_S5, S5_STATE)

    def out_mat(x):
        return jnp.einsum("dghp,gk->dgpkh", x, eye).reshape(2, S5_STATE, D_S5)

    bd = jnp.concatenate([in_mat(bb_re), in_mat(bb_im)], axis=2).astype(BF16)
    cd = jnp.concatenate([out_mat(c_re), out_mat(-c_im)], axis=1).astype(BF16)
    a_bar = jnp.concatenate([a_re.reshape(2, S5_STATE), a_im.reshape(2, S5_STATE)], axis=1)
    return (lr * dt).reshape(2, S5_STATE), (li * dt).reshape(2, S5_STATE), a_bar, bd, cd


def _s5_row_perm(tstep):
    ns = S5_STREAMS
    r = jnp.arange(ns * tstep)
    src = (r % ns) * tstep + r // ns
    return (src[:, None] == jnp.arange(ns * tstep)[None, :]).astype(BF16)


def _s5_mixer(proj, nb, seq, s5p):
    la_re, la_im, a_bar, bd, cd = s5p
    ns = S5_STREAMS
    ncs = ns // nb
    tlen = seq // ncs
    pmat = _s5_row_perm(_tile(tlen, S5_TSTEP))
    a_b = jnp.broadcast_to(a_bar[:, None, :], (2, ns, 2 * S5_STATE))
    zeros = jnp.zeros((2, ns, 2 * S5_STATE), F32)
    x_end = _s5_scan(proj, pmat, bd, a_b, cd, zeros, tlen, True)
    mag = jnp.exp(la_re * tlen)
    at_re = (mag * jnp.cos(la_im * tlen))[:, None, :]
    at_im = (mag * jnp.sin(la_im * tlen))[:, None, :]
    xe_re = x_end[..., :S5_STATE].reshape(2, nb, ncs, S5_STATE)
    xe_im = x_end[..., S5_STATE:].reshape(2, nb, ncs, S5_STATE)
    x0 = []
    for d in range(2):
        order = range(ncs) if d == 0 else range(ncs - 1, -1, -1)
        cr = jnp.zeros((nb, S5_STATE), F32)
        ci = jnp.zeros((nb, S5_STATE), F32)
        rows = [None] * ncs
        for c in order:
            rows[c] = jnp.concatenate([cr, ci], axis=-1)
            cr, ci = (at_re[d] * cr - at_im[d] * ci + xe_re[d, :, c],
                      at_re[d] * ci + at_im[d] * cr + xe_im[d, :, c])
        x0.append(jnp.stack(rows, axis=1).reshape(ns, 2 * S5_STATE))
    y = _s5_scan(proj, pmat, bd, a_b, cd, jnp.stack(x0), tlen, False)
    return y.reshape(2, nb * seq, D_S5)


def _s5_post_body(y_ref, u_ref, d_ref, w_ref, b_ref, gn_ref, o_ref):
    y = y_ref[0] + y_ref[1] + d_ref[...] * u_ref[...]
    g = jax.nn.gelu(y)
    r = jnp.dot(g.astype(BF16), w_ref[...], preferred_element_type=F32) + b_ref[...]
    out = r[:, :D_S5] * jax.nn.sigmoid(r[:, D_S5:])
    ms = jnp.mean(out * out, axis=-1, keepdims=True)
    o_ref[...] = out * lax.rsqrt(ms + EPS) * gn_ref[...]


def _s5_post(y2, proj, d, glu_w, glu_b, gn):
    m = y2.shape[1]
    tm = _tile(m, 512)
    full = lambda shape: pl.BlockSpec(shape, lambda i: (0, 0))
    return pl.pallas_call(
        _s5_post_body,
        grid=(m // tm,),
        in_specs=[pl.BlockSpec((2, tm, D_S5), lambda i: (0, i, 0)),
                  pl.BlockSpec((tm, D_S5), lambda i: (i, COL_U)),
                  full((1, D_S5)), full((D_S5, 2 * D_S5)), full((1, 2 * D_S5)), full((1, D_S5))],
        out_specs=pl.BlockSpec((tm, D_S5), lambda i: (i, 0)),
        out_shape=jax.ShapeDtypeStruct((m, D_S5), F32),
        compiler_params=_cp("arbitrary"),
        name="s5_glu_norm",
    )(y2, proj, d, glu_w, glu_b, gn)


def _mlstm_body(*refs, reverse, epilogue):
    if epilogue:
        (q_ref, k_ref, v_ref, g_ref, gt_ref, bias_ref, biast_ref, hf_ref, o_gate_ref, gn_ref,
         out_ref, c_ref, n_ref, m_ref) = refs
    else:
        (q_ref, k_ref, v_ref, g_ref, gt_ref, bias_ref, biast_ref,
         out_ref, c_ref, n_ref, m_ref) = refs
    head = pl.program_id(1)
    tc = ML_CHUNK

    @pl.when(pl.program_id(2) == 0)
    def _():
        c_ref[...] = jnp.zeros_like(c_ref)
        n_ref[...] = jnp.zeros_like(n_ref)
        m_ref[...] = jnp.zeros_like(m_ref)

    gate_i = 2 if reverse else 0
    idx_i = gate_i * ML_HEADS + head
    idx_f = idx_i + ML_HEADS
    g = g_ref[...] + bias_ref[...]
    lane = lax.broadcasted_iota(jnp.int32, g.shape, 1)
    ig_col = jnp.sum(jnp.where(lane == idx_i, g, 0.0), axis=1, keepdims=True)
    fg_col = jnp.sum(jnp.where(lane == idx_f, g, 0.0), axis=1, keepdims=True)
    gt = gt_ref[...] + biast_ref[:, 0:1]
    sub = lax.broadcasted_iota(jnp.int32, gt.shape, 0)
    ig_row = jnp.sum(jnp.where(sub == idx_i, gt, 0.0), axis=0, keepdims=True)
    fg_row = jnp.sum(jnp.where(sub == idx_f, gt, 0.0), axis=0, keepdims=True)
    lf_col = jax.nn.log_sigmoid(fg_col)
    lf_row = jax.nn.log_sigmoid(fg_row)

    r = lax.broadcasted_iota(jnp.int32, (tc, tc), 0)
    s = lax.broadcasted_iota(jnp.int32, (tc, tc), 1)
    seen = (s >= r) if reverse else (s <= r)
    seen_t = (r >= s) if reverse else (r <= s)
    b_col = jnp.sum(jnp.where(seen, lf_row, 0.0), axis=1, keepdims=True)
    b_row = jnp.sum(jnp.where(seen_t, lf_col, 0.0), axis=0, keepdims=True)
    b_last = jnp.sum(lf_row, axis=1, keepdims=True)

    m_prev = m_ref[...]
    dmat = jnp.where(seen, b_col - b_row + ig_row, -jnp.inf)
    g_car = b_col + m_prev
    mt = jnp.maximum(g_car, jnp.max(dmat, axis=1, keepdims=True))
    inter = jnp.exp(g_car - mt)
    q = q_ref[...]
    k = k_ref[...]
    v = v_ref[...]
    qb = q.astype(BF16)
    kb = k.astype(BF16)
    vb = v.astype(BF16)
    qk = lax.dot_general(qb, kb, (((1,), (1,)), ((), ())), preferred_element_type=F32)
    sc = qk * jnp.exp(dmat - mt)
    c_state = c_ref[...]
    num = (inter * jnp.dot(qb, c_state.astype(BF16), preferred_element_type=F32)
           + jnp.dot(sc.astype(BF16), vb, preferred_element_type=F32))
    n_state = n_ref[...]
    den = inter * jnp.sum(q * n_state, axis=1, keepdims=True) + jnp.sum(sc, axis=1, keepdims=True)
    h = num / jnp.maximum(jnp.abs(den), jnp.exp(-mt))

    a_row = b_last - b_row + ig_row
    a_col = b_last - b_col + ig_col
    m_new = jnp.maximum(b_last + m_prev, jnp.max(a_row, axis=1, keepdims=True))
    decay = jnp.exp(b_last + m_prev - m_new)
    kw = k * jnp.exp(a_col - m_new)
    c_ref[...] = decay * c_state + lax.dot_general(
        kw.astype(BF16), vb, (((0,), (0,)), ((), ())), preferred_element_type=F32)
    n_ref[...] = decay * n_state + jnp.sum(kw, axis=0, keepdims=True)
    m_ref[...] = m_new

    if epilogue:
        h = h + hf_ref[...]
        ms = jnp.mean(h * h, axis=-1, keepdims=True)
        hn = h * lax.rsqrt(ms + EPS) * gn_ref[...]
        out_ref[...] = hn * jax.nn.sigmoid(o_gate_ref[...])
    else:
        out_ref[...] = h


def _mlstm_dir(conv_out, proj, gates_t, bias, bias_t, nb, seq, reverse, h_fwd=None, gn=None):
    tc = ML_CHUNK
    nch = seq // tc

    def chunk(b, ci):
        return b * nch + (nch - 1 - ci if reverse else ci)

    in_specs = [
        pl.BlockSpec((None, tc, ML_DH), lambda b, h, ci: (3 + h // 2, chunk(b, ci), h % 2)),
        pl.BlockSpec((None, tc, ML_DH), lambda b, h, ci: (5 + h // 2, chunk(b, ci), h % 2)),
        pl.BlockSpec((tc, ML_DH), lambda b, h, ci: (chunk(b, ci), COL_V + h)),
        pl.BlockSpec((tc, LANE), lambda b, h, ci: (chunk(b, ci), COL_G)),
        pl.BlockSpec((4 * ML_HEADS, tc), lambda b, h, ci: (0, chunk(b, ci))),
        pl.BlockSpec((1, LANE), lambda b, h, ci: (0, 0)),
        pl.BlockSpec((4 * ML_HEADS, LANE), lambda b, h, ci: (0, 0)),
    ]
    args = [conv_out, conv_out, proj, proj, gates_t, bias, bias_t]
    epilogue = h_fwd is not None
    if epilogue:
        in_specs += [
            pl.BlockSpec((tc, ML_DH), lambda b, h, ci: (chunk(b, ci), h)),
            pl.BlockSpec((tc, ML_DH), lambda b, h, ci: (chunk(b, ci), COL_O + h)),
            pl.BlockSpec((1, ML_DH), lambda b, h, ci: (0, h)),
        ]
        args += [h_fwd, proj, gn]
    return pl.pallas_call(
        functools.partial(_mlstm_body, reverse=reverse, epilogue=epilogue),
        grid=(nb, ML_HEADS, nch),
        in_specs=in_specs,
        out_specs=pl.BlockSpec((tc, ML_DH), lambda b, h, ci: (chunk(b, ci), h)),
        out_shape=jax.ShapeDtypeStruct((nb * seq, D_ML), F32),
        scratch_shapes=[pltpu.VMEM((ML_DH, ML_DH), F32), pltpu.VMEM((1, ML_DH), F32),
                        pltpu.VMEM((1, 1), F32)],
        compiler_params=_cp("arbitrary", "arbitrary", "arbitrary"),
        name="mlstm_bwd_norm" if reverse else "mlstm_fwd",
    )(*args)


def _layer(x, nb, seq, tb, p):
    proj = _norm_matmul(x, p["ln_g"][0:1], p["w_in"], relu2=False, out_dtype=F32, tn=P_PAD // 7)
    gates_t = proj[:, COL_G * LANE:COL_G * LANE + 4 * ML_HEADS].T
    gn = p["group_norm"]
    gn_hy = gn[None, :D_HY]
    gn_s5 = gn[None, D_HY:D_HY + D_S5]
    gn_ml = gn[None, D_HY + D_S5:]

    conv_out = _short_conv(proj, p["conv_w"], p["conv_b"][None, :], nb, seq)
    kf_re, kf_im = _hyena_spectra(seq, tb, p["hy_w1"], p["hy_b1"], p["hy_freq"], p["hy_w2"],
                                  p["hy_b2"], p["hy_w3"])
    y_hy = _hyena(conv_out, nb, seq, tb, kf_re, kf_im, p["hy_skip"], gn_hy)
    y_s5 = _s5_post(_s5_mixer(proj, nb, seq, p["s5"]), proj, p["s5_d"][None, :],
                    p["s5_glu_w"], p["s5_glu_b"][None, :], gn_s5)
    h_f = _mlstm_dir(conv_out, proj, gates_t, p["ml_bias"], p["ml_bias_t"], nb, seq, False)
    y_ml = _mlstm_dir(conv_out, proj, gates_t, p["ml_bias"], p["ml_bias_t"], nb, seq, True, h_f, gn_ml)

    w_out = p["w_out"]
    x = _matmul_resnorm([(y_hy, w_out[:D_HY]), (y_s5, w_out[D_HY:D_HY + D_S5]),
                         (y_ml, w_out[D_HY + D_S5:])], x, p["ln_g"][1:2], nk=1, tm=256)
    hid = _norm_matmul(x, p["ln_g"][2:3], p["mlp_w1"], relu2=True, out_dtype=BF16, tn=1024)
    return _matmul_resnorm([(hid, p["mlp_w2"])], x, p["ln_g"][3:4], nk=D_FF // 1024, tm=512)


_PARAM_NAMES = ("ln_g", "w_in", "conv_w", "conv_b", "hy_w1", "hy_b1", "hy_freq", "hy_w2", "hy_b2",
                "hy_w3", "hy_skip", "s5_lam_re", "s5_lam_im", "s5_log_dt", "s5_b_re", "s5_b_im",
                "s5_c_re", "s5_c_im", "s5_d", "s5_glu_w", "s5_glu_b", "ml_gate_b", "group_norm",
                "w_out", "mlp_w1", "mlp_w2")


def _prepare(p):
    p = dict(p)
    p["w_in"] = jnp.pad(p["w_in"], ((0, 0), (0, P_PAD - P_IN))).astype(BF16)
    p["w_out"] = p["w_out"].astype(BF16)
    p["mlp_w1"] = p["mlp_w1"].astype(BF16)
    p["mlp_w2"] = p["mlp_w2"].astype(BF16)
    p["s5_glu_w"] = p["s5_glu_w"].astype(BF16)
    p["ml_bias"] = jnp.pad(p["ml_gate_b"].reshape(1, -1), ((0, 0), (0, LANE - 4 * ML_HEADS)))
    p["ml_bias_t"] = jnp.broadcast_to(p["ml_gate_b"].reshape(-1, 1), (4 * ML_HEADS, LANE))
    p["s5"] = _s5_params(p["s5_lam_re"], p["s5_lam_im"], p["s5_log_dt"], p["s5_b_re"],
                         p["s5_b_im"], p["s5_c_re"], p["s5_c_im"])
    return p


def kernel(x_prompt, x_sample, ln_g, w_in, conv_w, conv_b, hy_w1, hy_b1, hy_freq, hy_w2, hy_b2, hy_w3,
           hy_skip, s5_lam_re, s5_lam_im, s5_log_dt, s5_b_re, s5_b_im, s5_c_re, s5_c_im, s5_d,
           s5_glu_w, s5_glu_b, ml_gate_b, group_norm, w_out, mlp_w1, mlp_w2):
    params = (ln_g, w_in, conv_w, conv_b, hy_w1, hy_b1, hy_freq, hy_w2, hy_b2, hy_w3, hy_skip,
              s5_lam_re, s5_lam_im, s5_log_dt, s5_b_re, s5_b_im, s5_c_re, s5_c_im, s5_d,
              s5_glu_w, s5_glu_b, ml_gate_b, group_norm, w_out, mlp_w1, mlp_w2)
    streams = []
    for x in (x_prompt, x_sample):
        nb, seq, _ = x.shape
        streams.append([x.reshape(nb * seq, D_MODEL), nb, seq, _dft_tables(seq)])
    for layer in range(DEPTH):
        p = _prepare({name: arr[layer] for name, arr in zip(_PARAM_NAMES, params)})
        for st in streams:
            st[0] = _layer(st[0], st[1], st[2], st[3], p)
    return tuple(st[0].reshape(x.shape) for st, x in zip(streams, (x_prompt, x_sample)))
```

```python
import functools
import math

import jax
import jax.numpy as jnp
from jax import lax
from jax.experimental import pallas as pl
from jax.experimental.pallas import tpu as pltpu

F32 = jnp.float32
BF16 = jnp.bfloat16

D_MODEL = 2048
DEPTH = 4
D_HY = 512
D_S5 = 512
D_ML = 1024
HY_EMB = 33
HY_BANDS = 16
HY_FILT = 64
HY_FAST_DECAY = 0.3
HY_SLOW_DECAY = 1.5
HY_TARGET = 1e-2
S5_GROUP = 16
S5_G = 32
S5_P = 64
S5_STATE = S5_G * S5_P
ML_HEADS = 4
ML_DH = 256
D_FF = 4 * D_MODEL
N_CONV = 3 * D_HY + 2 * D_ML
P_IN = N_CONV + D_S5 + 2 * D_ML + 4 * ML_HEADS
P_MIX = P_IN - 4 * ML_HEADS
EPS = 1e-6

LANE = 128
SUBLANE = 8
VMEM_LIMIT_BYTES = 48 * 1024 * 1024

DFT_N2 = LANE
S5_STREAMS = SUBLANE
S5_TSTEP = 64
S5_BLOCK_GROUPS = LANE // S5_GROUP
S5_NBLOCKS = S5_G // S5_BLOCK_GROUPS
ML_CHUNK = 256
ML_HEADS_PER_STEP = 2
COL_U = N_CONV // D_S5


def _cp(*sem):
    return pltpu.CompilerParams(dimension_semantics=sem, vmem_limit_bytes=VMEM_LIMIT_BYTES)


def _tile(n, pref):
    t = min(n, pref)
    while n % t:
        t //= 2
    return t


def _rms(x, g):
    ms = jnp.mean(x * x, axis=-1, keepdims=True)
    return x * lax.rsqrt(ms + EPS) * g


def _proj_in_body(x_ref, g_ref, w_ref, wg_ref, o_ref, og_ref, xn_ref):
    @pl.when(pl.program_id(1) == 0)
    def _():
        xn = _rms(x_ref[...], g_ref[...]).astype(BF16)
        xn_ref[...] = xn
        og_ref[...] = jnp.dot(xn, wg_ref[...], preferred_element_type=F32)

    o_ref[...] = jnp.dot(xn_ref[...], w_ref[...], preferred_element_type=F32)


def _proj_in(x, g, w, w_gate):
    m, k = x.shape
    n = w.shape[1]
    tm = _tile(m, 1024)
    tn = 1024
    return pl.pallas_call(
        _proj_in_body,
        grid=(m // tm, n // tn),
        in_specs=[pl.BlockSpec((tm, k), lambda i, j: (i, 0)),
                  pl.BlockSpec((1, k), lambda i, j: (0, 0)),
                  pl.BlockSpec((k, tn), lambda i, j: (0, j)),
                  pl.BlockSpec((k, LANE), lambda i, j: (0, 0))],
        out_specs=[pl.BlockSpec((tm, tn), lambda i, j: (i, j)),
                   pl.BlockSpec((tm, LANE), lambda i, j: (i, 0))],
        out_shape=[jax.ShapeDtypeStruct((m, n), F32), jax.ShapeDtypeStruct((m, LANE), F32)],
        scratch_shapes=[pltpu.VMEM((tm, k), BF16)],
        compiler_params=_cp("arbitrary", "arbitrary"),
        name="norm_proj_in",
    )(x, g, w, w_gate)


def _proj_out_body(a0_ref, a1_ref, a2_ref, w0_ref, w1_ref, w2_ref, r_ref, g_ref, o_ref):
    f = (jnp.dot(a0_ref[...].astype(BF16), w0_ref[...], preferred_element_type=F32)
         + jnp.dot(a1_ref[...].astype(BF16), w1_ref[...], preferred_element_type=F32)
         + jnp.dot(a2_ref[...].astype(BF16), w2_ref[...], preferred_element_type=F32))
    o_ref[...] = r_ref[...] + _rms(f, g_ref[...])


def _proj_out(parts, weights, resid, g):
    m, n = resid.shape
    tm = _tile(m, 256)
    row = lambda width: pl.BlockSpec((tm, width), lambda i: (i, 0))
    full = lambda arr: pl.BlockSpec(arr.shape, lambda i: (0, 0))
    return pl.pallas_call(
        _proj_out_body,
        grid=(m // tm,),
        in_specs=[row(a.shape[1]) for a in parts] + [full(w) for w in weights] + [row(n), full(g)],
        out_specs=row(n),
        out_shape=jax.ShapeDtypeStruct((m, n), F32),
        compiler_params=_cp("arbitrary"),
        name="proj_out_resnorm",
    )(*parts, *weights, resid, g)


def _mlp_body(x_ref, g_in_ref, w1_ref, w2_ref, g_out_ref, o_ref, xn_ref, acc_ref):
    j = pl.program_id(1)

    @pl.when(j == 0)
    def _():
        xn_ref[...] = _rms(x_ref[...], g_in_ref[...]).astype(BF16)

    h = jnp.dot(xn_ref[...], w1_ref[...], preferred_element_type=F32)
    h = jnp.square(jnp.maximum(h, 0.0)).astype(BF16)
    part = jnp.dot(h, w2_ref[...], preferred_element_type=F32)

    @pl.when(j == 0)
    def _():
        acc_ref[...] = part

    @pl.when(j > 0)
    def _():
        acc_ref[...] += part

    @pl.when(j == pl.num_programs(1) - 1)
    def _():
        o_ref[...] = x_ref[...] + _rms(acc_ref[...], g_out_ref[...])


def _mlp(x, g_in, w1, w2, g_out):
    m, k = x.shape
    ff = w1.shape[1]
    tm = _tile(m, 512)
    tf = 1024
    return pl.pallas_call(
        _mlp_body,
        grid=(m // tm, ff // tf),
        in_specs=[pl.BlockSpec((tm, k), lambda i, j: (i, 0)),
                  pl.BlockSpec((1, k), lambda i, j: (0, 0)),
                  pl.BlockSpec((k, tf), lambda i, j: (0, j)),
                  pl.BlockSpec((tf, k), lambda i, j: (j, 0)),
                  pl.BlockSpec((1, k), lambda i, j: (0, 0))],
        out_specs=pl.BlockSpec((tm, k), lambda i, j: (i, 0)),
        out_shape=jax.ShapeDtypeStruct((m, k), F32),
        scratch_shapes=[pltpu.VMEM((tm, k), BF16), pltpu.VMEM((tm, k), F32)],
        compiler_params=_cp("arbitrary", "arbitrary"),
        name="mlp_resnorm",
    )(x, g_in, w1, w2, g_out)


def _conv_body(x_ref, p_ref, n_ref, w_ref, b_ref, o_ref, *, nblk, qk):
    i = pl.program_id(1)
    j = pl.program_id(2)
    x = x_ref[...]
    tm = x.shape[0]
    row = lax.broadcasted_iota(jnp.int32, x.shape, 0)
    prev_row = jnp.where(i == 0, 0.0, p_ref[SUBLANE - 1:SUBLANE, :])
    next_row = jnp.where(i == nblk - 1, 0.0, n_ref[0:1, :])
    xm = jnp.where(row == 0, prev_row, pltpu.roll(x, 1, 0))
    xp = jnp.where(row == tm - 1, next_row, pltpu.roll(x, tm - 1, 0))
    w = w_ref[...]
    y = b_ref[...] + xm * w[0:1] + x * w[1:2] + xp * w[2:3]
    if qk:
        y = y * jax.nn.sigmoid(y) * jnp.where(j >= 2, ML_DH ** -0.5, 1.0)
    o_ref[...] = y.astype(o_ref.dtype)


def _short_conv(proj, conv_w, conv_b, nb, seq, qk):
    mtot = proj.shape[0]
    tm = _tile(seq, 512)
    nblk = seq // tm
    col0, ncol = (3, 4) if qk else (0, 3)
    last8 = mtot // SUBLANE - 1
    tm8 = tm // SUBLANE

    def x_map(b, i, j):
        return (b * nblk + i, col0 + j)

    def p_map(b, i, j):
        return (jnp.maximum((b * nblk + i) * tm8 - 1, 0), col0 + j)

    def n_map(b, i, j):
        return (jnp.minimum((b * nblk + i + 1) * tm8, last8), col0 + j)

    return pl.pallas_call(
        functools.partial(_conv_body, nblk=nblk, qk=qk),
        grid=(nb, nblk, ncol),
        in_specs=[pl.BlockSpec((tm, D_HY), x_map),
                  pl.BlockSpec((SUBLANE, D_HY), p_map),
                  pl.BlockSpec((SUBLANE, D_HY), n_map),
                  pl.BlockSpec((3, D_HY), lambda b, i, j: (0, col0 + j)),
                  pl.BlockSpec((1, D_HY), lambda b, i, j: (0, col0 + j))],
        out_specs=pl.BlockSpec((None, tm, D_HY), lambda b, i, j: (j, b * nblk + i, 0)),
        out_shape=jax.ShapeDtypeStruct((ncol, nb * seq, D_HY), BF16 if qk else F32),
        compiler_params=_cp("arbitrary", "arbitrary", "arbitrary"),
        name="short_conv_qk" if qk else "short_conv_hy",
    )(proj, proj, proj, conv_w, conv_b)


def _dft_tables(seq):
    n = 2 * seq
    n1 = n // DFT_N2
    n1h = n1 // 2
    k1 = jnp.arange(n1, dtype=jnp.int32)
    t1 = jnp.arange(n1h, dtype=jnp.int32)
    ang = ((k1[:, None] * t1[None, :]) % n1).astype(F32) * (2.0 * math.pi / n1)
    lead_f = jnp.concatenate([jnp.cos(ang), -jnp.sin(ang)], axis=0).astype(BF16)
    lead_ic = jnp.cos(ang).T.astype(BF16)
    lead_is = (-jnp.sin(ang)).T.astype(BF16)
    t2 = jnp.arange(DFT_N2, dtype=jnp.int32)
    k2 = jnp.arange(DFT_N2, dtype=jnp.int32)
    idx = (t2[None, None, :] * (k2[None, :, None] * n1 + k1[:, None, None])) % n
    ang2 = idx.astype(F32) * (2.0 * math.pi / n)
    f_re = jnp.cos(ang2)
    f_im = -jnp.sin(ang2)
    mid_f = jnp.concatenate([jnp.concatenate([f_re, -f_im], axis=2),
                             jnp.concatenate([f_im, f_re], axis=2)], axis=1).astype(BF16)
    g_re = jnp.swapaxes(f_re, 1, 2)
    g_im = jnp.swapaxes(f_im, 1, 2)
    mid_g = jnp.concatenate([jnp.concatenate([g_re, g_im], axis=2),
                             jnp.concatenate([-g_im, g_re], axis=2)], axis=1).astype(BF16)
    return dict(n=n, n1=n1, n1h=n1h, lead_f=lead_f, lead_ic=lead_ic, lead_is=lead_is,
                mid_f=mid_f, mid_g=mid_g)


def _hy_features(seq):
    t = jnp.linspace(0.0, 1.0, seq, dtype=F32)[:, None]
    w = 2.0 * math.pi * jnp.arange(seq, dtype=F32)[:, None] / seq
    f = jnp.linspace(1e-4, HY_BANDS - 1, HY_BANDS, dtype=F32)[None, :]
    z = jnp.concatenate([t, jnp.cos(f * w), -jnp.sin(f * w)], axis=-1)
    return jnp.pad(z, ((0, 0), (0, LANE - HY_EMB)))


def _hy_deltas():
    d = jnp.abs(jnp.linspace(math.log(HY_TARGET) / HY_SLOW_DECAY,
                             math.log(HY_TARGET) / HY_FAST_DECAY, D_HY, dtype=F32))
    return jnp.tile(d, 4)[None, :]


def _hy_filter_body(z_ref, w1_ref, b1_ref, fr_ref, w2_ref, b2_ref, w3_ref, dl_ref, h_ref, ss_ref):
    i = pl.program_id(0)
    hi = lax.Precision.HIGHEST
    z = z_ref[...]
    fr = fr_ref[...]
    h = jnp.sin(fr * (jnp.dot(z, w1_ref[...], precision=hi, preferred_element_type=F32) + b1_ref[...]))
    h = jnp.sin(fr * (jnp.dot(h, w2_ref[...], precision=hi, preferred_element_type=F32) + b2_ref[...]))
    h = jnp.dot(h.astype(BF16), w3_ref[...], preferred_element_type=F32)
    h = h * jnp.exp(-z[:, 0:1] * dl_ref[...])
    row = lax.broadcasted_iota(jnp.int32, h.shape, 0)
    col = lax.broadcasted_iota(jnp.int32, h.shape, 1)
    bwd = (col // D_HY) % 2 == 1
    h = jnp.where((row + i * h.shape[0] == 0) & bwd, 0.0, h)
    h_ref[...] = h
    part = jnp.broadcast_to(jnp.sum(h * h, axis=0, keepdims=True), ss_ref.shape)

    @pl.when(i == 0)
    def _():
        ss_ref[...] = part

    @pl.when(i > 0)
    def _():
        ss_ref[...] += part


def _hy_filter(seq, z, w1, b1, fr, w2, b2, w3, deltas):
    tm = _tile(seq, 512)
    nc = 4 * D_HY
    full = lambda shape: pl.BlockSpec(shape, lambda i: (0, 0))
    return pl.pallas_call(
        _hy_filter_body,
        grid=(seq // tm,),
        in_specs=[pl.BlockSpec((tm, LANE), lambda i: (i, 0)),
                  full((LANE, LANE)), full((1, LANE)), full((1, LANE)),
                  full((LANE, LANE)), full((1, LANE)), full((LANE, nc)), full((1, nc))],
        out_specs=[pl.BlockSpec((tm, nc), lambda i: (i, 0)), full((SUBLANE, nc))],
        out_shape=[jax.ShapeDtypeStruct((seq, nc), F32), jax.ShapeDtypeStruct((SUBLANE, nc), F32)],
        compiler_params=_cp("arbitrary"),
        name="hyena_filter",
    )(z, w1, b1, fr, w2, b2, w3, deltas)


def _lead_fwd_body(x_ref, f_ref, re_ref, im_ref):
    n1 = re_ref.shape[0]
    r = jnp.dot(f_ref[...], x_ref[...].astype(BF16), preferred_element_type=F32)
    re_ref[...] = r[:n1].astype(BF16)
    im_ref[...] = r[n1:].astype(BF16)


def _lead_fwd(x, tb, plane, nb):
    n1, n1h = tb["n1"], tb["n1h"]
    cols = x.shape[-1]
    tn = _tile(cols, 4096)
    return pl.pallas_call(
        _lead_fwd_body,
        grid=(nb, cols // tn),
        in_specs=[pl.BlockSpec((None, n1h, tn), lambda b, j: (plane, b, j)),
                  pl.BlockSpec((2 * n1, n1h), lambda b, j: (0, 0))],
        out_specs=[pl.BlockSpec((None, n1, tn), lambda b, j: (b, 0, j))] * 2,
        out_shape=[jax.ShapeDtypeStruct((nb, n1, cols), BF16)] * 2,
        compiler_params=_cp("arbitrary", "arbitrary"),
        name="hyena_lead_fwd",
    )(x, tb["lead_f"])


def _mid_spec_body(are_ref, aim_ref, f_ref, ss_ref, kr_ref, ki_ref, *, bk):
    n2 = DFT_N2
    ss = ss_ref[0:1, :]
    scale = lax.rsqrt(ss[:, :D_HY] + ss[:, D_HY:] + EPS)

    def body(i, c):
        a = jnp.concatenate([are_ref[i], aim_ref[i]], axis=0)
        x = jnp.dot(f_ref[i], a, preferred_element_type=F32)
        xr = x[:n2]
        xi = x[n2:]
        kr_ref[i] = (xr[:, :D_HY] + xr[:, D_HY:]) * scale
        ki_ref[i] = (xi[:, :D_HY] - xi[:, D_HY:]) * scale
        return c

    lax.fori_loop(0, bk, body, 0)


def _mid_spec(a_re, a_im, sumsq, tb):
    n1 = tb["n1"]
    bk = _tile(n1, 8)
    a_re = a_re.reshape(n1, DFT_N2, 4 * D_HY)
    a_im = a_im.reshape(n1, DFT_N2, 4 * D_HY)
    a_spec = pl.BlockSpec((bk, DFT_N2, 2 * D_HY), lambda o, i: (i, 0, o))
    k_spec = pl.BlockSpec((None, bk, DFT_N2, D_HY), lambda o, i: (o, i, 0, 0))
    return pl.pallas_call(
        functools.partial(_mid_spec_body, bk=bk),
        grid=(2, n1 // bk),
        in_specs=[a_spec, a_spec,
                  pl.BlockSpec((bk, 2 * DFT_N2, 2 * DFT_N2), lambda o, i: (i, 0, 0)),
                  pl.BlockSpec((SUBLANE, 2 * D_HY), lambda o, i: (0, o))],
        out_specs=[k_spec, k_spec],
        out_shape=[jax.ShapeDtypeStruct((2, n1, DFT_N2, D_HY), F32)] * 2,
        compiler_params=_cp("arbitrary", "arbitrary"),
        name="hyena_mid_spectrum",
    )(a_re, a_im, tb["mid_f"], sumsq)


def _mid_conv_body(are_ref, aim_ref, f_ref, g_ref, kr_ref, ki_ref, bre_ref, bim_ref, *, bk):
    n2 = DFT_N2

    def body(i, c):
        a = jnp.concatenate([are_ref[i], aim_ref[i]], axis=0)
        x = jnp.dot(f_ref[i], a, preferred_element_type=F32)
        xr = x[:n2]
        xi = x[n2:]
        kr = kr_ref[i]
        ki = ki_ref[i]
        p = jnp.concatenate([xr * kr - xi * ki, xr * ki + xi * kr], axis=0).astype(BF16)
        q = jnp.dot(g_ref[i], p, preferred_element_type=F32)
        bre_ref[i] = q[:n2].astype(BF16)
        bim_ref[i] = q[n2:].astype(BF16)
        return c

    lax.fori_loop(0, bk, body, 0)


def _mid_conv(a_re, a_im, kf_re, kf_im, order, tb):
    n1 = tb["n1"]
    nb = a_re.shape[0]
    bk = _tile(n1, 8)
    a_re = a_re.reshape(nb, n1, DFT_N2, D_HY)
    a_im = a_im.reshape(nb, n1, DFT_N2, D_HY)
    a_spec = pl.BlockSpec((None, bk, DFT_N2, D_HY), lambda b, i: (b, i, 0, 0))
    t_spec = pl.BlockSpec((bk, 2 * DFT_N2, 2 * DFT_N2), lambda b, i: (i, 0, 0))
    k_spec = pl.BlockSpec((None, bk, DFT_N2, D_HY), lambda b, i: (order, i, 0, 0))
    b_re, b_im = pl.pallas_call(
        functools.partial(_mid_conv_body, bk=bk),
        grid=(nb, n1 // bk),
        in_specs=[a_spec, a_spec, t_spec, t_spec, k_spec, k_spec],
        out_specs=[a_spec, a_spec],
        out_shape=[jax.ShapeDtypeStruct((nb, n1, DFT_N2, D_HY), BF16)] * 2,
        compiler_params=_cp("arbitrary", "arbitrary"),
        name="hyena_mid_conv",
    )(a_re, a_im, tb["mid_f"], tb["mid_g"], kf_re, kf_im)
    return b_re.reshape(nb, n1, DFT_N2 * D_HY), b_im.reshape(nb, n1, DFT_N2 * D_HY)


def _lead_inv_body(bre_ref, bim_ref, c_ref, s_ref, z_ref, gate_ref, skip_ref, gn_ref, o_ref, *, inv_n, last):
    y = (jnp.dot(c_ref[...], bre_ref[...], preferred_element_type=F32)
         + jnp.dot(s_ref[...], bim_ref[...], preferred_element_type=F32)) * inv_n
    out = gate_ref[...] * (y + skip_ref[...] * z_ref[...])
    if not last:
        o_ref[...] = out
    else:
        gn = gn_ref[...]
        for c in range(out.shape[1] // D_HY):
            blk = out[:, c * D_HY:(c + 1) * D_HY]
            ms = jnp.mean(blk * blk, axis=-1, keepdims=True)
            o_ref[:, c * D_HY:(c + 1) * D_HY] = blk * lax.rsqrt(ms + EPS) * gn


def _lead_inv(b_re, b_im, z, z_plane, gates, gate_plane, skip, gn, tb, last):
    n1, n1h = tb["n1"], tb["n1h"]
    nb, _, cols = b_re.shape
    tn = _tile(cols, 4096)
    skip_t = jnp.tile(skip[None, :], (1, tn // D_HY))
    b_spec = pl.BlockSpec((None, n1, tn), lambda b, j: (b, 0, j))
    t_spec = pl.BlockSpec((n1h, n1), lambda b, j: (0, 0))
    return pl.pallas_call(
        functools.partial(_lead_inv_body, inv_n=1.0 / tb["n"], last=last),
        grid=(nb, cols // tn),
        in_specs=[b_spec, b_spec, t_spec, t_spec,
                  pl.BlockSpec((None, n1h, tn), lambda b, j: (z_plane, b, j)),
                  pl.BlockSpec((None, n1h, tn), lambda b, j: (gate_plane, b, j)),
                  pl.BlockSpec((1, tn), lambda b, j: (0, 0)),
                  pl.BlockSpec((1, D_HY), lambda b, j: (0, 0))],
        out_specs=pl.BlockSpec((None, n1h, tn), lambda b, j: (0, b, j)),
        out_shape=jax.ShapeDtypeStruct((1, nb * n1h, cols), F32),
        compiler_params=_cp("arbitrary", "arbitrary"),
        name="hyena_lead_inv",
    )(b_re, b_im, tb["lead_ic"], tb["lead_is"], z, gates, skip_t, gn)


def _hyena(conv_out, nb, seq, tb, kf_re, kf_im, skip, gn):
    n1h = tb["n1h"]
    planes = conv_out.reshape(conv_out.shape[0], nb * n1h, DFT_N2 * D_HY)
    z, z_plane = planes, 0
    for order in range(2):
        a_re, a_im = _lead_fwd(z, tb, z_plane, nb)
        b_re, b_im = _mid_conv(a_re, a_im, kf_re, kf_im, order, tb)
        z = _lead_inv(b_re, b_im, z, z_plane, planes, 1 + order, skip[order], gn, tb, order == 1)
        z_plane = 0
    return z.reshape(nb * seq, D_HY)


def _hyena_spectra(seq, tb, w1, b1, fr, w2, b2, w3):
    pad_r = LANE - HY_EMB
    pad_c = LANE - HY_FILT
    hfilt, sumsq = _hy_filter(
        seq, _hy_features(seq),
        jnp.pad(w1, ((0, pad_r), (0, pad_c))), jnp.pad(b1, (0, pad_c))[None, :],
        jnp.pad(fr, (0, pad_c))[None, :], jnp.pad(w2, ((0, pad_c), (0, pad_c))),
        jnp.pad(b2, (0, pad_c))[None, :], jnp.pad(w3, ((0, pad_c), (0, 0))).astype(BF16), _hy_deltas())
    x = hfilt.reshape(1, tb["n1h"], DFT_N2 * 4 * D_HY)
    a_re, a_im = _lead_fwd(x, tb, 0, 1)
    return _mid_spec(a_re[0], a_im[0], sumsq, tb)


def _s5_scan_body(u_ref, pm_ref, bd_ref, a_ref, cd_ref, x0_ref, o_ref, bu_ref, st_ref, y_ref,
                  *, tstep, final_state):
    d = pl.program_id(0)
    tb = pl.program_id(1)
    ns = S5_STREAMS
    half = S5_STATE

    @pl.when(tb == 0)
    def _():
        st_ref[...] = x0_ref[...]

    u = u_ref[...].reshape(ns * tstep, D_S5).astype(BF16)
    u_tm = jnp.dot(pm_ref[...], u, preferred_element_type=F32).astype(BF16)

    cw = S5_BLOCK_GROUPS * S5_P
    for blk in range(S5_NBLOCKS):
        bre = pl.ds(2 * blk * cw, cw)
        bim = pl.ds((2 * blk + 1) * cw, cw)
        sre = pl.ds(blk * cw, cw)
        sim = pl.ds(half + blk * cw, cw)
        bu_ref[:, pl.ds(2 * blk * cw, 2 * cw)] = jnp.dot(
            u_tm[:, blk * LANE:(blk + 1) * LANE], bd_ref[blk], preferred_element_type=F32)
        ar = a_ref[:, sre]
        ai = a_ref[:, sim]

        def step(i, carry, bre=bre, bim=bim, ar=ar, ai=ai):
            xr, xi = carry
            t = i + d * (tstep - 1 - 2 * i)
            rows = pl.ds(pl.multiple_of(t * ns, ns), ns)
            nxr = ar * xr - ai * xi + bu_ref[rows, bre]
            nxi = ar * xi + ai * xr + bu_ref[rows, bim]
            bu_ref[rows, bre] = nxr
            bu_ref[rows, bim] = nxi
            return nxr, nxi

        xr, xi = lax.fori_loop(0, tstep, step, (st_ref[:, sre], st_ref[:, sim]), unroll=2)
        st_ref[:, sre] = xr
        st_ref[:, sim] = xi
        if not final_state:
            y_ref[blk] = jnp.dot(bu_ref[:, pl.ds(2 * blk * cw, 2 * cw)].astype(BF16), cd_ref[blk],
                                 preferred_element_type=F32)

    if final_state:
        @pl.when(tb == pl.num_programs(1) - 1)
        def _():
            o_ref[...] = st_ref[...]
    else:
        for s in range(ns):
            for blk in range(S5_NBLOCKS):
                o_ref[s, :, blk * LANE:(blk + 1) * LANE] = y_ref[blk, pl.ds(s, tstep, stride=ns), :]


def _s5_scan(proj, pmat, bd, a_b, cd, x0, tlen, final_state):
    ns = S5_STREAMS
    tstep = pmat.shape[0] // ns
    nt = tlen // tstep
    proj3 = proj.reshape(ns, tlen, proj.shape[1])
    blk_in = S5_BLOCK_GROUPS * S5_GROUP
    blk_state = 2 * S5_BLOCK_GROUPS * S5_P

    def window(d, t):
        return t + d * (nt - 1 - 2 * t)

    if final_state:
        out_spec = pl.BlockSpec((None, ns, 2 * S5_STATE), lambda d, t: (d, 0, 0))
        out_shape = jax.ShapeDtypeStruct((2, ns, 2 * S5_STATE), F32)
    else:
        out_spec = pl.BlockSpec((None, ns, tstep, D_S5), lambda d, t: (d, 0, window(d, t), 0))
        out_shape = jax.ShapeDtypeStruct((2, ns, tlen, D_S5), F32)
    return pl.pallas_call(
        functools.partial(_s5_scan_body, tstep=tstep, final_state=final_state),
        grid=(2, nt),
        in_specs=[pl.BlockSpec((ns, tstep, D_S5), lambda d, t: (0, window(d, t), COL_U)),
                  pl.BlockSpec((ns * tstep, ns * tstep), lambda d, t: (0, 0)),
                  pl.BlockSpec((None, S5_NBLOCKS, blk_in, blk_state), lambda d, t: (d, 0, 0, 0)),
                  pl.BlockSpec((None, ns, 2 * S5_STATE), lambda d, t: (d, 0, 0)),
                  pl.BlockSpec((None, S5_NBLOCKS, blk_state, blk_in), lambda d, t: (d, 0, 0, 0)),
                  pl.BlockSpec((None, ns, 2 * S5_STATE), lambda d, t: (d, 0, 0))],
        out_specs=out_spec,
        out_shape=out_shape,
        scratch_shapes=[pltpu.VMEM((tstep * ns, 2 * S5_STATE), F32),
                        pltpu.VMEM((ns, 2 * S5_STATE), F32),
                        pltpu.VMEM((D_S5 // LANE, tstep * ns, LANE), F32)],
        compiler_params=_cp("arbitrary", "arbitrary"),
        name="s5_end_state" if final_state else "s5_scan",
    )(proj3, pmat, bd, a_b, cd, x0)


def _s5_params(lam_re, lam_im, log_dt, b_re, b_im, c_re, c_im):
    lr = jnp.minimum(lam_re, -1e-4)
    li = lam_im
    dt = jnp.exp(log_dt)[..., None]
    er = jnp.exp(lr * dt)
    a_re = er * jnp.cos(li * dt)
    a_im = er * jnp.sin(li * dt)
    den = lr * lr + li * li
    q_re = ((a_re - 1.0) * lr + a_im * li) / den
    q_im = (a_im * lr - (a_re - 1.0) * li) / den
    bb_re = q_re[..., None] * b_re - q_im[..., None] * b_im
    bb_im = q_re[..., None] * b_im + q_im[..., None] * b_re
    nbk, gb = S5_NBLOCKS, S5_BLOCK_GROUPS
    eye = jnp.eye(gb, dtype=F32)

    def in_mat(x):
        x = x.reshape(2, nbk, gb, S5_P, S5_GROUP)
        return jnp.einsum("dbgph,gk->dbghkp", x, eye).reshape(2, nbk, gb * S5_GROUP, gb * S5_P)

    def out_mat(x):
        x = x.reshape(2, nbk, gb, S5_GROUP, S5_P)
        return jnp.einsum("dbghp,gk->dbgpkh", x, eye).reshape(2, nbk, gb * S5_P, gb * S5_GROUP)

    bd = jnp.concatenate([in_mat(bb_re), in_mat(bb_im)], axis=3).astype(BF16)
    cd = jnp.concatenate([out_mat(c_re), out_mat(-c_im)], axis=2).astype(BF16)
    a_bar = jnp.concatenate([a_re.reshape(2, S5_STATE), a_im.reshape(2, S5_STATE)], axis=1)
    return (lr * dt).reshape(2, S5_STATE), (li * dt).reshape(2, S5_STATE), a_bar, bd, cd


def _s5_row_perm(tstep):
    ns = S5_STREAMS
    r = jnp.arange(ns * tstep)
    src = (r % ns) * tstep + r // ns
    return (src[:, None] == jnp.arange(ns * tstep)[None, :]).astype(BF16)


def _s5_mixer(proj, nb, seq, s5p):
    la_re, la_im, a_bar, bd, cd = s5p
    ns = S5_STREAMS
    ncs = ns // nb
    tlen = seq // ncs
    pmat = _s5_row_perm(_tile(tlen, S5_TSTEP))
    a_b = jnp.broadcast_to(a_bar[:, None, :], (2, ns, 2 * S5_STATE))
    zeros = jnp.zeros((2, ns, 2 * S5_STATE), F32)
    x_end = _s5_scan(proj, pmat, bd, a_b, cd, zeros, tlen, True)
    mag = jnp.exp(la_re * tlen)
    at_re = (mag * jnp.cos(la_im * tlen))[:, None, :]
    at_im = (mag * jnp.sin(la_im * tlen))[:, None, :]
    xe_re = x_end[..., :S5_STATE].reshape(2, nb, ncs, S5_STATE)
    xe_im = x_end[..., S5_STATE:].reshape(2, nb, ncs, S5_STATE)
    x0 = []
    for d in range(2):
        order = range(ncs) if d == 0 else range(ncs - 1, -1, -1)
        cr = jnp.zeros((nb, S5_STATE), F32)
        ci = jnp.zeros((nb, S5_STATE), F32)
        rows = [None] * ncs
        for c in order:
            rows[c] = jnp.concatenate([cr, ci], axis=-1)
            cr, ci = (at_re[d] * cr - at_im[d] * ci + xe_re[d, :, c],
                      at_re[d] * ci + at_im[d] * cr + xe_im[d, :, c])
        x0.append(jnp.stack(rows, axis=1).reshape(ns, 2 * S5_STATE))
    y = _s5_scan(proj, pmat, bd, a_b, cd, jnp.stack(x0), tlen, False)
    return y.reshape(2, nb * seq, D_S5)


def _s5_post_body(y_ref, u_ref, d_ref, w_ref, b_ref, gn_ref, o_ref):
    y = y_ref[0] + y_ref[1] + d_ref[...] * u_ref[...]
    g = jax.nn.gelu(y)
    r = jnp.dot(g.astype(BF16), w_ref[...], preferred_element_type=F32) + b_ref[...]
    out = r[:, :D_S5] * jax.nn.sigmoid(r[:, D_S5:])
    ms = jnp.mean(out * out, axis=-1, keepdims=True)
    o_ref[...] = out * lax.rsqrt(ms + EPS) * gn_ref[...]


def _s5_post(y2, proj, d, glu_w, glu_b, gn):
    m = y2.shape[1]
    tm = _tile(m, 512)
    full = lambda shape: pl.BlockSpec(shape, lambda i: (0, 0))
    return pl.pallas_call(
        _s5_post_body,
        grid=(m // tm,),
        in_specs=[pl.BlockSpec((2, tm, D_S5), lambda i: (0, i, 0)),
                  pl.BlockSpec((tm, D_S5), lambda i: (i, COL_U)),
                  full((1, D_S5)), full((D_S5, 2 * D_S5)), full((1, 2 * D_S5)), full((1, D_S5))],
        out_specs=pl.BlockSpec((tm, D_S5), lambda i: (i, 0)),
        out_shape=jax.ShapeDtypeStruct((m, D_S5), F32),
        compiler_params=_cp("arbitrary"),
        name="s5_glu_norm",
    )(y2, proj, d, glu_w, glu_b, gn)


def _mlstm_body(*refs, reverse, epilogue):
    if epilogue:
        (q_ref, k_ref, v_ref, g_ref, gt_ref, bias_ref, biast_ref, hf_ref, o_gate_ref, gn_ref,
         out_ref, c_ref, n_ref, m_ref) = refs
    else:
        (q_ref, k_ref, v_ref, g_ref, gt_ref, bias_ref, biast_ref,
         out_ref, c_ref, n_ref, m_ref) = refs
        hf_ref = o_gate_ref = gn_ref = None

    @pl.when(pl.program_id(2) == 0)
    def _():
        c_ref[...] = jnp.zeros_like(c_ref)
        n_ref[...] = jnp.zeros_like(n_ref)
        m_ref[...] = jnp.zeros_like(m_ref)

    for hh in range(ML_HEADS_PER_STEP):
        _mlstm_head(hh, q_ref, k_ref, v_ref, g_ref, gt_ref, bias_ref, biast_ref, hf_ref, o_gate_ref,
                    gn_ref, out_ref, c_ref, n_ref, m_ref, reverse=reverse)


def _mlstm_head(hh, q_ref, k_ref, v_ref, g_ref, gt_ref, bias_ref, biast_ref, hf_ref, o_gate_ref, gn_ref,
                out_ref, c_ref, n_ref, m_ref, *, reverse):
    head = pl.program_id(1) * ML_HEADS_PER_STEP + hh
    cols = slice(hh * ML_DH, (hh + 1) * ML_DH)
    tc = ML_CHUNK
    gate_i = 2 if reverse else 0
    idx_i = gate_i * ML_HEADS + head
    idx_f = idx_i + ML_HEADS
    g = g_ref[...] + bias_ref[...]
    lane = lax.broadcasted_iota(jnp.int32, g.shape, 1)
    ig_col = jnp.sum(jnp.where(lane == idx_i, g, 0.0), axis=1, keepdims=True)
    fg_col = jnp.sum(jnp.where(lane == idx_f, g, 0.0), axis=1, keepdims=True)
    gt = gt_ref[...] + biast_ref[:, 0:1]
    sub = lax.broadcasted_iota(jnp.int32, gt.shape, 0)
    ig_row = jnp.sum(jnp.where(sub == idx_i, gt, 0.0), axis=0, keepdims=True)
    fg_row = jnp.sum(jnp.where(sub == idx_f, gt, 0.0), axis=0, keepdims=True)
    lf_col = jax.nn.log_sigmoid(fg_col)
    lf_row = jax.nn.log_sigmoid(fg_row)

    r = lax.broadcasted_iota(jnp.int32, (tc, tc), 0)
    s = lax.broadcasted_iota(jnp.int32, (tc, tc), 1)
    seen = (s >= r) if reverse else (s <= r)
    seen_t = (r >= s) if reverse else (r <= s)
    b_col = jnp.sum(jnp.where(seen, lf_row, 0.0), axis=1, keepdims=True)
    b_row = jnp.sum(jnp.where(seen_t, lf_col, 0.0), axis=0, keepdims=True)
    b_last = jnp.sum(lf_row, axis=1, keepdims=True)

    m_prev = m_ref[hh:hh + 1, 0:1]
    dmat = jnp.where(seen, b_col - b_row + ig_row, -jnp.inf)
    g_car = b_col + m_prev
    mt = jnp.maximum(g_car, jnp.max(dmat, axis=1, keepdims=True))
    inter = jnp.exp(g_car - mt)
    qb = q_ref[:, cols]
    kb = k_ref[:, cols]
    vb = v_ref[:, cols].astype(BF16)
    qk = lax.dot_general(qb, kb, (((1,), (1,)), ((), ())), preferred_element_type=F32)
    sc = qk * jnp.exp(dmat - mt)
    c_state = c_ref[hh]
    num = (inter * jnp.dot(qb, c_state.astype(BF16), preferred_element_type=F32)
           + jnp.dot(sc.astype(BF16), vb, preferred_element_type=F32))
    n_state = n_ref[hh:hh + 1, :]
    den = (inter * jnp.sum(qb.astype(F32) * n_state, axis=1, keepdims=True)
           + jnp.sum(sc, axis=1, keepdims=True))
    h = num / jnp.maximum(jnp.abs(den), jnp.exp(-mt))

    a_row = b_last - b_row + ig_row
    a_col = b_last - b_col + ig_col
    m_new = jnp.maximum(b_last + m_prev, jnp.max(a_row, axis=1, keepdims=True))
    decay = jnp.exp(b_last + m_prev - m_new)
    kw = kb.astype(F32) * jnp.exp(a_col - m_new)
    c_ref[hh] = decay * c_state + lax.dot_general(
        kw.astype(BF16), vb, (((0,), (0,)), ((), ())), preferred_element_type=F32)
    n_ref[hh:hh + 1, :] = decay * n_state + jnp.sum(kw, axis=0, keepdims=True)
    m_ref[hh:hh + 1, 0:1] = m_new

    if hf_ref is not None:
        h = h + hf_ref[:, cols]
        ms = jnp.mean(h * h, axis=-1, keepdims=True)
        hn = h * lax.rsqrt(ms + EPS) * gn_ref[:, cols]
        out_ref[:, cols] = hn * jax.nn.sigmoid(o_gate_ref[:, cols])
    else:
        out_ref[:, cols] = h


def _mlstm_dir(qk_planes, proj, gates, gates_t, bias, bias_t, nb, seq, reverse, h_fwd=None, gn=None):
    tc = ML_CHUNK
    nch = seq // tc
    hps = ML_HEADS_PER_STEP
    width = hps * ML_DH
    col_v = (N_CONV + D_S5) // width
    col_o = (N_CONV + D_S5 + D_ML) // width
    k_plane0 = D_ML // D_HY

    def chunk(b, ci):
        return b * nch + (nch - 1 - ci if reverse else ci)

    in_specs = [
        pl.BlockSpec((None, tc, width), lambda b, h, ci: (h, chunk(b, ci), 0)),
        pl.BlockSpec((None, tc, width), lambda b, h, ci: (k_plane0 + h, chunk(b, ci), 0)),
        pl.BlockSpec((tc, width), lambda b, h, ci: (chunk(b, ci), col_v + h)),
        pl.BlockSpec((tc, LANE), lambda b, h, ci: (chunk(b, ci), 0)),
        pl.BlockSpec((4 * ML_HEADS, tc), lambda b, h, ci: (0, chunk(b, ci))),
        pl.BlockSpec((1, LANE), lambda b, h, ci: (0, 0)),
        pl.BlockSpec((4 * ML_HEADS, LANE), lambda b, h, ci: (0, 0)),
    ]
    args = [qk_planes, qk_planes, proj, gates, gates_t, bias, bias_t]
    epilogue = h_fwd is not None
    if epilogue:
        in_specs += [
            pl.BlockSpec((tc, width), lambda b, h, ci: (chunk(b, ci), h)),
            pl.BlockSpec((tc, width), lambda b, h, ci: (chunk(b, ci), col_o + h)),
            pl.BlockSpec((1, width), lambda b, h, ci: (0, h)),
        ]
        args += [h_fwd, proj, gn]
    return pl.pallas_call(
        functools.partial(_mlstm_body, reverse=reverse, epilogue=epilogue),
        grid=(nb, ML_HEADS // hps, nch),
        in_specs=in_specs,
        out_specs=pl.BlockSpec((tc, width), lambda b, h, ci: (chunk(b, ci), h)),
        out_shape=jax.ShapeDtypeStruct((nb * seq, D_ML), F32),
        scratch_shapes=[pltpu.VMEM((hps, ML_DH, ML_DH), F32), pltpu.VMEM((SUBLANE, ML_DH), F32),
                        pltpu.VMEM((SUBLANE, LANE), F32)],
        compiler_params=_cp("arbitrary", "arbitrary", "arbitrary"),
        name="mlstm_bwd_norm" if reverse else "mlstm_fwd",
    )(*args)


def _layer(x, nb, seq, tb, p):
    proj, gates = _proj_in(x, p["ln_g"][0:1], p["w_in"], p["w_gate"])
    gates_t = gates[:, :4 * ML_HEADS].T
    gn = p["group_norm"]
    gn_hy = gn[None, :D_HY]
    gn_s5 = gn[None, D_HY:D_HY + D_S5]
    gn_ml = gn[None, D_HY + D_S5:]

    hy_planes = _short_conv(proj, p["conv_w"], p["conv_b"][None, :], nb, seq, False)
    qk_planes = _short_conv(proj, p["conv_w"], p["conv_b"][None, :], nb, seq, True)
    kf_re, kf_im = _hyena_spectra(seq, tb, p["hy_w1"], p["hy_b1"], p["hy_freq"], p["hy_w2"],
                                  p["hy_b2"], p["hy_w3"])
    y_hy = _hyena(hy_planes, nb, seq, tb, kf_re, kf_im, p["hy_skip"], gn_hy)
    y_s5 = _s5_post(_s5_mixer(proj, nb, seq, p["s5"]), proj, p["s5_d"][None, :],
                    p["s5_glu_w"], p["s5_glu_b"][None, :], gn_s5)
    ml_args = (qk_planes, proj, gates, gates_t, p["ml_bias"], p["ml_bias_t"], nb, seq)
    h_f = _mlstm_dir(*ml_args, False)
    y_ml = _mlstm_dir(*ml_args, True, h_f, gn_ml)

    w_out = p["w_out"]
    x = _proj_out([y_hy, y_s5, y_ml], [w_out[:D_HY], w_out[D_HY:D_HY + D_S5], w_out[D_HY + D_S5:]],
                  x, p["ln_g"][1:2])
    return _mlp(x, p["ln_g"][2:3], p["mlp_w1"], p["mlp_w2"], p["ln_g"][3:4])


_PARAM_NAMES = ("ln_g", "w_in", "conv_w", "conv_b", "hy_w1", "hy_b1", "hy_freq", "hy_w2", "hy_b2",
                "hy_w3", "hy_skip", "s5_lam_re", "s5_lam_im", "s5_log_dt", "s5_b_re", "s5_b_im",
                "s5_c_re", "s5_c_im", "s5_d", "s5_glu_w", "s5_glu_b", "ml_gate_b", "group_norm",
                "w_out", "mlp_w1", "mlp_w2")


def _prepare(p):
    p = dict(p)
    w_in = p["w_in"].astype(BF16)
    p["w_in"] = w_in[:, :P_MIX]
    p["w_gate"] = jnp.pad(w_in[:, P_MIX:], ((0, 0), (0, LANE - 4 * ML_HEADS)))
    p["w_out"] = p["w_out"].astype(BF16)
    p["mlp_w1"] = p["mlp_w1"].astype(BF16)
    p["mlp_w2"] = p["mlp_w2"].astype(BF16)
    p["s5_glu_w"] = p["s5_glu_w"].astype(BF16)
    p["ml_bias"] = jnp.pad(p["ml_gate_b"].reshape(1, -1), ((0, 0), (0, LANE - 4 * ML_HEADS)))
    p["ml_bias_t"] = jnp.broadcast_to(p["ml_gate_b"].reshape(-1, 1), (4 * ML_HEADS, LANE))
    p["s5"] = _s5_params(p["s5_lam_re"], p["s5_lam_im"], p["s5_log_dt"], p["s5_b_re"],
                         p["s5_b_im"], p["s5_c_re"], p["s5_c_im"])
    return p


def kernel(x_prompt, x_sample, ln_g, w_in, conv_w, conv_b, hy_w1, hy_b1, hy_freq, hy_w2, hy_b2, hy_w3,
           hy_skip, s5_lam_re, s5_lam_im, s5_log_dt, s5_b_re, s5_b_im, s5_c_re, s5_c_im, s5_d,
           s5_glu_w, s5_glu_b, ml_gate_b, group_norm, w_out, mlp_w1, mlp_w2):
    params = (ln_g, w_in, conv_w, conv_b, hy_w1, hy_b1, hy_freq, hy_w2, hy_b2, hy_w3, hy_skip,
              s5_lam_re, s5_lam_im, s5_log_dt, s5_b_re, s5_b_im, s5_c_re, s5_c_im, s5_d,
              s5_glu_w, s5_glu_b, ml_gate_b, group_norm, w_out, mlp_w1, mlp_w2)
    streams = []
    for x in (x_prompt, x_sample):
        nb, seq, _ = x.shape
        streams.append([x.reshape(nb * seq, D_MODEL), nb, seq, _dft_tables(seq)])
    for layer in range(DEPTH):
        p = _prepare({name: arr[layer] for name, arr in zip(_PARAM_NAMES, params)})
        for st in streams:
            st[0] = _layer(st[0], st[1], st[2], st[3], p)
    return tuple(st[0].reshape(x.shape) for st, x in zip(streams, (x_prompt, x_sample)))
```

```python
import functools
import math

import jax
import jax.numpy as jnp
from jax import lax
from jax.experimental import pallas as pl
from jax.experimental.pallas import tpu as pltpu

F32 = jnp.float32
BF16 = jnp.bfloat16

D_MODEL = 2048
DEPTH = 4
D_HY = 512
D_S5 = 512
D_ML = 1024
HY_EMB = 33
HY_BANDS = 16
HY_FILT = 64
HY_FAST_DECAY = 0.3
HY_SLOW_DECAY = 1.5
HY_TARGET = 1e-2
S5_GROUP = 16
S5_G = 32
S5_P = 64
S5_STATE = S5_G * S5_P
ML_HEADS = 4
ML_DH = 256
D_FF = 4 * D_MODEL
N_CONV = 3 * D_HY + 2 * D_ML
P_IN = N_CONV + D_S5 + 2 * D_ML + 4 * ML_HEADS
P_MIX = P_IN - 4 * ML_HEADS
EPS = 1e-6

LANE = 128
SUBLANE = 8
VMEM_LIMIT_BYTES = 48 * 1024 * 1024

DFT_N2 = LANE
S5_STREAMS = SUBLANE
S5_TSTEP = 64
S5_BLOCK_GROUPS = LANE // S5_GROUP
S5_NBLOCKS = S5_G // S5_BLOCK_GROUPS
ML_CHUNK = 256
ML_HEADS_PER_STEP = 4
COL_U = N_CONV // D_S5


def _cp(*sem):
    return pltpu.CompilerParams(dimension_semantics=sem, vmem_limit_bytes=VMEM_LIMIT_BYTES)


def _tile(n, pref):
    t = min(n, pref)
    while n % t:
        t //= 2
    return t


def _rms(x, g):
    ms = jnp.mean(x * x, axis=-1, keepdims=True)
    return x * lax.rsqrt(ms + EPS) * g


def _proj_in_body(x_ref, g_ref, w_ref, wg_ref, o_ref, og_ref, xn_ref):
    @pl.when(pl.program_id(1) == 0)
    def _():
        xn = _rms(x_ref[...], g_ref[...]).astype(BF16)
        xn_ref[...] = xn
        og_ref[...] = jnp.dot(xn, wg_ref[...], preferred_element_type=F32)

    o_ref[...] = jnp.dot(xn_ref[...], w_ref[...], preferred_element_type=F32)


def _proj_in(x, g, w, w_gate):
    m, k = x.shape
    n = w.shape[1]
    tm = _tile(m, 1024)
    tn = 1024
    return pl.pallas_call(
        _proj_in_body,
        grid=(m // tm, n // tn),
        in_specs=[pl.BlockSpec((tm, k), lambda i, j: (i, 0)),
                  pl.BlockSpec((1, k), lambda i, j: (0, 0)),
                  pl.BlockSpec((k, tn), lambda i, j: (0, j)),
                  pl.BlockSpec((k, LANE), lambda i, j: (0, 0))],
        out_specs=[pl.BlockSpec((tm, tn), lambda i, j: (i, j)),
                   pl.BlockSpec((tm, LANE), lambda i, j: (i, 0))],
        out_shape=[jax.ShapeDtypeStruct((m, n), F32), jax.ShapeDtypeStruct((m, LANE), F32)],
        scratch_shapes=[pltpu.VMEM((tm, k), BF16)],
        compiler_params=_cp("arbitrary", "arbitrary"),
        name="norm_proj_in",
    )(x, g, w, w_gate)


def _proj_out_body(a0_ref, a1_ref, a2_ref, w0_ref, w1_ref, w2_ref, r_ref, g_ref, o_ref):
    f = (jnp.dot(a0_ref[...].astype(BF16), w0_ref[...], preferred_element_type=F32)
         + jnp.dot(a1_ref[...].astype(BF16), w1_ref[...], preferred_element_type=F32)
         + jnp.dot(a2_ref[...].astype(BF16), w2_ref[...], preferred_element_type=F32))
    o_ref[...] = r_ref[...] + _rms(f, g_ref[...])


def _proj_out(parts, weights, resid, g):
    m, n = resid.shape
    tm = _tile(m, 256)
    row = lambda width: pl.BlockSpec((tm, width), lambda i: (i, 0))
    full = lambda arr: pl.BlockSpec(arr.shape, lambda i: (0, 0))
    return pl.pallas_call(
        _proj_out_body,
        grid=(m // tm,),
        in_specs=[row(a.shape[1]) for a in parts] + [full(w) for w in weights] + [row(n), full(g)],
        out_specs=row(n),
        out_shape=jax.ShapeDtypeStruct((m, n), F32),
        compiler_params=_cp("arbitrary"),
        name="proj_out_resnorm",
    )(*parts, *weights, resid, g)


def _mlp_body(x_ref, g_in_ref, w1_ref, w2_ref, g_out_ref, o_ref, xn_ref, acc_ref):
    j = pl.program_id(1)

    @pl.when(j == 0)
    def _():
        xn_ref[...] = _rms(x_ref[...], g_in_ref[...]).astype(BF16)
        acc_ref[...] = jnp.zeros_like(acc_ref)

    h = jnp.dot(xn_ref[...], w1_ref[...], preferred_element_type=F32)
    h = jnp.square(jnp.maximum(h, 0.0)).astype(BF16)
    acc_ref[...] += jnp.dot(h, w2_ref[...], preferred_element_type=F32)

    @pl.when(j == pl.num_programs(1) - 1)
    def _():
        o_ref[...] = x_ref[...] + _rms(acc_ref[...], g_out_ref[...])


def _mlp(x, g_in, w1, w2, g_out):
    m, k = x.shape
    ff = w1.shape[1]
    tm = _tile(m, 512)
    tf = 1024
    return pl.pallas_call(
        _mlp_body,
        grid=(m // tm, ff // tf),
        in_specs=[pl.BlockSpec((tm, k), lambda i, j: (i, 0)),
                  pl.BlockSpec((1, k), lambda i, j: (0, 0)),
                  pl.BlockSpec((k, tf), lambda i, j: (0, j)),
                  pl.BlockSpec((tf, k), lambda i, j: (j, 0)),
                  pl.BlockSpec((1, k), lambda i, j: (0, 0))],
        out_specs=pl.BlockSpec((tm, k), lambda i, j: (i, 0)),
        out_shape=jax.ShapeDtypeStruct((m, k), F32),
        scratch_shapes=[pltpu.VMEM((tm, k), BF16), pltpu.VMEM((tm, k), F32)],
        compiler_params=_cp("arbitrary", "arbitrary"),
        name="mlp_resnorm",
    )(x, g_in, w1, w2, g_out)


def _conv_body(x_ref, p_ref, n_ref, w_ref, b_ref, o_ref, *, nblk, qk):
    i = pl.program_id(1)
    j = pl.program_id(2)
    x = x_ref[...]
    tm = x.shape[0]
    row = lax.broadcasted_iota(jnp.int32, x.shape, 0)
    prev_row = jnp.where(i == 0, 0.0, p_ref[SUBLANE - 1:SUBLANE, :])
    next_row = jnp.where(i == nblk - 1, 0.0, n_ref[0:1, :])
    xm = jnp.where(row == 0, prev_row, pltpu.roll(x, 1, 0))
    xp = jnp.where(row == tm - 1, next_row, pltpu.roll(x, tm - 1, 0))
    w = w_ref[...]
    y = b_ref[...] + xm * w[0:1] + x * w[1:2] + xp * w[2:3]
    if qk:
        y = y * jax.nn.sigmoid(y) * jnp.where(j >= 2, ML_DH ** -0.5, 1.0)
    o_ref[...] = y.astype(o_ref.dtype)


def _short_conv(proj, conv_w, conv_b, nb, seq, qk):
    mtot = proj.shape[0]
    tm = _tile(seq, 512)
    nblk = seq // tm
    col0, ncol = (3, 4) if qk else (0, 3)
    ppo = 2 if qk else 1
    last8 = mtot // SUBLANE - 1
    tm8 = tm // SUBLANE

    def x_map(b, i, j):
        return (b * nblk + i, col0 + j)

    def p_map(b, i, j):
        return (jnp.maximum((b * nblk + i) * tm8 - 1, 0), col0 + j)

    def n_map(b, i, j):
        return (jnp.minimum((b * nblk + i + 1) * tm8, last8), col0 + j)

    return pl.pallas_call(
        functools.partial(_conv_body, nblk=nblk, qk=qk),
        grid=(nb, nblk, ncol),
        in_specs=[pl.BlockSpec((tm, D_HY), x_map),
                  pl.BlockSpec((SUBLANE, D_HY), p_map),
                  pl.BlockSpec((SUBLANE, D_HY), n_map),
                  pl.BlockSpec((3, D_HY), lambda b, i, j: (0, col0 + j)),
                  pl.BlockSpec((1, D_HY), lambda b, i, j: (0, col0 + j))],
        out_specs=pl.BlockSpec((None, tm, D_HY), lambda b, i, j: (j // ppo, b * nblk + i, j % ppo)),
        out_shape=jax.ShapeDtypeStruct((ncol // ppo, nb * seq, ppo * D_HY), BF16 if qk else F32),
        compiler_params=_cp("arbitrary", "arbitrary", "arbitrary"),
        name="short_conv_qk" if qk else "short_conv_hy",
    )(proj, proj, proj, conv_w, conv_b)


def _dft_tables(seq):
    n = 2 * seq
    n1 = n // DFT_N2
    n1h = n1 // 2
    n1k = n1h + 2 * SUBLANE
    k1 = jnp.arange(n1k, dtype=jnp.int32)
    t1 = jnp.arange(n1h, dtype=jnp.int32)
    kept = (k1 <= n1h).astype(F32)[:, None]
    weight = kept * jnp.where((k1 == 0) | (k1 == n1h), 1.0, 2.0)[:, None]
    ang = ((k1[:, None] * t1[None, :]) % n1).astype(F32) * (2.0 * math.pi / n1)
    lead_f = jnp.concatenate([kept * jnp.cos(ang), -kept * jnp.sin(ang)], axis=0).astype(BF16)
    lead_ic = (weight * jnp.cos(ang)).T.astype(BF16)
    lead_is = (-weight * jnp.sin(ang)).T.astype(BF16)
    t2 = jnp.arange(DFT_N2, dtype=jnp.int32)
    k2 = jnp.arange(DFT_N2, dtype=jnp.int32)
    idx = (t2[None, None, :] * (k2[None, :, None] * n1 + k1[:, None, None])) % n
    ang2 = idx.astype(F32) * (2.0 * math.pi / n)
    f_re = jnp.cos(ang2)
    f_im = -jnp.sin(ang2)
    mid_f = jnp.concatenate([jnp.concatenate([f_re, -f_im], axis=2),
                             jnp.concatenate([f_im, f_re], axis=2)], axis=1).astype(BF16)
    g_re = jnp.swapaxes(f_re, 1, 2)
    g_im = jnp.swapaxes(f_im, 1, 2)
    mid_g = jnp.concatenate([jnp.concatenate([g_re, g_im], axis=2),
                             jnp.concatenate([-g_im, g_re], axis=2)], axis=1).astype(BF16)
    return dict(n=n, n1=n1k, n1h=n1h, lead_f=lead_f, lead_ic=lead_ic, lead_is=lead_is,
                mid_f=mid_f, mid_g=mid_g)


def _hy_features(seq):
    t = jnp.linspace(0.0, 1.0, seq, dtype=F32)[:, None]
    w = 2.0 * math.pi * jnp.arange(seq, dtype=F32)[:, None] / seq
    f = jnp.linspace(1e-4, HY_BANDS - 1, HY_BANDS, dtype=F32)[None, :]
    z = jnp.concatenate([t, jnp.cos(f * w), -jnp.sin(f * w)], axis=-1)
    return jnp.pad(z, ((0, 0), (0, LANE - HY_EMB)))


def _hy_deltas():
    d = jnp.abs(jnp.linspace(math.log(HY_TARGET) / HY_SLOW_DECAY,
                             math.log(HY_TARGET) / HY_FAST_DECAY, D_HY, dtype=F32))
    return jnp.tile(d, 4)[None, :]


def _hy_filter_body(z_ref, w1_ref, b1_ref, fr_ref, w2_ref, b2_ref, w3_ref, dl_ref, h_ref, ss_ref):
    i = pl.program_id(0)
    hi = lax.Precision.HIGHEST
    z = z_ref[...]
    fr = fr_ref[...]
    h = jnp.sin(fr * (jnp.dot(z, w1_ref[...], precision=hi, preferred_element_type=F32) + b1_ref[...]))
    h = jnp.sin(fr * (jnp.dot(h, w2_ref[...], precision=hi, preferred_element_type=F32) + b2_ref[...]))
    h = jnp.dot(h.astype(BF16), w3_ref[...], preferred_element_type=F32)
    h = h * jnp.exp(-z[:, 0:1] * dl_ref[...])
    row = lax.broadcasted_iota(jnp.int32, h.shape, 0)
    col = lax.broadcasted_iota(jnp.int32, h.shape, 1)
    bwd = (col // D_HY) % 2 == 1
    h = jnp.where((row + i * h.shape[0] == 0) & bwd, 0.0, h)
    h_ref[...] = h
    part = jnp.broadcast_to(jnp.sum(h * h, axis=0, keepdims=True), ss_ref.shape)

    @pl.when(i == 0)
    def _():
        ss_ref[...] = part

    @pl.when(i > 0)
    def _():
        ss_ref[...] += part


def _hy_filter(seq, z, w1, b1, fr, w2, b2, w3, deltas):
    tm = _tile(seq, 512)
    nc = 4 * D_HY
    full = lambda shape: pl.BlockSpec(shape, lambda i: (0, 0))
    return pl.pallas_call(
        _hy_filter_body,
        grid=(seq // tm,),
        in_specs=[pl.BlockSpec((tm, LANE), lambda i: (i, 0)),
                  full((LANE, LANE)), full((1, LANE)), full((1, LANE)),
                  full((LANE, LANE)), full((1, LANE)), full((LANE, nc)), full((1, nc))],
        out_specs=[pl.BlockSpec((tm, nc), lambda i: (i, 0)), full((SUBLANE, nc))],
        out_shape=[jax.ShapeDtypeStruct((seq, nc), F32), jax.ShapeDtypeStruct((SUBLANE, nc), F32)],
        compiler_params=_cp("arbitrary"),
        name="hyena_filter",
    )(z, w1, b1, fr, w2, b2, w3, deltas)


def _lead_fwd_body(x_ref, f_ref, re_ref, im_ref):
    n1 = re_ref.shape[0]
    r = jnp.dot(f_ref[...], x_ref[...].astype(BF16), preferred_element_type=F32)
    re_ref[...] = r[:n1].astype(BF16)
    im_ref[...] = r[n1:].astype(BF16)


def _lead_fwd(x, tb, plane, nb):
    n1, n1h = tb["n1"], tb["n1h"]
    cols = x.shape[-1]
    tn = _tile(cols, 4096)
    return pl.pallas_call(
        _lead_fwd_body,
        grid=(nb, cols // tn),
        in_specs=[pl.BlockSpec((None, n1h, tn), lambda b, j: (plane, b, j)),
                  pl.BlockSpec((2 * n1, n1h), lambda b, j: (0, 0))],
        out_specs=[pl.BlockSpec((None, n1, tn), lambda b, j: (b, 0, j))] * 2,
        out_shape=[jax.ShapeDtypeStruct((nb, n1, cols), BF16)] * 2,
        compiler_params=_cp("arbitrary", "arbitrary"),
        name="hyena_lead_fwd",
    )(x, tb["lead_f"])


def _mid_spec_body(are_ref, aim_ref, f_ref, ss_ref, kr_ref, ki_ref, *, bk):
    n2 = DFT_N2
    ss = ss_ref[0:1, :]
    scale = lax.rsqrt(ss[:, :D_HY] + ss[:, D_HY:] + EPS)

    def body(i, c):
        a = jnp.concatenate([are_ref[i], aim_ref[i]], axis=0)
        x = jnp.dot(f_ref[i], a, preferred_element_type=F32)
        xr = x[:n2]
        xi = x[n2:]
        kr_ref[i] = (xr[:, :D_HY] + xr[:, D_HY:]) * scale
        ki_ref[i] = (xi[:, :D_HY] - xi[:, D_HY:]) * scale
        return c

    lax.fori_loop(0, bk, body, 0)


def _mid_spec(a_re, a_im, sumsq, tb):
    n1 = tb["n1"]
    bk = _tile(n1, 8)
    a_re = a_re.reshape(n1, DFT_N2, 4 * D_HY)
    a_im = a_im.reshape(n1, DFT_N2, 4 * D_HY)
    a_spec = pl.BlockSpec((bk, DFT_N2, 2 * D_HY), lambda o, i: (i, 0, o))
    k_spec = pl.BlockSpec((None, bk, DFT_N2, D_HY), lambda o, i: (o, i, 0, 0))
    return pl.pallas_call(
        functools.partial(_mid_spec_body, bk=bk),
        grid=(2, n1 // bk),
        in_specs=[a_spec, a_spec,
                  pl.BlockSpec((bk, 2 * DFT_N2, 2 * DFT_N2), lambda o, i: (i, 0, 0)),
                  pl.BlockSpec((SUBLANE, 2 * D_HY), lambda o, i: (0, o))],
        out_specs=[k_spec, k_spec],
        out_shape=[jax.ShapeDtypeStruct((2, n1, DFT_N2, D_HY), F32)] * 2,
        compiler_params=_cp("arbitrary", "arbitrary"),
        name="hyena_mid_spectrum",
    )(a_re, a_im, tb["mid_f"], sumsq)


def _mid_conv_body(are_ref, aim_ref, f_ref, g_ref, kr_ref, ki_ref, bre_ref, bim_ref, *, bk):
    n2 = DFT_N2

    def body(i, c):
        a = jnp.concatenate([are_ref[i], aim_ref[i]], axis=0)
        x = jnp.dot(f_ref[i], a, preferred_element_type=F32)
        xr = x[:n2]
        xi = x[n2:]
        kr = kr_ref[i]
        ki = ki_ref[i]
        p = jnp.concatenate([xr * kr - xi * ki, xr * ki + xi * kr], axis=0).astype(BF16)
        q = jnp.dot(g_ref[i], p, preferred_element_type=F32)
        bre_ref[i] = q[:n2].astype(BF16)
        bim_ref[i] = q[n2:].astype(BF16)
        return c

    lax.fori_loop(0, bk, body, 0)


def _mid_conv(a_re, a_im, kf_re, kf_im, order, tb):
    n1 = tb["n1"]
    nb = a_re.shape[0]
    bk = _tile(n1, 8)
    a_re = a_re.reshape(nb, n1, DFT_N2, D_HY)
    a_im = a_im.reshape(nb, n1, DFT_N2, D_HY)
    a_spec = pl.BlockSpec((None, bk, DFT_N2, D_HY), lambda b, i: (b, i, 0, 0))
    t_spec = pl.BlockSpec((bk, 2 * DFT_N2, 2 * DFT_N2), lambda b, i: (i, 0, 0))
    k_spec = pl.BlockSpec((None, bk, DFT_N2, D_HY), lambda b, i: (order, i, 0, 0))
    b_re, b_im = pl.pallas_call(
        functools.partial(_mid_conv_body, bk=bk),
        grid=(nb, n1 // bk),
        in_specs=[a_spec, a_spec, t_spec, t_spec, k_spec, k_spec],
        out_specs=[a_spec, a_spec],
        out_shape=[jax.ShapeDtypeStruct((nb, n1, DFT_N2, D_HY), BF16)] * 2,
        compiler_params=_cp("arbitrary", "arbitrary"),
        name="hyena_mid_conv",
    )(a_re, a_im, tb["mid_f"], tb["mid_g"], kf_re, kf_im)
    return b_re.reshape(nb, n1, DFT_N2 * D_HY), b_im.reshape(nb, n1, DFT_N2 * D_HY)


def _lead_inv_body(bre_ref, bim_ref, c_ref, s_ref, z_ref, gate_ref, skip_ref, gn_ref, o_ref, *, inv_n, last):
    y = (jnp.dot(c_ref[...], bre_ref[...], preferred_element_type=F32)
         + jnp.dot(s_ref[...], bim_ref[...], preferred_element_type=F32)) * inv_n
    out = gate_ref[...] * (y + skip_ref[...] * z_ref[...])
    if not last:
        o_ref[...] = out
    else:
        gn = gn_ref[...]
        for c in range(out.shape[1] // D_HY):
            blk = out[:, c * D_HY:(c + 1) * D_HY]
            ms = jnp.mean(blk * blk, axis=-1, keepdims=True)
            o_ref[:, c * D_HY:(c + 1) * D_HY] = blk * lax.rsqrt(ms + EPS) * gn


def _lead_inv(b_re, b_im, z, z_plane, gates, gate_plane, skip, gn, tb, last):
    n1, n1h = tb["n1"], tb["n1h"]
    nb, _, cols = b_re.shape
    tn = _tile(cols, 4096)
    skip_t = jnp.tile(skip[None, :], (1, tn // D_HY))
    b_spec = pl.BlockSpec((None, n1, tn), lambda b, j: (b, 0, j))
    t_spec = pl.BlockSpec((n1h, n1), lambda b, j: (0, 0))
    return pl.pallas_call(
        functools.partial(_lead_inv_body, inv_n=1.0 / tb["n"], last=last),
        grid=(nb, cols // tn),
        in_specs=[b_spec, b_spec, t_spec, t_spec,
                  pl.BlockSpec((None, n1h, tn), lambda b, j: (z_plane, b, j)),
                  pl.BlockSpec((None, n1h, tn), lambda b, j: (gate_plane, b, j)),
                  pl.BlockSpec((1, tn), lambda b, j: (0, 0)),
                  pl.BlockSpec((1, D_HY), lambda b, j: (0, 0))],
        out_specs=pl.BlockSpec((None, n1h, tn), lambda b, j: (0, b, j)),
        out_shape=jax.ShapeDtypeStruct((1, nb * n1h, cols), F32),
        compiler_params=_cp("arbitrary", "arbitrary"),
        name="hyena_lead_inv",
    )(b_re, b_im, tb["lead_ic"], tb["lead_is"], z, gates, skip_t, gn)


def _hyena(conv_out, nb, seq, tb, kf_re, kf_im, skip, gn):
    n1h = tb["n1h"]
    planes = conv_out.reshape(conv_out.shape[0], nb * n1h, DFT_N2 * D_HY)
    z, z_plane = planes, 0
    for order in range(2):
        a_re, a_im = _lead_fwd(z, tb, z_plane, nb)
        b_re, b_im = _mid_conv(a_re, a_im, kf_re, kf_im, order, tb)
        z = _lead_inv(b_re, b_im, z, z_plane, planes, 1 + order, skip[order], gn, tb, order == 1)
        z_plane = 0
    return z.reshape(nb * seq, D_HY)


def _hyena_spectra(seq, tb, w1, b1, fr, w2, b2, w3):
    pad_r = LANE - HY_EMB
    pad_c = LANE - HY_FILT
    hfilt, sumsq = _hy_filter(
        seq, _hy_features(seq),
        jnp.pad(w1, ((0, pad_r), (0, pad_c))), jnp.pad(b1, (0, pad_c))[None, :],
        jnp.pad(fr, (0, pad_c))[None, :], jnp.pad(w2, ((0, pad_c), (0, pad_c))),
        jnp.pad(b2, (0, pad_c))[None, :], jnp.pad(w3, ((0, pad_c), (0, 0))).astype(BF16), _hy_deltas())
    x = hfilt.reshape(1, tb["n1h"], DFT_N2 * 4 * D_HY)
    a_re, a_im = _lead_fwd(x, tb, 0, 1)
    return _mid_spec(a_re[0], a_im[0], sumsq, tb)


def _s5_scan_body(u_ref, pm_ref, bd_ref, a_ref, cd_ref, x0_ref, o_ref, bu_ref, st_ref, y_ref,
                  *, tstep, final_state):
    d = pl.program_id(0)
    tb = pl.program_id(1)
    ns = S5_STREAMS
    half = S5_STATE

    @pl.when(tb == 0)
    def _():
        st_ref[...] = x0_ref[...]

    u = u_ref[...].reshape(ns * tstep, D_S5).astype(BF16)
    u_tm = jnp.dot(pm_ref[...], u, preferred_element_type=F32).astype(BF16)

    cw = S5_BLOCK_GROUPS * S5_P
    for blk in range(S5_NBLOCKS):
        bre = pl.ds(2 * blk * cw, cw)
        bim = pl.ds((2 * blk + 1) * cw, cw)
        sre = pl.ds(blk * cw, cw)
        sim = pl.ds(half + blk * cw, cw)
        bu_ref[:, pl.ds(2 * blk * cw, 2 * cw)] = jnp.dot(
            u_tm[:, blk * LANE:(blk + 1) * LANE], bd_ref[blk], preferred_element_type=F32)
        ar = a_ref[:, sre]
        ai = a_ref[:, sim]

        def step(i, carry, bre=bre, bim=bim, ar=ar, ai=ai):
            xr, xi = carry
            t = i + d * (tstep - 1 - 2 * i)
            rows = pl.ds(pl.multiple_of(t * ns, ns), ns)
            nxr = ar * xr - ai * xi + bu_ref[rows, bre]
            nxi = ar * xi + ai * xr + bu_ref[rows, bim]
            if not final_state:
                bu_ref[rows, bre] = nxr
                bu_ref[rows, bim] = nxi
            return nxr, nxi

        xr, xi = lax.fori_loop(0, tstep, step, (st_ref[:, sre], st_ref[:, sim]), unroll=True)
        st_ref[:, sre] = xr
        st_ref[:, sim] = xi
        if not final_state:
            y_ref[blk] = jnp.dot(bu_ref[:, pl.ds(2 * blk * cw, 2 * cw)].astype(BF16), cd_ref[blk],
                                 preferred_element_type=F32)

    if final_state:
        @pl.when(tb == pl.num_programs(1) - 1)
        def _():
            o_ref[...] = st_ref[...]
    else:
        for s in range(ns):
            for blk in range(S5_NBLOCKS):
                o_ref[s, :, blk * LANE:(blk + 1) * LANE] = y_ref[blk, pl.ds(s, tstep, stride=ns), :]


def _s5_scan(proj, pmat, bd, a_b, cd, x0, tlen, final_state):
    ns = S5_STREAMS
    tstep = pmat.shape[0] // ns
    nt = tlen // tstep
    proj3 = proj.reshape(ns, tlen, proj.shape[1])
    blk_in = S5_BLOCK_GROUPS * S5_GROUP
    blk_state = 2 * S5_BLOCK_GROUPS * S5_P

    def window(d, t):
        return t + d * (nt - 1 - 2 * t)

    if final_state:
        out_spec = pl.BlockSpec((None, ns, 2 * S5_STATE), lambda d, t: (d, 0, 0))
        out_shape = jax.ShapeDtypeStruct((2, ns, 2 * S5_STATE), F32)
    else:
        out_spec = pl.BlockSpec((None, ns, tstep, D_S5), lambda d, t: (d, 0, window(d, t), 0))
        out_shape = jax.ShapeDtypeStruct((2, ns, tlen, D_S5), F32)
    return pl.pallas_call(
        functools.partial(_s5_scan_body, tstep=tstep, final_state=final_state),
        grid=(2, nt),
        in_specs=[pl.BlockSpec((ns, tstep, D_S5), lambda d, t: (0, window(d, t), COL_U)),
                  pl.BlockSpec((ns * tstep, ns * tstep), lambda d, t: (0, 0)),
                  pl.BlockSpec((None, S5_NBLOCKS, blk_in, blk_state), lambda d, t: (d, 0, 0, 0)),
                  pl.BlockSpec((None, ns, 2 * S5_STATE), lambda d, t: (d, 0, 0)),
                  pl.BlockSpec((None, S5_NBLOCKS, blk_state, blk_in), lambda d, t: (d, 0, 0, 0)),
                  pl.BlockSpec((None, ns, 2 * S5_STATE), lambda d, t: (d, 0, 0))],
        out_specs=out_spec,
        out_shape=out_shape,
        scratch_shapes=[pltpu.VMEM((tstep * ns, 2 * S5_STATE), F32),
                        pltpu.VMEM((ns, 2 * S5_STATE), F32),
                        pltpu.VMEM((D_S5 // LANE, tstep * ns, LANE), F32)],
        compiler_params=_cp("arbitrary", "arbitrary"),
        name="s5_end_state" if final_state else "s5_scan",
    )(proj3, pmat, bd, a_b, cd, x0)


def _s5_params(lam_re, lam_im, log_dt, b_re, b_im, c_re, c_im):
    lr = jnp.minimum(lam_re, -1e-4)
    li = lam_im
    dt = jnp.exp(log_dt)[..., None]
    er = jnp.exp(lr * dt)
    a_re = er * jnp.cos(li * dt)
    a_im = er * jnp.sin(li * dt)
    den = lr * lr + li * li
    q_re = ((a_re - 1.0) * lr + a_im * li) / den
    q_im = (a_im * lr - (a_re - 1.0) * li) / den
    bb_re = q_re[..., None] * b_re - q_im[..., None] * b_im
    bb_im = q_re[..., None] * b_im + q_im[..., None] * b_re
    nbk, gb = S5_NBLOCKS, S5_BLOCK_GROUPS
    eye = jnp.eye(gb, dtype=F32)

    def in_mat(x):
        x = x.reshape(2, nbk, gb, S5_P, S5_GROUP)
        return jnp.einsum("dbgph,gk->dbghkp", x, eye).reshape(2, nbk, gb * S5_GROUP, gb * S5_P)

    def out_mat(x):
        x = x.reshape(2, nbk, gb, S5_GROUP, S5_P)
        return jnp.einsum("dbghp,gk->dbgpkh", x, eye).reshape(2, nbk, gb * S5_P, gb * S5_GROUP)

    bd = jnp.concatenate([in_mat(bb_re), in_mat(bb_im)], axis=3).astype(BF16)
    cd = jnp.concatenate([out_mat(c_re), out_mat(-c_im)], axis=2).astype(BF16)
    a_bar = jnp.concatenate([a_re.reshape(2, S5_STATE), a_im.reshape(2, S5_STATE)], axis=1)
    return (lr * dt).reshape(2, S5_STATE), (li * dt).reshape(2, S5_STATE), a_bar, bd, cd


def _s5_row_perm(tstep):
    ns = S5_STREAMS
    r = jnp.arange(ns * tstep)
    src = (r % ns) * tstep + r // ns
    return (src[:, None] == jnp.arange(ns * tstep)[None, :]).astype(BF16)


def _s5_mixer(proj, nb, seq, s5p):
    la_re, la_im, a_bar, bd, cd = s5p
    ns = S5_STREAMS
    ncs = ns // nb
    tlen = seq // ncs
    pmat = _s5_row_perm(_tile(tlen, S5_TSTEP))
    a_b = jnp.broadcast_to(a_bar[:, None, :], (2, ns, 2 * S5_STATE))
    zeros = jnp.zeros((2, ns, 2 * S5_STATE), F32)
    x_end = _s5_scan(proj, pmat, bd, a_b, cd, zeros, tlen, True)
    mag = jnp.exp(la_re * tlen)
    at_re = (mag * jnp.cos(la_im * tlen))[:, None, :]
    at_im = (mag * jnp.sin(la_im * tlen))[:, None, :]
    xe_re = x_end[..., :S5_STATE].reshape(2, nb, ncs, S5_STATE)
    xe_im = x_end[..., S5_STATE:].reshape(2, nb, ncs, S5_STATE)
    x0 = []
    for d in range(2):
        order = range(ncs) if d == 0 else range(ncs - 1, -1, -1)
        cr = jnp.zeros((nb, S5_STATE), F32)
        ci = jnp.zeros((nb, S5_STATE), F32)
        rows = [None] * ncs
        for c in order:
            rows[c] = jnp.concatenate([cr, ci], axis=-1)
            cr, ci = (at_re[d] * cr - at_im[d] * ci + xe_re[d, :, c],
                      at_re[d] * ci + at_im[d] * cr + xe_im[d, :, c])
        x0.append(jnp.stack(rows, axis=1).reshape(ns, 2 * S5_STATE))
    y = _s5_scan(proj, pmat, bd, a_b, cd, jnp.stack(x0), tlen, False)
    return y.reshape(2, nb * seq, D_S5)


def _s5_post_body(y_ref, u_ref, d_ref, w_ref, b_ref, gn_ref, o_ref):
    y = y_ref[0] + y_ref[1] + d_ref[...] * u_ref[...]
    g = jax.nn.gelu(y)
    r = jnp.dot(g.astype(BF16), w_ref[...], preferred_element_type=F32) + b_ref[...]
    out = r[:, :D_S5] * jax.nn.sigmoid(r[:, D_S5:])
    ms = jnp.mean(out * out, axis=-1, keepdims=True)
    o_ref[...] = out * lax.rsqrt(ms + EPS) * gn_ref[...]


def _s5_post(y2, proj, d, glu_w, glu_b, gn):
    m = y2.shape[1]
    tm = _tile(m, 512)
    full = lambda shape: pl.BlockSpec(shape, lambda i: (0, 0))
    return pl.pallas_call(
        _s5_post_body,
        grid=(m // tm,),
        in_specs=[pl.BlockSpec((2, tm, D_S5), lambda i: (0, i, 0)),
                  pl.BlockSpec((tm, D_S5), lambda i: (i, COL_U)),
                  full((1, D_S5)), full((D_S5, 2 * D_S5)), full((1, 2 * D_S5)), full((1, D_S5))],
        out_specs=pl.BlockSpec((tm, D_S5), lambda i: (i, 0)),
        out_shape=jax.ShapeDtypeStruct((m, D_S5), F32),
        compiler_params=_cp("arbitrary"),
        name="s5_glu_norm",
    )(y2, proj, d, glu_w, glu_b, gn)


def _mlstm_body(*refs, reverse, epilogue):
    if epilogue:
        (q_ref, k_ref, v_ref, g_ref, gt_ref, bias_ref, biast_ref, hf_ref, o_gate_ref, gn_ref,
         out_ref, c_ref, n_ref, m_ref) = refs
    else:
        (q_ref, k_ref, v_ref, g_ref, gt_ref, bias_ref, biast_ref,
         out_ref, c_ref, n_ref, m_ref) = refs
        hf_ref = o_gate_ref = gn_ref = None

    @pl.when(pl.program_id(2) == 0)
    def _():
        c_ref[...] = jnp.zeros_like(c_ref)
        n_ref[...] = jnp.zeros_like(n_ref)
        m_ref[...] = jnp.zeros_like(m_ref)

    for hh in range(ML_HEADS_PER_STEP):
        _mlstm_head(hh, q_ref, k_ref, v_ref, g_ref, gt_ref, bias_ref, biast_ref, hf_ref, o_gate_ref,
                    gn_ref, out_ref, c_ref, n_ref, m_ref, reverse=reverse)


def _mlstm_head(hh, q_ref, k_ref, v_ref, g_ref, gt_ref, bias_ref, biast_ref, hf_ref, o_gate_ref, gn_ref,
                out_ref, c_ref, n_ref, m_ref, *, reverse):
    head = pl.program_id(1) * ML_HEADS_PER_STEP + hh
    cols = slice(hh * ML_DH, (hh + 1) * ML_DH)
    tc = ML_CHUNK
    gate_i = 2 if reverse else 0
    idx_i = gate_i * ML_HEADS + head
    idx_f = idx_i + ML_HEADS
    g = g_ref[...] + bias_ref[...]
    lane = lax.broadcasted_iota(jnp.int32, g.shape, 1)
    ig_col = jnp.sum(jnp.where(lane == idx_i, g, 0.0), axis=1, keepdims=True)
    fg_col = jnp.sum(jnp.where(lane == idx_f, g, 0.0), axis=1, keepdims=True)
    gt = gt_ref[...] + biast_ref[:, 0:1]
    sub = lax.broadcasted_iota(jnp.int32, gt.shape, 0)
    ig_row = jnp.sum(jnp.where(sub == idx_i, gt, 0.0), axis=0, keepdims=True)
    fg_row = jnp.sum(jnp.where(sub == idx_f, gt, 0.0), axis=0, keepdims=True)
    lf_col = jax.nn.log_sigmoid(fg_col)
    lf_row = jax.nn.log_sigmoid(fg_row)

    r = lax.broadcasted_iota(jnp.int32, (tc, tc), 0)
    s = lax.broadcasted_iota(jnp.int32, (tc, tc), 1)
    seen = (s >= r) if reverse else (s <= r)
    seen_t = (r >= s) if reverse else (r <= s)
    b_col = jnp.sum(jnp.where(seen, lf_row, 0.0), axis=1, keepdims=True)
    b_row = jnp.sum(jnp.where(seen_t, lf_col, 0.0), axis=0, keepdims=True)
    b_last = jnp.sum(lf_row, axis=1, keepdims=True)

    m_prev = m_ref[hh:hh + 1, 0:1]
    dmat = jnp.where(seen, b_col - b_row + ig_row, -jnp.inf)
    g_car = b_col + m_prev
    mt = jnp.maximum(g_car, jnp.max(dmat, axis=1, keepdims=True))
    inter = jnp.exp(g_car - mt)
    qb = q_ref[:, cols]
    kb = k_ref[:, cols]
    vb = v_ref[:, cols].astype(BF16)
    qk = lax.dot_general(qb, kb, (((1,), (1,)), ((), ())), preferred_element_type=F32)
    sc = qk * jnp.exp(dmat - mt)
    c_state = c_ref[hh]
    num = (inter * jnp.dot(qb, c_state.astype(BF16), preferred_element_type=F32)
           + jnp.dot(sc.astype(BF16), vb, preferred_element_type=F32))
    n_state = n_ref[hh:hh + 1, :]
    den = (inter * jnp.sum(qb.astype(F32) * n_state, axis=1, keepdims=True)
           + jnp.sum(sc, axis=1, keepdims=True))
    h = num / jnp.maximum(jnp.abs(den), jnp.exp(-mt))

    a_row = b_last - b_row + ig_row
    a_col = b_last - b_col + ig_col
    m_new = jnp.maximum(b_last + m_prev, jnp.max(a_row, axis=1, keepdims=True))
    decay = jnp.exp(b_last + m_prev - m_new)
    kw = kb.astype(F32) * jnp.exp(a_col - m_new)
    c_ref[hh] = decay * c_state + lax.dot_general(
        kw.astype(BF16), vb, (((0,), (0,)), ((), ())), preferred_element_type=F32)
    n_ref[hh:hh + 1, :] = decay * n_state + jnp.sum(kw, axis=0, keepdims=True)
    m_ref[hh:hh + 1, 0:1] = m_new

    if hf_ref is not None:
        h = h + hf_ref[:, cols]
        ms = jnp.mean(h * h, axis=-1, keepdims=True)
        hn = h * lax.rsqrt(ms + EPS) * gn_ref[:, cols]
        out_ref[:, cols] = hn * jax.nn.sigmoid(o_gate_ref[:, cols])
    else:
        out_ref[:, cols] = h


def _mlstm_dir(qk_planes, proj, gates, gates_t, bias, bias_t, nb, seq, reverse, h_fwd=None, gn=None):
    tc = ML_CHUNK
    nch = seq // tc
    hps = ML_HEADS_PER_STEP
    width = hps * ML_DH
    col_v = (N_CONV + D_S5) // width
    col_o = (N_CONV + D_S5 + D_ML) // width

    def chunk(b, ci):
        return b * nch + (nch - 1 - ci if reverse else ci)

    in_specs = [
        pl.BlockSpec((None, tc, width), lambda b, h, ci: (0, chunk(b, ci), h)),
        pl.BlockSpec((None, tc, width), lambda b, h, ci: (1, chunk(b, ci), h)),
        pl.BlockSpec((tc, width), lambda b, h, ci: (chunk(b, ci), col_v + h)),
        pl.BlockSpec((tc, LANE), lambda b, h, ci: (chunk(b, ci), 0)),
        pl.BlockSpec((4 * ML_HEADS, tc), lambda b, h, ci: (0, chunk(b, ci))),
        pl.BlockSpec((1, LANE), lambda b, h, ci: (0, 0)),
        pl.BlockSpec((4 * ML_HEADS, LANE), lambda b, h, ci: (0, 0)),
    ]
    args = [qk_planes, qk_planes, proj, gates, gates_t, bias, bias_t]
    epilogue = h_fwd is not None
    if epilogue:
        in_specs += [
            pl.BlockSpec((tc, width), lambda b, h, ci: (chunk(b, ci), h)),
            pl.BlockSpec((tc, width), lambda b, h, ci: (chunk(b, ci), col_o + h)),
            pl.BlockSpec((1, width), lambda b, h, ci: (0, h)),
        ]
        args += [h_fwd, proj, gn]
    return pl.pallas_call(
        functools.partial(_mlstm_body, reverse=reverse, epilogue=epilogue),
        grid=(nb, ML_HEADS // hps, nch),
        in_specs=in_specs,
        out_specs=pl.BlockSpec((tc, width), lambda b, h, ci: (chunk(b, ci), h)),
        out_shape=jax.ShapeDtypeStruct((nb * seq, D_ML), F32),
        scratch_shapes=[pltpu.VMEM((hps, ML_DH, ML_DH), F32), pltpu.VMEM((SUBLANE, ML_DH), F32),
                        pltpu.VMEM((SUBLANE, LANE), F32)],
        compiler_params=_cp("arbitrary", "arbitrary", "arbitrary"),
        name="mlstm_bwd_norm" if reverse else "mlstm_fwd",
    )(*args)


def _layer(x, nb, seq, tb, p):
    proj, gates = _proj_in(x, p["ln_g"][0:1], p["w_in"], p["w_gate"])
    gates_t = gates[:, :4 * ML_HEADS].T
    gn = p["group_norm"]
    gn_hy = gn[None, :D_HY]
    gn_s5 = gn[None, D_HY:D_HY + D_S5]
    gn_ml = gn[None, D_HY + D_S5:]

    hy_planes = _short_conv(proj, p["conv_w"], p["conv_b"][None, :], nb, seq, False)
    qk_planes = _short_conv(proj, p["conv_w"], p["conv_b"][None, :], nb, seq, True)
    kf_re, kf_im = _hyena_spectra(seq, tb, p["hy_w1"], p["hy_b1"], p["hy_freq"], p["hy_w2"],
                                  p["hy_b2"], p["hy_w3"])
    y_hy = _hyena(hy_planes, nb, seq, tb, kf_re, kf_im, p["hy_skip"], gn_hy)
    y_s5 = _s5_post(_s5_mixer(proj, nb, seq, p["s5"]), proj, p["s5_d"][None, :],
                    p["s5_glu_w"], p["s5_glu_b"][None, :], gn_s5)
    ml_args = (qk_planes, proj, gates, gates_t, p["ml_bias"], p["ml_bias_t"], nb, seq)
    h_f = _mlstm_dir(*ml_args, False)
    y_ml = _mlstm_dir(*ml_args, True, h_f, gn_ml)

    w_out = p["w_out"]
    x = _proj_out([y_hy, y_s5, y_ml], [w_out[:D_HY], w_out[D_HY:D_HY + D_S5], w_out[D_HY + D_S5:]],
                  x, p["ln_g"][1:2])
    return _mlp(x, p["ln_g"][2:3], p["mlp_w1"], p["mlp_w2"], p["ln_g"][3:4])


_PARAM_NAMES = ("ln_g", "w_in", "conv_w", "conv_b", "hy_w1", "hy_b1", "hy_freq", "hy_w2", "hy_b2",
                "hy_w3", "hy_skip", "s5_lam_re", "s5_lam_im", "s5_log_dt", "s5_b_re", "s5_b_im",
                "s5_c_re", "s5_c_im", "s5_d", "s5_glu_w", "s5_glu_b", "ml_gate_b", "group_norm",
                "w_out", "mlp_w1", "mlp_w2")


def _prepare(p):
    p = dict(p)
    w_in = p["w_in"].astype(BF16)
    p["w_in"] = w_in[:, :P_MIX]
    p["w_gate"] = jnp.pad(w_in[:, P_MIX:], ((0, 0), (0, LANE - 4 * ML_HEADS)))
    p["w_out"] = p["w_out"].astype(BF16)
    p["mlp_w1"] = p["mlp_w1"].astype(BF16)
    p["mlp_w2"] = p["mlp_w2"].astype(BF16)
    p["s5_glu_w"] = p["s5_glu_w"].astype(BF16)
    p["ml_bias"] = jnp.pad(p["ml_gate_b"].reshape(1, -1), ((0, 0), (0, LANE - 4 * ML_HEADS)))
    p["ml_bias_t"] = jnp.broadcast_to(p["ml_gate_b"].reshape(-1, 1), (4 * ML_HEADS, LANE))
    p["s5"] = _s5_params(p["s5_lam_re"], p["s5_lam_im"], p["s5_log_dt"], p["s5_b_re"],
                         p["s5_b_im"], p["s5_c_re"], p["s5_c_im"])
    return p


def kernel(x_prompt, x_sample, ln_g, w_in, conv_w, conv_b, hy_w1, hy_b1, hy_freq, hy_w2, hy_b2, hy_w3,
           hy_skip, s5_lam_re, s5_lam_im, s5_log_dt, s5_b_re, s5_b_im, s5_c_re, s5_c_im, s5_d,
           s5_glu_w, s5_glu_b, ml_gate_b, group_norm, w_out, mlp_w1, mlp_w2):
    params = (ln_g, w_in, conv_w, conv_b, hy_w1, hy_b1, hy_freq, hy_w2, hy_b2, hy_w3, hy_skip,
              s5_lam_re, s5_lam_im, s5_log_dt, s5_b_re, s5_b_im, s5_c_re, s5_c_im, s5_d,
              s5_glu_w, s5_glu_b, ml_gate_b, group_norm, w_out, mlp_w1, mlp_w2)
    streams = []
    for x in (x_prompt, x_sample):
        nb, seq, _ = x.shape
        streams.append([x.reshape(nb * seq, D_MODEL), nb, seq, _dft_tables(seq)])
    for layer in range(DEPTH):
        p = _prepare({name: arr[layer] for name, arr in zip(_PARAM_NAMES, params)})
        for st in streams:
            st[0] = _layer(st[0], st[1], st[2], st[3], p)
    return tuple(st[0].reshape(x.shape) for st, x in zip(streams, (x_prompt, x_sample)))
```

```python
import functools
import math

import jax
import jax.numpy as jnp
from jax import lax
from jax.experimental import pallas as pl
from jax.experimental.pallas import tpu as pltpu

F32 = jnp.float32
BF16 = jnp.bfloat16

D_MODEL = 2048
DEPTH = 4
D_HY = 512
D_S5 = 512
D_ML = 1024
HY_EMB = 33
HY_BANDS = 16
HY_FILT = 64
HY_FAST_DECAY = 0.3
HY_SLOW_DECAY = 1.5
HY_TARGET = 1e-2
S5_GROUP = 16
S5_G = 32
S5_P = 64
S5_STATE = S5_G * S5_P
ML_HEADS = 4
ML_DH = 256
D_FF = 4 * D_MODEL
N_CONV = 3 * D_HY + 2 * D_ML
P_IN = N_CONV + D_S5 + 2 * D_ML + 4 * ML_HEADS
P_MIX = P_IN - 4 * ML_HEADS
EPS = 1e-6

LANE = 128
SUBLANE = 8
VMEM_LIMIT_BYTES = 48 * 1024 * 1024

DFT_N2 = LANE
S5_STREAMS = SUBLANE
S5_TSTEP = 64
S5_BLOCK_GROUPS = LANE // S5_GROUP
S5_NBLOCKS = S5_G // S5_BLOCK_GROUPS
ML_CHUNK = 256
ML_HEADS_PER_STEP = 4
COL_U = N_CONV // D_S5


def _cp(*sem):
    return pltpu.CompilerParams(dimension_semantics=sem, vmem_limit_bytes=VMEM_LIMIT_BYTES)


def _tile(n, pref):
    t = min(n, pref)
    while n % t:
        t //= 2
    return t


def _rms(x, g):
    ms = jnp.mean(x * x, axis=-1, keepdims=True)
    return x * lax.rsqrt(ms + EPS) * g


def _proj_in_body(x_ref, g_ref, w_ref, wg_ref, o_ref, og_ref, xn_ref):
    @pl.when(pl.program_id(1) == 0)
    def _():
        xn = _rms(x_ref[...], g_ref[...]).astype(BF16)
        xn_ref[...] = xn
        og_ref[...] = jnp.dot(xn, wg_ref[...], preferred_element_type=F32)

    o_ref[...] = jnp.dot(xn_ref[...], w_ref[...], preferred_element_type=F32).astype(o_ref.dtype)


def _proj_in(x, g, w, w_gate):
    m, k = x.shape
    n = w.shape[1]
    tm = _tile(m, 1024)
    tn = 1024
    return pl.pallas_call(
        _proj_in_body,
        grid=(m // tm, n // tn),
        in_specs=[pl.BlockSpec((tm, k), lambda i, j: (i, 0)),
                  pl.BlockSpec((1, k), lambda i, j: (0, 0)),
                  pl.BlockSpec((k, tn), lambda i, j: (0, j)),
                  pl.BlockSpec((k, LANE), lambda i, j: (0, 0))],
        out_specs=[pl.BlockSpec((tm, tn), lambda i, j: (i, j)),
                   pl.BlockSpec((tm, LANE), lambda i, j: (i, 0))],
        out_shape=[jax.ShapeDtypeStruct((m, n), BF16), jax.ShapeDtypeStruct((m, LANE), F32)],
        scratch_shapes=[pltpu.VMEM((tm, k), BF16)],
        compiler_params=_cp("arbitrary", "arbitrary"),
        name="norm_proj_in",
    )(x, g, w, w_gate)


def _proj_out_body(a0_ref, a1_ref, a2_ref, w0_ref, w1_ref, w2_ref, r_ref, g_ref, o_ref):
    f = (jnp.dot(a0_ref[...].astype(BF16), w0_ref[...], preferred_element_type=F32)
         + jnp.dot(a1_ref[...].astype(BF16), w1_ref[...], preferred_element_type=F32)
         + jnp.dot(a2_ref[...].astype(BF16), w2_ref[...], preferred_element_type=F32))
    o_ref[...] = r_ref[...] + _rms(f, g_ref[...])


def _proj_out(parts, weights, resid, g):
    m, n = resid.shape
    tm = _tile(m, 256)
    row = lambda width: pl.BlockSpec((tm, width), lambda i: (i, 0))
    full = lambda arr: pl.BlockSpec(arr.shape, lambda i: (0, 0))
    return pl.pallas_call(
        _proj_out_body,
        grid=(m // tm,),
        in_specs=[row(a.shape[1]) for a in parts] + [full(w) for w in weights] + [row(n), full(g)],
        out_specs=row(n),
        out_shape=jax.ShapeDtypeStruct((m, n), F32),
        compiler_params=_cp("arbitrary"),
        name="proj_out_resnorm",
    )(*parts, *weights, resid, g)


def _mlp_body(x_ref, g_in_ref, w1_ref, w2_ref, g_out_ref, o_ref, xn_ref, acc_ref):
    j = pl.program_id(1)

    @pl.when(j == 0)
    def _():
        xn_ref[...] = _rms(x_ref[...], g_in_ref[...]).astype(BF16)
        acc_ref[...] = jnp.zeros_like(acc_ref)

    h = jnp.dot(xn_ref[...], w1_ref[...], preferred_element_type=F32)
    h = jnp.square(jnp.maximum(h, 0.0)).astype(BF16)
    acc_ref[...] += jnp.dot(h, w2_ref[...], preferred_element_type=F32)

    @pl.when(j == pl.num_programs(1) - 1)
    def _():
        o_ref[...] = x_ref[...] + _rms(acc_ref[...], g_out_ref[...])


def _mlp(x, g_in, w1, w2, g_out):
    m, k = x.shape
    ff = w1.shape[1]
    tm = _tile(m, 512)
    tf = 1024
    return pl.pallas_call(
        _mlp_body,
        grid=(m // tm, ff // tf),
        in_specs=[pl.BlockSpec((tm, k), lambda i, j: (i, 0)),
                  pl.BlockSpec((1, k), lambda i, j: (0, 0)),
                  pl.BlockSpec((k, tf), lambda i, j: (0, j)),
                  pl.BlockSpec((tf, k), lambda i, j: (j, 0)),
                  pl.BlockSpec((1, k), lambda i, j: (0, 0))],
        out_specs=pl.BlockSpec((tm, k), lambda i, j: (i, 0)),
        out_shape=jax.ShapeDtypeStruct((m, k), F32),
        scratch_shapes=[pltpu.VMEM((tm, k), BF16), pltpu.VMEM((tm, k), F32)],
        compiler_params=_cp("arbitrary", "arbitrary"),
        name="mlp_resnorm",
    )(x, g_in, w1, w2, g_out)


def _conv_body(x_ref, p_ref, n_ref, w_ref, b_ref, o_ref, *, nblk, qk):
    i = pl.program_id(1)
    j = pl.program_id(2)
    x = x_ref[...].astype(F32)
    tm = x.shape[0]
    row = lax.broadcasted_iota(jnp.int32, x.shape, 0)
    halo = p_ref.shape[0]
    prev_row = jnp.where(i == 0, 0.0, p_ref[halo - 1:halo, :].astype(F32))
    next_row = jnp.where(i == nblk - 1, 0.0, n_ref[0:1, :].astype(F32))
    xm = jnp.where(row == 0, prev_row, pltpu.roll(x, 1, 0))
    xp = jnp.where(row == tm - 1, next_row, pltpu.roll(x, tm - 1, 0))
    w = w_ref[...]
    y = b_ref[...] + xm * w[0:1] + x * w[1:2] + xp * w[2:3]
    if qk:
        y = y * jax.nn.sigmoid(y) * jnp.where(j >= 2, ML_DH ** -0.5, 1.0)
    o_ref[...] = y.astype(o_ref.dtype)


def _short_conv(proj, conv_w, conv_b, nb, seq, qk):
    mtot = proj.shape[0]
    tm = _tile(seq, 2048)
    nblk = seq // tm
    col0, ncol = (3, 4) if qk else (0, 3)
    ppo = 2 if qk else 1
    halo = 2 * SUBLANE
    last_halo = mtot // halo - 1
    tmh = tm // halo

    def x_map(b, i, j):
        return (b * nblk + i, col0 + j)

    def p_map(b, i, j):
        return (jnp.maximum((b * nblk + i) * tmh - 1, 0), col0 + j)

    def n_map(b, i, j):
        return (jnp.minimum((b * nblk + i + 1) * tmh, last_halo), col0 + j)

    return pl.pallas_call(
        functools.partial(_conv_body, nblk=nblk, qk=qk),
        grid=(nb, nblk, ncol),
        in_specs=[pl.BlockSpec((tm, D_HY), x_map),
                  pl.BlockSpec((halo, D_HY), p_map),
                  pl.BlockSpec((halo, D_HY), n_map),
                  pl.BlockSpec((3, D_HY), lambda b, i, j: (0, col0 + j)),
                  pl.BlockSpec((1, D_HY), lambda b, i, j: (0, col0 + j))],
        out_specs=pl.BlockSpec((None, tm, D_HY), lambda b, i, j: (j // ppo, b * nblk + i, j % ppo)),
        out_shape=jax.ShapeDtypeStruct((ncol // ppo, nb * seq, ppo * D_HY), BF16 if qk else F32),
        compiler_params=_cp("arbitrary", "arbitrary", "arbitrary"),
        name="short_conv_qk" if qk else "short_conv_hy",
    )(proj, proj, proj, conv_w, conv_b)


def _dft_tables(seq):
    n = 2 * seq
    n1 = n // DFT_N2
    n1h = n1 // 2
    n1k = n1h + 2 * SUBLANE
    k1 = jnp.arange(n1k, dtype=jnp.int32)
    t1 = jnp.arange(n1h, dtype=jnp.int32)
    kept = (k1 <= n1h).astype(F32)[:, None]
    weight = kept * jnp.where((k1 == 0) | (k1 == n1h), 1.0, 2.0)[:, None]
    ang = ((k1[:, None] * t1[None, :]) % n1).astype(F32) * (2.0 * math.pi / n1)
    lead_f = jnp.concatenate([kept * jnp.cos(ang), -kept * jnp.sin(ang)], axis=0).astype(BF16)
    lead_ic = (weight * jnp.cos(ang)).T.astype(BF16)
    lead_is = (-weight * jnp.sin(ang)).T.astype(BF16)
    t2 = jnp.arange(DFT_N2, dtype=jnp.int32)
    k2 = jnp.arange(DFT_N2, dtype=jnp.int32)
    idx = (t2[None, None, :] * (k2[None, :, None] * n1 + k1[:, None, None])) % n
    ang2 = idx.astype(F32) * (2.0 * math.pi / n)
    f_re = jnp.cos(ang2)
    f_im = -jnp.sin(ang2)
    mid_f = jnp.concatenate([jnp.concatenate([f_re, -f_im], axis=2),
                             jnp.concatenate([f_im, f_re], axis=2)], axis=1).astype(BF16)
    g_re = jnp.swapaxes(f_re, 1, 2)
    g_im = jnp.swapaxes(f_im, 1, 2)
    mid_g = jnp.concatenate([jnp.concatenate([g_re, g_im], axis=2),
                             jnp.concatenate([-g_im, g_re], axis=2)], axis=1).astype(BF16)
    return dict(n=n, n1=n1k, n1h=n1h, lead_f=lead_f, lead_ic=lead_ic, lead_is=lead_is,
                mid_f=mid_f, mid_g=mid_g)


def _hy_features(seq):
    t = jnp.linspace(0.0, 1.0, seq, dtype=F32)[:, None]
    w = 2.0 * math.pi * jnp.arange(seq, dtype=F32)[:, None] / seq
    f = jnp.linspace(1e-4, HY_BANDS - 1, HY_BANDS, dtype=F32)[None, :]
    z = jnp.concatenate([t, jnp.cos(f * w), -jnp.sin(f * w)], axis=-1)
    return jnp.pad(z, ((0, 0), (0, LANE - HY_EMB)))


def _hy_deltas():
    d = jnp.abs(jnp.linspace(math.log(HY_TARGET) / HY_SLOW_DECAY,
                             math.log(HY_TARGET) / HY_FAST_DECAY, D_HY, dtype=F32))
    return jnp.tile(d, 4)[None, :]


def _hy_filter_body(z_ref, w1_ref, b1_ref, fr_ref, w2_ref, b2_ref, w3_ref, dl_ref, h_ref, ss_ref):
    i = pl.program_id(0)
    hi = lax.Precision.HIGHEST
    z = z_ref[...]
    fr = fr_ref[...]
    h = jnp.sin(fr * (jnp.dot(z, w1_ref[...], precision=hi, preferred_element_type=F32) + b1_ref[...]))
    h = jnp.sin(fr * (jnp.dot(h, w2_ref[...], precision=hi, preferred_element_type=F32) + b2_ref[...]))
    h = jnp.dot(h.astype(BF16), w3_ref[...], preferred_element_type=F32)
    h = h * jnp.exp(-z[:, 0:1] * dl_ref[...])
    row = lax.broadcasted_iota(jnp.int32, h.shape, 0)
    col = lax.broadcasted_iota(jnp.int32, h.shape, 1)
    bwd = (col // D_HY) % 2 == 1
    h = jnp.where((row + i * h.shape[0] == 0) & bwd, 0.0, h)
    h_ref[...] = h
    part = jnp.broadcast_to(jnp.sum(h * h, axis=0, keepdims=True), ss_ref.shape)

    @pl.when(i == 0)
    def _():
        ss_ref[...] = part

    @pl.when(i > 0)
    def _():
        ss_ref[...] += part


def _hy_filter(seq, z, w1, b1, fr, w2, b2, w3, deltas):
    tm = _tile(seq, 512)
    nc = 4 * D_HY
    full = lambda shape: pl.BlockSpec(shape, lambda i: (0, 0))
    return pl.pallas_call(
        _hy_filter_body,
        grid=(seq // tm,),
        in_specs=[pl.BlockSpec((tm, LANE), lambda i: (i, 0)),
                  full((LANE, LANE)), full((1, LANE)), full((1, LANE)),
                  full((LANE, LANE)), full((1, LANE)), full((LANE, nc)), full((1, nc))],
        out_specs=[pl.BlockSpec((tm, nc), lambda i: (i, 0)), full((SUBLANE, nc))],
        out_shape=[jax.ShapeDtypeStruct((seq, nc), F32), jax.ShapeDtypeStruct((SUBLANE, nc), F32)],
        compiler_params=_cp("arbitrary"),
        name="hyena_filter",
    )(z, w1, b1, fr, w2, b2, w3, deltas)


def _lead_fwd_body(x_ref, f_ref, re_ref, im_ref):
    n1 = re_ref.shape[0]
    r = jnp.dot(f_ref[...], x_ref[...].astype(BF16), preferred_element_type=F32)
    re_ref[...] = r[:n1].astype(BF16)
    im_ref[...] = r[n1:].astype(BF16)


def _lead_fwd(x, tb, plane, nb):
    n1, n1h = tb["n1"], tb["n1h"]
    cols = x.shape[-1]
    tn = _tile(cols, 4096)
    return pl.pallas_call(
        _lead_fwd_body,
        grid=(nb, cols // tn),
        in_specs=[pl.BlockSpec((None, n1h, tn), lambda b, j: (plane, b, j)),
                  pl.BlockSpec((2 * n1, n1h), lambda b, j: (0, 0))],
        out_specs=[pl.BlockSpec((None, n1, tn), lambda b, j: (b, 0, j))] * 2,
        out_shape=[jax.ShapeDtypeStruct((nb, n1, cols), BF16)] * 2,
        compiler_params=_cp("arbitrary", "arbitrary"),
        name="hyena_lead_fwd",
    )(x, tb["lead_f"])


def _mid_spec_body(are_ref, aim_ref, f_ref, ss_ref, kr_ref, ki_ref, *, bk):
    n2 = DFT_N2
    ss = ss_ref[0:1, :]
    scale = lax.rsqrt(ss[:, :D_HY] + ss[:, D_HY:] + EPS)

    def body(i, c):
        a = jnp.concatenate([are_ref[i], aim_ref[i]], axis=0)
        x = jnp.dot(f_ref[i], a, preferred_element_type=F32)
        xr = x[:n2]
        xi = x[n2:]
        kr_ref[i] = (xr[:, :D_HY] + xr[:, D_HY:]) * scale
        ki_ref[i] = (xi[:, :D_HY] - xi[:, D_HY:]) * scale
        return c

    lax.fori_loop(0, bk, body, 0)


def _mid_spec(a_re, a_im, sumsq, tb):
    n1 = tb["n1"]
    bk = _tile(n1, 8)
    a_re = a_re.reshape(n1, DFT_N2, 4 * D_HY)
    a_im = a_im.reshape(n1, DFT_N2, 4 * D_HY)
    a_spec = pl.BlockSpec((bk, DFT_N2, 2 * D_HY), lambda o, i: (i, 0, o))
    k_spec = pl.BlockSpec((None, bk, DFT_N2, D_HY), lambda o, i: (o, i, 0, 0))
    return pl.pallas_call(
        functools.partial(_mid_spec_body, bk=bk),
        grid=(2, n1 // bk),
        in_specs=[a_spec, a_spec,
                  pl.BlockSpec((bk, 2 * DFT_N2, 2 * DFT_N2), lambda o, i: (i, 0, 0)),
                  pl.BlockSpec((SUBLANE, 2 * D_HY), lambda o, i: (0, o))],
        out_specs=[k_spec, k_spec],
        out_shape=[jax.ShapeDtypeStruct((2, n1, DFT_N2, D_HY), F32)] * 2,
        compiler_params=_cp("arbitrary", "arbitrary"),
        name="hyena_mid_spectrum",
    )(a_re, a_im, tb["mid_f"], sumsq)


def _mid_conv_body(are_ref, aim_ref, f_ref, g_ref, kr_ref, ki_ref, bre_ref, bim_ref, *, bk):
    n2 = DFT_N2

    def body(i, c):
        a = jnp.concatenate([are_ref[i], aim_ref[i]], axis=0)
        x = jnp.dot(f_ref[i], a, preferred_element_type=F32)
        xr = x[:n2]
        xi = x[n2:]
        kr = kr_ref[i]
        ki = ki_ref[i]
        p = jnp.concatenate([xr * kr - xi * ki, xr * ki + xi * kr], axis=0).astype(BF16)
        q = jnp.dot(g_ref[i], p, preferred_element_type=F32)
        bre_ref[i] = q[:n2].astype(BF16)
        bim_ref[i] = q[n2:].astype(BF16)
        return c

    lax.fori_loop(0, bk, body, 0)


def _mid_conv(a_re, a_im, kf_re, kf_im, order, tb):
    n1 = tb["n1"]
    nb = a_re.shape[0]
    bk = _tile(n1, 8)
    a_re = a_re.reshape(nb, n1, DFT_N2, D_HY)
    a_im = a_im.reshape(nb, n1, DFT_N2, D_HY)
    a_spec = pl.BlockSpec((None, bk, DFT_N2, D_HY), lambda b, i: (b, i, 0, 0))
    t_spec = pl.BlockSpec((bk, 2 * DFT_N2, 2 * DFT_N2), lambda b, i: (i, 0, 0))
    k_spec = pl.BlockSpec((None, bk, DFT_N2, D_HY), lambda b, i: (order, i, 0, 0))
    b_re, b_im = pl.pallas_call(
        functools.partial(_mid_conv_body, bk=bk),
        grid=(nb, n1 // bk),
        in_specs=[a_spec, a_spec, t_spec, t_spec, k_spec, k_spec],
        out_specs=[a_spec, a_spec],
        out_shape=[jax.ShapeDtypeStruct((nb, n1, DFT_N2, D_HY), BF16)] * 2,
        compiler_params=_cp("arbitrary", "arbitrary"),
        name="hyena_mid_conv",
    )(a_re, a_im, tb["mid_f"], tb["mid_g"], kf_re, kf_im)
    return b_re.reshape(nb, n1, DFT_N2 * D_HY), b_im.reshape(nb, n1, DFT_N2 * D_HY)


def _lead_inv_body(bre_ref, bim_ref, c_ref, s_ref, z_ref, gate_ref, skip_ref, gn_ref, o_ref, *, inv_n, last):
    y = (jnp.dot(c_ref[...], bre_ref[...], preferred_element_type=F32)
         + jnp.dot(s_ref[...], bim_ref[...], preferred_element_type=F32)) * inv_n
    out = gate_ref[...] * (y + skip_ref[...] * z_ref[...])
    if not last:
        o_ref[...] = out
    else:
        gn = gn_ref[...]
        for c in range(out.shape[1] // D_HY):
            blk = out[:, c * D_HY:(c + 1) * D_HY]
            ms = jnp.mean(blk * blk, axis=-1, keepdims=True)
            o_ref[:, c * D_HY:(c + 1) * D_HY] = blk * lax.rsqrt(ms + EPS) * gn


def _lead_inv(b_re, b_im, z, z_plane, gates, gate_plane, skip, gn, tb, last):
    n1, n1h = tb["n1"], tb["n1h"]
    nb, _, cols = b_re.shape
    tn = _tile(cols, 4096)
    skip_t = jnp.tile(skip[None, :], (1, tn // D_HY))
    b_spec = pl.BlockSpec((None, n1, tn), lambda b, j: (b, 0, j))
    t_spec = pl.BlockSpec((n1h, n1), lambda b, j: (0, 0))
    return pl.pallas_call(
        functools.partial(_lead_inv_body, inv_n=1.0 / tb["n"], last=last),
        grid=(nb, cols // tn),
        in_specs=[b_spec, b_spec, t_spec, t_spec,
                  pl.BlockSpec((None, n1h, tn), lambda b, j: (z_plane, b, j)),
                  pl.BlockSpec((None, n1h, tn), lambda b, j: (gate_plane, b, j)),
                  pl.BlockSpec((1, tn), lambda b, j: (0, 0)),
                  pl.BlockSpec((1, D_HY), lambda b, j: (0, 0))],
        out_specs=pl.BlockSpec((None, n1h, tn), lambda b, j: (0, b, j)),
        out_shape=jax.ShapeDtypeStruct((1, nb * n1h, cols), F32),
        compiler_params=_cp("arbitrary", "arbitrary"),
        name="hyena_lead_inv",
    )(b_re, b_im, tb["lead_ic"], tb["lead_is"], z, gates, skip_t, gn)


def _hyena(conv_out, nb, seq, tb, kf_re, kf_im, skip, gn):
    n1h = tb["n1h"]
    planes = conv_out.reshape(conv_out.shape[0], nb * n1h, DFT_N2 * D_HY)
    z, z_plane = planes, 0
    for order in range(2):
        a_re, a_im = _lead_fwd(z, tb, z_plane, nb)
        b_re, b_im = _mid_conv(a_re, a_im, kf_re, kf_im, order, tb)
        z = _lead_inv(b_re, b_im, z, z_plane, planes, 1 + order, skip[order], gn, tb, order == 1)
        z_plane = 0
    return z.reshape(nb * seq, D_HY)


def _hyena_spectra(seq, tb, w1, b1, fr, w2, b2, w3):
    pad_r = LANE - HY_EMB
    pad_c = LANE - HY_FILT
    hfilt, sumsq = _hy_filter(
        seq, _hy_features(seq),
        jnp.pad(w1, ((0, pad_r), (0, pad_c))), jnp.pad(b1, (0, pad_c))[None, :],
        jnp.pad(fr, (0, pad_c))[None, :], jnp.pad(w2, ((0, pad_c), (0, pad_c))),
        jnp.pad(b2, (0, pad_c))[None, :], jnp.pad(w3, ((0, pad_c), (0, 0))).astype(BF16), _hy_deltas())
    x = hfilt.reshape(1, tb["n1h"], DFT_N2 * 4 * D_HY)
    a_re, a_im = _lead_fwd(x, tb, 0, 1)
    return _mid_spec(a_re[0], a_im[0], sumsq, tb)


def _s5_scan_body(u_ref, pm_ref, bd_ref, a_ref, cd_ref, x0_ref, o_ref, bu_ref, st_ref, y_ref,
                  *, tstep, reverse, final_state):
    tb = pl.program_id(0)
    ns = S5_STREAMS
    half = S5_STATE
    cw = S5_BLOCK_GROUPS * S5_P

    @pl.when(tb == 0)
    def _():
        st_ref[...] = x0_ref[...]

    u = u_ref[...].reshape(ns * tstep, D_S5).astype(BF16)
    u_tm = jnp.dot(pm_ref[...], u, preferred_element_type=F32).astype(BF16)

    def project_in(blk):
        bu_ref[:, 2 * blk * cw:2 * (blk + 1) * cw] = jnp.dot(
            u_tm[:, blk * LANE:(blk + 1) * LANE], bd_ref[blk], preferred_element_type=F32)

    def scan(blk):
        bre = slice(2 * blk * cw, (2 * blk + 1) * cw)
        bim = slice((2 * blk + 1) * cw, (2 * blk + 2) * cw)
        sre = slice(blk * cw, (blk + 1) * cw)
        sim = slice(half + blk * cw, half + (blk + 1) * cw)
        ar = a_ref[:, sre]
        ai = a_ref[:, sim]
        xr = st_ref[:, sre]
        xi = st_ref[:, sim]
        for i in range(tstep):
            t = tstep - 1 - i if reverse else i
            rows = slice(t * ns, (t + 1) * ns)
            xr, xi = (ar * xr - ai * xi + bu_ref[rows, bre],
                      ar * xi + ai * xr + bu_ref[rows, bim])
            if not final_state:
                bu_ref[rows, bre] = xr
                bu_ref[rows, bim] = xi
        st_ref[:, sre] = xr
        st_ref[:, sim] = xi

    def project_out(blk):
        y_ref[blk] = jnp.dot(bu_ref[:, 2 * blk * cw:2 * (blk + 1) * cw].astype(BF16), cd_ref[blk],
                             preferred_element_type=F32)

    project_in(0)
    for blk in range(S5_NBLOCKS):
        if blk + 1 < S5_NBLOCKS:
            project_in(blk + 1)
        scan(blk)
        if not final_state:
            project_out(blk)

    if final_state:
        @pl.when(tb == pl.num_programs(0) - 1)
        def _():
            o_ref[...] = st_ref[...]
    else:
        for s in range(ns):
            for blk in range(S5_NBLOCKS):
                o_ref[s, :, blk * LANE:(blk + 1) * LANE] = y_ref[blk, pl.ds(s, tstep, stride=ns), :]


def _s5_scan(proj, pmat, bd, a_b, cd, x0, tlen, reverse, final_state):
    ns = S5_STREAMS
    d = 1 if reverse else 0
    tstep = pmat.shape[0] // ns
    nt = tlen // tstep
    proj3 = proj.reshape(ns, tlen, proj.shape[1])
    blk_in = S5_BLOCK_GROUPS * S5_GROUP
    blk_state = 2 * S5_BLOCK_GROUPS * S5_P

    def window(t):
        return nt - 1 - t if reverse else t

    if final_state:
        out_spec = pl.BlockSpec((ns, 2 * S5_STATE), lambda t: (0, 0))
        out_shape = jax.ShapeDtypeStruct((ns, 2 * S5_STATE), F32)
    else:
        out_spec = pl.BlockSpec((ns, tstep, D_S5), lambda t: (0, window(t), 0))
        out_shape = jax.ShapeDtypeStruct((ns, tlen, D_S5), F32)
    out = pl.pallas_call(
        functools.partial(_s5_scan_body, tstep=tstep, reverse=reverse, final_state=final_state),
        grid=(nt,),
        in_specs=[pl.BlockSpec((ns, tstep, D_S5), lambda t: (0, window(t), COL_U)),
                  pl.BlockSpec((ns * tstep, ns * tstep), lambda t: (0, 0)),
                  pl.BlockSpec((None, S5_NBLOCKS, blk_in, blk_state), lambda t: (d, 0, 0, 0)),
                  pl.BlockSpec((None, ns, 2 * S5_STATE), lambda t: (d, 0, 0)),
                  pl.BlockSpec((None, S5_NBLOCKS, blk_state, blk_in), lambda t: (d, 0, 0, 0)),
                  pl.BlockSpec((None, ns, 2 * S5_STATE), lambda t: (d, 0, 0))],
        out_specs=out_spec,
        out_shape=out_shape,
        scratch_shapes=[pltpu.VMEM((tstep * ns, 2 * S5_STATE), F32),
                        pltpu.VMEM((ns, 2 * S5_STATE), F32),
                        pltpu.VMEM((D_S5 // LANE, tstep * ns, LANE), F32)],
        compiler_params=_cp("arbitrary"),
        name="s5_end_state" if final_state else "s5_scan",
    )(proj3, pmat, bd, a_b, cd, x0)
    return out if final_state else out.reshape(ns * tlen, D_S5)


def _s5_params(lam_re, lam_im, log_dt, b_re, b_im, c_re, c_im):
    lr = jnp.minimum(lam_re, -1e-4)
    li = lam_im
    dt = jnp.exp(log_dt)[..., None]
    er = jnp.exp(lr * dt)
    a_re = er * jnp.cos(li * dt)
    a_im = er * jnp.sin(li * dt)
    den = lr * lr + li * li
    q_re = ((a_re - 1.0) * lr + a_im * li) / den
    q_im = (a_im * lr - (a_re - 1.0) * li) / den
    bb_re = q_re[..., None] * b_re - q_im[..., None] * b_im
    bb_im = q_re[..., None] * b_im + q_im[..., None] * b_re
    nbk, gb = S5_NBLOCKS, S5_BLOCK_GROUPS
    eye = jnp.eye(gb, dtype=F32)

    def in_mat(x):
        x = x.reshape(2, nbk, gb, S5_P, S5_GROUP)
        return jnp.einsum("dbgph,gk->dbghkp", x, eye).reshape(2, nbk, gb * S5_GROUP, gb * S5_P)

    def out_mat(x):
        x = x.reshape(2, nbk, gb, S5_GROUP, S5_P)
        return jnp.einsum("dbghp,gk->dbgpkh", x, eye).reshape(2, nbk, gb * S5_P, gb * S5_GROUP)

    bd = jnp.concatenate([in_mat(bb_re), in_mat(bb_im)], axis=3).astype(BF16)
    cd = jnp.concatenate([out_mat(c_re), out_mat(-c_im)], axis=2).astype(BF16)
    a_bar = jnp.concatenate([a_re.reshape(2, S5_STATE), a_im.reshape(2, S5_STATE)], axis=1)
    return (lr * dt).reshape(2, S5_STATE), (li * dt).reshape(2, S5_STATE), a_bar, bd, cd


def _s5_row_perm(tstep):
    ns = S5_STREAMS
    r = jnp.arange(ns * tstep)
    src = (r % ns) * tstep + r // ns
    return (src[:, None] == jnp.arange(ns * tstep)[None, :]).astype(BF16)


def _s5_mixer(proj, nb, seq, s5p):
    la_re, la_im, a_bar, bd, cd = s5p
    ns = S5_STREAMS
    ncs = ns // nb
    tlen = seq // ncs
    pmat = _s5_row_perm(_tile(tlen, S5_TSTEP))
    a_b = jnp.broadcast_to(a_bar[:, None, :], (2, ns, 2 * S5_STATE))
    zeros = jnp.zeros((2, ns, 2 * S5_STATE), F32)
    mag = jnp.exp(la_re * tlen)
    at_re = (mag * jnp.cos(la_im * tlen))[:, None, :]
    at_im = (mag * jnp.sin(la_im * tlen))[:, None, :]
    ys = []
    for d in range(2):
        x_end = _s5_scan(proj, pmat, bd, a_b, cd, zeros, tlen, d == 1, True)
        xe_re = x_end[:, :S5_STATE].reshape(nb, ncs, S5_STATE)
        xe_im = x_end[:, S5_STATE:].reshape(nb, ncs, S5_STATE)
        order = range(ncs) if d == 0 else range(ncs - 1, -1, -1)
        cr = jnp.zeros((nb, S5_STATE), F32)
        ci = jnp.zeros((nb, S5_STATE), F32)
        rows = [None] * ncs
        for c in order:
            rows[c] = jnp.concatenate([cr, ci], axis=-1)
            cr, ci = (at_re[d] * cr - at_im[d] * ci + xe_re[:, c],
                      at_re[d] * ci + at_im[d] * cr + xe_im[:, c])
        x0 = jnp.stack(rows, axis=1).reshape(ns, 2 * S5_STATE)
        x0 = jnp.stack([x0, x0])
        ys.append(_s5_scan(proj, pmat, bd, a_b, cd, x0, tlen, d == 1, False))
    return ys


def _s5_post_body(yf_ref, yb_ref, u_ref, d_ref, w_ref, b_ref, gn_ref, o_ref):
    y = yf_ref[...] + yb_ref[...] + d_ref[...] * u_ref[...].astype(F32)
    g = jax.nn.gelu(y)
    r = jnp.dot(g.astype(BF16), w_ref[...], preferred_element_type=F32) + b_ref[...]
    out = r[:, :D_S5] * jax.nn.sigmoid(r[:, D_S5:])
    ms = jnp.mean(out * out, axis=-1, keepdims=True)
    o_ref[...] = out * lax.rsqrt(ms + EPS) * gn_ref[...]


def _s5_post(y_f, y_b, proj, d, glu_w, glu_b, gn):
    m = y_f.shape[0]
    tm = _tile(m, 512)
    full = lambda shape: pl.BlockSpec(shape, lambda i: (0, 0))
    return pl.pallas_call(
        _s5_post_body,
        grid=(m // tm,),
        in_specs=[pl.BlockSpec((tm, D_S5), lambda i: (i, 0)),
                  pl.BlockSpec((tm, D_S5), lambda i: (i, 0)),
                  pl.BlockSpec((tm, D_S5), lambda i: (i, COL_U)),
                  full((1, D_S5)), full((D_S5, 2 * D_S5)), full((1, 2 * D_S5)), full((1, D_S5))],
        out_specs=pl.BlockSpec((tm, D_S5), lambda i: (i, 0)),
        out_shape=jax.ShapeDtypeStruct((m, D_S5), F32),
        compiler_params=_cp("arbitrary"),
        name="s5_glu_norm",
    )(y_f, y_b, proj, d, glu_w, glu_b, gn)


def _mlstm_body(*refs, reverse, epilogue):
    if epilogue:
        (q_ref, k_ref, v_ref, g_ref, gt_ref, bias_ref, biast_ref, hf_ref, o_gate_ref, gn_ref,
         out_ref, c_ref, n_ref, m_ref) = refs
    else:
        (q_ref, k_ref, v_ref, g_ref, gt_ref, bias_ref, biast_ref,
         out_ref, c_ref, n_ref, m_ref) = refs
        hf_ref = o_gate_ref = gn_ref = None

    @pl.when(pl.program_id(2) == 0)
    def _():
        c_ref[...] = jnp.zeros_like(c_ref)
        n_ref[...] = jnp.zeros_like(n_ref)
        m_ref[...] = jnp.zeros_like(m_ref)

    for hh in range(ML_HEADS_PER_STEP):
        _mlstm_head(hh, q_ref, k_ref, v_ref, g_ref, gt_ref, bias_ref, biast_ref, hf_ref, o_gate_ref,
                    gn_ref, out_ref, c_ref, n_ref, m_ref, reverse=reverse)


def _mlstm_head(hh, q_ref, k_ref, v_ref, g_ref, gt_ref, bias_ref, biast_ref, hf_ref, o_gate_ref, gn_ref,
                out_ref, c_ref, n_ref, m_ref, *, reverse):
    head = pl.program_id(1) * ML_HEADS_PER_STEP + hh
    cols = slice(hh * ML_DH, (hh + 1) * ML_DH)
    tc = ML_CHUNK
    gate_i = 2 if reverse else 0
    idx_i = gate_i * ML_HEADS + head
    idx_f = idx_i + ML_HEADS
    g = g_ref[...] + bias_ref[...]
    lane = lax.broadcasted_iota(jnp.int32, g.shape, 1)
    ig_col = jnp.sum(jnp.where(lane == idx_i, g, 0.0), axis=1, keepdims=True)
    fg_col = jnp.sum(jnp.where(lane == idx_f, g, 0.0), axis=1, keepdims=True)
    gt = gt_ref[...] + biast_ref[:, 0:1]
    sub = lax.broadcasted_iota(jnp.int32, gt.shape, 0)
    ig_row = jnp.sum(jnp.where(sub == idx_i, gt, 0.0), axis=0, keepdims=True)
    fg_row = jnp.sum(jnp.where(sub == idx_f, gt, 0.0), axis=0, keepdims=True)
    lf_col = jax.nn.log_sigmoid(fg_col)
    lf_row = jax.nn.log_sigmoid(fg_row)

    r = lax.broadcasted_iota(jnp.int32, (tc, tc), 0)
    s = lax.broadcasted_iota(jnp.int32, (tc, tc), 1)
    seen = (s >= r) if reverse else (s <= r)
    seen_t = (r >= s) if reverse else (r <= s)
    b_col = jnp.sum(jnp.where(seen, lf_row, 0.0), axis=1, keepdims=True)
    b_row = jnp.sum(jnp.where(seen_t, lf_col, 0.0), axis=0, keepdims=True)
    b_last = jnp.sum(lf_row, axis=1, keepdims=True)

    m_prev = m_ref[hh:hh + 1, 0:1]
    dmat = jnp.where(seen, b_col - b_row + ig_row, -jnp.inf)
    g_car = b_col + m_prev
    mt = jnp.maximum(g_car, jnp.max(dmat, axis=1, keepdims=True))
    inter = jnp.exp(g_car - mt)
    qb = q_ref[:, cols]
    kb = k_ref[:, cols]
    vb = v_ref[:, cols].astype(BF16)
    qk = lax.dot_general(qb, kb, (((1,), (1,)), ((), ())), preferred_element_type=F32)
    sc = qk * jnp.exp(dmat - mt)
    c_state = c_ref[hh]
    num = (inter * jnp.dot(qb, c_state.astype(BF16), preferred_element_type=F32)
           + jnp.dot(sc.astype(BF16), vb, preferred_element_type=F32))
    n_state = n_ref[hh:hh + 1, :]
    den = (inter * jnp.sum(qb.astype(F32) * n_state, axis=1, keepdims=True)
           + jnp.sum(sc, axis=1, keepdims=True))
    h = num / jnp.maximum(jnp.abs(den), jnp.exp(-mt))

    a_row = b_last - b_row + ig_row
    a_col = b_last - b_col + ig_col
    m_new = jnp.maximum(b_last + m_prev, jnp.max(a_row, axis=1, keepdims=True))
    decay = jnp.exp(b_last + m_prev - m_new)
    kw = kb.astype(F32) * jnp.exp(a_col - m_new)
    c_ref[hh] = decay * c_state + lax.dot_general(
        kw.astype(BF16), vb, (((0,), (0,)), ((), ())), preferred_element_type=F32)
    n_ref[hh:hh + 1, :] = decay * n_state + jnp.sum(kw, axis=0, keepdims=True)
    m_ref[hh:hh + 1, 0:1] = m_new

    if hf_ref is not None:
        h = h + hf_ref[:, cols]
        ms = jnp.mean(h * h, axis=-1, keepdims=True)
        hn = h * lax.rsqrt(ms + EPS) * gn_ref[:, cols]
        out_ref[:, cols] = hn * jax.nn.sigmoid(o_gate_ref[:, cols].astype(F32))
    else:
        out_ref[:, cols] = h


def _mlstm_dir(qk_planes, proj, gates, gates_t, bias, bias_t, nb, seq, reverse, h_fwd=None, gn=None):
    tc = ML_CHUNK
    nch = seq // tc
    hps = ML_HEADS_PER_STEP
    width = hps * ML_DH
    col_v = (N_CONV + D_S5) // width
    col_o = (N_CONV + D_S5 + D_ML) // width

    def chunk(b, ci):
        return b * nch + (nch - 1 - ci if reverse else ci)

    in_specs = [
        pl.BlockSpec((None, tc, width), lambda b, h, ci: (0, chunk(b, ci), h)),
        pl.BlockSpec((None, tc, width), lambda b, h, ci: (1, chunk(b, ci), h)),
        pl.BlockSpec((tc, width), lambda b, h, ci: (chunk(b, ci), col_v + h)),
        pl.BlockSpec((tc, LANE), lambda b, h, ci: (chunk(b, ci), 0)),
        pl.BlockSpec((4 * ML_HEADS, tc), lambda b, h, ci: (0, chunk(b, ci))),
        pl.BlockSpec((1, LANE), lambda b, h, ci: (0, 0)),
        pl.BlockSpec((4 * ML_HEADS, LANE), lambda b, h, ci: (0, 0)),
    ]
    args = [qk_planes, qk_planes, proj, gates, gates_t, bias, bias_t]
    epilogue = h_fwd is not None
    if epilogue:
        in_specs += [
            pl.BlockSpec((tc, width), lambda b, h, ci: (chunk(b, ci), h)),
            pl.BlockSpec((tc, width), lambda b, h, ci: (chunk(b, ci), col_o + h)),
            pl.BlockSpec((1, width), lambda b, h, ci: (0, h)),
        ]
        args += [h_fwd, proj, gn]
    return pl.pallas_call(
        functools.partial(_mlstm_body, reverse=reverse, epilogue=epilogue),
        grid=(nb, ML_HEADS // hps, nch),
        in_specs=in_specs,
        out_specs=pl.BlockSpec((tc, width), lambda b, h, ci: (chunk(b, ci), h)),
        out_shape=jax.ShapeDtypeStruct((nb * seq, D_ML), F32),
        scratch_shapes=[pltpu.VMEM((hps, ML_DH, ML_DH), F32), pltpu.VMEM((SUBLANE, ML_DH), F32),
                        pltpu.VMEM((SUBLANE, LANE), F32)],
        compiler_params=_cp("arbitrary", "arbitrary", "arbitrary"),
        name="mlstm_bwd_norm" if reverse else "mlstm_fwd",
    )(*args)


def _layer(x, nb, seq, tb, p):
    proj, gates = _proj_in(x, p["ln_g"][0:1], p["w_in"], p["w_gate"])
    gates_t = gates[:, :4 * ML_HEADS].T
    gn = p["group_norm"]
    gn_hy = gn[None, :D_HY]
    gn_s5 = gn[None, D_HY:D_HY + D_S5]
    gn_ml = gn[None, D_HY + D_S5:]

    hy_planes = _short_conv(proj, p["conv_w"], p["conv_b"][None, :], nb, seq, False)
    qk_planes = _short_conv(proj, p["conv_w"], p["conv_b"][None, :], nb, seq, True)
    kf_re, kf_im = _hyena_spectra(seq, tb, p["hy_w1"], p["hy_b1"], p["hy_freq"], p["hy_w2"],
                                  p["hy_b2"], p["hy_w3"])
    y_hy = _hyena(hy_planes, nb, seq, tb, kf_re, kf_im, p["hy_skip"], gn_hy)
    y_s5 = _s5_post(*_s5_mixer(proj, nb, seq, p["s5"]), proj, p["s5_d"][None, :],
                    p["s5_glu_w"], p["s5_glu_b"][None, :], gn_s5)
    ml_args = (qk_planes, proj, gates, gates_t, p["ml_bias"], p["ml_bias_t"], nb, seq)
    h_f = _mlstm_dir(*ml_args, False)
    y_ml = _mlstm_dir(*ml_args, True, h_f, gn_ml)

    w_out = p["w_out"]
    x = _proj_out([y_hy, y_s5, y_ml], [w_out[:D_HY], w_out[D_HY:D_HY + D_S5], w_out[D_HY + D_S5:]],
                  x, p["ln_g"][1:2])
    return _mlp(x, p["ln_g"][2:3], p["mlp_w1"], p["mlp_w2"], p["ln_g"][3:4])


_PARAM_NAMES = ("ln_g", "w_in", "conv_w", "conv_b", "hy_w1", "hy_b1", "hy_freq", "hy_w2", "hy_b2",
                "hy_w3", "hy_skip", "s5_lam_re", "s5_lam_im", "s5_log_dt", "s5_b_re", "s5_b_im",
                "s5_c_re", "s5_c_im", "s5_d", "s5_glu_w", "s5_glu_b", "ml_gate_b", "group_norm",
                "w_out", "mlp_w1", "mlp_w2")


def _prepare(p):
    p = dict(p)
    w_in = p["w_in"].astype(BF16)
    p["w_in"] = w_in[:, :P_MIX]
    p["w_gate"] = jnp.pad(w_in[:, P_MIX:], ((0, 0), (0, LANE - 4 * ML_HEADS)))
    p["w_out"] = p["w_out"].astype(BF16)
    p["mlp_w1"] = p["mlp_w1"].astype(BF16)
    p["mlp_w2"] = p["mlp_w2"].astype(BF16)
    p["s5_glu_w"] = p["s5_glu_w"].astype(BF16)
    p["ml_bias"] = jnp.pad(p["ml_gate_b"].reshape(1, -1), ((0, 0), (0, LANE - 4 * ML_HEADS)))
    p["ml_bias_t"] = jnp.broadcast_to(p["ml_gate_b"].reshape(-1, 1), (4 * ML_HEADS, LANE))
    p["s5"] = _s5_params(p["s5_lam_re"], p["s5_lam_im"], p["s5_log_dt"], p["s5_b_re"],
                         p["s5_b_im"], p["s5_c_re"], p["s5_c_im"])
    return p


def kernel(x_prompt, x_sample, ln_g, w_in, conv_w, conv_b, hy_w1, hy_b1, hy_freq, hy_w2, hy_b2, hy_w3,
           hy_skip, s5_lam_re, s5_lam_im, s5_log_dt, s5_b_re, s5_b_im, s5_c_re, s5_c_im, s5_d,
           s5_glu_w, s5_glu_b, ml_gate_b, group_norm, w_out, mlp_w1, mlp_w2):
    params = (ln_g, w_in, conv_w, conv_b, hy_w1, hy_b1, hy_freq, hy_w2, hy_b2, hy_w3, hy_skip,
              s5_lam_re, s5_lam_im, s5_log_dt, s5_b_re, s5_b_im, s5_c_re, s5_c_im, s5_d,
              s5_glu_w, s5_glu_b, ml_gate_b, group_norm, w_out, mlp_w1, mlp_w2)
    streams = []
    for x in (x_prompt, x_sample):
        nb, seq, _ = x.shape
        streams.append([x.reshape(nb * seq, D_MODEL), nb, seq, _dft_tables(seq)])
    for layer in range(DEPTH):
        p = _prepare({name: arr[layer] for name, arr in zip(_PARAM_NAMES, params)})
        for st in streams:
            st[0] = _layer(st[0], st[1], st[2], st[3], p)
    return tuple(st[0].reshape(x.shape) for st, x in zip(streams, (x_prompt, x_sample)))
```

```python
import functools
import math

import jax
import jax.numpy as jnp
from jax import lax
from jax.experimental import pallas as pl
from jax.experimental.pallas import tpu as pltpu

F32 = jnp.float32
BF16 = jnp.bfloat16

D_MODEL = 2048
DEPTH = 4
D_HY = 512
D_S5 = 512
D_ML = 1024
HY_EMB = 33
HY_BANDS = 16
HY_FILT = 64
HY_FAST_DECAY = 0.3
HY_SLOW_DECAY = 1.5
HY_TARGET = 1e-2
S5_GROUP = 16
S5_G = 32
S5_P = 64
S5_STATE = S5_G * S5_P
ML_HEADS = 4
ML_DH = 256
D_FF = 4 * D_MODEL
N_CONV = 3 * D_HY + 2 * D_ML
P_IN = N_CONV + D_S5 + 2 * D_ML + 4 * ML_HEADS
P_MIX = P_IN - 4 * ML_HEADS
EPS = 1e-6

LANE = 128
SUBLANE = 8
VMEM_LIMIT_BYTES = 48 * 1024 * 1024

DFT_N2 = LANE
S5_STREAMS = SUBLANE
S5_TSTEP = 64
S5_BLOCK_GROUPS = LANE // S5_GROUP
S5_NBLOCKS = S5_G // S5_BLOCK_GROUPS
ML_CHUNK = 256
ML_HEADS_PER_STEP = 4
COL_U = N_CONV // D_S5


def _cp(*sem):
    return pltpu.CompilerParams(dimension_semantics=sem, vmem_limit_bytes=VMEM_LIMIT_BYTES)


def _tile(n, pref):
    t = min(n, pref)
    while n % t:
        t //= 2
    return t


def _rms(x, g):
    ms = jnp.mean(x * x, axis=-1, keepdims=True)
    return x * lax.rsqrt(ms + EPS) * g


def _proj_in_body(x_ref, g_ref, w_ref, wg_ref, o_ref, og_ref, xn_ref):
    @pl.when(pl.program_id(1) == 0)
    def _():
        xn = _rms(x_ref[...], g_ref[...]).astype(BF16)
        xn_ref[...] = xn
        og_ref[...] = jnp.dot(xn, wg_ref[...], preferred_element_type=F32)

    o_ref[...] = jnp.dot(xn_ref[...], w_ref[...], preferred_element_type=F32).astype(o_ref.dtype)


def _proj_in(x, g, w, w_gate):
    m, k = x.shape
    n = w.shape[1]
    tm = _tile(m, 1024)
    tn = 1024
    return pl.pallas_call(
        _proj_in_body,
        grid=(m // tm, n // tn),
        in_specs=[pl.BlockSpec((tm, k), lambda i, j: (i, 0)),
                  pl.BlockSpec((1, k), lambda i, j: (0, 0)),
                  pl.BlockSpec((k, tn), lambda i, j: (0, j)),
                  pl.BlockSpec((k, LANE), lambda i, j: (0, 0))],
        out_specs=[pl.BlockSpec((tm, tn), lambda i, j: (i, j)),
                   pl.BlockSpec((tm, LANE), lambda i, j: (i, 0))],
        out_shape=[jax.ShapeDtypeStruct((m, n), BF16), jax.ShapeDtypeStruct((m, LANE), F32)],
        scratch_shapes=[pltpu.VMEM((tm, k), BF16)],
        compiler_params=_cp("arbitrary", "arbitrary"),
        name="norm_proj_in",
    )(x, g, w, w_gate)


def _proj_out_body(a0_ref, a1_ref, a2_ref, w0_ref, w1_ref, w2_ref, r_ref, g_ref, o_ref):
    f = (jnp.dot(a0_ref[...].astype(BF16), w0_ref[...], preferred_element_type=F32)
         + jnp.dot(a1_ref[...].astype(BF16), w1_ref[...], preferred_element_type=F32)
         + jnp.dot(a2_ref[...].astype(BF16), w2_ref[...], preferred_element_type=F32))
    o_ref[...] = r_ref[...] + _rms(f, g_ref[...])


def _proj_out(parts, weights, resid, g):
    m, n = resid.shape
    tm = _tile(m, 512)
    row = lambda width: pl.BlockSpec((tm, width), lambda i: (i, 0))
    full = lambda arr: pl.BlockSpec(arr.shape, lambda i: (0, 0))
    return pl.pallas_call(
        _proj_out_body,
        grid=(m // tm,),
        in_specs=[row(a.shape[1]) for a in parts] + [full(w) for w in weights] + [row(n), full(g)],
        out_specs=row(n),
        out_shape=jax.ShapeDtypeStruct((m, n), F32),
        compiler_params=_cp("arbitrary"),
        name="proj_out_resnorm",
    )(*parts, *weights, resid, g)


def _mlp_body(x_ref, g_in_ref, w1_ref, w2_ref, g_out_ref, o_ref, xn_ref, acc_ref):
    j = pl.program_id(1)

    @pl.when(j == 0)
    def _():
        xn_ref[...] = _rms(x_ref[...], g_in_ref[...]).astype(BF16)
        acc_ref[...] = jnp.zeros_like(acc_ref)

    h = jnp.dot(xn_ref[...], w1_ref[...], preferred_element_type=F32)
    h = jnp.square(jnp.maximum(h, 0.0)).astype(BF16)
    acc_ref[...] += jnp.dot(h, w2_ref[...], preferred_element_type=F32)

    @pl.when(j == pl.num_programs(1) - 1)
    def _():
        o_ref[...] = x_ref[...] + _rms(acc_ref[...], g_out_ref[...])


def _mlp(x, g_in, w1, w2, g_out):
    m, k = x.shape
    ff = w1.shape[1]
    tm = _tile(m, 512)
    tf = 1024
    return pl.pallas_call(
        _mlp_body,
        grid=(m // tm, ff // tf),
        in_specs=[pl.BlockSpec((tm, k), lambda i, j: (i, 0)),
                  pl.BlockSpec((1, k), lambda i, j: (0, 0)),
                  pl.BlockSpec((k, tf), lambda i, j: (0, j)),
                  pl.BlockSpec((tf, k), lambda i, j: (j, 0)),
                  pl.BlockSpec((1, k), lambda i, j: (0, 0))],
        out_specs=pl.BlockSpec((tm, k), lambda i, j: (i, 0)),
        out_shape=jax.ShapeDtypeStruct((m, k), F32),
        scratch_shapes=[pltpu.VMEM((tm, k), BF16), pltpu.VMEM((tm, k), F32)],
        compiler_params=_cp("arbitrary", "arbitrary"),
        name="mlp_resnorm",
    )(x, g_in, w1, w2, g_out)


def _conv_body(x_ref, p_ref, n_ref, w_ref, b_ref, o_ref, *, nblk, qk):
    i = pl.program_id(1)
    j = pl.program_id(2)
    x = x_ref[...].astype(F32)
    tm = x.shape[0]
    row = lax.broadcasted_iota(jnp.int32, x.shape, 0)
    halo = p_ref.shape[0]
    prev_row = jnp.where(i == 0, 0.0, p_ref[halo - 1:halo, :].astype(F32))
    next_row = jnp.where(i == nblk - 1, 0.0, n_ref[0:1, :].astype(F32))
    xm = jnp.where(row == 0, prev_row, pltpu.roll(x, 1, 0))
    xp = jnp.where(row == tm - 1, next_row, pltpu.roll(x, tm - 1, 0))
    w = w_ref[...]
    y = b_ref[...] + xm * w[0:1] + x * w[1:2] + xp * w[2:3]
    if qk:
        y = y * jax.nn.sigmoid(y) * jnp.where(j >= 2, ML_DH ** -0.5, 1.0)
    o_ref[...] = y.astype(o_ref.dtype)


def _short_conv(proj, conv_w, conv_b, nb, seq, qk):
    mtot = proj.shape[0]
    tm = _tile(seq, 2048)
    nblk = seq // tm
    col0, ncol = (3, 4) if qk else (0, 3)
    ppo = 2 if qk else 1
    halo = 2 * SUBLANE
    last_halo = mtot // halo - 1
    tmh = tm // halo

    def x_map(b, i, j):
        return (b * nblk + i, col0 + j)

    def p_map(b, i, j):
        return (jnp.maximum((b * nblk + i) * tmh - 1, 0), col0 + j)

    def n_map(b, i, j):
        return (jnp.minimum((b * nblk + i + 1) * tmh, last_halo), col0 + j)

    return pl.pallas_call(
        functools.partial(_conv_body, nblk=nblk, qk=qk),
        grid=(nb, nblk, ncol),
        in_specs=[pl.BlockSpec((tm, D_HY), x_map),
                  pl.BlockSpec((halo, D_HY), p_map),
                  pl.BlockSpec((halo, D_HY), n_map),
                  pl.BlockSpec((3, D_HY), lambda b, i, j: (0, col0 + j)),
                  pl.BlockSpec((1, D_HY), lambda b, i, j: (0, col0 + j))],
        out_specs=pl.BlockSpec((None, tm, D_HY), lambda b, i, j: (j // ppo, b * nblk + i, j % ppo)),
        out_shape=jax.ShapeDtypeStruct((ncol // ppo, nb * seq, ppo * D_HY), BF16 if qk else F32),
        compiler_params=_cp("arbitrary", "arbitrary", "arbitrary"),
        name="short_conv_qk" if qk else "short_conv_hy",
    )(proj, proj, proj, conv_w, conv_b)


def _dft_tables(seq):
    n = 2 * seq
    n1 = n // DFT_N2
    n1h = n1 // 2
    n1k = n1h + 2 * SUBLANE
    k1 = jnp.arange(n1k, dtype=jnp.int32)
    t1 = jnp.arange(n1h, dtype=jnp.int32)
    kept = (k1 <= n1h).astype(F32)[:, None]
    weight = kept * jnp.where((k1 == 0) | (k1 == n1h), 1.0, 2.0)[:, None]
    ang = ((k1[:, None] * t1[None, :]) % n1).astype(F32) * (2.0 * math.pi / n1)
    lead_f = jnp.concatenate([kept * jnp.cos(ang), -kept * jnp.sin(ang)], axis=0).astype(BF16)
    lead_ic = (weight * jnp.cos(ang)).T.astype(BF16)
    lead_is = (-weight * jnp.sin(ang)).T.astype(BF16)
    t1f = jnp.arange(n1, dtype=jnp.int32)
    angf = ((k1[:, None] * t1f[None, :]) % n1).astype(F32) * (2.0 * math.pi / n1)
    lead_ff = jnp.concatenate([kept * jnp.cos(angf), -kept * jnp.sin(angf)], axis=0).astype(BF16)
    t2 = jnp.arange(DFT_N2, dtype=jnp.int32)
    k2 = jnp.arange(DFT_N2, dtype=jnp.int32)
    idx = (t2[None, None, :] * (k2[None, :, None] * n1 + k1[:, None, None])) % n
    ang2 = idx.astype(F32) * (2.0 * math.pi / n)
    f_re = jnp.cos(ang2)
    f_im = -jnp.sin(ang2)
    mid_f = jnp.concatenate([jnp.concatenate([f_re, -f_im], axis=2),
                             jnp.concatenate([f_im, f_re], axis=2)], axis=1).astype(BF16)
    g_re = jnp.swapaxes(f_re, 1, 2)
    g_im = jnp.swapaxes(f_im, 1, 2)
    mid_g = jnp.concatenate([jnp.concatenate([g_re, g_im], axis=2),
                             jnp.concatenate([-g_im, g_re], axis=2)], axis=1).astype(BF16)
    return dict(n=n, n1=n1k, n1h=n1h, lead_f=lead_f, lead_ff=lead_ff, lead_ic=lead_ic, lead_is=lead_is,
                mid_f=mid_f, mid_g=mid_g)


def _hy_features(seq):
    t = jnp.linspace(0.0, 1.0, seq, dtype=F32)[:, None]
    w = 2.0 * math.pi * jnp.arange(seq, dtype=F32)[:, None] / seq
    f = jnp.linspace(1e-4, HY_BANDS - 1, HY_BANDS, dtype=F32)[None, :]
    z = jnp.concatenate([t, jnp.cos(f * w), -jnp.sin(f * w)], axis=-1)
    z = jnp.pad(z, ((0, 0), (0, LANE - HY_EMB)))
    slot = jnp.arange(2 * seq)
    z = z[jnp.where(slot < seq, slot, (2 * seq - slot) % seq)]
    n1 = 2 * seq // DFT_N2
    return z.reshape(n1 // SUBLANE, SUBLANE, DFT_N2, LANE).transpose(0, 2, 1, 3).reshape(2 * seq, LANE)


def _hy_deltas():
    d = jnp.abs(jnp.linspace(math.log(HY_TARGET) / HY_SLOW_DECAY,
                             math.log(HY_TARGET) / HY_FAST_DECAY, D_HY, dtype=F32))
    return jnp.tile(d, 2)[None, :]


def _hy_filter_body(z_ref, w1_ref, b1_ref, fr_ref, w2_ref, b2_ref, w3_ref, dl_ref, k_ref, ss_ref, *, half):
    i = pl.program_id(0)
    hi = lax.Precision.HIGHEST
    z = z_ref[...]
    fr = fr_ref[...]
    h = jnp.sin(fr * (jnp.dot(z, w1_ref[...], precision=hi, preferred_element_type=F32) + b1_ref[...]))
    h = jnp.sin(fr * (jnp.dot(h, w2_ref[...], precision=hi, preferred_element_type=F32) + b2_ref[...]))
    h = jnp.dot(h.astype(BF16), w3_ref[...], preferred_element_type=F32)
    h = h * jnp.exp(-z[:, 0:1] * dl_ref[...])
    row = lax.broadcasted_iota(jnp.int32, h.shape, 0)
    h = jnp.where(row + (i - half) * h.shape[0] == 0, 0.0, h)
    nc = h.shape[1]
    for t2 in range(DFT_N2):
        k_ref[:, t2 * nc:(t2 + 1) * nc] = h[t2 * SUBLANE:(t2 + 1) * SUBLANE, :]
    part = jnp.broadcast_to(jnp.sum(h * h, axis=0, keepdims=True), ss_ref.shape)

    @pl.when(i == 0)
    def _():
        ss_ref[...] = part

    @pl.when(i > 0)
    def _():
        ss_ref[...] += part


def _hy_filter(seq, z, w1, b1, fr, w2, b2, w3, deltas):
    nc = 2 * D_HY
    tm = SUBLANE * DFT_N2
    steps = 2 * seq // tm
    half = steps // 2
    full = lambda shape: pl.BlockSpec(shape, lambda i: (0, 0))
    return pl.pallas_call(
        functools.partial(_hy_filter_body, half=half),
        grid=(steps,),
        in_specs=[pl.BlockSpec((tm, LANE), lambda i: (i, 0)),
                  full((LANE, LANE)), full((1, LANE)), full((1, LANE)),
                  full((LANE, LANE)), full((1, LANE)),
                  pl.BlockSpec((None, LANE, nc), lambda i: (i // half, 0, 0)), full((1, nc))],
        out_specs=[pl.BlockSpec((SUBLANE, DFT_N2 * nc), lambda i: (i, 0)), full((SUBLANE, nc))],
        out_shape=[jax.ShapeDtypeStruct((2 * seq // DFT_N2, DFT_N2 * nc), F32),
                   jax.ShapeDtypeStruct((SUBLANE, nc), F32)],
        compiler_params=_cp("arbitrary"),
        name="hyena_filter",
    )(z, w1, b1, fr, w2, b2, w3, deltas)


def _lead_fwd_body(x_ref, f_ref, re_ref, im_ref):
    n1 = re_ref.shape[0]
    r = jnp.dot(f_ref[...], x_ref[...].astype(BF16), preferred_element_type=F32)
    re_ref[...] = r[:n1].astype(BF16)
    im_ref[...] = r[n1:].astype(BF16)


def _lead_fwd(x, table, plane, nb):
    n1, rows = table.shape[0] // 2, table.shape[1]
    cols = x.shape[-1]
    tn = _tile(cols, 4096)
    return pl.pallas_call(
        _lead_fwd_body,
        grid=(nb, cols // tn),
        in_specs=[pl.BlockSpec((None, rows, tn), lambda b, j: (plane, b, j)),
                  pl.BlockSpec((2 * n1, rows), lambda b, j: (0, 0))],
        out_specs=[pl.BlockSpec((None, n1, tn), lambda b, j: (b, 0, j))] * 2,
        out_shape=[jax.ShapeDtypeStruct((nb, n1, cols), BF16)] * 2,
        compiler_params=_cp("arbitrary", "arbitrary"),
        name="hyena_lead_fwd",
    )(x, table)


def _mid_spec_body(are_ref, aim_ref, f_ref, ss_ref, kr_ref, ki_ref, *, bk):
    n2 = DFT_N2
    scale = lax.rsqrt(ss_ref[0:1, :] + EPS)

    def body(i, c):
        a = jnp.concatenate([are_ref[i], aim_ref[i]], axis=0)
        x = jnp.dot(f_ref[i], a, preferred_element_type=F32)
        kr_ref[i] = x[:n2] * scale
        ki_ref[i] = x[n2:] * scale
        return c

    lax.fori_loop(0, bk, body, 0)


def _mid_spec(a_re, a_im, sumsq, tb):
    n1 = tb["n1"]
    bk = _tile(n1, 8)
    a_re = a_re.reshape(n1, DFT_N2, 2 * D_HY)
    a_im = a_im.reshape(n1, DFT_N2, 2 * D_HY)
    a_spec = pl.BlockSpec((bk, DFT_N2, D_HY), lambda o, i: (i, 0, o))
    k_spec = pl.BlockSpec((None, bk, DFT_N2, D_HY), lambda o, i: (o, i, 0, 0))
    return pl.pallas_call(
        functools.partial(_mid_spec_body, bk=bk),
        grid=(2, n1 // bk),
        in_specs=[a_spec, a_spec,
                  pl.BlockSpec((bk, 2 * DFT_N2, 2 * DFT_N2), lambda o, i: (i, 0, 0)),
                  pl.BlockSpec((SUBLANE, D_HY), lambda o, i: (0, o))],
        out_specs=[k_spec, k_spec],
        out_shape=[jax.ShapeDtypeStruct((2, n1, DFT_N2, D_HY), F32)] * 2,
        compiler_params=_cp("arbitrary", "arbitrary"),
        name="hyena_mid_spectrum",
    )(a_re, a_im, tb["mid_f"], sumsq)


def _mid_conv_body(are_ref, aim_ref, f_ref, g_ref, kr_ref, ki_ref, bre_ref, bim_ref, *, bk):
    n2 = DFT_N2

    def body(i, c):
        a = jnp.concatenate([are_ref[i], aim_ref[i]], axis=0)
        x = jnp.dot(f_ref[i], a, preferred_element_type=F32)
        xr = x[:n2]
        xi = x[n2:]
        kr = kr_ref[i]
        ki = ki_ref[i]
        p = jnp.concatenate([xr * kr - xi * ki, xr * ki + xi * kr], axis=0).astype(BF16)
        q = jnp.dot(g_ref[i], p, preferred_element_type=F32)
        bre_ref[i] = q[:n2].astype(BF16)
        bim_ref[i] = q[n2:].astype(BF16)
        return c

    lax.fori_loop(0, bk, body, 0)


def _mid_conv(a_re, a_im, kf_re, kf_im, order, tb):
    n1 = tb["n1"]
    nb = a_re.shape[0]
    bk = _tile(n1, 8)
    a_re = a_re.reshape(nb, n1, DFT_N2, D_HY)
    a_im = a_im.reshape(nb, n1, DFT_N2, D_HY)
    a_spec = pl.BlockSpec((None, bk, DFT_N2, D_HY), lambda b, i: (b, i, 0, 0))
    t_spec = pl.BlockSpec((bk, 2 * DFT_N2, 2 * DFT_N2), lambda b, i: (i, 0, 0))
    k_spec = pl.BlockSpec((None, bk, DFT_N2, D_HY), lambda b, i: (order, i, 0, 0))
    b_re, b_im = pl.pallas_call(
        functools.partial(_mid_conv_body, bk=bk),
        grid=(nb, n1 // bk),
        in_specs=[a_spec, a_spec, t_spec, t_spec, k_spec, k_spec],
        out_specs=[a_spec, a_spec],
        out_shape=[jax.ShapeDtypeStruct((nb, n1, DFT_N2, D_HY), BF16)] * 2,
        compiler_params=_cp("arbitrary", "arbitrary"),
        name="hyena_mid_conv",
    )(a_re, a_im, tb["mid_f"], tb["mid_g"], kf_re, kf_im)
    return b_re.reshape(nb, n1, DFT_N2 * D_HY), b_im.reshape(nb, n1, DFT_N2 * D_HY)


def _lead_inv_body(bre_ref, bim_ref, c_ref, s_ref, z_ref, gate_ref, skip_ref, gn_ref, o_ref, *, inv_n, last):
    y = (jnp.dot(c_ref[...], bre_ref[...], preferred_element_type=F32)
         + jnp.dot(s_ref[...], bim_ref[...], preferred_element_type=F32)) * inv_n
    out = gate_ref[...] * (y + skip_ref[...] * z_ref[...])
    if not last:
        o_ref[...] = out
    else:
        gn = gn_ref[...]
        for c in range(out.shape[1] // D_HY):
            blk = out[:, c * D_HY:(c + 1) * D_HY]
            ms = jnp.mean(blk * blk, axis=-1, keepdims=True)
            o_ref[:, c * D_HY:(c + 1) * D_HY] = (blk * lax.rsqrt(ms + EPS) * gn).astype(o_ref.dtype)


def _lead_inv(b_re, b_im, z, z_plane, gates, gate_plane, skip, gn, tb, last):
    n1, n1h = tb["n1"], tb["n1h"]
    nb, _, cols = b_re.shape
    tn = _tile(cols, 4096)
    skip_t = jnp.tile(skip[None, :], (1, tn // D_HY))
    b_spec = pl.BlockSpec((None, n1, tn), lambda b, j: (b, 0, j))
    t_spec = pl.BlockSpec((n1h, n1), lambda b, j: (0, 0))
    return pl.pallas_call(
        functools.partial(_lead_inv_body, inv_n=1.0 / tb["n"], last=last),
        grid=(nb, cols // tn),
        in_specs=[b_spec, b_spec, t_spec, t_spec,
                  pl.BlockSpec((None, n1h, tn), lambda b, j: (z_plane, b, j)),
                  pl.BlockSpec((None, n1h, tn), lambda b, j: (gate_plane, b, j)),
                  pl.BlockSpec((1, tn), lambda b, j: (0, 0)),
                  pl.BlockSpec((1, D_HY), lambda b, j: (0, 0))],
        out_specs=pl.BlockSpec((None, n1h, tn), lambda b, j: (0, b, j)),
        out_shape=jax.ShapeDtypeStruct((1, nb * n1h, cols), BF16 if last else F32),
        compiler_params=_cp("arbitrary", "arbitrary"),
        name="hyena_lead_inv",
    )(b_re, b_im, tb["lead_ic"], tb["lead_is"], z, gates, skip_t, gn)


def _hyena(conv_out, nb, seq, tb, kf_re, kf_im, skip, gn):
    n1h = tb["n1h"]
    planes = conv_out.reshape(conv_out.shape[0], nb * n1h, DFT_N2 * D_HY)
    z, z_plane = planes, 0
    for order in range(2):
        a_re, a_im = _lead_fwd(z, tb["lead_f"], z_plane, nb)
        b_re, b_im = _mid_conv(a_re, a_im, kf_re, kf_im, order, tb)
        z = _lead_inv(b_re, b_im, z, z_plane, planes, 1 + order, skip[order], gn, tb, order == 1)
        z_plane = 0
    return z.reshape(nb * seq, D_HY)


def _hyena_spectra(seq, tb, w1, b1, fr, w2, b2, w3):
    pad_r = LANE - HY_EMB
    pad_c = LANE - HY_FILT
    w3 = w3.reshape(HY_FILT, 2, 2, D_HY).transpose(2, 0, 1, 3).reshape(2, HY_FILT, 2 * D_HY)
    taps, sumsq = _hy_filter(
        seq, _hy_features(seq),
        jnp.pad(w1, ((0, pad_r), (0, pad_c))), jnp.pad(b1, (0, pad_c))[None, :],
        jnp.pad(fr, (0, pad_c))[None, :], jnp.pad(w2, ((0, pad_c), (0, pad_c))),
        jnp.pad(b2, (0, pad_c))[None, :], jnp.pad(w3, ((0, 0), (0, pad_c), (0, 0))).astype(BF16),
        _hy_deltas())
    a_re, a_im = _lead_fwd(taps[None], tb["lead_ff"], 0, 1)
    return _mid_spec(a_re[0], a_im[0], sumsq, tb)


def _s5_scan_body(u_ref, pm_ref, bd_ref, a_ref, cd_ref, x0_ref, o_ref, bu_ref, st_ref, y_ref,
                  *, tstep, reverse, final_state):
    tb = pl.program_id(0)
    ns = S5_STREAMS
    half = S5_STATE
    cw = S5_BLOCK_GROUPS * S5_P

    @pl.when(tb == 0)
    def _():
        st_ref[...] = x0_ref[...]

    u = u_ref[...].reshape(ns * tstep, D_S5).astype(BF16)
    u_tm = jnp.dot(pm_ref[...], u, preferred_element_type=F32).astype(BF16)

    def project_in(blk):
        bu_ref[:, 2 * blk * cw:2 * (blk + 1) * cw] = jnp.dot(
            u_tm[:, blk * LANE:(blk + 1) * LANE], bd_ref[blk], preferred_element_type=F32)

    def scan(blk):
        bre = slice(2 * blk * cw, (2 * blk + 1) * cw)
        bim = slice((2 * blk + 1) * cw, (2 * blk + 2) * cw)
        sre = slice(blk * cw, (blk + 1) * cw)
        sim = slice(half + blk * cw, half + (blk + 1) * cw)
        ar = a_ref[:, sre]
        ai = a_ref[:, sim]
        xr = st_ref[:, sre]
        xi = st_ref[:, sim]
        for i in range(tstep):
            t = tstep - 1 - i if reverse else i
            rows = slice(t * ns, (t + 1) * ns)
            xr, xi = (ar * xr - ai * xi + bu_ref[rows, bre],
                      ar * xi + ai * xr + bu_ref[rows, bim])
            if not final_state:
                bu_ref[rows, bre] = xr
                bu_ref[rows, bim] = xi
        st_ref[:, sre] = xr
        st_ref[:, sim] = xi

    def project_out(blk):
        y_ref[blk] = jnp.dot(bu_ref[:, 2 * blk * cw:2 * (blk + 1) * cw].astype(BF16), cd_ref[blk],
                             preferred_element_type=F32)

    project_in(0)
    for blk in range(S5_NBLOCKS):
        if blk + 1 < S5_NBLOCKS:
            project_in(blk + 1)
        scan(blk)
        if not final_state:
            project_out(blk)

    if final_state:
        @pl.when(tb == pl.num_programs(0) - 1)
        def _():
            o_ref[...] = st_ref[...]
    else:
        for s in range(ns):
            for blk in range(S5_NBLOCKS):
                o_ref[s, :, blk * LANE:(blk + 1) * LANE] = y_ref[blk, pl.ds(s, tstep, stride=ns), :]


def _s5_scan(proj, pmat, bd, a_b, cd, x0, tlen, reverse, final_state):
    ns = S5_STREAMS
    d = 1 if reverse else 0
    tstep = pmat.shape[0] // ns
    nt = tlen // tstep
    proj3 = proj.reshape(ns, tlen, proj.shape[1])
    blk_in = S5_BLOCK_GROUPS * S5_GROUP
    blk_state = 2 * S5_BLOCK_GROUPS * S5_P

    def window(t):
        return nt - 1 - t if reverse else t

    if final_state:
        out_spec = pl.BlockSpec((ns, 2 * S5_STATE), lambda t: (0, 0))
        out_shape = jax.ShapeDtypeStruct((ns, 2 * S5_STATE), F32)
    else:
        out_spec = pl.BlockSpec((ns, tstep, D_S5), lambda t: (0, window(t), 0))
        out_shape = jax.ShapeDtypeStruct((ns, tlen, D_S5), F32)
    out = pl.pallas_call(
        functools.partial(_s5_scan_body, tstep=tstep, reverse=reverse, final_state=final_state),
        grid=(nt,),
        in_specs=[pl.BlockSpec((ns, tstep, D_S5), lambda t: (0, window(t), COL_U)),
                  pl.BlockSpec((ns * tstep, ns * tstep), lambda t: (0, 0)),
                  pl.BlockSpec((None, S5_NBLOCKS, blk_in, blk_state), lambda t: (d, 0, 0, 0)),
                  pl.BlockSpec((None, ns, 2 * S5_STATE), lambda t: (d, 0, 0)),
                  pl.BlockSpec((None, S5_NBLOCKS, blk_state, blk_in), lambda t: (d, 0, 0, 0)),
                  pl.BlockSpec((None, ns, 2 * S5_STATE), lambda t: (d, 0, 0))],
        out_specs=out_spec,
        out_shape=out_shape,
        scratch_shapes=[pltpu.VMEM((tstep * ns, 2 * S5_STATE), F32),
                        pltpu.VMEM((ns, 2 * S5_STATE), F32),
                        pltpu.VMEM((D_S5 // LANE, tstep * ns, LANE), F32)],
        compiler_params=_cp("arbitrary"),
        name="s5_end_state" if final_state else "s5_scan",
    )(proj3, pmat, bd, a_b, cd, x0)
    return out if final_state else out.reshape(ns * tlen, D_S5)


def _s5_params(lam_re, lam_im, log_dt, b_re, b_im, c_re, c_im):
    lr = jnp.minimum(lam_re, -1e-4)
    li = lam_im
    dt = jnp.exp(log_dt)[..., None]
    er = jnp.exp(lr * dt)
    a_re = er * jnp.cos(li * dt)
    a_im = er * jnp.sin(li * dt)
    den = lr * lr + li * li
    q_re = ((a_re - 1.0) * lr + a_im * li) / den
    q_im = (a_im * lr - (a_re - 1.0) * li) / den
    bb_re = q_re[..., None] * b_re - q_im[..., None] * b_im
    bb_im = q_re[..., None] * b_im + q_im[..., None] * b_re
    nbk, gb = S5_NBLOCKS, S5_BLOCK_GROUPS
    eye = jnp.eye(gb, dtype=F32)

    def in_mat(x):
        x = x.reshape(2, nbk, gb, S5_P, S5_GROUP)
        return jnp.einsum("dbgph,gk->dbghkp", x, eye).reshape(2, nbk, gb * S5_GROUP, gb * S5_P)

    def out_mat(x):
        x = x.reshape(2, nbk, gb, S5_GROUP, S5_P)
        return jnp.einsum("dbghp,gk->dbgpkh", x, eye).reshape(2, nbk, gb * S5_P, gb * S5_GROUP)

    bd = jnp.concatenate([in_mat(bb_re), in_mat(bb_im)], axis=3).astype(BF16)
    cd = jnp.concatenate([out_mat(c_re), out_mat(-c_im)], axis=2).astype(BF16)
    a_bar = jnp.concatenate([a_re.reshape(2, S5_STATE), a_im.reshape(2, S5_STATE)], axis=1)
    return (lr * dt).reshape(2, S5_STATE), (li * dt).reshape(2, S5_STATE), a_bar, bd, cd


def _s5_row_perm(tstep):
    ns = S5_STREAMS
    r = jnp.arange(ns * tstep)
    src = (r % ns) * tstep + r // ns
    return (src[:, None] == jnp.arange(ns * tstep)[None, :]).astype(BF16)


def _s5_mixer(proj, nb, seq, s5p):
    la_re, la_im, a_bar, bd, cd = s5p
    ns = S5_STREAMS
    ncs = ns // nb
    tlen = seq // ncs
    pmat = _s5_row_perm(_tile(tlen, S5_TSTEP))
    a_b = jnp.broadcast_to(a_bar[:, None, :], (2, ns, 2 * S5_STATE))
    zeros = jnp.zeros((2, ns, 2 * S5_STATE), F32)
    mag = jnp.exp(la_re * tlen)
    at_re = (mag * jnp.cos(la_im * tlen))[:, None, :]
    at_im = (mag * jnp.sin(la_im * tlen))[:, None, :]
    ys = []
    for d in range(2):
        x_end = _s5_scan(proj, pmat, bd, a_b, cd, zeros, tlen, d == 1, True)
        xe_re = x_end[:, :S5_STATE].reshape(nb, ncs, S5_STATE)
        xe_im = x_end[:, S5_STATE:].reshape(nb, ncs, S5_STATE)
        order = range(ncs) if d == 0 else range(ncs - 1, -1, -1)
        cr = jnp.zeros((nb, S5_STATE), F32)
        ci = jnp.zeros((nb, S5_STATE), F32)
        rows = [None] * ncs
        for c in order:
            rows[c] = jnp.concatenate([cr, ci], axis=-1)
            cr, ci = (at_re[d] * cr - at_im[d] * ci + xe_re[:, c],
                      at_re[d] * ci + at_im[d] * cr + xe_im[:, c])
        x0 = jnp.stack(rows, axis=1).reshape(ns, 2 * S5_STATE)
        x0 = jnp.stack([x0, x0])
        ys.append(_s5_scan(proj, pmat, bd, a_b, cd, x0, tlen, d == 1, False))
    return ys


def _s5_post_body(yf_ref, yb_ref, u_ref, d_ref, w_ref, b_ref, gn_ref, o_ref):
    y = yf_ref[...] + yb_ref[...] + d_ref[...] * u_ref[...].astype(F32)
    g = jax.nn.gelu(y)
    r = jnp.dot(g.astype(BF16), w_ref[...], preferred_element_type=F32) + b_ref[...]
    out = r[:, :D_S5] * jax.nn.sigmoid(r[:, D_S5:])
    ms = jnp.mean(out * out, axis=-1, keepdims=True)
    o_ref[...] = (out * lax.rsqrt(ms + EPS) * gn_ref[...]).astype(o_ref.dtype)


def _s5_post(y_f, y_b, proj, d, glu_w, glu_b, gn):
    m = y_f.shape[0]
    tm = _tile(m, 512)
    full = lambda shape: pl.BlockSpec(shape, lambda i: (0, 0))
    return pl.pallas_call(
        _s5_post_body,
        grid=(m // tm,),
        in_specs=[pl.BlockSpec((tm, D_S5), lambda i: (i, 0)),
                  pl.BlockSpec((tm, D_S5), lambda i: (i, 0)),
                  pl.BlockSpec((tm, D_S5), lambda i: (i, COL_U)),
                  full((1, D_S5)), full((D_S5, 2 * D_S5)), full((1, 2 * D_S5)), full((1, D_S5))],
        out_specs=pl.BlockSpec((tm, D_S5), lambda i: (i, 0)),
        out_shape=jax.ShapeDtypeStruct((m, D_S5), BF16),
        compiler_params=_cp("arbitrary"),
        name="s5_glu_norm",
    )(y_f, y_b, proj, d, glu_w, glu_b, gn)


def _mlstm_body(*refs, reverse, epilogue):
    if epilogue:
        (q_ref, k_ref, v_ref, g_ref, gt_ref, bias_ref, biast_ref, hf_ref, o_gate_ref, gn_ref,
         out_ref, c_ref, n_ref, m_ref) = refs
    else:
        (q_ref, k_ref, v_ref, g_ref, gt_ref, bias_ref, biast_ref,
         out_ref, c_ref, n_ref, m_ref) = refs
        hf_ref = o_gate_ref = gn_ref = None

    @pl.when(pl.program_id(2) == 0)
    def _():
        c_ref[...] = jnp.zeros_like(c_ref)
        n_ref[...] = jnp.zeros_like(n_ref)
        m_ref[...] = jnp.zeros_like(m_ref)

    for hh in range(ML_HEADS_PER_STEP):
        _mlstm_head(hh, q_ref, k_ref, v_ref, g_ref, gt_ref, bias_ref, biast_ref, hf_ref, o_gate_ref,
                    gn_ref, out_ref, c_ref, n_ref, m_ref, reverse=reverse)


def _mlstm_head(hh, q_ref, k_ref, v_ref, g_ref, gt_ref, bias_ref, biast_ref, hf_ref, o_gate_ref, gn_ref,
                out_ref, c_ref, n_ref, m_ref, *, reverse):
    head = pl.program_id(1) * ML_HEADS_PER_STEP + hh
    cols = slice(hh * ML_DH, (hh + 1) * ML_DH)
    tc = ML_CHUNK
    gate_i = 2 if reverse else 0
    idx_i = gate_i * ML_HEADS + head
    idx_f = idx_i + ML_HEADS
    g = g_ref[...] + bias_ref[...]
    lane = lax.broadcasted_iota(jnp.int32, g.shape, 1)
    ig_col = jnp.sum(jnp.where(lane == idx_i, g, 0.0), axis=1, keepdims=True)
    fg_col = jnp.sum(jnp.where(lane == idx_f, g, 0.0), axis=1, keepdims=True)
    gt = gt_ref[...] + biast_ref[:, 0:1]
    sub = lax.broadcasted_iota(jnp.int32, gt.shape, 0)
    ig_row = jnp.sum(jnp.where(sub == idx_i, gt, 0.0), axis=0, keepdims=True)
    fg_row = jnp.sum(jnp.where(sub == idx_f, gt, 0.0), axis=0, keepdims=True)
    lf_col = jax.nn.log_sigmoid(fg_col)
    lf_row = jax.nn.log_sigmoid(fg_row)

    r = lax.broadcasted_iota(jnp.int32, (tc, tc), 0)
    s = lax.broadcasted_iota(jnp.int32, (tc, tc), 1)
    seen = (s >= r) if reverse else (s <= r)
    seen_t = (r >= s) if reverse else (r <= s)
    b_col = jnp.sum(jnp.where(seen, lf_row, 0.0), axis=1, keepdims=True)
    b_row = jnp.sum(jnp.where(seen_t, lf_col, 0.0), axis=0, keepdims=True)
    b_last = jnp.sum(lf_row, axis=1, keepdims=True)

    m_prev = m_ref[hh:hh + 1, 0:1]
    dmat = jnp.where(seen, b_col - b_row + ig_row, -jnp.inf)
    g_car = b_col + m_prev
    mt = jnp.maximum(g_car, jnp.max(dmat, axis=1, keepdims=True))
    inter = jnp.exp(g_car - mt)
    qb = q_ref[:, cols]
    kb = k_ref[:, cols]
    vb = v_ref[:, cols].astype(BF16)
    qk = lax.dot_general(qb, kb, (((1,), (1,)), ((), ())), preferred_element_type=F32)
    sc = qk * jnp.exp(dmat - mt)
    c_state = c_ref[hh]
    num = (inter * jnp.dot(qb, c_state.astype(BF16), preferred_element_type=F32)
           + jnp.dot(sc.astype(BF16), vb, preferred_element_type=F32))
    n_state = n_ref[hh:hh + 1, :]
    den = (inter * jnp.sum(qb.astype(F32) * n_state, axis=1, keepdims=True)
           + jnp.sum(sc, axis=1, keepdims=True))
    h = num / jnp.maximum(jnp.abs(den), jnp.exp(-mt))

    a_row = b_last - b_row + ig_row
    a_col = b_last - b_col + ig_col
    m_new = jnp.maximum(b_last + m_prev, jnp.max(a_row, axis=1, keepdims=True))
    decay = jnp.exp(b_last + m_prev - m_new)
    kw = kb.astype(F32) * jnp.exp(a_col - m_new)
    c_ref[hh] = decay * c_state + lax.dot_general(
        kw.astype(BF16), vb, (((0,), (0,)), ((), ())), preferred_element_type=F32)
    n_ref[hh:hh + 1, :] = decay * n_state + jnp.sum(kw, axis=0, keepdims=True)
    m_ref[hh:hh + 1, 0:1] = m_new

    if hf_ref is not None:
        h = h + hf_ref[:, cols]
        ms = jnp.mean(h * h, axis=-1, keepdims=True)
        hn = h * lax.rsqrt(ms + EPS) * gn_ref[:, cols]
        out_ref[:, cols] = (hn * jax.nn.sigmoid(o_gate_ref[:, cols].astype(F32))).astype(out_ref.dtype)
    else:
        out_ref[:, cols] = h


def _mlstm_dir(qk_planes, proj, gates, gates_t, bias, bias_t, nb, seq, reverse, h_fwd=None, gn=None):
    tc = ML_CHUNK
    nch = seq // tc
    hps = ML_HEADS_PER_STEP
    width = hps * ML_DH
    col_v = (N_CONV + D_S5) // width
    col_o = (N_CONV + D_S5 + D_ML) // width

    def chunk(b, ci):
        return b * nch + (nch - 1 - ci if reverse else ci)

    in_specs = [
        pl.BlockSpec((None, tc, width), lambda b, h, ci: (0, chunk(b, ci), h)),
        pl.BlockSpec((None, tc, width), lambda b, h, ci: (1, chunk(b, ci), h)),
        pl.BlockSpec((tc, width), lambda b, h, ci: (chunk(b, ci), col_v + h)),
        pl.BlockSpec((tc, LANE), lambda b, h, ci: (chunk(b, ci), 0)),
        pl.BlockSpec((4 * ML_HEADS, tc), lambda b, h, ci: (0, chunk(b, ci))),
        pl.BlockSpec((1, LANE), lambda b, h, ci: (0, 0)),
        pl.BlockSpec((4 * ML_HEADS, LANE), lambda b, h, ci: (0, 0)),
    ]
    args = [qk_planes, qk_planes, proj, gates, gates_t, bias, bias_t]
    epilogue = h_fwd is not None
    if epilogue:
        in_specs += [
            pl.BlockSpec((tc, width), lambda b, h, ci: (chunk(b, ci), h)),
            pl.BlockSpec((tc, width), lambda b, h, ci: (chunk(b, ci), col_o + h)),
            pl.BlockSpec((1, width), lambda b, h, ci: (0, h)),
        ]
        args += [h_fwd, proj, gn]
    return pl.pallas_call(
        functools.partial(_mlstm_body, reverse=reverse, epilogue=epilogue),
        grid=(nb, ML_HEADS // hps, nch),
        in_specs=in_specs,
        out_specs=pl.BlockSpec((tc, width), lambda b, h, ci: (chunk(b, ci), h)),
        out_shape=jax.ShapeDtypeStruct((nb * seq, D_ML), BF16 if epilogue else F32),
        scratch_shapes=[pltpu.VMEM((hps, ML_DH, ML_DH), F32), pltpu.VMEM((SUBLANE, ML_DH), F32),
                        pltpu.VMEM((SUBLANE, LANE), F32)],
        compiler_params=_cp("arbitrary", "arbitrary", "arbitrary"),
        name="mlstm_bwd_norm" if reverse else "mlstm_fwd",
    )(*args)


def _layer(x, nb, seq, tb, p):
    proj, gates = _proj_in(x, p["ln_g"][0:1], p["w_in"], p["w_gate"])
    gates_t = gates[:, :4 * ML_HEADS].T
    gn = p["group_norm"]
    gn_hy = gn[None, :D_HY]
    gn_s5 = gn[None, D_HY:D_HY + D_S5]
    gn_ml = gn[None, D_HY + D_S5:]

    hy_planes = _short_conv(proj, p["conv_w"], p["conv_b"][None, :], nb, seq, False)
    qk_planes = _short_conv(proj, p["conv_w"], p["conv_b"][None, :], nb, seq, True)
    kf_re, kf_im = _hyena_spectra(seq, tb, p["hy_w1"], p["hy_b1"], p["hy_freq"], p["hy_w2"],
                                  p["hy_b2"], p["hy_w3"])
    y_hy = _hyena(hy_planes, nb, seq, tb, kf_re, kf_im, p["hy_skip"], gn_hy)
    y_s5 = _s5_post(*_s5_mixer(proj, nb, seq, p["s5"]), proj, p["s5_d"][None, :],
                    p["s5_glu_w"], p["s5_glu_b"][None, :], gn_s5)
    ml_args = (qk_planes, proj, gates, gates_t, p["ml_bias"], p["ml_bias_t"], nb, seq)
    h_f = _mlstm_dir(*ml_args, False)
    y_ml = _mlstm_dir(*ml_args, True, h_f, gn_ml)

    w_out = p["w_out"]
    x = _proj_out([y_hy, y_s5, y_ml], [w_out[:D_HY], w_out[D_HY:D_HY + D_S5], w_out[D_HY + D_S5:]],
                  x, p["ln_g"][1:2])
    return _mlp(x, p["ln_g"][2:3], p["mlp_w1"], p["mlp_w2"], p["ln_g"][3:4])


_PARAM_NAMES = ("ln_g", "w_in", "conv_w", "conv_b", "hy_w1", "hy_b1", "hy_freq", "hy_w2", "hy_b2",
                "hy_w3", "hy_skip", "s5_lam_re", "s5_lam_im", "s5_log_dt", "s5_b_re", "s5_b_im",
                "s5_c_re", "s5_c_im", "s5_d", "s5_glu_w", "s5_glu_b", "ml_gate_b", "group_norm",
                "w_out", "mlp_w1", "mlp_w2")


def _prepare(p):
    p = dict(p)
    w_in = p["w_in"].astype(BF16)
    p["w_in"] = w_in[:, :P_MIX]
    p["w_gate"] = jnp.pad(w_in[:, P_MIX:], ((0, 0), (0, LANE - 4 * ML_HEADS)))
    p["w_out"] = p["w_out"].astype(BF16)
    p["mlp_w1"] = p["mlp_w1"].astype(BF16)
    p["mlp_w2"] = p["mlp_w2"].astype(BF16)
    p["s5_glu_w"] = p["s5_glu_w"].astype(BF16)
    p["ml_bias"] = jnp.pad(p["ml_gate_b"].reshape(1, -1), ((0, 0), (0, LANE - 4 * ML_HEADS)))
    p["ml_bias_t"] = jnp.broadcast_to(p["ml_gate_b"].reshape(-1, 1), (4 * ML_HEADS, LANE))
    p["s5"] = _s5_params(p["s5_lam_re"], p["s5_lam_im"], p["s5_log_dt"], p["s5_b_re"],
                         p["s5_b_im"], p["s5_c_re"], p["s5_c_im"])
    return p


def kernel(x_prompt, x_sample, ln_g, w_in, conv_w, conv_b, hy_w1, hy_b1, hy_freq, hy_w2, hy_b2, hy_w3,
           hy_skip, s5_lam_re, s5_lam_im, s5_log_dt, s5_b_re, s5_b_im, s5_c_re, s5_c_im, s5_d,
           s5_glu_w, s5_glu_b, ml_gate_b, group_norm, w_out, mlp_w1, mlp_w2):
    params = (ln_g, w_in, conv_w, conv_b, hy_w1, hy_b1, hy_freq, hy_w2, hy_b2, hy_w3, hy_skip,
              s5_lam_re, s5_lam_im, s5_log_dt, s5_b_re, s5_b_im, s5_c_re, s5_c_im, s5_d,
              s5_glu_w, s5_glu_b, ml_gate_b, group_norm, w_out, mlp_w1, mlp_w2)
    streams = []
    for x in (x_prompt, x_sample):
        nb, seq, _ = x.shape
        streams.append([x.reshape(nb * seq, D_MODEL), nb, seq, _dft_tables(seq)])
    for layer in range(DEPTH):
        p = _prepare({name: arr[layer] for name, arr in zip(_PARAM_NAMES, params)})
        for st in streams:
            st[0] = _layer(st[0], st[1], st[2], st[3], p)
    return tuple(st[0].reshape(x.shape) for st, x in zip(streams, (x_prompt, x_sample)))
```

```python
import functools
import math

import jax
import jax.numpy as jnp
from jax import lax
from jax.experimental import pallas as pl
from jax.experimental.pallas import tpu as pltpu

F32 = jnp.float32
BF16 = jnp.bfloat16

D_MODEL = 2048
DEPTH = 4
D_HY = 512
D_S5 = 512
D_ML = 1024
HY_EMB = 33
HY_BANDS = 16
HY_FILT = 64
HY_FAST_DECAY = 0.3
HY_SLOW_DECAY = 1.5
HY_TARGET = 1e-2
S5_GROUP = 16
S5_G = 32
S5_P = 64
S5_STATE = S5_G * S5_P
ML_HEADS = 4
ML_DH = 256
D_FF = 4 * D_MODEL
N_CONV = 3 * D_HY + 2 * D_ML
P_IN = N_CONV + D_S5 + 2 * D_ML + 4 * ML_HEADS
P_MIX = P_IN - 4 * ML_HEADS
EPS = 1e-6

LANE = 128
SUBLANE = 8
VMEM_LIMIT_BYTES = 48 * 1024 * 1024

DFT_N2 = LANE
S5_STREAMS = SUBLANE
S5_TSTEP = 64
S5_BLOCK_GROUPS = LANE // S5_GROUP
S5_NBLOCKS = S5_G // S5_BLOCK_GROUPS
ML_CHUNK = 256
ML_HEADS_PER_STEP = 4
COL_U = N_CONV // D_S5


def _cp(*sem):
    return pltpu.CompilerParams(dimension_semantics=sem, vmem_limit_bytes=VMEM_LIMIT_BYTES)


def _tile(n, pref):
    t = min(n, pref)
    while n % t:
        t //= 2
    return t


def _rms(x, g):
    ms = jnp.mean(x * x, axis=-1, keepdims=True)
    return x * lax.rsqrt(ms + EPS) * g


def _proj_in_body(x_ref, g_ref, w_ref, wg_ref, o_ref, og_ref, xn_ref):
    @pl.when(pl.program_id(1) == 0)
    def _():
        xn = _rms(x_ref[...], g_ref[...]).astype(BF16)
        xn_ref[...] = xn
        og_ref[...] = jnp.dot(xn, wg_ref[...], preferred_element_type=F32)

    o_ref[...] = jnp.dot(xn_ref[...], w_ref[...], preferred_element_type=F32).astype(o_ref.dtype)


def _proj_in(x, g, w, w_gate):
    m, k = x.shape
    n = w.shape[1]
    tm = _tile(m, 1024)
    tn = 1024
    return pl.pallas_call(
        _proj_in_body,
        grid=(m // tm, n // tn),
        in_specs=[pl.BlockSpec((tm, k), lambda i, j: (i, 0)),
                  pl.BlockSpec((1, k), lambda i, j: (0, 0)),
                  pl.BlockSpec((k, tn), lambda i, j: (0, j)),
                  pl.BlockSpec((k, LANE), lambda i, j: (0, 0))],
        out_specs=[pl.BlockSpec((tm, tn), lambda i, j: (i, j)),
                   pl.BlockSpec((tm, LANE), lambda i, j: (i, 0))],
        out_shape=[jax.ShapeDtypeStruct((m, n), BF16), jax.ShapeDtypeStruct((m, LANE), F32)],
        scratch_shapes=[pltpu.VMEM((tm, k), BF16)],
        compiler_params=_cp("arbitrary", "arbitrary"),
        name="norm_proj_in",
    )(x, g, w, w_gate)


def _proj_out_body(a0_ref, a1_ref, a2_ref, w0_ref, w1_ref, w2_ref, r_ref, g_ref, o_ref):
    f = (jnp.dot(a0_ref[...].astype(BF16), w0_ref[...], preferred_element_type=F32)
         + jnp.dot(a1_ref[...].astype(BF16), w1_ref[...], preferred_element_type=F32)
         + jnp.dot(a2_ref[...].astype(BF16), w2_ref[...], preferred_element_type=F32))
    o_ref[...] = r_ref[...] + _rms(f, g_ref[...])


def _proj_out(parts, weights, resid, g):
    m, n = resid.shape
    tm = _tile(m, 512)
    row = lambda width: pl.BlockSpec((tm, width), lambda i: (i, 0))
    full = lambda arr: pl.BlockSpec(arr.shape, lambda i: (0, 0))
    return pl.pallas_call(
        _proj_out_body,
        grid=(m // tm,),
        in_specs=[row(a.shape[1]) for a in parts] + [full(w) for w in weights] + [row(n), full(g)],
        out_specs=row(n),
        out_shape=jax.ShapeDtypeStruct((m, n), F32),
        compiler_params=_cp("arbitrary"),
        name="proj_out_resnorm",
    )(*parts, *weights, resid, g)


def _mlp_body(x_ref, g_in_ref, w1_ref, w2_ref, g_out_ref, o_ref, xn_ref, acc_ref):
    j = pl.program_id(1)

    @pl.when(j == 0)
    def _():
        xn_ref[...] = _rms(x_ref[...], g_in_ref[...]).astype(BF16)
        acc_ref[...] = jnp.zeros_like(acc_ref)

    h = jnp.dot(xn_ref[...], w1_ref[...], preferred_element_type=F32)
    h = jnp.square(jnp.maximum(h, 0.0)).astype(BF16)
    acc_ref[...] += jnp.dot(h, w2_ref[...], preferred_element_type=F32)

    @pl.when(j == pl.num_programs(1) - 1)
    def _():
        o_ref[...] = x_ref[...] + _rms(acc_ref[...], g_out_ref[...])


def _mlp(x, g_in, w1, w2, g_out):
    m, k = x.shape
    ff = w1.shape[1]
    tm = _tile(m, 512)
    tf = 1024
    return pl.pallas_call(
        _mlp_body,
        grid=(m // tm, ff // tf),
        in_specs=[pl.BlockSpec((tm, k), lambda i, j: (i, 0)),
                  pl.BlockSpec((1, k), lambda i, j: (0, 0)),
                  pl.BlockSpec((k, tf), lambda i, j: (0, j)),
                  pl.BlockSpec((tf, k), lambda i, j: (j, 0)),
                  pl.BlockSpec((1, k), lambda i, j: (0, 0))],
        out_specs=pl.BlockSpec((tm, k), lambda i, j: (i, 0)),
        out_shape=jax.ShapeDtypeStruct((m, k), F32),
        scratch_shapes=[pltpu.VMEM((tm, k), BF16), pltpu.VMEM((tm, k), F32)],
        compiler_params=_cp("arbitrary", "arbitrary"),
        name="mlp_resnorm",
    )(x, g_in, w1, w2, g_out)


def _conv_body(x_ref, p_ref, n_ref, w_ref, b_ref, o_ref, *, nblk, qk):
    i = pl.program_id(1)
    j = pl.program_id(2)
    x = x_ref[...].astype(F32)
    tm = x.shape[0]
    row = lax.broadcasted_iota(jnp.int32, x.shape, 0)
    halo = p_ref.shape[0]
    prev_row = jnp.where(i == 0, 0.0, p_ref[halo - 1:halo, :].astype(F32))
    next_row = jnp.where(i == nblk - 1, 0.0, n_ref[0:1, :].astype(F32))
    xm = jnp.where(row == 0, prev_row, pltpu.roll(x, 1, 0))
    xp = jnp.where(row == tm - 1, next_row, pltpu.roll(x, tm - 1, 0))
    w = w_ref[...]
    y = b_ref[...] + xm * w[0:1] + x * w[1:2] + xp * w[2:3]
    if qk:
        y = y * jax.nn.sigmoid(y) * jnp.where(j >= 2, ML_DH ** -0.5, 1.0)
    o_ref[...] = y.astype(o_ref.dtype)


def _short_conv(proj, conv_w, conv_b, nb, seq, qk):
    mtot = proj.shape[0]
    tm = _tile(seq, 2048)
    nblk = seq // tm
    col0, ncol = (3, 4) if qk else (0, 3)
    ppo = 2 if qk else 1
    halo = 2 * SUBLANE
    last_halo = mtot // halo - 1
    tmh = tm // halo

    def x_map(b, i, j):
        return (b * nblk + i, col0 + j)

    def p_map(b, i, j):
        return (jnp.maximum((b * nblk + i) * tmh - 1, 0), col0 + j)

    def n_map(b, i, j):
        return (jnp.minimum((b * nblk + i + 1) * tmh, last_halo), col0 + j)

    return pl.pallas_call(
        functools.partial(_conv_body, nblk=nblk, qk=qk),
        grid=(nb, nblk, ncol),
        in_specs=[pl.BlockSpec((tm, D_HY), x_map),
                  pl.BlockSpec((halo, D_HY), p_map),
                  pl.BlockSpec((halo, D_HY), n_map),
                  pl.BlockSpec((3, D_HY), lambda b, i, j: (0, col0 + j)),
                  pl.BlockSpec((1, D_HY), lambda b, i, j: (0, col0 + j))],
        out_specs=pl.BlockSpec((None, tm, D_HY), lambda b, i, j: (j // ppo, b * nblk + i, j % ppo)),
        out_shape=jax.ShapeDtypeStruct((ncol // ppo, nb * seq, ppo * D_HY), BF16 if qk else F32),
        compiler_params=_cp("arbitrary", "arbitrary", "arbitrary"),
        name="short_conv_qk" if qk else "short_conv_hy",
    )(proj, proj, proj, conv_w, conv_b)


def _dft_tables(seq):
    n = 2 * seq
    n1 = n // DFT_N2
    n1h = n1 // 2
    n1k = n1h + 2 * SUBLANE
    k1 = jnp.arange(n1k, dtype=jnp.int32)
    t1 = jnp.arange(n1h, dtype=jnp.int32)
    kept = (k1 <= n1h).astype(F32)[:, None]
    weight = kept * jnp.where((k1 == 0) | (k1 == n1h), 1.0, 2.0)[:, None]
    ang = ((k1[:, None] * t1[None, :]) % n1).astype(F32) * (2.0 * math.pi / n1)
    lead_f = jnp.concatenate([kept * jnp.cos(ang), -kept * jnp.sin(ang)], axis=0).astype(BF16)
    lead_ic = (weight * jnp.cos(ang)).T.astype(BF16)
    lead_is = (-weight * jnp.sin(ang)).T.astype(BF16)
    t1f = jnp.arange(n1, dtype=jnp.int32)
    angf = ((k1[:, None] * t1f[None, :]) % n1).astype(F32) * (2.0 * math.pi / n1)
    lead_ff = jnp.concatenate([kept * jnp.cos(angf), -kept * jnp.sin(angf)], axis=0).astype(BF16)
    t2 = jnp.arange(DFT_N2, dtype=jnp.int32)
    k2 = jnp.arange(DFT_N2, dtype=jnp.int32)
    idx = (t2[None, None, :] * (k2[None, :, None] * n1 + k1[:, None, None])) % n
    ang2 = idx.astype(F32) * (2.0 * math.pi / n)
    f_re = jnp.cos(ang2)
    f_im = -jnp.sin(ang2)
    mid_f = jnp.concatenate([jnp.concatenate([f_re, -f_im], axis=2),
                             jnp.concatenate([f_im, f_re], axis=2)], axis=1).astype(BF16)
    g_re = jnp.swapaxes(f_re, 1, 2)
    g_im = jnp.swapaxes(f_im, 1, 2)
    mid_g = jnp.concatenate([jnp.concatenate([g_re, g_im], axis=2),
                             jnp.concatenate([-g_im, g_re], axis=2)], axis=1).astype(BF16)
    return dict(n=n, n1=n1k, n1h=n1h, lead_f=lead_f, lead_ff=lead_ff, lead_ic=lead_ic, lead_is=lead_is,
                mid_f=mid_f, mid_g=mid_g)


def _hy_features(seq):
    t = jnp.linspace(0.0, 1.0, seq, dtype=F32)[:, None]
    w = 2.0 * math.pi * jnp.arange(seq, dtype=F32)[:, None] / seq
    f = jnp.linspace(1e-4, HY_BANDS - 1, HY_BANDS, dtype=F32)[None, :]
    z = jnp.concatenate([t, jnp.cos(f * w), -jnp.sin(f * w)], axis=-1)
    z = jnp.pad(z, ((0, 0), (0, LANE - HY_EMB)))
    slot = jnp.arange(2 * seq)
    z = z[jnp.where(slot < seq, slot, (2 * seq - slot) % seq)]
    tiles = 2 * seq // (SUBLANE * DFT_N2)
    z = z.reshape(tiles, SUBLANE, DFT_N2, LANE).transpose(0, 2, 1, 3)
    half_rows = SUBLANE * DFT_N2 // 2
    z = z.reshape(tiles, 2, half_rows, LANE).transpose(0, 2, 1, 3)
    return z.reshape(tiles * half_rows, 2 * LANE)


def _hy_deltas():
    d = jnp.abs(jnp.linspace(math.log(HY_TARGET) / HY_SLOW_DECAY,
                             math.log(HY_TARGET) / HY_FAST_DECAY, D_HY, dtype=F32))
    return jnp.tile(d, 2)[None, :]


def _hy_filter_body(z_ref, w1_ref, b1_ref, fr_ref, w2_ref, b2_ref, w3_ref, dl_ref, k_ref, ss_ref, *, half):
    i = pl.program_id(0)
    hi = lax.Precision.HIGHEST
    z = z_ref[...]
    fr = fr_ref[...]
    h = jnp.sin(fr * (jnp.dot(z, w1_ref[...], precision=hi, preferred_element_type=F32) + b1_ref[...]))
    h = jnp.sin(fr * (jnp.dot(h, w2_ref[...], precision=hi, preferred_element_type=F32) + b2_ref[...]))
    hb = h.astype(BF16)
    dl = dl_ref[...]
    ha = jnp.dot(hb, w3_ref[0], preferred_element_type=F32) * jnp.exp(-z[:, 0:1] * dl)
    hb = jnp.dot(hb, w3_ref[1], preferred_element_type=F32) * jnp.exp(-z[:, LANE:LANE + 1] * dl)
    row = lax.broadcasted_iota(jnp.int32, ha.shape, 0)
    ha = jnp.where(row + (i - half) * ha.shape[0] == 0, 0.0, ha)
    nc = ha.shape[1]
    h2 = DFT_N2 // 2
    for t2 in range(h2):
        k_ref[:, t2 * nc:(t2 + 1) * nc] = ha[t2 * SUBLANE:(t2 + 1) * SUBLANE, :]
        k_ref[:, (h2 + t2) * nc:(h2 + t2 + 1) * nc] = hb[t2 * SUBLANE:(t2 + 1) * SUBLANE, :]
    part = jnp.broadcast_to(jnp.sum(ha * ha, axis=0, keepdims=True)
                            + jnp.sum(hb * hb, axis=0, keepdims=True), ss_ref.shape)

    @pl.when(i == 0)
    def _():
        ss_ref[...] = part

    @pl.when(i > 0)
    def _():
        ss_ref[...] += part


def _hy_filter(seq, z, w1, b1, fr, w2, b2, w3, deltas):
    nc = 2 * D_HY
    tm = SUBLANE * DFT_N2
    steps = 2 * seq // tm
    half = steps // 2
    full = lambda shape: pl.BlockSpec(shape, lambda i: (0, 0))
    return pl.pallas_call(
        functools.partial(_hy_filter_body, half=half),
        grid=(steps,),
        in_specs=[pl.BlockSpec((tm // 2, 2 * LANE), lambda i: (i, 0)),
                  full((2 * LANE, LANE)), full((1, LANE)), full((1, LANE)),
                  full((LANE, LANE)), full((1, LANE)),
                  pl.BlockSpec((None, 2, LANE, nc), lambda i: (i // half, 0, 0, 0)), full((1, nc))],
        out_specs=[pl.BlockSpec((SUBLANE, DFT_N2 * nc), lambda i: (i, 0)), full((SUBLANE, nc))],
        out_shape=[jax.ShapeDtypeStruct((2 * seq // DFT_N2, DFT_N2 * nc), F32),
                   jax.ShapeDtypeStruct((SUBLANE, nc), F32)],
        compiler_params=_cp("arbitrary"),
        name="hyena_filter",
    )(z, w1, b1, fr, w2, b2, w3, deltas)


def _lead_fwd_body(x_ref, f_ref, re_ref, im_ref):
    n1 = re_ref.shape[0]
    r = jnp.dot(f_ref[...], x_ref[...].astype(BF16), preferred_element_type=F32)
    re_ref[...] = r[:n1].astype(BF16)
    im_ref[...] = r[n1:].astype(BF16)


def _lead_fwd(x, table, plane, nb):
    n1, rows = table.shape[0] // 2, table.shape[1]
    cols = x.shape[-1]
    tn = _tile(cols, 4096)
    return pl.pallas_call(
        _lead_fwd_body,
        grid=(nb, cols // tn),
        in_specs=[pl.BlockSpec((None, rows, tn), lambda b, j: (plane, b, j)),
                  pl.BlockSpec((2 * n1, rows), lambda b, j: (0, 0))],
        out_specs=[pl.BlockSpec((None, n1, tn), lambda b, j: (b, 0, j))] * 2,
        out_shape=[jax.ShapeDtypeStruct((nb, n1, cols), BF16)] * 2,
        compiler_params=_cp("arbitrary", "arbitrary"),
        name="hyena_lead_fwd",
    )(x, table)


def _mid_spec_body(are_ref, aim_ref, f_ref, ss_ref, kr_ref, ki_ref, *, bk):
    n2 = DFT_N2
    scale = lax.rsqrt(ss_ref[0:1, :] + EPS)

    def body(i, c):
        a = jnp.concatenate([are_ref[i], aim_ref[i]], axis=0)
        x = jnp.dot(f_ref[i], a, preferred_element_type=F32)
        kr_ref[i] = x[:n2] * scale
        ki_ref[i] = x[n2:] * scale
        return c

    lax.fori_loop(0, bk, body, 0, unroll=True)


def _mid_spec(a_re, a_im, sumsq, tb):
    n1 = tb["n1"]
    bk = _tile(n1, 8)
    a_re = a_re.reshape(n1, DFT_N2, 2 * D_HY)
    a_im = a_im.reshape(n1, DFT_N2, 2 * D_HY)
    a_spec = pl.BlockSpec((bk, DFT_N2, D_HY), lambda o, i: (i, 0, o))
    k_spec = pl.BlockSpec((None, bk, DFT_N2, D_HY), lambda o, i: (o, i, 0, 0))
    return pl.pallas_call(
        functools.partial(_mid_spec_body, bk=bk),
        grid=(2, n1 // bk),
        in_specs=[a_spec, a_spec,
                  pl.BlockSpec((bk, 2 * DFT_N2, 2 * DFT_N2), lambda o, i: (i, 0, 0)),
                  pl.BlockSpec((SUBLANE, D_HY), lambda o, i: (0, o))],
        out_specs=[k_spec, k_spec],
        out_shape=[jax.ShapeDtypeStruct((2, n1, DFT_N2, D_HY), F32)] * 2,
        compiler_params=_cp("arbitrary", "arbitrary"),
        name="hyena_mid_spectrum",
    )(a_re, a_im, tb["mid_f"], sumsq)


def _mid_conv_body(are_ref, aim_ref, f_ref, g_ref, kr_ref, ki_ref, bre_ref, bim_ref, *, bk):
    n2 = DFT_N2

    def body(i, c):
        a = jnp.concatenate([are_ref[i], aim_ref[i]], axis=0)
        x = jnp.dot(f_ref[i], a, preferred_element_type=F32)
        xr = x[:n2]
        xi = x[n2:]
        kr = kr_ref[i]
        ki = ki_ref[i]
        p = jnp.concatenate([xr * kr - xi * ki, xr * ki + xi * kr], axis=0).astype(BF16)
        q = jnp.dot(g_ref[i], p, preferred_element_type=F32)
        bre_ref[i] = q[:n2].astype(BF16)
        bim_ref[i] = q[n2:].astype(BF16)
        return c

    lax.fori_loop(0, bk, body, 0, unroll=True)


def _mid_conv(a_re, a_im, kf_re, kf_im, order, tb):
    n1 = tb["n1"]
    nb = a_re.shape[0]
    bk = _tile(n1, 8)
    a_re = a_re.reshape(nb, n1, DFT_N2, D_HY)
    a_im = a_im.reshape(nb, n1, DFT_N2, D_HY)
    a_spec = pl.BlockSpec((None, bk, DFT_N2, D_HY), lambda b, i: (b, i, 0, 0))
    t_spec = pl.BlockSpec((bk, 2 * DFT_N2, 2 * DFT_N2), lambda b, i: (i, 0, 0))
    k_spec = pl.BlockSpec((None, bk, DFT_N2, D_HY), lambda b, i: (order, i, 0, 0))
    b_re, b_im = pl.pallas_call(
        functools.partial(_mid_conv_body, bk=bk),
        grid=(nb, n1 // bk),
        in_specs=[a_spec, a_spec, t_spec, t_spec, k_spec, k_spec],
        out_specs=[a_spec, a_spec],
        out_shape=[jax.ShapeDtypeStruct((nb, n1, DFT_N2, D_HY), BF16)] * 2,
        compiler_params=_cp("arbitrary", "arbitrary"),
        name="hyena_mid_conv",
    )(a_re, a_im, tb["mid_f"], tb["mid_g"], kf_re, kf_im)
    return b_re.reshape(nb, n1, DFT_N2 * D_HY), b_im.reshape(nb, n1, DFT_N2 * D_HY)


def _lead_inv_body(bre_ref, bim_ref, c_ref, s_ref, z_ref, gate_ref, skip_ref, gn_ref, o_ref, *, inv_n, last):
    y = (jnp.dot(c_ref[...], bre_ref[...], preferred_element_type=F32)
         + jnp.dot(s_ref[...], bim_ref[...], preferred_element_type=F32)) * inv_n
    out = gate_ref[...] * (y + skip_ref[...] * z_ref[...])
    if not last:
        o_ref[...] = out
    else:
        gn = gn_ref[...]
        for c in range(out.shape[1] // D_HY):
            blk = out[:, c * D_HY:(c + 1) * D_HY]
            ms = jnp.mean(blk * blk, axis=-1, keepdims=True)
            o_ref[:, c * D_HY:(c + 1) * D_HY] = (blk * lax.rsqrt(ms + EPS) * gn).astype(o_ref.dtype)


def _lead_inv(b_re, b_im, z, z_plane, gates, gate_plane, skip, gn, tb, last):
    n1, n1h = tb["n1"], tb["n1h"]
    nb, _, cols = b_re.shape
    tn = _tile(cols, 4096)
    skip_t = jnp.tile(skip[None, :], (1, tn // D_HY))
    b_spec = pl.BlockSpec((None, n1, tn), lambda b, j: (b, 0, j))
    t_spec = pl.BlockSpec((n1h, n1), lambda b, j: (0, 0))
    return pl.pallas_call(
        functools.partial(_lead_inv_body, inv_n=1.0 / tb["n"], last=last),
        grid=(nb, cols // tn),
        in_specs=[b_spec, b_spec, t_spec, t_spec,
                  pl.BlockSpec((None, n1h, tn), lambda b, j: (z_plane, b, j)),
                  pl.BlockSpec((None, n1h, tn), lambda b, j: (gate_plane, b, j)),
                  pl.BlockSpec((1, tn), lambda b, j: (0, 0)),
                  pl.BlockSpec((1, D_HY), lambda b, j: (0, 0))],
        out_specs=pl.BlockSpec((None, n1h, tn), lambda b, j: (0, b, j)),
        out_shape=jax.ShapeDtypeStruct((1, nb * n1h, cols), BF16 if last else F32),
        compiler_params=_cp("arbitrary", "arbitrary"),
        name="hyena_lead_inv",
    )(b_re, b_im, tb["lead_ic"], tb["lead_is"], z, gates, skip_t, gn)


def _hyena(conv_out, nb, seq, tb, kf_re, kf_im, skip, gn):
    n1h = tb["n1h"]
    planes = conv_out.reshape(conv_out.shape[0], nb * n1h, DFT_N2 * D_HY)
    z, z_plane = planes, 0
    for order in range(2):
        a_re, a_im = _lead_fwd(z, tb["lead_f"], z_plane, nb)
        b_re, b_im = _mid_conv(a_re, a_im, kf_re, kf_im, order, tb)
        z = _lead_inv(b_re, b_im, z, z_plane, planes, 1 + order, skip[order], gn, tb, order == 1)
        z_plane = 0
    return z.reshape(nb * seq, D_HY)


def _hyena_spectra(seq, tb, w1, b1, fr, w2, b2, w3):
    w1 = jnp.pad(w1, ((0, LANE - HY_EMB), (0, 0)))
    zero1 = jnp.zeros_like(w1)
    w1 = jnp.concatenate([jnp.concatenate([w1, zero1], axis=1),
                          jnp.concatenate([zero1, w1], axis=1)], axis=0)
    zero2 = jnp.zeros_like(w2)
    w2 = jnp.concatenate([jnp.concatenate([w2, zero2], axis=1),
                          jnp.concatenate([zero2, w2], axis=1)], axis=0)
    pair = lambda v: jnp.concatenate([v, v])[None, :]
    w3 = w3.reshape(HY_FILT, 2, 2, D_HY).transpose(2, 0, 1, 3).reshape(2, HY_FILT, 2 * D_HY)
    zero3 = jnp.zeros_like(w3)
    w3 = jnp.stack([jnp.concatenate([w3, zero3], axis=1),
                    jnp.concatenate([zero3, w3], axis=1)], axis=1).astype(BF16)
    taps, sumsq = _hy_filter(seq, _hy_features(seq), w1, pair(b1), pair(fr), w2, pair(b2), w3,
                             _hy_deltas())
    a_re, a_im = _lead_fwd(taps[None], tb["lead_ff"], 0, 1)
    return _mid_spec(a_re[0], a_im[0], sumsq, tb)


def _s5_scan_body(u_ref, pm_ref, bd_ref, a_ref, cd_ref, x0_ref, o_ref, bu_ref, st_ref, y_ref,
                  *, tstep, reverse, final_state):
    tb = pl.program_id(0)
    ns = S5_STREAMS
    half = S5_STATE
    cw = S5_BLOCK_GROUPS * S5_P

    @pl.when(tb == 0)
    def _():
        st_ref[...] = x0_ref[...]

    u = u_ref[...].reshape(ns * tstep, D_S5).astype(BF16)
    u_tm = jnp.dot(pm_ref[...], u, preferred_element_type=F32).astype(BF16)

    def project_in(blk):
        bu_ref[:, 2 * blk * cw:2 * (blk + 1) * cw] = jnp.dot(
            u_tm[:, blk * LANE:(blk + 1) * LANE], bd_ref[blk], preferred_element_type=F32)

    def scan(blk):
        bre = slice(2 * blk * cw, (2 * blk + 1) * cw)
        bim = slice((2 * blk + 1) * cw, (2 * blk + 2) * cw)
        sre = slice(blk * cw, (blk + 1) * cw)
        sim = slice(half + blk * cw, half + (blk + 1) * cw)
        ar = a_ref[:, sre]
        ai = a_ref[:, sim]
        xr = st_ref[:, sre]
        xi = st_ref[:, sim]
        for i in range(tstep):
            t = tstep - 1 - i if reverse else i
            rows = slice(t * ns, (t + 1) * ns)
            xr, xi = (ar * xr - ai * xi + bu_ref[rows, bre],
                      ar * xi + ai * xr + bu_ref[rows, bim])
            if not final_state:
                bu_ref[rows, bre] = xr
                bu_ref[rows, bim] = xi
        st_ref[:, sre] = xr
        st_ref[:, sim] = xi

    def project_out(blk):
        y_ref[blk] = jnp.dot(bu_ref[:, 2 * blk * cw:2 * (blk + 1) * cw].astype(BF16), cd_ref[blk],
                             preferred_element_type=F32)

    project_in(0)
    for blk in range(S5_NBLOCKS):
        if blk + 1 < S5_NBLOCKS:
            project_in(blk + 1)
        scan(blk)
        if not final_state:
            project_out(blk)

    if final_state:
        @pl.when(tb == pl.num_programs(0) - 1)
        def _():
            o_ref[...] = st_ref[...]
    else:
        for s in range(ns):
            for blk in range(S5_NBLOCKS):
                o_ref[s, :, blk * LANE:(blk + 1) * LANE] = y_ref[blk, pl.ds(s, tstep, stride=ns), :]


def _s5_scan(proj, pmat, bd, a_b, cd, x0, tlen, reverse, final_state):
    ns = S5_STREAMS
    d = 1 if reverse else 0
    tstep = pmat.shape[0] // ns
    nt = tlen // tstep
    proj3 = proj.reshape(ns, tlen, proj.shape[1])
    blk_in = S5_BLOCK_GROUPS * S5_GROUP
    blk_state = 2 * S5_BLOCK_GROUPS * S5_P

    def window(t):
        return nt - 1 - t if reverse else t

    if final_state:
        out_spec = pl.BlockSpec((ns, 2 * S5_STATE), lambda t: (0, 0))
        out_shape = jax.ShapeDtypeStruct((ns, 2 * S5_STATE), F32)
    else:
        out_spec = pl.BlockSpec((ns, tstep, D_S5), lambda t: (0, window(t), 0))
        out_shape = jax.ShapeDtypeStruct((ns, tlen, D_S5), F32)
    out = pl.pallas_call(
        functools.partial(_s5_scan_body, tstep=tstep, reverse=reverse, final_state=final_state),
        grid=(nt,),
        in_specs=[pl.BlockSpec((ns, tstep, D_S5), lambda t: (0, window(t), COL_U)),
                  pl.BlockSpec((ns * tstep, ns * tstep), lambda t: (0, 0)),
                  pl.BlockSpec((None, S5_NBLOCKS, blk_in, blk_state), lambda t: (d, 0, 0, 0)),
                  pl.BlockSpec((None, ns, 2 * S5_STATE), lambda t: (d, 0, 0)),
                  pl.BlockSpec((None, S5_NBLOCKS, blk_state, blk_in), lambda t: (d, 0, 0, 0)),
                  pl.BlockSpec((None, ns, 2 * S5_STATE), lambda t: (d, 0, 0))],
        out_specs=out_spec,
        out_shape=out_shape,
        scratch_shapes=[pltpu.VMEM((tstep * ns, 2 * S5_STATE), F32),
                        pltpu.VMEM((ns, 2 * S5_STATE), F32),
                        pltpu.VMEM((D_S5 // LANE, tstep * ns, LANE), F32)],
        compiler_params=_cp("arbitrary"),
        name="s5_end_state" if final_state else "s5_scan",
    )(proj3, pmat, bd, a_b, cd, x0)
    return out if final_state else out.reshape(ns * tlen, D_S5)


def _s5_params(lam_re, lam_im, log_dt, b_re, b_im, c_re, c_im):
    lr = jnp.minimum(lam_re, -1e-4)
    li = lam_im
    dt = jnp.exp(log_dt)[..., None]
    er = jnp.exp(lr * dt)
    a_re = er * jnp.cos(li * dt)
    a_im = er * jnp.sin(li * dt)
    den = lr * lr + li * li
    q_re = ((a_re - 1.0) * lr + a_im * li) / den
    q_im = (a_im * lr - (a_re - 1.0) * li) / den
    bb_re = q_re[..., None] * b_re - q_im[..., None] * b_im
    bb_im = q_re[..., None] * b_im + q_im[..., None] * b_re
    nbk, gb = S5_NBLOCKS, S5_BLOCK_GROUPS
    eye = jnp.eye(gb, dtype=F32)

    def in_mat(x):
        x = x.reshape(2, nbk, gb, S5_P, S5_GROUP)
        return jnp.einsum("dbgph,gk->dbghkp", x, eye).reshape(2, nbk, gb * S5_GROUP, gb * S5_P)

    def out_mat(x):
        x = x.reshape(2, nbk, gb, S5_GROUP, S5_P)
        return jnp.einsum("dbghp,gk->dbgpkh", x, eye).reshape(2, nbk, gb * S5_P, gb * S5_GROUP)

    bd = jnp.concatenate([in_mat(bb_re), in_mat(bb_im)], axis=3).astype(BF16)
    cd = jnp.concatenate([out_mat(c_re), out_mat(-c_im)], axis=2).astype(BF16)
    a_bar = jnp.concatenate([a_re.reshape(2, S5_STATE), a_im.reshape(2, S5_STATE)], axis=1)
    return (lr * dt).reshape(2, S5_STATE), (li * dt).reshape(2, S5_STATE), a_bar, bd, cd


def _s5_row_perm(tstep):
    ns = S5_STREAMS
    r = jnp.arange(ns * tstep)
    src = (r % ns) * tstep + r // ns
    return (src[:, None] == jnp.arange(ns * tstep)[None, :]).astype(BF16)


def _s5_mixer(proj, nb, seq, s5p):
    la_re, la_im, a_bar, bd, cd = s5p
    ns = S5_STREAMS
    ncs = ns // nb
    tlen = seq // ncs
    pmat = _s5_row_perm(_tile(tlen, S5_TSTEP))
    a_b = jnp.broadcast_to(a_bar[:, None, :], (2, ns, 2 * S5_STATE))
    zeros = jnp.zeros((2, ns, 2 * S5_STATE), F32)
    mag = jnp.exp(la_re * tlen)
    at_re = (mag * jnp.cos(la_im * tlen))[:, None, :]
    at_im = (mag * jnp.sin(la_im * tlen))[:, None, :]
    ys = []
    for d in range(2):
        x_end = _s5_scan(proj, pmat, bd, a_b, cd, zeros, tlen, d == 1, True)
        xe_re = x_end[:, :S5_STATE].reshape(nb, ncs, S5_STATE)
        xe_im = x_end[:, S5_STATE:].reshape(nb, ncs, S5_STATE)
        order = range(ncs) if d == 0 else range(ncs - 1, -1, -1)
        cr = jnp.zeros((nb, S5_STATE), F32)
        ci = jnp.zeros((nb, S5_STATE), F32)
        rows = [None] * ncs
        for c in order:
            rows[c] = jnp.concatenate([cr, ci], axis=-1)
            cr, ci = (at_re[d] * cr - at_im[d] * ci + xe_re[:, c],
                      at_re[d] * ci + at_im[d] * cr + xe_im[:, c])
        x0 = jnp.stack(rows, axis=1).reshape(ns, 2 * S5_STATE)
        x0 = jnp.stack([x0, x0])
        ys.append(_s5_scan(proj, pmat, bd, a_b, cd, x0, tlen, d == 1, False))
    return ys


def _s5_post_body(yf_ref, yb_ref, u_ref, d_ref, w_ref, b_ref, gn_ref, o_ref):
    y = yf_ref[...] + yb_ref[...] + d_ref[...] * u_ref[...].astype(F32)
    g = jax.nn.gelu(y)
    r = jnp.dot(g.astype(BF16), w_ref[...], preferred_element_type=F32) + b_ref[...]
    out = r[:, :D_S5] * jax.nn.sigmoid(r[:, D_S5:])
    ms = jnp.mean(out * out, axis=-1, keepdims=True)
    o_ref[...] = (out * lax.rsqrt(ms + EPS) * gn_ref[...]).astype(o_ref.dtype)


def _s5_post(y_f, y_b, proj, d, glu_w, glu_b, gn):
    m = y_f.shape[0]
    tm = _tile(m, 512)
    full = lambda shape: pl.BlockSpec(shape, lambda i: (0, 0))
    return pl.pallas_call(
        _s5_post_body,
        grid=(m // tm,),
        in_specs=[pl.BlockSpec((tm, D_S5), lambda i: (i, 0)),
                  pl.BlockSpec((tm, D_S5), lambda i: (i, 0)),
                  pl.BlockSpec((tm, D_S5), lambda i: (i, COL_U)),
                  full((1, D_S5)), full((D_S5, 2 * D_S5)), full((1, 2 * D_S5)), full((1, D_S5))],
        out_specs=pl.BlockSpec((tm, D_S5), lambda i: (i, 0)),
        out_shape=jax.ShapeDtypeStruct((m, D_S5), BF16),
        compiler_params=_cp("arbitrary"),
        name="s5_glu_norm",
    )(y_f, y_b, proj, d, glu_w, glu_b, gn)


def _split3(x):
    hi = x.astype(BF16)
    rest = x - hi.astype(F32)
    mid = rest.astype(BF16)
    return hi, mid, (rest - mid.astype(F32)).astype(BF16)


def _mlstm_body(*refs, reverse, epilogue):
    if epilogue:
        (q_ref, k_ref, v_ref, g_ref, gt_ref, bias_ref, biast_ref, seen_ref, upto_ref, hf_ref, o_gate_ref,
         gn_ref, out_ref, c_ref, n_ref, m_ref) = refs
    else:
        (q_ref, k_ref, v_ref, g_ref, gt_ref, bias_ref, biast_ref, seen_ref, upto_ref,
         out_ref, c_ref, n_ref, m_ref) = refs
        hf_ref = o_gate_ref = gn_ref = None

    @pl.when(pl.program_id(2) == 0)
    def _():
        c_ref[...] = jnp.zeros_like(c_ref)
        n_ref[...] = jnp.zeros_like(n_ref)
        m_ref[...] = jnp.zeros_like(m_ref)

    tc = ML_CHUNK
    r = lax.broadcasted_iota(jnp.int32, (tc, tc), 0)
    s = lax.broadcasted_iota(jnp.int32, (tc, tc), 1)
    seen = (s >= r) if reverse else (s <= r)
    g = g_ref[...] + bias_ref[...]
    gt = gt_ref[...] + biast_ref[:, 0:1]
    for hh in range(ML_HEADS_PER_STEP):
        _mlstm_head(hh, g, gt, seen, q_ref, k_ref, v_ref, seen_ref, upto_ref, hf_ref, o_gate_ref,
                    gn_ref, out_ref, c_ref, n_ref, m_ref, reverse=reverse)


def _mlstm_head(hh, g, gt, seen, q_ref, k_ref, v_ref, seen_ref, upto_ref, hf_ref, o_gate_ref, gn_ref,
                out_ref, c_ref, n_ref, m_ref, *, reverse):
    head = pl.program_id(1) * ML_HEADS_PER_STEP + hh
    cols = slice(hh * ML_DH, (hh + 1) * ML_DH)
    tc = ML_CHUNK
    gate_i = 2 if reverse else 0
    idx_i = gate_i * ML_HEADS + head
    idx_f = idx_i + ML_HEADS
    tiled = lambda x, n: x if n == LANE else jnp.concatenate([x] * (n // LANE), axis=1)
    wide = lambda x: tiled(x, ML_DH)
    span = lambda x: tiled(x, tc)
    dot = functools.partial(jnp.dot, preferred_element_type=F32)

    lane = lax.broadcasted_iota(jnp.int32, g.shape, 1)
    ig = jnp.broadcast_to(jnp.sum(jnp.where(lane == idx_i, g, 0.0), axis=1, keepdims=True), g.shape)
    fg = jnp.broadcast_to(jnp.sum(jnp.where(lane == idx_f, g, 0.0), axis=1, keepdims=True), g.shape)
    sub = lax.broadcasted_iota(jnp.int32, gt.shape, 0)
    ig_row = jnp.sum(jnp.where(sub == idx_i, gt, 0.0), axis=0, keepdims=True)
    fg_row = jnp.sum(jnp.where(sub == idx_f, gt, 0.0), axis=0, keepdims=True)
    lf = jax.nn.log_sigmoid(fg)
    lf_row = jax.nn.log_sigmoid(fg_row)

    seen_b = seen_ref[...]
    upto_b = upto_ref[...]
    b_col = sum(dot(seen_b, piece) for piece in _split3(lf))
    lf_rows = jnp.broadcast_to(lf_row, (SUBLANE, tc))
    b_row = sum(dot(piece, upto_b) for piece in _split3(lf_rows))[0:1]
    b_last = jnp.sum(lf_row, axis=1, keepdims=True)

    m_prev = m_ref[hh:hh + 1, 0:1]
    src = ig_row - b_row
    dmat = jnp.where(seen, span(b_col) + src, -jnp.inf)
    g_car = b_col + m_prev
    mt = jnp.maximum(g_car, jnp.max(dmat, axis=1, keepdims=True))
    inter = jnp.exp(g_car - mt)
    qb = q_ref[:, cols]
    kb = k_ref[:, cols]
    vb = v_ref[:, cols].astype(BF16)
    ones = jnp.ones((tc, LANE), BF16)
    qk = lax.dot_general(qb, kb, (((1,), (1,)), ((), ())), preferred_element_type=F32)
    sc = (qk * jnp.exp(dmat - span(mt))).astype(BF16)
    c_state = c_ref[hh]
    n_state = n_ref[hh]
    num = wide(inter) * dot(qb, c_state.astype(BF16)) + dot(sc, vb)
    den = inter * dot(qb, n_state.astype(BF16)) + dot(sc, ones)
    h = num * wide(1.0 / jnp.maximum(jnp.abs(den), jnp.exp(-mt)))

    a_row = b_last + src
    a_col = b_last - b_col + ig
    m_new = jnp.maximum(b_last + m_prev, jnp.max(a_row, axis=1, keepdims=True))
    decay = jnp.exp(b_last + m_prev - m_new)
    kw = (kb.astype(F32) * wide(jnp.exp(a_col - m_new))).astype(BF16)
    upd = lax.dot_general(kw, jnp.concatenate([vb, ones], axis=1), (((0,), (0,)), ((), ())),
                          preferred_element_type=F32)
    c_ref[hh] = decay * c_state + upd[:, :ML_DH]
    n_ref[hh] = decay * n_state + upd[:, ML_DH:]
    m_ref[hh:hh + 1, 0:1] = m_new

    if hf_ref is not None:
        h = h + hf_ref[:, cols]
        ms = jnp.mean(h * h, axis=-1, keepdims=True)
        hn = h * lax.rsqrt(ms + EPS) * gn_ref[:, cols]
        out_ref[:, cols] = (hn * jax.nn.sigmoid(o_gate_ref[:, cols].astype(F32))).astype(out_ref.dtype)
    else:
        out_ref[:, cols] = h


def _mlstm_dir(qk_planes, proj, gates, gates_t, bias, bias_t, nb, seq, reverse, h_fwd=None, gn=None):
    tc = ML_CHUNK
    nch = seq // tc
    hps = ML_HEADS_PER_STEP
    width = hps * ML_DH
    col_v = (N_CONV + D_S5) // width
    col_o = (N_CONV + D_S5 + D_ML) // width

    def chunk(b, ci):
        return b * nch + (nch - 1 - ci if reverse else ci)

    in_specs = [
        pl.BlockSpec((None, tc, width), lambda b, h, ci: (0, chunk(b, ci), h)),
        pl.BlockSpec((None, tc, width), lambda b, h, ci: (1, chunk(b, ci), h)),
        pl.BlockSpec((tc, width), lambda b, h, ci: (chunk(b, ci), col_v + h)),
        pl.BlockSpec((tc, LANE), lambda b, h, ci: (chunk(b, ci), 0)),
        pl.BlockSpec((4 * ML_HEADS, tc), lambda b, h, ci: (0, chunk(b, ci))),
        pl.BlockSpec((1, LANE), lambda b, h, ci: (0, 0)),
        pl.BlockSpec((4 * ML_HEADS, LANE), lambda b, h, ci: (0, 0)),
        pl.BlockSpec((tc, tc), lambda b, h, ci: (0, 0)),
        pl.BlockSpec((tc, tc), lambda b, h, ci: (0, 0)),
    ]
    pos = jnp.arange(tc)
    seen = (pos[None, :] >= pos[:, None]) if reverse else (pos[None, :] <= pos[:, None])
    seen = seen.astype(BF16)
    args = [qk_planes, qk_planes, proj, gates, gates_t, bias, bias_t, seen, seen.T]
    epilogue = h_fwd is not None
    if epilogue:
        in_specs += [
            pl.BlockSpec((tc, width), lambda b, h, ci: (chunk(b, ci), h)),
            pl.BlockSpec((tc, width), lambda b, h, ci: (chunk(b, ci), col_o + h)),
            pl.BlockSpec((1, width), lambda b, h, ci: (0, h)),
        ]
        args += [h_fwd, proj, gn]
    return pl.pallas_call(
        functools.partial(_mlstm_body, reverse=reverse, epilogue=epilogue),
        grid=(nb, ML_HEADS // hps, nch),
        in_specs=in_specs,
        out_specs=pl.BlockSpec((tc, width), lambda b, h, ci: (chunk(b, ci), h)),
        out_shape=jax.ShapeDtypeStruct((nb * seq, D_ML), BF16 if epilogue else F32),
        scratch_shapes=[pltpu.VMEM((hps, ML_DH, ML_DH), F32), pltpu.VMEM((hps, ML_DH, LANE), F32),
                        pltpu.VMEM((SUBLANE, LANE), F32)],
        compiler_params=_cp("arbitrary", "arbitrary", "arbitrary"),
        name="mlstm_bwd_norm" if reverse else "mlstm_fwd",
    )(*args)


def _layer(x, nb, seq, tb, p):
    proj, gates = _proj_in(x, p["ln_g"][0:1], p["w_in"], p["w_gate"])
    gates_t = gates[:, :4 * ML_HEADS].T
    gn = p["group_norm"]
    gn_hy = gn[None, :D_HY]
    gn_s5 = gn[None, D_HY:D_HY + D_S5]
    gn_ml = gn[None, D_HY + D_S5:]

    hy_planes = _short_conv(proj, p["conv_w"], p["conv_b"][None, :], nb, seq, False)
    qk_planes = _short_conv(proj, p["conv_w"], p["conv_b"][None, :], nb, seq, True)
    kf_re, kf_im = _hyena_spectra(seq, tb, p["hy_w1"], p["hy_b1"], p["hy_freq"], p["hy_w2"],
                                  p["hy_b2"], p["hy_w3"])
    y_hy = _hyena(hy_planes, nb, seq, tb, kf_re, kf_im, p["hy_skip"], gn_hy)
    y_s5 = _s5_post(*_s5_mixer(proj, nb, seq, p["s5"]), proj, p["s5_d"][None, :],
                    p["s5_glu_w"], p["s5_glu_b"][None, :], gn_s5)
    ml_args = (qk_planes, proj, gates, gates_t, p["ml_bias"], p["ml_bias_t"], nb, seq)
    h_f = _mlstm_dir(*ml_args, False)
    y_ml = _mlstm_dir(*ml_args, True, h_f, gn_ml)

    w_out = p["w_out"]
    x = _proj_out([y_hy, y_s5, y_ml], [w_out[:D_HY], w_out[D_HY:D_HY + D_S5], w_out[D_HY + D_S5:]],
                  x, p["ln_g"][1:2])
    return _mlp(x, p["ln_g"][2:3], p["mlp_w1"], p["mlp_w2"], p["ln_g"][3:4])


_PARAM_NAMES = ("ln_g", "w_in", "conv_w", "conv_b", "hy_w1", "hy_b1", "hy_freq", "hy_w2", "hy_b2",
                "hy_w3", "hy_skip", "s5_lam_re", "s5_lam_im", "s5_log_dt", "s5_b_re", "s5_b_im",
                "s5_c_re", "s5_c_im", "s5_d", "s5_glu_w", "s5_glu_b", "ml_gate_b", "group_norm",
                "w_out", "mlp_w1", "mlp_w2")


def _prepare(p):
    p = dict(p)
    w_in = p["w_in"].astype(BF16)
    p["w_in"] = w_in[:, :P_MIX]
    p["w_gate"] = jnp.pad(w_in[:, P_MIX:], ((0, 0), (0, LANE - 4 * ML_HEADS)))
    p["w_out"] = p["w_out"].astype(BF16)
    p["mlp_w1"] = p["mlp_w1"].astype(BF16)
    p["mlp_w2"] = p["mlp_w2"].astype(BF16)
    p["s5_glu_w"] = p["s5_glu_w"].astype(BF16)
    p["ml_bias"] = jnp.pad(p["ml_gate_b"].reshape(1, -1), ((0, 0), (0, LANE - 4 * ML_HEADS)))
    p["ml_bias_t"] = jnp.broadcast_to(p["ml_gate_b"].reshape(-1, 1), (4 * ML_HEADS, LANE))
    p["s5"] = _s5_params(p["s5_lam_re"], p["s5_lam_im"], p["s5_log_dt"], p["s5_b_re"],
                         p["s5_b_im"], p["s5_c_re"], p["s5_c_im"])
    return p


def kernel(x_prompt, x_sample, ln_g, w_in, conv_w, conv_b, hy_w1, hy_b1, hy_freq, hy_w2, hy_b2, hy_w3,
           hy_skip, s5_lam_re, s5_lam_im, s5_log_dt, s5_b_re, s5_b_im, s5_c_re, s5_c_im, s5_d,
           s5_glu_w, s5_glu_b, ml_gate_b, group_norm, w_out, mlp_w1, mlp_w2):
    params = (ln_g, w_in, conv_w, conv_b, hy_w1, hy_b1, hy_freq, hy_w2, hy_b2, hy_w3, hy_skip,
              s5_lam_re, s5_lam_im, s5_log_dt, s5_b_re, s5_b_im, s5_c_re, s5_c_im, s5_d,
              s5_glu_w, s5_glu_b, ml_gate_b, group_norm, w_out, mlp_w1, mlp_w2)
    streams = []
    for x in (x_prompt, x_sample):
        nb, seq, _ = x.shape
        streams.append([x.reshape(nb * seq, D_MODEL), nb, seq, _dft_tables(seq)])
    for layer in range(DEPTH):
        p = _prepare({name: arr[layer] for name, arr in zip(_PARAM_NAMES, params)})
        for st in streams:
            st[0] = _layer(st[0], st[1], st[2], st[3], p)
    return tuple(st[0].reshape(x.shape) for st, x in zip(streams, (x_prompt, x_sample)))
```

```python
import functools
import math

import jax
import jax.numpy as jnp
from jax import lax
from jax.experimental import pallas as pl
from jax.experimental.pallas import tpu as pltpu

F32 = jnp.float32
BF16 = jnp.bfloat16

D_MODEL = 2048
DEPTH = 4
D_HY = 512
D_S5 = 512
D_ML = 1024
HY_EMB = 33
HY_BANDS = 16
HY_FILT = 64
HY_FAST_DECAY = 0.3
HY_SLOW_DECAY = 1.5
HY_TARGET = 1e-2
S5_GROUP = 16
S5_G = 32
S5_P = 64
S5_STATE = S5_G * S5_P
ML_HEADS = 4
ML_DH = 256
D_FF = 4 * D_MODEL
N_CONV = 3 * D_HY + 2 * D_ML
P_IN = N_CONV + D_S5 + 2 * D_ML + 4 * ML_HEADS
P_MIX = P_IN - 4 * ML_HEADS
EPS = 1e-6

LANE = 128
SUBLANE = 8
VMEM_LIMIT_BYTES = 48 * 1024 * 1024

DFT_N2 = LANE
S5_STREAMS = SUBLANE
S5_TSTEP = 64
S5_BLOCK_GROUPS = LANE // S5_GROUP
S5_NBLOCKS = S5_G // S5_BLOCK_GROUPS
ML_CHUNK = 256
ML_HEADS_PER_STEP = 4
COL_U = N_CONV // D_S5


def _cp(*sem):
    return pltpu.CompilerParams(dimension_semantics=sem, vmem_limit_bytes=VMEM_LIMIT_BYTES)


def _tile(n, pref):
    t = min(n, pref)
    while n % t:
        t //= 2
    return t


def _rms(x, g):
    ms = jnp.mean(x * x, axis=-1, keepdims=True)
    return x * lax.rsqrt(ms + EPS) * g


def _proj_in_body(x_ref, g_ref, w_ref, wg_ref, o_ref, og_ref, xn_ref):
    @pl.when(pl.program_id(1) == 0)
    def _():
        xn = _rms(x_ref[...], g_ref[...]).astype(BF16)
        xn_ref[...] = xn
        og_ref[...] = jnp.dot(xn, wg_ref[...], preferred_element_type=F32)

    o_ref[...] = jnp.dot(xn_ref[...], w_ref[...], preferred_element_type=F32).astype(o_ref.dtype)


def _proj_in(x, g, w, w_gate):
    m, k = x.shape
    n = w.shape[1]
    tm = _tile(m, 1024)
    tn = 1024
    return pl.pallas_call(
        _proj_in_body,
        grid=(m // tm, n // tn),
        in_specs=[pl.BlockSpec((tm, k), lambda i, j: (i, 0)),
                  pl.BlockSpec((1, k), lambda i, j: (0, 0)),
                  pl.BlockSpec((k, tn), lambda i, j: (0, j)),
                  pl.BlockSpec((k, LANE), lambda i, j: (0, 0))],
        out_specs=[pl.BlockSpec((tm, tn), lambda i, j: (i, j)),
                   pl.BlockSpec((tm, LANE), lambda i, j: (i, 0))],
        out_shape=[jax.ShapeDtypeStruct((m, n), BF16), jax.ShapeDtypeStruct((m, LANE), F32)],
        scratch_shapes=[pltpu.VMEM((tm, k), BF16)],
        compiler_params=_cp("arbitrary", "arbitrary"),
        name="norm_proj_in",
    )(x, g, w, w_gate)


def _proj_out_body(a0_ref, a1_ref, a2_ref, w0_ref, w1_ref, w2_ref, r_ref, g_ref, o_ref):
    half = o_ref.shape[0] // 2
    for rows in (slice(0, half), slice(half, 2 * half)):
        f = (jnp.dot(a0_ref[rows, :].astype(BF16), w0_ref[...], preferred_element_type=F32)
             + jnp.dot(a1_ref[rows, :].astype(BF16), w1_ref[...], preferred_element_type=F32)
             + jnp.dot(a2_ref[rows, :].astype(BF16), w2_ref[...], preferred_element_type=F32))
        o_ref[rows, :] = r_ref[rows, :] + _rms(f, g_ref[...])


def _proj_out(parts, weights, resid, g):
    m, n = resid.shape
    tm = _tile(m, 512)
    row = lambda width: pl.BlockSpec((tm, width), lambda i: (i, 0))
    full = lambda arr: pl.BlockSpec(arr.shape, lambda i: (0, 0))
    return pl.pallas_call(
        _proj_out_body,
        grid=(m // tm,),
        in_specs=[row(a.shape[1]) for a in parts] + [full(w) for w in weights] + [row(n), full(g)],
        out_specs=row(n),
        out_shape=jax.ShapeDtypeStruct((m, n), F32),
        compiler_params=_cp("arbitrary"),
        name="proj_out_resnorm",
    )(*parts, *weights, resid, g)


def _mlp_body(x_ref, g_in_ref, w1_ref, w2_ref, g_out_ref, o_ref, xn_ref, acc_ref):
    j = pl.program_id(1)

    @pl.when(j == 0)
    def _():
        xn_ref[...] = _rms(x_ref[...], g_in_ref[...]).astype(BF16)
        acc_ref[...] = jnp.zeros_like(acc_ref)

    h = jnp.dot(xn_ref[...], w1_ref[...], preferred_element_type=F32)
    h = jnp.square(jnp.maximum(h, 0.0)).astype(BF16)
    acc_ref[...] += jnp.dot(h, w2_ref[...], preferred_element_type=F32)

    @pl.when(j == pl.num_programs(1) - 1)
    def _():
        o_ref[...] = x_ref[...] + _rms(acc_ref[...], g_out_ref[...])


def _mlp(x, g_in, w1, w2, g_out):
    m, k = x.shape
    ff = w1.shape[1]
    tm = _tile(m, 512)
    tf = 1024
    return pl.pallas_call(
        _mlp_body,
        grid=(m // tm, ff // tf),
        in_specs=[pl.BlockSpec((tm, k), lambda i, j: (i, 0)),
                  pl.BlockSpec((1, k), lambda i, j: (0, 0)),
                  pl.BlockSpec((k, tf), lambda i, j: (0, j)),
                  pl.BlockSpec((tf, k), lambda i, j: (j, 0)),
                  pl.BlockSpec((1, k), lambda i, j: (0, 0))],
        out_specs=pl.BlockSpec((tm, k), lambda i, j: (i, 0)),
        out_shape=jax.ShapeDtypeStruct((m, k), F32),
        scratch_shapes=[pltpu.VMEM((tm, k), BF16), pltpu.VMEM((tm, k), F32)],
        compiler_params=_cp("arbitrary", "arbitrary"),
        name="mlp_resnorm",
    )(x, g_in, w1, w2, g_out)


def _conv_body(x_ref, p_ref, n_ref, w_ref, b_ref, o_ref, *, nblk, qk):
    i = pl.program_id(1)
    j = pl.program_id(2)
    x = x_ref[...].astype(F32)
    tm = x.shape[0]
    row = lax.broadcasted_iota(jnp.int32, x.shape, 0)
    halo = p_ref.shape[0]
    prev_row = jnp.where(i == 0, 0.0, p_ref[halo - 1:halo, :].astype(F32))
    next_row = jnp.where(i == nblk - 1, 0.0, n_ref[0:1, :].astype(F32))
    xm = jnp.where(row == 0, prev_row, pltpu.roll(x, 1, 0))
    xp = jnp.where(row == tm - 1, next_row, pltpu.roll(x, tm - 1, 0))
    w = w_ref[...]
    y = b_ref[...] + xm * w[0:1] + x * w[1:2] + xp * w[2:3]
    if qk:
        y = y * jax.nn.sigmoid(y) * jnp.where(j >= 2, ML_DH ** -0.5, 1.0)
    o_ref[...] = y.astype(o_ref.dtype)


def _short_conv(proj, conv_w, conv_b, nb, seq, qk):
    mtot = proj.shape[0]
    tm = _tile(seq, 2048)
    nblk = seq // tm
    col0, ncol = (3, 4) if qk else (0, 3)
    ppo = 2 if qk else 1
    halo = 2 * SUBLANE
    last_halo = mtot // halo - 1
    tmh = tm // halo

    def x_map(b, i, j):
        return (b * nblk + i, col0 + j)

    def p_map(b, i, j):
        return (jnp.maximum((b * nblk + i) * tmh - 1, 0), col0 + j)

    def n_map(b, i, j):
        return (jnp.minimum((b * nblk + i + 1) * tmh, last_halo), col0 + j)

    return pl.pallas_call(
        functools.partial(_conv_body, nblk=nblk, qk=qk),
        grid=(nb, nblk, ncol),
        in_specs=[pl.BlockSpec((tm, D_HY), x_map),
                  pl.BlockSpec((halo, D_HY), p_map),
                  pl.BlockSpec((halo, D_HY), n_map),
                  pl.BlockSpec((3, D_HY), lambda b, i, j: (0, col0 + j)),
                  pl.BlockSpec((1, D_HY), lambda b, i, j: (0, col0 + j))],
        out_specs=pl.BlockSpec((None, tm, D_HY), lambda b, i, j: (j // ppo, b * nblk + i, j % ppo)),
        out_shape=jax.ShapeDtypeStruct((ncol // ppo, nb * seq, ppo * D_HY), BF16 if qk else F32),
        compiler_params=_cp("arbitrary", "arbitrary", "arbitrary"),
        name="short_conv_qk" if qk else "short_conv_hy",
    )(proj, proj, proj, conv_w, conv_b)


def _dft_tables(seq):
    n = 2 * seq
    n1 = n // DFT_N2
    n1h = n1 // 2
    n1k = n1h + 2 * SUBLANE
    k1 = jnp.arange(n1k, dtype=jnp.int32)
    t1 = jnp.arange(n1h, dtype=jnp.int32)
    kept = (k1 <= n1h).astype(F32)[:, None]
    weight = kept * jnp.where((k1 == 0) | (k1 == n1h), 1.0, 2.0)[:, None]
    ang = ((k1[:, None] * t1[None, :]) % n1).astype(F32) * (2.0 * math.pi / n1)
    lead_f = jnp.concatenate([kept * jnp.cos(ang), -kept * jnp.sin(ang)], axis=0).astype(BF16)
    lead_ic = (weight * jnp.cos(ang)).T.astype(BF16)
    lead_is = (-weight * jnp.sin(ang)).T.astype(BF16)
    t1f = jnp.arange(n1, dtype=jnp.int32)
    angf = ((k1[:, None] * t1f[None, :]) % n1).astype(F32) * (2.0 * math.pi / n1)
    lead_ff = jnp.concatenate([kept * jnp.cos(angf), -kept * jnp.sin(angf)], axis=0).astype(BF16)
    t2 = jnp.arange(DFT_N2, dtype=jnp.int32)
    k2 = jnp.arange(DFT_N2, dtype=jnp.int32)
    idx = (t2[None, None, :] * (k2[None, :, None] * n1 + k1[:, None, None])) % n
    ang2 = idx.astype(F32) * (2.0 * math.pi / n)
    f_re = jnp.cos(ang2)
    f_im = -jnp.sin(ang2)
    mid_f = jnp.concatenate([jnp.concatenate([f_re, -f_im], axis=2),
                             jnp.concatenate([f_im, f_re], axis=2)], axis=1).astype(BF16)
    g_re = jnp.swapaxes(f_re, 1, 2)
    g_im = jnp.swapaxes(f_im, 1, 2)
    mid_g = jnp.concatenate([jnp.concatenate([g_re, g_im], axis=2),
                             jnp.concatenate([-g_im, g_re], axis=2)], axis=1).astype(BF16)
    return dict(n=n, n1=n1k, n1h=n1h, lead_f=lead_f, lead_ff=lead_ff, lead_ic=lead_ic, lead_is=lead_is,
                mid_f=mid_f, mid_g=mid_g)


def _hy_features(seq):
    tiles = 2 * seq // (SUBLANE * DFT_N2)
    half_rows = SUBLANE * DFT_N2 // 2
    tile = jnp.arange(tiles, dtype=jnp.int32)[:, None, None]
    row = jnp.arange(half_rows, dtype=jnp.int32)[None, :, None]
    side = jnp.arange(2, dtype=jnp.int32)[None, None, :]
    t2 = side * (DFT_N2 // 2) + row // SUBLANE
    t1 = tile * SUBLANE + row % SUBLANE
    slot = t1 * DFT_N2 + t2
    pos = jnp.where(slot < seq, slot, (2 * seq - slot) % seq).astype(F32)[..., None]
    t = pos * (1.0 / (seq - 1))
    w = 2.0 * math.pi * pos / seq
    f = jnp.linspace(1e-4, HY_BANDS - 1, HY_BANDS, dtype=F32)
    z = jnp.concatenate([t, jnp.cos(f * w), -jnp.sin(f * w)], axis=-1)
    z = jnp.pad(z, ((0, 0), (0, 0), (0, 0), (0, LANE - HY_EMB)))
    return z.reshape(tiles * half_rows, 2 * LANE)


def _hy_deltas():
    d = jnp.abs(jnp.linspace(math.log(HY_TARGET) / HY_SLOW_DECAY,
                             math.log(HY_TARGET) / HY_FAST_DECAY, D_HY, dtype=F32))
    return jnp.tile(d, 2)[None, :]


def _hy_filter_body(z_ref, w1_ref, b1_ref, fr_ref, w2_ref, b2_ref, w3_ref, dl_ref, k_ref, ss_ref, *, half):
    i = pl.program_id(0)
    hi = lax.Precision.HIGHEST
    z = z_ref[...]
    fr = fr_ref[...]
    h = jnp.sin(fr * (jnp.dot(z, w1_ref[...], precision=hi, preferred_element_type=F32) + b1_ref[...]))
    h = jnp.sin(fr * (jnp.dot(h, w2_ref[...], precision=hi, preferred_element_type=F32) + b2_ref[...]))
    hb = h.astype(BF16)
    dl = dl_ref[...]
    ha = jnp.dot(hb, w3_ref[0], preferred_element_type=F32) * jnp.exp(-z[:, 0:1] * dl)
    hb = jnp.dot(hb, w3_ref[1], preferred_element_type=F32) * jnp.exp(-z[:, LANE:LANE + 1] * dl)
    row = lax.broadcasted_iota(jnp.int32, ha.shape, 0)
    ha = jnp.where(row + (i - half) * ha.shape[0] == 0, 0.0, ha)
    nc = ha.shape[1]
    h2 = DFT_N2 // 2
    for t2 in range(h2):
        k_ref[:, t2 * nc:(t2 + 1) * nc] = ha[t2 * SUBLANE:(t2 + 1) * SUBLANE, :]
        k_ref[:, (h2 + t2) * nc:(h2 + t2 + 1) * nc] = hb[t2 * SUBLANE:(t2 + 1) * SUBLANE, :]
    part = jnp.broadcast_to(jnp.sum(ha * ha, axis=0, keepdims=True)
                            + jnp.sum(hb * hb, axis=0, keepdims=True), ss_ref.shape)

    @pl.when(i == 0)
    def _():
        ss_ref[...] = part

    @pl.when(i > 0)
    def _():
        ss_ref[...] += part


def _hy_filter(seq, z, w1, b1, fr, w2, b2, w3, deltas):
    nc = 2 * D_HY
    tm = SUBLANE * DFT_N2
    steps = 2 * seq // tm
    half = steps // 2
    full = lambda shape: pl.BlockSpec(shape, lambda i: (0, 0))
    return pl.pallas_call(
        functools.partial(_hy_filter_body, half=half),
        grid=(steps,),
        in_specs=[pl.BlockSpec((tm // 2, 2 * LANE), lambda i: (i, 0)),
                  full((2 * LANE, LANE)), full((1, LANE)), full((1, LANE)),
                  full((LANE, LANE)), full((1, LANE)),
                  pl.BlockSpec((None, 2, LANE, nc), lambda i: (i // half, 0, 0, 0)), full((1, nc))],
        out_specs=[pl.BlockSpec((SUBLANE, DFT_N2 * nc), lambda i: (i, 0)), full((SUBLANE, nc))],
        out_shape=[jax.ShapeDtypeStruct((2 * seq // DFT_N2, DFT_N2 * nc), F32),
                   jax.ShapeDtypeStruct((SUBLANE, nc), F32)],
        compiler_params=_cp("arbitrary"),
        name="hyena_filter",
    )(z, w1, b1, fr, w2, b2, w3, deltas)


def _lead_fwd_body(x_ref, f_ref, re_ref, im_ref):
    n1 = re_ref.shape[0]
    r = jnp.dot(f_ref[...], x_ref[...].astype(BF16), preferred_element_type=F32)
    re_ref[...] = r[:n1].astype(BF16)
    im_ref[...] = r[n1:].astype(BF16)


def _lead_fwd(x, table, plane, nb):
    n1, rows = table.shape[0] // 2, table.shape[1]
    cols = x.shape[-1]
    tn = _tile(cols, 4096)
    return pl.pallas_call(
        _lead_fwd_body,
        grid=(nb, cols // tn),
        in_specs=[pl.BlockSpec((None, rows, tn), lambda b, j: (plane, b, j)),
                  pl.BlockSpec((2 * n1, rows), lambda b, j: (0, 0))],
        out_specs=[pl.BlockSpec((None, n1, tn), lambda b, j: (b, 0, j))] * 2,
        out_shape=[jax.ShapeDtypeStruct((nb, n1, cols), BF16)] * 2,
        compiler_params=_cp("arbitrary", "arbitrary"),
        name="hyena_lead_fwd",
    )(x, table)


def _mid_spec_body(are_ref, aim_ref, f_ref, ss_ref, kr_ref, ki_ref, *, bk):
    n2 = DFT_N2
    scale = lax.rsqrt(ss_ref[0:1, :] + EPS)

    def body(i, c):
        a = jnp.concatenate([are_ref[i], aim_ref[i]], axis=0)
        x = jnp.dot(f_ref[i], a, preferred_element_type=F32)
        kr_ref[i] = x[:n2] * scale
        ki_ref[i] = x[n2:] * scale
        return c

    lax.fori_loop(0, bk, body, 0, unroll=True)


def _mid_spec(a_re, a_im, sumsq, tb):
    n1 = tb["n1"]
    bk = _tile(n1, 8)
    a_re = a_re.reshape(n1, DFT_N2, 2 * D_HY)
    a_im = a_im.reshape(n1, DFT_N2, 2 * D_HY)
    a_spec = pl.BlockSpec((bk, DFT_N2, D_HY), lambda o, i: (i, 0, o))
    k_spec = pl.BlockSpec((None, bk, DFT_N2, D_HY), lambda o, i: (o, i, 0, 0))
    return pl.pallas_call(
        functools.partial(_mid_spec_body, bk=bk),
        grid=(2, n1 // bk),
        in_specs=[a_spec, a_spec,
                  pl.BlockSpec((bk, 2 * DFT_N2, 2 * DFT_N2), lambda o, i: (i, 0, 0)),
                  pl.BlockSpec((SUBLANE, D_HY), lambda o, i: (0, o))],
        out_specs=[k_spec, k_spec],
        out_shape=[jax.ShapeDtypeStruct((2, n1, DFT_N2, D_HY), F32)] * 2,
        compiler_params=_cp("arbitrary", "arbitrary"),
        name="hyena_mid_spectrum",
    )(a_re, a_im, tb["mid_f"], sumsq)


def _mid_conv_body(are_ref, aim_ref, f_ref, g_ref, kr_ref, ki_ref, bre_ref, bim_ref, *, bk):
    n2 = DFT_N2

    def body(i, c):
        a = jnp.concatenate([are_ref[i], aim_ref[i]], axis=0)
        x = jnp.dot(f_ref[i], a, preferred_element_type=F32)
        xr = x[:n2]
        xi = x[n2:]
        kr = kr_ref[i]
        ki = ki_ref[i]
        p = jnp.concatenate([xr * kr - xi * ki, xr * ki + xi * kr], axis=0).astype(BF16)
        q = jnp.dot(g_ref[i], p, preferred_element_type=F32)
        bre_ref[i] = q[:n2].astype(BF16)
        bim_ref[i] = q[n2:].astype(BF16)
        return c

    lax.fori_loop(0, bk, body, 0, unroll=True)


def _mid_conv(a_re, a_im, kf_re, kf_im, order, tb):
    n1 = tb["n1"]
    nb = a_re.shape[0]
    bk = _tile(n1, 8)
    a_re = a_re.reshape(nb, n1, DFT_N2, D_HY)
    a_im = a_im.reshape(nb, n1, DFT_N2, D_HY)
    a_spec = pl.BlockSpec((None, bk, DFT_N2, D_HY), lambda b, i: (b, i, 0, 0))
    t_spec = pl.BlockSpec((bk, 2 * DFT_N2, 2 * DFT_N2), lambda b, i: (i, 0, 0))
    k_spec = pl.BlockSpec((None, bk, DFT_N2, D_HY), lambda b, i: (order, i, 0, 0))
    b_re, b_im = pl.pallas_call(
        functools.partial(_mid_conv_body, bk=bk),
        grid=(nb, n1 // bk),
        in_specs=[a_spec, a_spec, t_spec, t_spec, k_spec, k_spec],
        out_specs=[a_spec, a_spec],
        out_shape=[jax.ShapeDtypeStruct((nb, n1, DFT_N2, D_HY), BF16)] * 2,
        compiler_params=_cp("arbitrary", "arbitrary"),
        name="hyena_mid_conv",
    )(a_re, a_im, tb["mid_f"], tb["mid_g"], kf_re, kf_im)
    return b_re.reshape(nb, n1, DFT_N2 * D_HY), b_im.reshape(nb, n1, DFT_N2 * D_HY)


def _lead_inv_body(bre_ref, bim_ref, c_ref, s_ref, z_ref, gate_ref, skip_ref, gn_ref, o_ref, *, inv_n, last):
    y = (jnp.dot(c_ref[...], bre_ref[...], preferred_element_type=F32)
         + jnp.dot(s_ref[...], bim_ref[...], preferred_element_type=F32)) * inv_n
    out = gate_ref[...] * (y + skip_ref[...] * z_ref[...])
    if not last:
        o_ref[...] = out
    else:
        gn = gn_ref[...]
        for c in range(out.shape[1] // D_HY):
            blk = out[:, c * D_HY:(c + 1) * D_HY]
            ms = jnp.mean(blk * blk, axis=-1, keepdims=True)
            o_ref[:, c * D_HY:(c + 1) * D_HY] = (blk * lax.rsqrt(ms + EPS) * gn).astype(o_ref.dtype)


def _lead_inv(b_re, b_im, z, z_plane, gates, gate_plane, skip, gn, tb, last):
    n1, n1h = tb["n1"], tb["n1h"]
    nb, _, cols = b_re.shape
    tn = _tile(cols, 4096)
    skip_t = jnp.tile(skip[None, :], (1, tn // D_HY))
    b_spec = pl.BlockSpec((None, n1, tn), lambda b, j: (b, 0, j))
    t_spec = pl.BlockSpec((n1h, n1), lambda b, j: (0, 0))
    return pl.pallas_call(
        functools.partial(_lead_inv_body, inv_n=1.0 / tb["n"], last=last),
        grid=(nb, cols // tn),
        in_specs=[b_spec, b_spec, t_spec, t_spec,
                  pl.BlockSpec((None, n1h, tn), lambda b, j: (z_plane, b, j)),
                  pl.BlockSpec((None, n1h, tn), lambda b, j: (gate_plane, b, j)),
                  pl.BlockSpec((1, tn), lambda b, j: (0, 0)),
                  pl.BlockSpec((1, D_HY), lambda b, j: (0, 0))],
        out_specs=pl.BlockSpec((None, n1h, tn), lambda b, j: (0, b, j)),
        out_shape=jax.ShapeDtypeStruct((1, nb * n1h, cols), BF16 if last else F32),
        compiler_params=_cp("arbitrary", "arbitrary"),
        name="hyena_lead_inv",
    )(b_re, b_im, tb["lead_ic"], tb["lead_is"], z, gates, skip_t, gn)


def _hyena(conv_out, nb, seq, tb, kf_re, kf_im, skip, gn):
    n1h = tb["n1h"]
    planes = conv_out.reshape(conv_out.shape[0], nb * n1h, DFT_N2 * D_HY)
    z, z_plane = planes, 0
    for order in range(2):
        a_re, a_im = _lead_fwd(z, tb["lead_f"], z_plane, nb)
        b_re, b_im = _mid_conv(a_re, a_im, kf_re, kf_im, order, tb)
        z = _lead_inv(b_re, b_im, z, z_plane, planes, 1 + order, skip[order], gn, tb, order == 1)
        z_plane = 0
    return z.reshape(nb * seq, D_HY)


def _hyena_spectra(seq, tb, w1, b1, fr, w2, b2, w3):
    w1 = jnp.pad(w1, ((0, LANE - HY_EMB), (0, 0)))
    zero1 = jnp.zeros_like(w1)
    w1 = jnp.concatenate([jnp.concatenate([w1, zero1], axis=1),
                          jnp.concatenate([zero1, w1], axis=1)], axis=0)
    zero2 = jnp.zeros_like(w2)
    w2 = jnp.concatenate([jnp.concatenate([w2, zero2], axis=1),
                          jnp.concatenate([zero2, w2], axis=1)], axis=0)
    pair = lambda v: jnp.concatenate([v, v])[None, :]
    w3 = w3.reshape(HY_FILT, 2, 2, D_HY).transpose(2, 0, 1, 3).reshape(2, HY_FILT, 2 * D_HY)
    zero3 = jnp.zeros_like(w3)
    w3 = jnp.stack([jnp.concatenate([w3, zero3], axis=1),
                    jnp.concatenate([zero3, w3], axis=1)], axis=1).astype(BF16)
    taps, sumsq = _hy_filter(seq, _hy_features(seq), w1, pair(b1), pair(fr), w2, pair(b2), w3,
                             _hy_deltas())
    a_re, a_im = _lead_fwd(taps[None], tb["lead_ff"], 0, 1)
    return _mid_spec(a_re[0], a_im[0], sumsq, tb)


def _s5_scan_body(u_ref, pm_ref, bd_ref, a_ref, cd_ref, x0_ref, o_ref, bu_ref, st_ref, y_ref,
                  *, tstep, reverse, final_state):
    tb = pl.program_id(0)
    ns = S5_STREAMS
    half = S5_STATE
    cw = S5_BLOCK_GROUPS * S5_P

    @pl.when(tb == 0)
    def _():
        st_ref[...] = x0_ref[...]

    u = u_ref[...].reshape(ns * tstep, D_S5).astype(BF16)
    u_tm = jnp.dot(pm_ref[...], u, preferred_element_type=F32).astype(BF16)

    def project_in(blk):
        bu_ref[:, 2 * blk * cw:2 * (blk + 1) * cw] = jnp.dot(
            u_tm[:, blk * LANE:(blk + 1) * LANE], bd_ref[blk], preferred_element_type=F32)

    def scan(blk):
        bre = slice(2 * blk * cw, (2 * blk + 1) * cw)
        bim = slice((2 * blk + 1) * cw, (2 * blk + 2) * cw)
        sre = slice(blk * cw, (blk + 1) * cw)
        sim = slice(half + blk * cw, half + (blk + 1) * cw)
        ar = a_ref[:, sre]
        ai = a_ref[:, sim]
        xr = st_ref[:, sre]
        xi = st_ref[:, sim]
        for i in range(tstep):
            t = tstep - 1 - i if reverse else i
            rows = slice(t * ns, (t + 1) * ns)
            xr, xi = (ar * xr - ai * xi + bu_ref[rows, bre],
                      ar * xi + ai * xr + bu_ref[rows, bim])
            if not final_state:
                bu_ref[rows, bre] = xr
                bu_ref[rows, bim] = xi
        st_ref[:, sre] = xr
        st_ref[:, sim] = xi

    def project_out(blk):
        y_ref[blk] = jnp.dot(bu_ref[:, 2 * blk * cw:2 * (blk + 1) * cw].astype(BF16), cd_ref[blk],
                             preferred_element_type=F32)

    project_in(0)
    for blk in range(S5_NBLOCKS):
        if blk + 1 < S5_NBLOCKS:
            project_in(blk + 1)
        scan(blk)
        if not final_state:
            project_out(blk)

    if final_state:
        @pl.when(tb == pl.num_programs(0) - 1)
        def _():
            o_ref[...] = st_ref[...]
    else:
        for s in range(ns):
            for blk in range(S5_NBLOCKS):
                o_ref[s, :, blk * LANE:(blk + 1) * LANE] = y_ref[blk, pl.ds(s, tstep, stride=ns), :]


def _s5_scan(proj, pmat, bd, a_b, cd, x0, tlen, reverse, final_state):
    ns = S5_STREAMS
    d = 1 if reverse else 0
    tstep = pmat.shape[0] // ns
    nt = tlen // tstep
    proj3 = proj.reshape(ns, tlen, proj.shape[1])
    blk_in = S5_BLOCK_GROUPS * S5_GROUP
    blk_state = 2 * S5_BLOCK_GROUPS * S5_P

    def window(t):
        return nt - 1 - t if reverse else t

    if final_state:
        out_spec = pl.BlockSpec((ns, 2 * S5_STATE), lambda t: (0, 0))
        out_shape = jax.ShapeDtypeStruct((ns, 2 * S5_STATE), F32)
    else:
        out_spec = pl.BlockSpec((ns, tstep, D_S5), lambda t: (0, window(t), 0))
        out_shape = jax.ShapeDtypeStruct((ns, tlen, D_S5), F32)
    out = pl.pallas_call(
        functools.partial(_s5_scan_body, tstep=tstep, reverse=reverse, final_state=final_state),
        grid=(nt,),
        in_specs=[pl.BlockSpec((ns, tstep, D_S5), lambda t: (0, window(t), COL_U)),
                  pl.BlockSpec((ns * tstep, ns * tstep), lambda t: (0, 0)),
                  pl.BlockSpec((None, S5_NBLOCKS, blk_in, blk_state), lambda t: (d, 0, 0, 0)),
                  pl.BlockSpec((None, ns, 2 * S5_STATE), lambda t: (d, 0, 0)),
                  pl.BlockSpec((None, S5_NBLOCKS, blk_state, blk_in), lambda t: (d, 0, 0, 0)),
                  pl.BlockSpec((None, ns, 2 * S5_STATE), lambda t: (d, 0, 0))],
        out_specs=out_spec,
        out_shape=out_shape,
        scratch_shapes=[pltpu.VMEM((tstep * ns, 2 * S5_STATE), F32),
                        pltpu.VMEM((ns, 2 * S5_STATE), F32),
                        pltpu.VMEM((D_S5 // LANE, tstep * ns, LANE), F32)],
        compiler_params=_cp("arbitrary"),
        name="s5_end_state" if final_state else "s5_scan",
    )(proj3, pmat, bd, a_b, cd, x0)
    return out if final_state else out.reshape(ns * tlen, D_S5)


def _s5_params(lam_re, lam_im, log_dt, b_re, b_im, c_re, c_im):
    lr = jnp.minimum(lam_re, -1e-4)
    li = lam_im
    dt = jnp.exp(log_dt)[..., None]
    er = jnp.exp(lr * dt)
    a_re = er * jnp.cos(li * dt)
    a_im = er * jnp.sin(li * dt)
    den = lr * lr + li * li
    q_re = ((a_re - 1.0) * lr + a_im * li) / den
    q_im = (a_im * lr - (a_re - 1.0) * li) / den
    bb_re = q_re[..., None] * b_re - q_im[..., None] * b_im
    bb_im = q_re[..., None] * b_im + q_im[..., None] * b_re
    nbk, gb = S5_NBLOCKS, S5_BLOCK_GROUPS
    eye = jnp.eye(gb, dtype=F32)

    def in_mat(x):
        x = x.reshape(2, nbk, gb, S5_P, S5_GROUP)
        return jnp.einsum("dbgph,gk->dbghkp", x, eye).reshape(2, nbk, gb * S5_GROUP, gb * S5_P)

    def out_mat(x):
        x = x.reshape(2, nbk, gb, S5_GROUP, S5_P)
        return jnp.einsum("dbghp,gk->dbgpkh", x, eye).reshape(2, nbk, gb * S5_P, gb * S5_GROUP)

    bd = jnp.concatenate([in_mat(bb_re), in_mat(bb_im)], axis=3).astype(BF16)
    cd = jnp.concatenate([out_mat(c_re), out_mat(-c_im)], axis=2).astype(BF16)
    a_bar = jnp.concatenate([a_re.reshape(2, S5_STATE), a_im.reshape(2, S5_STATE)], axis=1)
    return (lr * dt).reshape(2, S5_STATE), (li * dt).reshape(2, S5_STATE), a_bar, bd, cd


def _s5_row_perm(tstep):
    ns = S5_STREAMS
    r = jnp.arange(ns * tstep)
    src = (r % ns) * tstep + r // ns
    return (src[:, None] == jnp.arange(ns * tstep)[None, :]).astype(BF16)


def _s5_mixer(proj, nb, seq, s5p):
    la_re, la_im, a_bar, bd, cd = s5p
    ns = S5_STREAMS
    ncs = ns // nb
    tlen = seq // ncs
    pmat = _s5_row_perm(_tile(tlen, S5_TSTEP))
    a_b = jnp.broadcast_to(a_bar[:, None, :], (2, ns, 2 * S5_STATE))
    zeros = jnp.zeros((2, ns, 2 * S5_STATE), F32)
    mag = jnp.exp(la_re * tlen)
    at_re = (mag * jnp.cos(la_im * tlen))[:, None, :]
    at_im = (mag * jnp.sin(la_im * tlen))[:, None, :]
    ys = []
    for d in range(2):
        x_end = _s5_scan(proj, pmat, bd, a_b, cd, zeros, tlen, d == 1, True)
        xe_re = x_end[:, :S5_STATE].reshape(nb, ncs, S5_STATE)
        xe_im = x_end[:, S5_STATE:].reshape(nb, ncs, S5_STATE)
        order = range(ncs) if d == 0 else range(ncs - 1, -1, -1)
        cr = jnp.zeros((nb, S5_STATE), F32)
        ci = jnp.zeros((nb, S5_STATE), F32)
        rows = [None] * ncs
        for c in order:
            rows[c] = jnp.concatenate([cr, ci], axis=-1)
            cr, ci = (at_re[d] * cr - at_im[d] * ci + xe_re[:, c],
                      at_re[d] * ci + at_im[d] * cr + xe_im[:, c])
        x0 = jnp.stack(rows, axis=1).reshape(ns, 2 * S5_STATE)
        x0 = jnp.stack([x0, x0])
        ys.append(_s5_scan(proj, pmat, bd, a_b, cd, x0, tlen, d == 1, False))
    return ys


def _s5_post_body(yf_ref, yb_ref, u_ref, d_ref, w_ref, b_ref, gn_ref, o_ref):
    y = yf_ref[...] + yb_ref[...] + d_ref[...] * u_ref[...].astype(F32)
    g = jax.nn.gelu(y)
    r = jnp.dot(g.astype(BF16), w_ref[...], preferred_element_type=F32) + b_ref[...]
    out = r[:, :D_S5] * jax.nn.sigmoid(r[:, D_S5:])
    ms = jnp.mean(out * out, axis=-1, keepdims=True)
    o_ref[...] = (out * lax.rsqrt(ms + EPS) * gn_ref[...]).astype(o_ref.dtype)


def _s5_post(y_f, y_b, proj, d, glu_w, glu_b, gn):
    m = y_f.shape[0]
    tm = _tile(m, 512)
    full = lambda shape: pl.BlockSpec(shape, lambda i: (0, 0))
    return pl.pallas_call(
        _s5_post_body,
        grid=(m // tm,),
        in_specs=[pl.BlockSpec((tm, D_S5), lambda i: (i, 0)),
                  pl.BlockSpec((tm, D_S5), lambda i: (i, 0)),
                  pl.BlockSpec((tm, D_S5), lambda i: (i, COL_U)),
                  full((1, D_S5)), full((D_S5, 2 * D_S5)), full((1, 2 * D_S5)), full((1, D_S5))],
        out_specs=pl.BlockSpec((tm, D_S5), lambda i: (i, 0)),
        out_shape=jax.ShapeDtypeStruct((m, D_S5), BF16),
        compiler_params=_cp("arbitrary"),
        name="s5_glu_norm",
    )(y_f, y_b, proj, d, glu_w, glu_b, gn)


def _split3(x):
    hi = x.astype(BF16)
    rest = x - hi.astype(F32)
    mid = rest.astype(BF16)
    return hi, mid, (rest - mid.astype(F32)).astype(BF16)


def _mlstm_body(*refs, reverse, epilogue):
    if epilogue:
        (q_ref, k_ref, v_ref, g_ref, gt_ref, bias_ref, biast_ref, seen_ref, upto_ref, hf_ref, o_gate_ref,
         gn_ref, out_ref, c_ref, n_ref, m_ref) = refs
    else:
        (q_ref, k_ref, v_ref, g_ref, gt_ref, bias_ref, biast_ref, seen_ref, upto_ref,
         out_ref, c_ref, n_ref, m_ref) = refs
        hf_ref = o_gate_ref = gn_ref = None

    @pl.when(pl.program_id(2) == 0)
    def _():
        c_ref[...] = jnp.zeros_like(c_ref)
        n_ref[...] = jnp.zeros_like(n_ref)
        m_ref[...] = jnp.zeros_like(m_ref)

    tc = ML_CHUNK
    r = lax.broadcasted_iota(jnp.int32, (tc, tc), 0)
    s = lax.broadcasted_iota(jnp.int32, (tc, tc), 1)
    seen = (s >= r) if reverse else (s <= r)
    g = g_ref[...] + bias_ref[...]
    gt = gt_ref[...] + biast_ref[:, 0:1]
    for hh in range(ML_HEADS_PER_STEP):
        _mlstm_head(hh, g, gt, seen, q_ref, k_ref, v_ref, seen_ref, upto_ref, hf_ref, o_gate_ref,
                    gn_ref, out_ref, c_ref, n_ref, m_ref, reverse=reverse)


def _mlstm_head(hh, g, gt, seen, q_ref, k_ref, v_ref, seen_ref, upto_ref, hf_ref, o_gate_ref, gn_ref,
                out_ref, c_ref, n_ref, m_ref, *, reverse):
    head = pl.program_id(1) * ML_HEADS_PER_STEP + hh
    cols = slice(hh * ML_DH, (hh + 1) * ML_DH)
    tc = ML_CHUNK
    gate_i = 2 if reverse else 0
    idx_i = gate_i * ML_HEADS + head
    idx_f = idx_i + ML_HEADS
    tiled = lambda x, n: x if n == LANE else jnp.concatenate([x] * (n // LANE), axis=1)
    wide = lambda x: tiled(x, ML_DH)
    span = lambda x: tiled(x, tc)
    dot = functools.partial(jnp.dot, preferred_element_type=F32)

    lane = lax.broadcasted_iota(jnp.int32, g.shape, 1)
    ig = jnp.broadcast_to(jnp.sum(jnp.where(lane == idx_i, g, 0.0), axis=1, keepdims=True), g.shape)
    fg = jnp.broadcast_to(jnp.sum(jnp.where(lane == idx_f, g, 0.0), axis=1, keepdims=True), g.shape)
    sub = lax.broadcasted_iota(jnp.int32, gt.shape, 0)
    ig_row = jnp.sum(jnp.where(sub == idx_i, gt, 0.0), axis=0, keepdims=True)
    fg_row = jnp.sum(jnp.where(sub == idx_f, gt, 0.0), axis=0, keepdims=True)
    lf = jax.nn.log_sigmoid(fg)
    lf_row = jax.nn.log_sigmoid(fg_row)

    seen_b = seen_ref[...]
    upto_b = upto_ref[...]
    b_col = sum(dot(seen_b, piece) for piece in _split3(lf))
    lf_rows = jnp.broadcast_to(lf_row, (SUBLANE, tc))
    b_row = sum(dot(piece, upto_b) for piece in _split3(lf_rows))[0:1]
    b_last = jnp.sum(lf_row, axis=1, keepdims=True)

    m_prev = m_ref[hh:hh + 1, 0:1]
    src = ig_row - b_row
    dmat = jnp.where(seen, span(b_col) + src, -jnp.inf)
    g_car = b_col + m_prev
    mt = jnp.maximum(g_car, jnp.max(dmat, axis=1, keepdims=True))
    inter = jnp.exp(g_car - mt)
    qb = q_ref[:, cols]
    kb = k_ref[:, cols]
    vb = v_ref[:, cols].astype(BF16)
    ones = jnp.ones((tc, LANE), BF16)
    qk = lax.dot_general(qb, kb, (((1,), (1,)), ((), ())), preferred_element_type=F32)
    sc = (qk * jnp.exp(dmat - span(mt))).astype(BF16)
    c_state = c_ref[hh]
    n_state = n_ref[hh]
    num = wide(inter) * dot(qb, c_state.astype(BF16)) + dot(sc, vb)
    den = inter * dot(qb, n_state.astype(BF16)) + dot(sc, ones)
    h = num * wide(1.0 / jnp.maximum(jnp.abs(den), jnp.exp(-mt)))

    a_row = b_last + src
    a_col = b_last - b_col + ig
    m_new = jnp.maximum(b_last + m_prev, jnp.max(a_row, axis=1, keepdims=True))
    decay = jnp.exp(b_last + m_prev - m_new)
    kw = (kb.astype(F32) * wide(jnp.exp(a_col - m_new))).astype(BF16)
    upd = lax.dot_general(kw, jnp.concatenate([vb, ones], axis=1), (((0,), (0,)), ((), ())),
                          preferred_element_type=F32)
    c_ref[hh] = decay * c_state + upd[:, :ML_DH]
    n_ref[hh] = decay * n_state + upd[:, ML_DH:]
    m_ref[hh:hh + 1, 0:1] = m_new

    if hf_ref is not None:
        h = h + hf_ref[:, cols]
        ms = jnp.mean(h * h, axis=-1, keepdims=True)
        hn = h * lax.rsqrt(ms + EPS) * gn_ref[:, cols]
        out_ref[:, cols] = (hn * jax.nn.sigmoid(o_gate_ref[:, cols].astype(F32))).astype(out_ref.dtype)
    else:
        out_ref[:, cols] = h


def _mlstm_dir(qk_planes, proj, gates, gates_t, bias, bias_t, nb, seq, reverse, h_fwd=None, gn=None):
    tc = ML_CHUNK
    nch = seq // tc
    hps = ML_HEADS_PER_STEP
    width = hps * ML_DH
    col_v = (N_CONV + D_S5) // width
    col_o = (N_CONV + D_S5 + D_ML) // width

    def chunk(b, ci):
        return b * nch + (nch - 1 - ci if reverse else ci)

    in_specs = [
        pl.BlockSpec((None, tc, width), lambda b, h, ci: (0, chunk(b, ci), h)),
        pl.BlockSpec((None, tc, width), lambda b, h, ci: (1, chunk(b, ci), h)),
        pl.BlockSpec((tc, width), lambda b, h, ci: (chunk(b, ci), col_v + h)),
        pl.BlockSpec((tc, LANE), lambda b, h, ci: (chunk(b, ci), 0)),
        pl.BlockSpec((4 * ML_HEADS, tc), lambda b, h, ci: (0, chunk(b, ci))),
        pl.BlockSpec((1, LANE), lambda b, h, ci: (0, 0)),
        pl.BlockSpec((4 * ML_HEADS, LANE), lambda b, h, ci: (0, 0)),
        pl.BlockSpec((tc, tc), lambda b, h, ci: (0, 0)),
        pl.BlockSpec((tc, tc), lambda b, h, ci: (0, 0)),
    ]
    pos = jnp.arange(tc)
    seen = (pos[None, :] >= pos[:, None]) if reverse else (pos[None, :] <= pos[:, None])
    seen = seen.astype(BF16)
    args = [qk_planes, qk_planes, proj, gates, gates_t, bias, bias_t, seen, seen.T]
    epilogue = h_fwd is not None
    if epilogue:
        in_specs += [
            pl.BlockSpec((tc, width), lambda b, h, ci: (chunk(b, ci), h)),
            pl.BlockSpec((tc, width), lambda b, h, ci: (chunk(b, ci), col_o + h)),
            pl.BlockSpec((1, width), lambda b, h, ci: (0, h)),
        ]
        args += [h_fwd, proj, gn]
    return pl.pallas_call(
        functools.partial(_mlstm_body, reverse=reverse, epilogue=epilogue),
        grid=(nb, ML_HEADS // hps, nch),
        in_specs=in_specs,
        out_specs=pl.BlockSpec((tc, width), lambda b, h, ci: (chunk(b, ci), h)),
        out_shape=jax.ShapeDtypeStruct((nb * seq, D_ML), BF16 if epilogue else F32),
        scratch_shapes=[pltpu.VMEM((hps, ML_DH, ML_DH), F32), pltpu.VMEM((hps, ML_DH, LANE), F32),
                        pltpu.VMEM((SUBLANE, LANE), F32)],
        compiler_params=_cp("arbitrary", "arbitrary", "arbitrary"),
        name="mlstm_bwd_norm" if reverse else "mlstm_fwd",
    )(*args)


def _layer(x, nb, seq, tb, p):
    proj, gates = _proj_in(x, p["ln_g"][0:1], p["w_in"], p["w_gate"])
    gates_t = gates[:, :4 * ML_HEADS].T
    gn = p["group_norm"]
    gn_hy = gn[None, :D_HY]
    gn_s5 = gn[None, D_HY:D_HY + D_S5]
    gn_ml = gn[None, D_HY + D_S5:]

    hy_planes = _short_conv(proj, p["conv_w"], p["conv_b"][None, :], nb, seq, False)
    qk_planes = _short_conv(proj, p["conv_w"], p["conv_b"][None, :], nb, seq, True)
    kf_re, kf_im = _hyena_spectra(seq, tb, p["hy_w1"], p["hy_b1"], p["hy_freq"], p["hy_w2"],
                                  p["hy_b2"], p["hy_w3"])
    y_hy = _hyena(hy_planes, nb, seq, tb, kf_re, kf_im, p["hy_skip"], gn_hy)
    y_s5 = _s5_post(*_s5_mixer(proj, nb, seq, p["s5"]), proj, p["s5_d"][None, :],
                    p["s5_glu_w"], p["s5_glu_b"][None, :], gn_s5)
    ml_args = (qk_planes, proj, gates, gates_t, p["ml_bias"], p["ml_bias_t"], nb, seq)
    h_f = _mlstm_dir(*ml_args, False)
    y_ml = _mlstm_dir(*ml_args, True, h_f, gn_ml)

    w_out = p["w_out"]
    x = _proj_out([y_hy, y_s5, y_ml], [w_out[:D_HY], w_out[D_HY:D_HY + D_S5], w_out[D_HY + D_S5:]],
                  x, p["ln_g"][1:2])
    return _mlp(x, p["ln_g"][2:3], p["mlp_w1"], p["mlp_w2"], p["ln_g"][3:4])


_PARAM_NAMES = ("ln_g", "w_in", "conv_w", "conv_b", "hy_w1", "hy_b1", "hy_freq", "hy_w2", "hy_b2",
                "hy_w3", "hy_skip", "s5_lam_re", "s5_lam_im", "s5_log_dt", "s5_b_re", "s5_b_im",
                "s5_c_re", "s5_c_im", "s5_d", "s5_glu_w", "s5_glu_b", "ml_gate_b", "group_norm",
                "w_out", "mlp_w1", "mlp_w2")


def _prepare(p):
    p = dict(p)
    w_in = p["w_in"].astype(BF16)
    p["w_in"] = w_in[:, :P_MIX]
    p["w_gate"] = jnp.pad(w_in[:, P_MIX:], ((0, 0), (0, LANE - 4 * ML_HEADS)))
    p["w_out"] = p["w_out"].astype(BF16)
    p["mlp_w1"] = p["mlp_w1"].astype(BF16)
    p["mlp_w2"] = p["mlp_w2"].astype(BF16)
    p["s5_glu_w"] = p["s5_glu_w"].astype(BF16)
    p["ml_bias"] = jnp.pad(p["ml_gate_b"].reshape(1, -1), ((0, 0), (0, LANE - 4 * ML_HEADS)))
    p["ml_bias_t"] = jnp.broadcast_to(p["ml_gate_b"].reshape(-1, 1), (4 * ML_HEADS, LANE))
    p["s5"] = _s5_params(p["s5_lam_re"], p["s5_lam_im"], p["s5_log_dt"], p["s5_b_re"],
                         p["s5_b_im"], p["s5_c_re"], p["s5_c_im"])
    return p


def kernel(x_prompt, x_sample, ln_g, w_in, conv_w, conv_b, hy_w1, hy_b1, hy_freq, hy_w2, hy_b2, hy_w3,
           hy_skip, s5_lam_re, s5_lam_im, s5_log_dt, s5_b_re, s5_b_im, s5_c_re, s5_c_im, s5_d,
           s5_glu_w, s5_glu_b, ml_gate_b, group_norm, w_out, mlp_w1, mlp_w2):
    params = (ln_g, w_in, conv_w, conv_b, hy_w1, hy_b1, hy_freq, hy_w2, hy_b2, hy_w3, hy_skip,
              s5_lam_re, s5_lam_im, s5_log_dt, s5_b_re, s5_b_im, s5_c_re, s5_c_im, s5_d,
              s5_glu_w, s5_glu_b, ml_gate_b, group_norm, w_out, mlp_w1, mlp_w2)
    streams = []
    for x in (x_prompt, x_sample):
        nb, seq, _ = x.shape
        streams.append([x.reshape(nb * seq, D_MODEL), nb, seq, _dft_tables(seq)])
    for layer in range(DEPTH):
        p = _prepare({name: arr[layer] for name, arr in zip(_PARAM_NAMES, params)})
        for st in streams:
            st[0] = _layer(st[0], st[1], st[2], st[3], p)
    return tuple(st[0].reshape(x.shape) for st, x in zip(streams, (x_prompt, x_sample)))
```

```python
import functools
import math

import jax
import jax.numpy as jnp
from jax import lax
from jax.experimental import pallas as pl
from jax.experimental.pallas import tpu as pltpu

F32 = jnp.float32
BF16 = jnp.bfloat16

D_MODEL = 2048
DEPTH = 4
D_HY = 512
D_S5 = 512
D_ML = 1024
HY_EMB = 33
HY_BANDS = 16
HY_FILT = 64
HY_FAST_DECAY = 0.3
HY_SLOW_DECAY = 1.5
HY_TARGET = 1e-2
S5_GROUP = 16
S5_G = 32
S5_P = 64
S5_STATE = S5_G * S5_P
ML_HEADS = 4
ML_DH = 256
D_FF = 4 * D_MODEL
N_CONV = 3 * D_HY + 2 * D_ML
P_IN = N_CONV + D_S5 + 2 * D_ML + 4 * ML_HEADS
P_MIX = P_IN - 4 * ML_HEADS
EPS = 1e-6

LANE = 128
SUBLANE = 8
VMEM_LIMIT_BYTES = 48 * 1024 * 1024

DFT_N2 = LANE
S5_STREAMS = SUBLANE
S5_TSTEP = 64
S5_BLOCK_GROUPS = LANE // S5_GROUP
S5_NBLOCKS = S5_G // S5_BLOCK_GROUPS
ML_CHUNK = 256
ML_HEADS_PER_STEP = 4
COL_U = N_CONV // D_S5


def _cp(*sem):
    return pltpu.CompilerParams(dimension_semantics=sem, vmem_limit_bytes=VMEM_LIMIT_BYTES)


def _tile(n, pref):
    t = min(n, pref)
    while n % t:
        t //= 2
    return t


def _rms(x, g):
    ms = jnp.mean(x * x, axis=-1, keepdims=True)
    return x * lax.rsqrt(ms + EPS) * g


def _proj_in_body(x_ref, g_ref, w_ref, wg_ref, o_ref, og_ref, xn_ref):
    @pl.when(pl.program_id(1) == 0)
    def _():
        xn = _rms(x_ref[...], g_ref[...]).astype(BF16)
        xn_ref[...] = xn
        og_ref[...] = jnp.dot(xn, wg_ref[...], preferred_element_type=F32)

    o_ref[...] = jnp.dot(xn_ref[...], w_ref[...], preferred_element_type=F32).astype(o_ref.dtype)


def _proj_in(x, g, w, w_gate):
    m, k = x.shape
    n = w.shape[1]
    tm = _tile(m, 1024)
    tn = 1024
    return pl.pallas_call(
        _proj_in_body,
        grid=(m // tm, n // tn),
        in_specs=[pl.BlockSpec((tm, k), lambda i, j: (i, 0)),
                  pl.BlockSpec((1, k), lambda i, j: (0, 0)),
                  pl.BlockSpec((k, tn), lambda i, j: (0, j)),
                  pl.BlockSpec((k, LANE), lambda i, j: (0, 0))],
        out_specs=[pl.BlockSpec((tm, tn), lambda i, j: (i, j)),
                   pl.BlockSpec((tm, LANE), lambda i, j: (i, 0))],
        out_shape=[jax.ShapeDtypeStruct((m, n), BF16), jax.ShapeDtypeStruct((m, LANE), F32)],
        scratch_shapes=[pltpu.VMEM((tm, k), BF16)],
        compiler_params=_cp("arbitrary", "arbitrary"),
        name="norm_proj_in",
    )(x, g, w, w_gate)


def _proj_out_body(a0_ref, a1_ref, a2_ref, w0_ref, w1_ref, w2_ref, r_ref, g_ref, o_ref):
    half = o_ref.shape[0] // 2
    for rows in (slice(0, half), slice(half, 2 * half)):
        f = (jnp.dot(a0_ref[rows, :].astype(BF16), w0_ref[...], preferred_element_type=F32)
             + jnp.dot(a1_ref[rows, :].astype(BF16), w1_ref[...], preferred_element_type=F32)
             + jnp.dot(a2_ref[rows, :].astype(BF16), w2_ref[...], preferred_element_type=F32))
        o_ref[rows, :] = r_ref[rows, :] + _rms(f, g_ref[...])


def _proj_out(parts, weights, resid, g):
    m, n = resid.shape
    tm = _tile(m, 512)
    row = lambda width: pl.BlockSpec((tm, width), lambda i: (i, 0))
    full = lambda arr: pl.BlockSpec(arr.shape, lambda i: (0, 0))
    return pl.pallas_call(
        _proj_out_body,
        grid=(m // tm,),
        in_specs=[row(a.shape[1]) for a in parts] + [full(w) for w in weights] + [row(n), full(g)],
        out_specs=row(n),
        out_shape=jax.ShapeDtypeStruct((m, n), F32),
        compiler_params=_cp("arbitrary"),
        name="proj_out_resnorm",
    )(*parts, *weights, resid, g)


def _mlp_body(x_ref, g_in_ref, w1_ref, w2_ref, g_out_ref, o_ref, xn_ref, acc_ref):
    j = pl.program_id(1)

    @pl.when(j == 0)
    def _():
        xn_ref[...] = _rms(x_ref[...], g_in_ref[...]).astype(BF16)
        acc_ref[...] = jnp.zeros_like(acc_ref)

    h = jnp.dot(xn_ref[...], w1_ref[...], preferred_element_type=F32)
    h = jnp.square(jnp.maximum(h, 0.0)).astype(BF16)
    acc_ref[...] += jnp.dot(h, w2_ref[...], preferred_element_type=F32)

    @pl.when(j == pl.num_programs(1) - 1)
    def _():
        o_ref[...] = x_ref[...] + _rms(acc_ref[...], g_out_ref[...])


def _mlp(x, g_in, w1, w2, g_out):
    m, k = x.shape
    ff = w1.shape[1]
    tm = _tile(m, 512)
    tf = 1024
    return pl.pallas_call(
        _mlp_body,
        grid=(m // tm, ff // tf),
        in_specs=[pl.BlockSpec((tm, k), lambda i, j: (i, 0)),
                  pl.BlockSpec((1, k), lambda i, j: (0, 0)),
                  pl.BlockSpec((k, tf), lambda i, j: (0, j)),
                  pl.BlockSpec((tf, k), lambda i, j: (j, 0)),
                  pl.BlockSpec((1, k), lambda i, j: (0, 0))],
        out_specs=pl.BlockSpec((tm, k), lambda i, j: (i, 0)),
        out_shape=jax.ShapeDtypeStruct((m, k), F32),
        scratch_shapes=[pltpu.VMEM((tm, k), BF16), pltpu.VMEM((tm, k), F32)],
        compiler_params=_cp("arbitrary", "arbitrary"),
        name="mlp_resnorm",
    )(x, g_in, w1, w2, g_out)


def _conv_body(x_ref, p_ref, n_ref, w_ref, b_ref, o_ref, *, nblk, qk):
    i = pl.program_id(1)
    j = pl.program_id(2)
    x = x_ref[...].astype(F32)
    tm = x.shape[0]
    row = lax.broadcasted_iota(jnp.int32, x.shape, 0)
    halo = p_ref.shape[0]
    prev_row = jnp.where(i == 0, 0.0, p_ref[halo - 1:halo, :].astype(F32))
    next_row = jnp.where(i == nblk - 1, 0.0, n_ref[0:1, :].astype(F32))
    xm = jnp.where(row == 0, prev_row, pltpu.roll(x, 1, 0))
    xp = jnp.where(row == tm - 1, next_row, pltpu.roll(x, tm - 1, 0))
    w = w_ref[...]
    y = b_ref[...] + xm * w[0:1] + x * w[1:2] + xp * w[2:3]
    if qk:
        y = y * jax.nn.sigmoid(y) * jnp.where(j >= 2, ML_DH ** -0.5, 1.0)
    o_ref[...] = y.astype(o_ref.dtype)


def _short_conv(proj, conv_w, conv_b, nb, seq, qk):
    mtot = proj.shape[0]
    tm = _tile(seq, 2048)
    nblk = seq // tm
    col0, ncol = (3, 4) if qk else (0, 3)
    ppo = 2 if qk else 1
    halo = 2 * SUBLANE
    last_halo = mtot // halo - 1
    tmh = tm // halo

    def x_map(b, i, j):
        return (b * nblk + i, col0 + j)

    def p_map(b, i, j):
        return (jnp.maximum((b * nblk + i) * tmh - 1, 0), col0 + j)

    def n_map(b, i, j):
        return (jnp.minimum((b * nblk + i + 1) * tmh, last_halo), col0 + j)

    return pl.pallas_call(
        functools.partial(_conv_body, nblk=nblk, qk=qk),
        grid=(nb, nblk, ncol),
        in_specs=[pl.BlockSpec((tm, D_HY), x_map),
                  pl.BlockSpec((halo, D_HY), p_map),
                  pl.BlockSpec((halo, D_HY), n_map),
                  pl.BlockSpec((3, D_HY), lambda b, i, j: (0, col0 + j)),
                  pl.BlockSpec((1, D_HY), lambda b, i, j: (0, col0 + j))],
        out_specs=pl.BlockSpec((None, tm, D_HY), lambda b, i, j: (j // ppo, b * nblk + i, j % ppo)),
        out_shape=jax.ShapeDtypeStruct((ncol // ppo, nb * seq, ppo * D_HY), BF16),
        compiler_params=_cp("arbitrary", "arbitrary", "arbitrary"),
        name="short_conv_qk" if qk else "short_conv_hy",
    )(proj, proj, proj, conv_w, conv_b)


def _dft_tables(seq):
    n = 2 * seq
    n1 = n // DFT_N2
    n1h = n1 // 2
    n1k = n1h + 2 * SUBLANE
    k1 = jnp.arange(n1k, dtype=jnp.int32)
    t1 = jnp.arange(n1h, dtype=jnp.int32)
    kept = (k1 <= n1h).astype(F32)[:, None]
    weight = kept * jnp.where((k1 == 0) | (k1 == n1h), 1.0, 2.0)[:, None]
    ang = ((k1[:, None] * t1[None, :]) % n1).astype(F32) * (2.0 * math.pi / n1)
    lead_f = jnp.concatenate([kept * jnp.cos(ang), -kept * jnp.sin(ang)], axis=0).astype(BF16)
    lead_ic = (weight * jnp.cos(ang)).T.astype(BF16)
    lead_is = (-weight * jnp.sin(ang)).T.astype(BF16)
    t1f = jnp.arange(n1, dtype=jnp.int32)
    angf = ((k1[:, None] * t1f[None, :]) % n1).astype(F32) * (2.0 * math.pi / n1)
    lead_ff = jnp.concatenate([kept * jnp.cos(angf), -kept * jnp.sin(angf)], axis=0).astype(BF16)
    t2 = jnp.arange(DFT_N2, dtype=jnp.int32)
    k2 = jnp.arange(DFT_N2, dtype=jnp.int32)
    idx = (t2[None, None, :] * (k2[None, :, None] * n1 + k1[:, None, None])) % n
    ang2 = idx.astype(F32) * (2.0 * math.pi / n)
    f_re = jnp.cos(ang2)
    f_im = -jnp.sin(ang2)
    mid_f = jnp.concatenate([jnp.concatenate([f_re, -f_im], axis=2),
                             jnp.concatenate([f_im, f_re], axis=2)], axis=1).astype(BF16)
    g_re = jnp.swapaxes(f_re, 1, 2)
    g_im = jnp.swapaxes(f_im, 1, 2)
    mid_g = jnp.concatenate([jnp.concatenate([g_re, g_im], axis=2),
                             jnp.concatenate([-g_im, g_re], axis=2)], axis=1).astype(BF16)
    return dict(n=n, n1=n1k, n1h=n1h, lead_f=lead_f, lead_ff=lead_ff, lead_ic=lead_ic, lead_is=lead_is,
                mid_f=mid_f, mid_g=mid_g)


def _hy_features(seq):
    tiles = 2 * seq // (SUBLANE * DFT_N2)
    half_rows = SUBLANE * DFT_N2 // 2
    tile = jnp.arange(tiles, dtype=jnp.int32)[:, None, None]
    row = jnp.arange(half_rows, dtype=jnp.int32)[None, :, None]
    side = jnp.arange(2, dtype=jnp.int32)[None, None, :]
    t2 = side * (DFT_N2 // 2) + row // SUBLANE
    t1 = tile * SUBLANE + row % SUBLANE
    slot = t1 * DFT_N2 + t2
    pos = jnp.where(slot < seq, slot, (2 * seq - slot) % seq).astype(F32)[..., None]
    t = pos * (1.0 / (seq - 1))
    w = 2.0 * math.pi * pos / seq
    f = jnp.linspace(1e-4, HY_BANDS - 1, HY_BANDS, dtype=F32)
    z = jnp.concatenate([t, jnp.cos(f * w), -jnp.sin(f * w)], axis=-1)
    z = jnp.pad(z, ((0, 0), (0, 0), (0, 0), (0, LANE - HY_EMB)))
    return z.reshape(tiles * half_rows, 2 * LANE)


def _hy_deltas():
    d = jnp.abs(jnp.linspace(math.log(HY_TARGET) / HY_SLOW_DECAY,
                             math.log(HY_TARGET) / HY_FAST_DECAY, D_HY, dtype=F32))
    return jnp.tile(d, 2)[None, :]


def _hy_filter_body(z_ref, w1_ref, b1_ref, fr_ref, w2_ref, b2_ref, w3_ref, dl_ref, k_ref, ss_ref, *, half):
    i = pl.program_id(0)
    hi = lax.Precision.HIGHEST
    z = z_ref[...]
    fr = fr_ref[...]
    h = jnp.sin(fr * (jnp.dot(z, w1_ref[...], precision=hi, preferred_element_type=F32) + b1_ref[...]))
    h = jnp.sin(fr * (jnp.dot(h, w2_ref[...], precision=hi, preferred_element_type=F32) + b2_ref[...]))
    hb = h.astype(BF16)
    dl = dl_ref[...]
    ha = jnp.dot(hb, w3_ref[0], preferred_element_type=F32) * jnp.exp(-z[:, 0:1] * dl)
    hb = jnp.dot(hb, w3_ref[1], preferred_element_type=F32) * jnp.exp(-z[:, LANE:LANE + 1] * dl)
    row = lax.broadcasted_iota(jnp.int32, ha.shape, 0)
    ha = jnp.where(row + (i - half) * ha.shape[0] == 0, 0.0, ha)
    nc = ha.shape[1]
    h2 = DFT_N2 // 2
    for t2 in range(h2):
        k_ref[:, t2 * nc:(t2 + 1) * nc] = ha[t2 * SUBLANE:(t2 + 1) * SUBLANE, :]
        k_ref[:, (h2 + t2) * nc:(h2 + t2 + 1) * nc] = hb[t2 * SUBLANE:(t2 + 1) * SUBLANE, :]
    part = jnp.broadcast_to(jnp.sum(ha * ha, axis=0, keepdims=True)
                            + jnp.sum(hb * hb, axis=0, keepdims=True), ss_ref.shape)

    @pl.when(i == 0)
    def _():
        ss_ref[...] = part

    @pl.when(i > 0)
    def _():
        ss_ref[...] += part


def _hy_filter(seq, z, w1, b1, fr, w2, b2, w3, deltas):
    nc = 2 * D_HY
    tm = SUBLANE * DFT_N2
    steps = 2 * seq // tm
    half = steps // 2
    full = lambda shape: pl.BlockSpec(shape, lambda i: (0, 0))
    return pl.pallas_call(
        functools.partial(_hy_filter_body, half=half),
        grid=(steps,),
        in_specs=[pl.BlockSpec((tm // 2, 2 * LANE), lambda i: (i, 0)),
                  full((2 * LANE, LANE)), full((1, LANE)), full((1, LANE)),
                  full((LANE, LANE)), full((1, LANE)),
                  pl.BlockSpec((None, 2, LANE, nc), lambda i: (i // half, 0, 0, 0)), full((1, nc))],
        out_specs=[pl.BlockSpec((SUBLANE, DFT_N2 * nc), lambda i: (i, 0)), full((SUBLANE, nc))],
        out_shape=[jax.ShapeDtypeStruct((2 * seq // DFT_N2, DFT_N2 * nc), F32),
                   jax.ShapeDtypeStruct((SUBLANE, nc), F32)],
        compiler_params=_cp("arbitrary"),
        name="hyena_filter",
    )(z, w1, b1, fr, w2, b2, w3, deltas)


def _lead_fwd_body(x_ref, f_ref, re_ref, im_ref):
    n1 = re_ref.shape[0]
    r = jnp.dot(f_ref[...], x_ref[...].astype(BF16), preferred_element_type=F32)
    re_ref[...] = r[:n1].astype(BF16)
    im_ref[...] = r[n1:].astype(BF16)


def _lead_fwd(x, table, plane, nb):
    n1, rows = table.shape[0] // 2, table.shape[1]
    cols = x.shape[-1]
    tn = _tile(cols, 4096)
    return pl.pallas_call(
        _lead_fwd_body,
        grid=(nb, cols // tn),
        in_specs=[pl.BlockSpec((None, rows, tn), lambda b, j: (plane, b, j)),
                  pl.BlockSpec((2 * n1, rows), lambda b, j: (0, 0))],
        out_specs=[pl.BlockSpec((None, n1, tn), lambda b, j: (b, 0, j))] * 2,
        out_shape=[jax.ShapeDtypeStruct((nb, n1, cols), BF16)] * 2,
        compiler_params=_cp("arbitrary", "arbitrary"),
        name="hyena_lead_fwd",
    )(x, table)


def _mid_spec_body(are_ref, aim_ref, f_ref, ss_ref, kr_ref, ki_ref, *, bk):
    n2 = DFT_N2
    scale = lax.rsqrt(ss_ref[0:1, :] + EPS)

    def body(i, c):
        a = jnp.concatenate([are_ref[i], aim_ref[i]], axis=0)
        x = jnp.dot(f_ref[i], a, preferred_element_type=F32)
        kr_ref[i] = x[:n2] * scale
        ki_ref[i] = x[n2:] * scale
        return c

    lax.fori_loop(0, bk, body, 0, unroll=True)


def _mid_spec(a_re, a_im, sumsq, tb):
    n1 = tb["n1"]
    bk = _tile(n1, 8)
    a_re = a_re.reshape(n1, DFT_N2, 2 * D_HY)
    a_im = a_im.reshape(n1, DFT_N2, 2 * D_HY)
    a_spec = pl.BlockSpec((bk, DFT_N2, D_HY), lambda o, i: (i, 0, o))
    k_spec = pl.BlockSpec((None, bk, DFT_N2, D_HY), lambda o, i: (o, i, 0, 0))
    return pl.pallas_call(
        functools.partial(_mid_spec_body, bk=bk),
        grid=(2, n1 // bk),
        in_specs=[a_spec, a_spec,
                  pl.BlockSpec((bk, 2 * DFT_N2, 2 * DFT_N2), lambda o, i: (i, 0, 0)),
                  pl.BlockSpec((SUBLANE, D_HY), lambda o, i: (0, o))],
        out_specs=[k_spec, k_spec],
        out_shape=[jax.ShapeDtypeStruct((2, n1, DFT_N2, D_HY), F32)] * 2,
        compiler_params=_cp("arbitrary", "arbitrary"),
        name="hyena_mid_spectrum",
    )(a_re, a_im, tb["mid_f"], sumsq)


def _mid_conv_body(are_ref, aim_ref, f_ref, g_ref, kr_ref, ki_ref, bre_ref, bim_ref, *, bk):
    n2 = DFT_N2

    def body(i, c):
        a = jnp.concatenate([are_ref[i], aim_ref[i]], axis=0)
        x = jnp.dot(f_ref[i], a, preferred_element_type=F32)
        xr = x[:n2]
        xi = x[n2:]
        kr = kr_ref[i]
        ki = ki_ref[i]
        p = jnp.concatenate([xr * kr - xi * ki, xr * ki + xi * kr], axis=0).astype(BF16)
        q = jnp.dot(g_ref[i], p, preferred_element_type=F32)
        bre_ref[i] = q[:n2].astype(BF16)
        bim_ref[i] = q[n2:].astype(BF16)
        return c

    lax.fori_loop(0, bk, body, 0, unroll=True)


def _mid_conv(a_re, a_im, kf_re, kf_im, order, tb):
    n1 = tb["n1"]
    nb = a_re.shape[0]
    bk = _tile(n1, 8)
    a_re = a_re.reshape(nb, n1, DFT_N2, D_HY)
    a_im = a_im.reshape(nb, n1, DFT_N2, D_HY)
    a_spec = pl.BlockSpec((None, bk, DFT_N2, D_HY), lambda b, i: (b, i, 0, 0))
    t_spec = pl.BlockSpec((bk, 2 * DFT_N2, 2 * DFT_N2), lambda b, i: (i, 0, 0))
    k_spec = pl.BlockSpec((None, bk, DFT_N2, D_HY), lambda b, i: (order, i, 0, 0))
    b_re, b_im = pl.pallas_call(
        functools.partial(_mid_conv_body, bk=bk),
        grid=(nb, n1 // bk),
        in_specs=[a_spec, a_spec, t_spec, t_spec, k_spec, k_spec],
        out_specs=[a_spec, a_spec],
        out_shape=[jax.ShapeDtypeStruct((nb, n1, DFT_N2, D_HY), BF16)] * 2,
        compiler_params=_cp("arbitrary", "arbitrary"),
        name="hyena_mid_conv",
    )(a_re, a_im, tb["mid_f"], tb["mid_g"], kf_re, kf_im)
    return b_re.reshape(nb, n1, DFT_N2 * D_HY), b_im.reshape(nb, n1, DFT_N2 * D_HY)


def _lead_inv_body(bre_ref, bim_ref, c_ref, s_ref, z_ref, gate_ref, skip_ref, gn_ref, o_ref, *, inv_n, last):
    y = (jnp.dot(c_ref[...], bre_ref[...], preferred_element_type=F32)
         + jnp.dot(s_ref[...], bim_ref[...], preferred_element_type=F32)) * inv_n
    out = gate_ref[...].astype(F32) * (y + skip_ref[...] * z_ref[...].astype(F32))
    if not last:
        o_ref[...] = out
    else:
        gn = gn_ref[...]
        for c in range(out.shape[1] // D_HY):
            blk = out[:, c * D_HY:(c + 1) * D_HY]
            ms = jnp.mean(blk * blk, axis=-1, keepdims=True)
            o_ref[:, c * D_HY:(c + 1) * D_HY] = (blk * lax.rsqrt(ms + EPS) * gn).astype(o_ref.dtype)


def _lead_inv(b_re, b_im, z, z_plane, gates, gate_plane, skip, gn, tb, last):
    n1, n1h = tb["n1"], tb["n1h"]
    nb, _, cols = b_re.shape
    tn = _tile(cols, 4096)
    skip_t = jnp.tile(skip[None, :], (1, tn // D_HY))
    b_spec = pl.BlockSpec((None, n1, tn), lambda b, j: (b, 0, j))
    t_spec = pl.BlockSpec((n1h, n1), lambda b, j: (0, 0))
    return pl.pallas_call(
        functools.partial(_lead_inv_body, inv_n=1.0 / tb["n"], last=last),
        grid=(nb, cols // tn),
        in_specs=[b_spec, b_spec, t_spec, t_spec,
                  pl.BlockSpec((None, n1h, tn), lambda b, j: (z_plane, b, j)),
                  pl.BlockSpec((None, n1h, tn), lambda b, j: (gate_plane, b, j)),
                  pl.BlockSpec((1, tn), lambda b, j: (0, 0)),
                  pl.BlockSpec((1, D_HY), lambda b, j: (0, 0))],
        out_specs=pl.BlockSpec((None, n1h, tn), lambda b, j: (0, b, j)),
        out_shape=jax.ShapeDtypeStruct((1, nb * n1h, cols), BF16 if last else F32),
        compiler_params=_cp("arbitrary", "arbitrary"),
        name="hyena_lead_inv",
    )(b_re, b_im, tb["lead_ic"], tb["lead_is"], z, gates, skip_t, gn)


def _hyena(conv_out, nb, seq, tb, kf_re, kf_im, skip, gn):
    n1h = tb["n1h"]
    planes = conv_out.reshape(conv_out.shape[0], nb * n1h, DFT_N2 * D_HY)
    z, z_plane = planes, 0
    for order in range(2):
        a_re, a_im = _lead_fwd(z, tb["lead_f"], z_plane, nb)
        b_re, b_im = _mid_conv(a_re, a_im, kf_re, kf_im, order, tb)
        z = _lead_inv(b_re, b_im, z, z_plane, planes, 1 + order, skip[order], gn, tb, order == 1)
        z_plane = 0
    return z.reshape(nb * seq, D_HY)


def _hyena_spectra(seq, tb, w1, b1, fr, w2, b2, w3):
    w1 = jnp.pad(w1, ((0, LANE - HY_EMB), (0, 0)))
    zero1 = jnp.zeros_like(w1)
    w1 = jnp.concatenate([jnp.concatenate([w1, zero1], axis=1),
                          jnp.concatenate([zero1, w1], axis=1)], axis=0)
    zero2 = jnp.zeros_like(w2)
    w2 = jnp.concatenate([jnp.concatenate([w2, zero2], axis=1),
                          jnp.concatenate([zero2, w2], axis=1)], axis=0)
    pair = lambda v: jnp.concatenate([v, v])[None, :]
    w3 = w3.reshape(HY_FILT, 2, 2, D_HY).transpose(2, 0, 1, 3).reshape(2, HY_FILT, 2 * D_HY)
    zero3 = jnp.zeros_like(w3)
    w3 = jnp.stack([jnp.concatenate([w3, zero3], axis=1),
                    jnp.concatenate([zero3, w3], axis=1)], axis=1).astype(BF16)
    taps, sumsq = _hy_filter(seq, _hy_features(seq), w1, pair(b1), pair(fr), w2, pair(b2), w3,
                             _hy_deltas())
    a_re, a_im = _lead_fwd(taps[None], tb["lead_ff"], 0, 1)
    return _mid_spec(a_re[0], a_im[0], sumsq, tb)


def _s5_scan_body(u_ref, pm_ref, bd_ref, a_ref, cd_ref, x0_ref, o_ref, bu_ref, st_ref, y_ref,
                  *, tstep, reverse, final_state):
    tb = pl.program_id(0)
    ns = S5_STREAMS
    half = S5_STATE
    cw = S5_BLOCK_GROUPS * S5_P

    @pl.when(tb == 0)
    def _():
        st_ref[...] = x0_ref[...]

    u = u_ref[...].reshape(ns * tstep, D_S5).astype(BF16)
    u_tm = jnp.dot(pm_ref[...], u, preferred_element_type=F32).astype(BF16)

    def project_in(blk):
        bu_ref[:, 2 * blk * cw:2 * (blk + 1) * cw] = jnp.dot(
            u_tm[:, blk * LANE:(blk + 1) * LANE], bd_ref[blk], preferred_element_type=F32)

    def scan(blk):
        bre = slice(2 * blk * cw, (2 * blk + 1) * cw)
        bim = slice((2 * blk + 1) * cw, (2 * blk + 2) * cw)
        sre = slice(blk * cw, (blk + 1) * cw)
        sim = slice(half + blk * cw, half + (blk + 1) * cw)
        ar = a_ref[:, sre]
        ai = a_ref[:, sim]
        xr = st_ref[:, sre]
        xi = st_ref[:, sim]
        for i in range(tstep):
            t = tstep - 1 - i if reverse else i
            rows = slice(t * ns, (t + 1) * ns)
            xr, xi = (ar * xr - ai * xi + bu_ref[rows, bre],
                      ar * xi + ai * xr + bu_ref[rows, bim])
            if not final_state:
                bu_ref[rows, bre] = xr
                bu_ref[rows, bim] = xi
        st_ref[:, sre] = xr
        st_ref[:, sim] = xi

    def project_out(blk):
        y_ref[blk] = jnp.dot(bu_ref[:, 2 * blk * cw:2 * (blk + 1) * cw].astype(BF16), cd_ref[blk],
                             preferred_element_type=F32)

    project_in(0)
    for blk in range(S5_NBLOCKS):
        if blk + 1 < S5_NBLOCKS:
            project_in(blk + 1)
        scan(blk)
        if not final_state:
            project_out(blk)

    if final_state:
        @pl.when(tb == pl.num_programs(0) - 1)
        def _():
            o_ref[...] = st_ref[...]
    else:
        for s in range(ns):
            for blk in range(S5_NBLOCKS):
                o_ref[s, :, blk * LANE:(blk + 1) * LANE] = y_ref[blk, pl.ds(s, tstep, stride=ns), :]


def _s5_scan(proj, pmat, bd, a_b, cd, x0, tlen, reverse, final_state):
    ns = S5_STREAMS
    d = 1 if reverse else 0
    tstep = pmat.shape[0] // ns
    nt = tlen // tstep
    proj3 = proj.reshape(ns, tlen, proj.shape[1])
    blk_in = S5_BLOCK_GROUPS * S5_GROUP
    blk_state = 2 * S5_BLOCK_GROUPS * S5_P

    def window(t):
        return nt - 1 - t if reverse else t

    if final_state:
        out_spec = pl.BlockSpec((ns, 2 * S5_STATE), lambda t: (0, 0))
        out_shape = jax.ShapeDtypeStruct((ns, 2 * S5_STATE), F32)
    else:
        out_spec = pl.BlockSpec((ns, tstep, D_S5), lambda t: (0, window(t), 0))
        out_shape = jax.ShapeDtypeStruct((ns, tlen, D_S5), F32)
    out = pl.pallas_call(
        functools.partial(_s5_scan_body, tstep=tstep, reverse=reverse, final_state=final_state),
        grid=(nt,),
        in_specs=[pl.BlockSpec((ns, tstep, D_S5), lambda t: (0, window(t), COL_U)),
                  pl.BlockSpec((ns * tstep, ns * tstep), lambda t: (0, 0)),
                  pl.BlockSpec((None, S5_NBLOCKS, blk_in, blk_state), lambda t: (d, 0, 0, 0)),
                  pl.BlockSpec((None, ns, 2 * S5_STATE), lambda t: (d, 0, 0)),
                  pl.BlockSpec((None, S5_NBLOCKS, blk_state, blk_in), lambda t: (d, 0, 0, 0)),
                  pl.BlockSpec((None, ns, 2 * S5_STATE), lambda t: (d, 0, 0))],
        out_specs=out_spec,
        out_shape=out_shape,
        scratch_shapes=[pltpu.VMEM((tstep * ns, 2 * S5_STATE), F32),
                        pltpu.VMEM((ns, 2 * S5_STATE), F32),
                        pltpu.VMEM((D_S5 // LANE, tstep * ns, LANE), F32)],
        compiler_params=_cp("arbitrary"),
        name="s5_end_state" if final_state else "s5_scan",
    )(proj3, pmat, bd, a_b, cd, x0)
    return out if final_state else out.reshape(ns * tlen, D_S5)


def _s5_params(lam_re, lam_im, log_dt, b_re, b_im, c_re, c_im):
    lr = jnp.minimum(lam_re, -1e-4)
    li = lam_im
    dt = jnp.exp(log_dt)[..., None]
    er = jnp.exp(lr * dt)
    a_re = er * jnp.cos(li * dt)
    a_im = er * jnp.sin(li * dt)
    den = lr * lr + li * li
    q_re = ((a_re - 1.0) * lr + a_im * li) / den
    q_im = (a_im * lr - (a_re - 1.0) * li) / den
    bb_re = q_re[..., None] * b_re - q_im[..., None] * b_im
    bb_im = q_re[..., None] * b_im + q_im[..., None] * b_re
    nbk, gb = S5_NBLOCKS, S5_BLOCK_GROUPS
    eye = jnp.eye(gb, dtype=F32)

    def in_mat(x):
        x = x.reshape(2, nbk, gb, S5_P, S5_GROUP)
        return jnp.einsum("dbgph,gk->dbghkp", x, eye).reshape(2, nbk, gb * S5_GROUP, gb * S5_P)

    def out_mat(x):
        x = x.reshape(2, nbk, gb, S5_GROUP, S5_P)
        return jnp.einsum("dbghp,gk->dbgpkh", x, eye).reshape(2, nbk, gb * S5_P, gb * S5_GROUP)

    bd = jnp.concatenate([in_mat(bb_re), in_mat(bb_im)], axis=3).astype(BF16)
    cd = jnp.concatenate([out_mat(c_re), out_mat(-c_im)], axis=2).astype(BF16)
    a_bar = jnp.concatenate([a_re.reshape(2, S5_STATE), a_im.reshape(2, S5_STATE)], axis=1)
    return (lr * dt).reshape(2, S5_STATE), (li * dt).reshape(2, S5_STATE), a_bar, bd, cd


def _s5_row_perm(tstep):
    ns = S5_STREAMS
    r = jnp.arange(ns * tstep)
    src = (r % ns) * tstep + r // ns
    return (src[:, None] == jnp.arange(ns * tstep)[None, :]).astype(BF16)


def _s5_mixer(proj, nb, seq, s5p):
    la_re, la_im, a_bar, bd, cd = s5p
    ns = S5_STREAMS
    ncs = ns // nb
    tlen = seq // ncs
    pmat = _s5_row_perm(_tile(tlen, S5_TSTEP))
    a_b = jnp.broadcast_to(a_bar[:, None, :], (2, ns, 2 * S5_STATE))
    zeros = jnp.zeros((2, ns, 2 * S5_STATE), F32)
    mag = jnp.exp(la_re * tlen)
    at_re = (mag * jnp.cos(la_im * tlen))[:, None, :]
    at_im = (mag * jnp.sin(la_im * tlen))[:, None, :]
    ys = []
    for d in range(2):
        x_end = _s5_scan(proj, pmat, bd, a_b, cd, zeros, tlen, d == 1, True)
        xe_re = x_end[:, :S5_STATE].reshape(nb, ncs, S5_STATE)
        xe_im = x_end[:, S5_STATE:].reshape(nb, ncs, S5_STATE)
        order = range(ncs) if d == 0 else range(ncs - 1, -1, -1)
        cr = jnp.zeros((nb, S5_STATE), F32)
        ci = jnp.zeros((nb, S5_STATE), F32)
        rows = [None] * ncs
        for c in order:
            rows[c] = jnp.concatenate([cr, ci], axis=-1)
            cr, ci = (at_re[d] * cr - at_im[d] * ci + xe_re[:, c],
                      at_re[d] * ci + at_im[d] * cr + xe_im[:, c])
        x0 = jnp.stack(rows, axis=1).reshape(ns, 2 * S5_STATE)
        x0 = jnp.stack([x0, x0])
        ys.append(_s5_scan(proj, pmat, bd, a_b, cd, x0, tlen, d == 1, False))
    return ys


def _s5_post_body(yf_ref, yb_ref, u_ref, d_ref, w_ref, b_ref, gn_ref, o_ref):
    y = yf_ref[...] + yb_ref[...] + d_ref[...] * u_ref[...].astype(F32)
    g = jax.nn.gelu(y)
    r = jnp.dot(g.astype(BF16), w_ref[...], preferred_element_type=F32) + b_ref[...]
    out = r[:, :D_S5] * jax.nn.sigmoid(r[:, D_S5:])
    ms = jnp.mean(out * out, axis=-1, keepdims=True)
    o_ref[...] = (out * lax.rsqrt(ms + EPS) * gn_ref[...]).astype(o_ref.dtype)


def _s5_post(y_f, y_b, proj, d, glu_w, glu_b, gn):
    m = y_f.shape[0]
    tm = _tile(m, 512)
    full = lambda shape: pl.BlockSpec(shape, lambda i: (0, 0))
    return pl.pallas_call(
        _s5_post_body,
        grid=(m // tm,),
        in_specs=[pl.BlockSpec((tm, D_S5), lambda i: (i, 0)),
                  pl.BlockSpec((tm, D_S5), lambda i: (i, 0)),
                  pl.BlockSpec((tm, D_S5), lambda i: (i, COL_U)),
                  full((1, D_S5)), full((D_S5, 2 * D_S5)), full((1, 2 * D_S5)), full((1, D_S5))],
        out_specs=pl.BlockSpec((tm, D_S5), lambda i: (i, 0)),
        out_shape=jax.ShapeDtypeStruct((m, D_S5), BF16),
        compiler_params=_cp("arbitrary"),
        name="s5_glu_norm",
    )(y_f, y_b, proj, d, glu_w, glu_b, gn)


def _split3(x):
    hi = x.astype(BF16)
    rest = x - hi.astype(F32)
    mid = rest.astype(BF16)
    return hi, mid, (rest - mid.astype(F32)).astype(BF16)


def _mlstm_body(*refs, reverse, epilogue):
    if epilogue:
        (q_ref, k_ref, v_ref, g_ref, gt_ref, bias_ref, biast_ref, seen_ref, upto_ref, hf_ref, o_gate_ref,
         gn_ref, out_ref, c_ref, n_ref, m_ref) = refs
    else:
        (q_ref, k_ref, v_ref, g_ref, gt_ref, bias_ref, biast_ref, seen_ref, upto_ref,
         out_ref, c_ref, n_ref, m_ref) = refs
        hf_ref = o_gate_ref = gn_ref = None

    @pl.when(pl.program_id(2) == 0)
    def _():
        c_ref[...] = jnp.zeros_like(c_ref)
        n_ref[...] = jnp.zeros_like(n_ref)
        m_ref[...] = jnp.zeros_like(m_ref)

    tc = ML_CHUNK
    r = lax.broadcasted_iota(jnp.int32, (tc, tc), 0)
    s = lax.broadcasted_iota(jnp.int32, (tc, tc), 1)
    seen = (s >= r) if reverse else (s <= r)
    g = g_ref[...] + bias_ref[...]
    gt = gt_ref[...] + biast_ref[:, 0:1]
    for hh in range(ML_HEADS_PER_STEP):
        _mlstm_head(hh, g, gt, seen, q_ref, k_ref, v_ref, seen_ref, upto_ref, hf_ref, o_gate_ref,
                    gn_ref, out_ref, c_ref, n_ref, m_ref, reverse=reverse)


def _mlstm_head(hh, g, gt, seen, q_ref, k_ref, v_ref, seen_ref, upto_ref, hf_ref, o_gate_ref, gn_ref,
                out_ref, c_ref, n_ref, m_ref, *, reverse):
    head = pl.program_id(1) * ML_HEADS_PER_STEP + hh
    cols = slice(hh * ML_DH, (hh + 1) * ML_DH)
    tc = ML_CHUNK
    gate_i = 2 if reverse else 0
    idx_i = gate_i * ML_HEADS + head
    idx_f = idx_i + ML_HEADS
    tiled = lambda x, n: x if n == LANE else jnp.concatenate([x] * (n // LANE), axis=1)
    wide = lambda x: tiled(x, ML_DH)
    span = lambda x: tiled(x, tc)
    dot = functools.partial(jnp.dot, preferred_element_type=F32)

    lane = lax.broadcasted_iota(jnp.int32, g.shape, 1)
    ig = jnp.broadcast_to(jnp.sum(jnp.where(lane == idx_i, g, 0.0), axis=1, keepdims=True), g.shape)
    fg = jnp.broadcast_to(jnp.sum(jnp.where(lane == idx_f, g, 0.0), axis=1, keepdims=True), g.shape)
    sub = lax.broadcasted_iota(jnp.int32, gt.shape, 0)
    ig_row = jnp.sum(jnp.where(sub == idx_i, gt, 0.0), axis=0, keepdims=True)
    fg_row = jnp.sum(jnp.where(sub == idx_f, gt, 0.0), axis=0, keepdims=True)
    lf = jax.nn.log_sigmoid(fg)
    lf_row = jax.nn.log_sigmoid(fg_row)

    seen_b = seen_ref[...]
    upto_b = upto_ref[...]
    b_col = sum(dot(seen_b, piece) for piece in _split3(lf))
    lf_rows = jnp.broadcast_to(lf_row, (SUBLANE, tc))
    b_row = sum(dot(piece, upto_b) for piece in _split3(lf_rows))[0:1]
    b_last = jnp.sum(lf_row, axis=1, keepdims=True)

    m_prev = m_ref[hh:hh + 1, 0:1]
    src = ig_row - b_row
    dmat = jnp.where(seen, span(b_col) + src, -jnp.inf)
    g_car = b_col + m_prev
    mt = jnp.maximum(g_car, jnp.max(dmat, axis=1, keepdims=True))
    inter = jnp.exp(g_car - mt)
    qb = q_ref[:, cols]
    kb = k_ref[:, cols]
    vb = v_ref[:, cols].astype(BF16)
    ones = jnp.ones((tc, LANE), BF16)
    qk = lax.dot_general(qb, kb, (((1,), (1,)), ((), ())), preferred_element_type=F32)
    sc = (qk * jnp.exp(dmat - span(mt))).astype(BF16)
    c_state = c_ref[hh]
    n_state = n_ref[hh]
    num = wide(inter) * dot(qb, c_state.astype(BF16)) + dot(sc, vb)
    den = inter * dot(qb, n_state.astype(BF16)) + dot(sc, ones)
    h = num * wide(1.0 / jnp.maximum(jnp.abs(den), jnp.exp(-mt)))

    a_row = b_last + src
    a_col = b_last - b_col + ig
    m_new = jnp.maximum(b_last + m_prev, jnp.max(a_row, axis=1, keepdims=True))
    decay = jnp.exp(b_last + m_prev - m_new)
    kw = (kb.astype(F32) * wide(jnp.exp(a_col - m_new))).astype(BF16)
    upd = lax.dot_general(kw, jnp.concatenate([vb, ones], axis=1), (((0,), (0,)), ((), ())),
                          preferred_element_type=F32)
    c_ref[hh] = decay * c_state + upd[:, :ML_DH]
    n_ref[hh] = decay * n_state + upd[:, ML_DH:]
    m_ref[hh:hh + 1, 0:1] = m_new

    if hf_ref is not None:
        h = h + hf_ref[:, cols]
        ms = jnp.mean(h * h, axis=-1, keepdims=True)
        hn = h * lax.rsqrt(ms + EPS) * gn_ref[:, cols]
        out_ref[:, cols] = (hn * jax.nn.sigmoid(o_gate_ref[:, cols].astype(F32))).astype(out_ref.dtype)
    else:
        out_ref[:, cols] = h


def _mlstm_dir(qk_planes, proj, gates, gates_t, bias, bias_t, nb, seq, reverse, h_fwd=None, gn=None):
    tc = ML_CHUNK
    nch = seq // tc
    hps = ML_HEADS_PER_STEP
    width = hps * ML_DH
    col_v = (N_CONV + D_S5) // width
    col_o = (N_CONV + D_S5 + D_ML) // width

    def chunk(b, ci):
        return b * nch + (nch - 1 - ci if reverse else ci)

    in_specs = [
        pl.BlockSpec((None, tc, width), lambda b, h, ci: (0, chunk(b, ci), h)),
        pl.BlockSpec((None, tc, width), lambda b, h, ci: (1, chunk(b, ci), h)),
        pl.BlockSpec((tc, width), lambda b, h, ci: (chunk(b, ci), col_v + h)),
        pl.BlockSpec((tc, LANE), lambda b, h, ci: (chunk(b, ci), 0)),
        pl.BlockSpec((4 * ML_HEADS, tc), lambda b, h, ci: (0, chunk(b, ci))),
        pl.BlockSpec((1, LANE), lambda b, h, ci: (0, 0)),
        pl.BlockSpec((4 * ML_HEADS, LANE), lambda b, h, ci: (0, 0)),
        pl.BlockSpec((tc, tc), lambda b, h, ci: (0, 0)),
        pl.BlockSpec((tc, tc), lambda b, h, ci: (0, 0)),
    ]
    pos = jnp.arange(tc)
    seen = (pos[None, :] >= pos[:, None]) if reverse else (pos[None, :] <= pos[:, None])
    seen = seen.astype(BF16)
    args = [qk_planes, qk_planes, proj, gates, gates_t, bias, bias_t, seen, seen.T]
    epilogue = h_fwd is not None
    if epilogue:
        in_specs += [
            pl.BlockSpec((tc, width), lambda b, h, ci: (chunk(b, ci), h)),
            pl.BlockSpec((tc, width), lambda b, h, ci: (chunk(b, ci), col_o + h)),
            pl.BlockSpec((1, width), lambda b, h, ci: (0, h)),
        ]
        args += [h_fwd, proj, gn]
    return pl.pallas_call(
        functools.partial(_mlstm_body, reverse=reverse, epilogue=epilogue),
        grid=(nb, ML_HEADS // hps, nch),
        in_specs=in_specs,
        out_specs=pl.BlockSpec((tc, width), lambda b, h, ci: (chunk(b, ci), h)),
        out_shape=jax.ShapeDtypeStruct((nb * seq, D_ML), BF16 if epilogue else F32),
        scratch_shapes=[pltpu.VMEM((hps, ML_DH, ML_DH), F32), pltpu.VMEM((hps, ML_DH, LANE), F32),
                        pltpu.VMEM((SUBLANE, LANE), F32)],
        compiler_params=_cp("arbitrary", "arbitrary", "arbitrary"),
        name="mlstm_bwd_norm" if reverse else "mlstm_fwd",
    )(*args)


def _layer(x, nb, seq, tb, p):
    proj, gates = _proj_in(x, p["ln_g"][0:1], p["w_in"], p["w_gate"])
    gates_t = gates[:, :4 * ML_HEADS].T
    gn = p["group_norm"]
    gn_hy = gn[None, :D_HY]
    gn_s5 = gn[None, D_HY:D_HY + D_S5]
    gn_ml = gn[None, D_HY + D_S5:]

    hy_planes = _short_conv(proj, p["conv_w"], p["conv_b"][None, :], nb, seq, False)
    qk_planes = _short_conv(proj, p["conv_w"], p["conv_b"][None, :], nb, seq, True)
    kf_re, kf_im = _hyena_spectra(seq, tb, p["hy_w1"], p["hy_b1"], p["hy_freq"], p["hy_w2"],
                                  p["hy_b2"], p["hy_w3"])
    y_hy = _hyena(hy_planes, nb, seq, tb, kf_re, kf_im, p["hy_skip"], gn_hy)
    y_s5 = _s5_post(*_s5_mixer(proj, nb, seq, p["s5"]), proj, p["s5_d"][None, :],
                    p["s5_glu_w"], p["s5_glu_b"][None, :], gn_s5)
    ml_args = (qk_planes, proj, gates, gates_t, p["ml_bias"], p["ml_bias_t"], nb, seq)
    h_f = _mlstm_dir(*ml_args, False)
    y_ml = _mlstm_dir(*ml_args, True, h_f, gn_ml)

    w_out = p["w_out"]
    x = _proj_out([y_hy, y_s5, y_ml], [w_out[:D_HY], w_out[D_HY:D_HY + D_S5], w_out[D_HY + D_S5:]],
                  x, p["ln_g"][1:2])
    return _mlp(x, p["ln_g"][2:3], p["mlp_w1"], p["mlp_w2"], p["ln_g"][3:4])


_PARAM_NAMES = ("ln_g", "w_in", "conv_w", "conv_b", "hy_w1", "hy_b1", "hy_freq", "hy_w2", "hy_b2",
                "hy_w3", "hy_skip", "s5_lam_re", "s5_lam_im", "s5_log_dt", "s5_b_re", "s5_b_im",
                "s5_c_re", "s5_c_im", "s5_d", "s5_glu_w", "s5_glu_b", "ml_gate_b", "group_norm",
                "w_out", "mlp_w1", "mlp_w2")


def _prepare(p):
    p = dict(p)
    w_in = p["w_in"].astype(BF16)
    p["w_in"] = w_in[:, :P_MIX]
    p["w_gate"] = jnp.pad(w_in[:, P_MIX:], ((0, 0), (0, LANE - 4 * ML_HEADS)))
    p["w_out"] = p["w_out"].astype(BF16)
    p["mlp_w1"] = p["mlp_w1"].astype(BF16)
    p["mlp_w2"] = p["mlp_w2"].astype(BF16)
    p["s5_glu_w"] = p["s5_glu_w"].astype(BF16)
    p["ml_bias"] = jnp.pad(p["ml_gate_b"].reshape(1, -1), ((0, 0), (0, LANE - 4 * ML_HEADS)))
    p["ml_bias_t"] = jnp.broadcast_to(p["ml_gate_b"].reshape(-1, 1), (4 * ML_HEADS, LANE))
    p["s5"] = _s5_params(p["s5_lam_re"], p["s5_lam_im"], p["s5_log_dt"], p["s5_b_re"],
                         p["s5_b_im"], p["s5_c_re"], p["s5_c_im"])
    return p


def kernel(x_prompt, x_sample, ln_g, w_in, conv_w, conv_b, hy_w1, hy_b1, hy_freq, hy_w2, hy_b2, hy_w3,
           hy_skip, s5_lam_re, s5_lam_im, s5_log_dt, s5_b_re, s5_b_im, s5_c_re, s5_c_im, s5_d,
           s5_glu_w, s5_glu_b, ml_gate_b, group_norm, w_out, mlp_w1, mlp_w2):
    params = (ln_g, w_in, conv_w, conv_b, hy_w1, hy_b1, hy_freq, hy_w2, hy_b2, hy_w3, hy_skip,
              s5_lam_re, s5_lam_im, s5_log_dt, s5_b_re, s5_b_im, s5_c_re, s5_c_im, s5_d,
              s5_glu_w, s5_glu_b, ml_gate_b, group_norm, w_out, mlp_w1, mlp_w2)
    streams = []
    for x in (x_prompt, x_sample):
        nb, seq, _ = x.shape
        streams.append([x.reshape(nb * seq, D_MODEL), nb, seq, _dft_tables(seq)])
    for layer in range(DEPTH):
        p = _prepare({name: arr[layer] for name, arr in zip(_PARAM_NAMES, params)})
        for st in streams:
            st[0] = _layer(st[0], st[1], st[2], st[3], p)
    return tuple(st[0].reshape(x.shape) for st, x in zip(streams, (x_prompt, x_sample)))
```

```python
import functools
import math

import jax
import jax.numpy as jnp
from jax import lax
from jax.experimental import pallas as pl
from jax.experimental.pallas import tpu as pltpu

F32 = jnp.float32
BF16 = jnp.bfloat16

D_MODEL = 2048
DEPTH = 4
D_HY = 512
D_S5 = 512
D_ML = 1024
HY_EMB = 33
HY_BANDS = 16
HY_FILT = 64
HY_FAST_DECAY = 0.3
HY_SLOW_DECAY = 1.5
HY_TARGET = 1e-2
S5_GROUP = 16
S5_G = 32
S5_P = 64
S5_STATE = S5_G * S5_P
ML_HEADS = 4
ML_DH = 256
D_FF = 4 * D_MODEL
N_CONV = 3 * D_HY + 2 * D_ML
P_IN = N_CONV + D_S5 + 2 * D_ML + 4 * ML_HEADS
P_MIX = P_IN - 4 * ML_HEADS
EPS = 1e-6

LANE = 128
SUBLANE = 8
VMEM_LIMIT_BYTES = 48 * 1024 * 1024

DFT_N2 = LANE
S5_STREAMS = SUBLANE
S5_TSTEP = 64
S5_BLOCK_GROUPS = LANE // S5_GROUP
S5_NBLOCKS = S5_G // S5_BLOCK_GROUPS
ML_CHUNK = 256
ML_HEADS_PER_STEP = 4
COL_U = N_CONV // D_S5


def _cp(*sem):
    return pltpu.CompilerParams(dimension_semantics=sem, vmem_limit_bytes=VMEM_LIMIT_BYTES)


def _tile(n, pref):
    t = min(n, pref)
    while n % t:
        t //= 2
    return t


def _rms(x, g):
    ms = jnp.mean(x * x, axis=-1, keepdims=True)
    return x * lax.rsqrt(ms + EPS) * g


def _proj_in_body(x_ref, g_ref, w_ref, wg_ref, o_ref, og_ref, xn_ref):
    @pl.when(pl.program_id(1) == 0)
    def _():
        xn = _rms(x_ref[...], g_ref[...]).astype(BF16)
        xn_ref[...] = xn
        og_ref[...] = jnp.dot(xn, wg_ref[...], preferred_element_type=F32)

    o_ref[...] = jnp.dot(xn_ref[...], w_ref[...], preferred_element_type=F32).astype(o_ref.dtype)


def _proj_in(x, g, w, w_gate):
    m, k = x.shape
    n = w.shape[1]
    tm = _tile(m, 1024)
    tn = 1024
    return pl.pallas_call(
        _proj_in_body,
        grid=(m // tm, n // tn),
        in_specs=[pl.BlockSpec((tm, k), lambda i, j: (i, 0)),
                  pl.BlockSpec((1, k), lambda i, j: (0, 0)),
                  pl.BlockSpec((k, tn), lambda i, j: (0, j)),
                  pl.BlockSpec((k, LANE), lambda i, j: (0, 0))],
        out_specs=[pl.BlockSpec((tm, tn), lambda i, j: (i, j)),
                   pl.BlockSpec((tm, LANE), lambda i, j: (i, 0))],
        out_shape=[jax.ShapeDtypeStruct((m, n), BF16), jax.ShapeDtypeStruct((m, LANE), F32)],
        scratch_shapes=[pltpu.VMEM((tm, k), BF16)],
        compiler_params=_cp("arbitrary", "arbitrary"),
        name="norm_proj_in",
    )(x, g, w, w_gate)


def _proj_out_body(a0_ref, a1_ref, a2_ref, w0_ref, w1_ref, w2_ref, r_ref, g_ref, o_ref):
    half = o_ref.shape[0] // 2
    for rows in (slice(0, half), slice(half, 2 * half)):
        f = (jnp.dot(a0_ref[rows, :].astype(BF16), w0_ref[...], preferred_element_type=F32)
             + jnp.dot(a1_ref[rows, :].astype(BF16), w1_ref[...], preferred_element_type=F32)
             + jnp.dot(a2_ref[rows, :].astype(BF16), w2_ref[...], preferred_element_type=F32))
        o_ref[rows, :] = r_ref[rows, :] + _rms(f, g_ref[...])


def _proj_out(parts, weights, resid, g):
    m, n = resid.shape
    tm = _tile(m, 512)
    row = lambda width: pl.BlockSpec((tm, width), lambda i: (i, 0))
    full = lambda arr: pl.BlockSpec(arr.shape, lambda i: (0, 0))
    return pl.pallas_call(
        _proj_out_body,
        grid=(m // tm,),
        in_specs=[row(a.shape[1]) for a in parts] + [full(w) for w in weights] + [row(n), full(g)],
        out_specs=row(n),
        out_shape=jax.ShapeDtypeStruct((m, n), F32),
        compiler_params=_cp("arbitrary"),
        name="proj_out_resnorm",
    )(*parts, *weights, resid, g)


def _mlp_body(x_ref, g_in_ref, w1_ref, w2_ref, g_out_ref, o_ref, xn_ref, acc_ref):
    j = pl.program_id(1)

    @pl.when(j == 0)
    def _():
        xn_ref[...] = _rms(x_ref[...], g_in_ref[...]).astype(BF16)
        acc_ref[...] = jnp.zeros_like(acc_ref)

    h = jnp.dot(xn_ref[...], w1_ref[...], preferred_element_type=F32)
    h = jnp.square(jnp.maximum(h, 0.0)).astype(BF16)
    acc_ref[...] += jnp.dot(h, w2_ref[...], preferred_element_type=F32)

    @pl.when(j == pl.num_programs(1) - 1)
    def _():
        o_ref[...] = x_ref[...] + _rms(acc_ref[...], g_out_ref[...])


def _mlp(x, g_in, w1, w2, g_out):
    m, k = x.shape
    ff = w1.shape[1]
    tm = _tile(m, 512)
    tf = 1024
    return pl.pallas_call(
        _mlp_body,
        grid=(m // tm, ff // tf),
        in_specs=[pl.BlockSpec((tm, k), lambda i, j: (i, 0)),
                  pl.BlockSpec((1, k), lambda i, j: (0, 0)),
                  pl.BlockSpec((k, tf), lambda i, j: (0, j)),
                  pl.BlockSpec((tf, k), lambda i, j: (j, 0)),
                  pl.BlockSpec((1, k), lambda i, j: (0, 0))],
        out_specs=pl.BlockSpec((tm, k), lambda i, j: (i, 0)),
        out_shape=jax.ShapeDtypeStruct((m, k), F32),
        scratch_shapes=[pltpu.VMEM((tm, k), BF16), pltpu.VMEM((tm, k), F32)],
        compiler_params=_cp("arbitrary", "arbitrary"),
        name="mlp_resnorm",
    )(x, g_in, w1, w2, g_out)


def _conv_body(x_ref, p_ref, n_ref, w_ref, b_ref, o_ref, *, nblk, qk):
    i = pl.program_id(1)
    j = pl.program_id(2)
    x = x_ref[...].astype(F32)
    tm = x.shape[0]
    row = lax.broadcasted_iota(jnp.int32, x.shape, 0)
    halo = p_ref.shape[0]
    prev_row = jnp.where(i == 0, 0.0, p_ref[halo - 1:halo, :].astype(F32))
    next_row = jnp.where(i == nblk - 1, 0.0, n_ref[0:1, :].astype(F32))
    xm = jnp.where(row == 0, prev_row, pltpu.roll(x, 1, 0))
    xp = jnp.where(row == tm - 1, next_row, pltpu.roll(x, tm - 1, 0))
    w = w_ref[...]
    y = b_ref[...] + xm * w[0:1] + x * w[1:2] + xp * w[2:3]
    if qk:
        y = y * jax.nn.sigmoid(y) * jnp.where(j >= 2, ML_DH ** -0.5, 1.0)
    o_ref[...] = y.astype(o_ref.dtype)


def _short_conv(proj, conv_w, conv_b, nb, seq, qk):
    mtot = proj.shape[0]
    tm = _tile(seq, 2048)
    nblk = seq // tm
    col0, ncol = (3, 4) if qk else (0, 3)
    ppo = 2 if qk else 1
    halo = 2 * SUBLANE
    last_halo = mtot // halo - 1
    tmh = tm // halo

    def x_map(b, i, j):
        return (b * nblk + i, col0 + j)

    def p_map(b, i, j):
        return (jnp.maximum((b * nblk + i) * tmh - 1, 0), col0 + j)

    def n_map(b, i, j):
        return (jnp.minimum((b * nblk + i + 1) * tmh, last_halo), col0 + j)

    return pl.pallas_call(
        functools.partial(_conv_body, nblk=nblk, qk=qk),
        grid=(nb, nblk, ncol),
        in_specs=[pl.BlockSpec((tm, D_HY), x_map),
                  pl.BlockSpec((halo, D_HY), p_map),
                  pl.BlockSpec((halo, D_HY), n_map),
                  pl.BlockSpec((3, D_HY), lambda b, i, j: (0, col0 + j)),
                  pl.BlockSpec((1, D_HY), lambda b, i, j: (0, col0 + j))],
        out_specs=pl.BlockSpec((None, tm, D_HY), lambda b, i, j: (j // ppo, b * nblk + i, j % ppo)),
        out_shape=jax.ShapeDtypeStruct((ncol // ppo, nb * seq, ppo * D_HY), BF16),
        compiler_params=_cp("arbitrary", "arbitrary", "arbitrary"),
        name="short_conv_qk" if qk else "short_conv_hy",
    )(proj, proj, proj, conv_w, conv_b)


def _dft_tables(seq):
    n = 2 * seq
    n1 = n // DFT_N2
    n1h = n1 // 2
    n1k = n1h + 2 * SUBLANE
    k1 = jnp.arange(n1k, dtype=jnp.int32)
    t1 = jnp.arange(n1h, dtype=jnp.int32)
    kept = (k1 <= n1h).astype(F32)[:, None]
    weight = kept * jnp.where((k1 == 0) | (k1 == n1h), 1.0, 2.0)[:, None]
    ang = ((k1[:, None] * t1[None, :]) % n1).astype(F32) * (2.0 * math.pi / n1)
    lead_f = jnp.concatenate([kept * jnp.cos(ang), -kept * jnp.sin(ang)], axis=0).astype(BF16)
    lead_ic = (weight * jnp.cos(ang)).T.astype(BF16)
    lead_is = (-weight * jnp.sin(ang)).T.astype(BF16)
    t1f = jnp.arange(n1, dtype=jnp.int32)
    angf = ((k1[:, None] * t1f[None, :]) % n1).astype(F32) * (2.0 * math.pi / n1)
    lead_ff = jnp.concatenate([kept * jnp.cos(angf), -kept * jnp.sin(angf)], axis=0).astype(BF16)
    row = jnp.arange(2 * DFT_N2, dtype=jnp.int32)[None, :, None]
    col = jnp.arange(2 * DFT_N2, dtype=jnp.int32)[None, None, :]
    idx = ((col % DFT_N2) * ((row % DFT_N2) * n1 + k1[:, None, None])) % n
    phase = jnp.where(row // DFT_N2 == col // DFT_N2, 0.5 * math.pi,
                      jnp.where(row < col, 0.0, math.pi))
    mid_f = jnp.sin(idx.astype(F32) * (2.0 * math.pi / n) + phase).astype(BF16)
    return dict(n=n, n1=n1k, n1h=n1h, lead_f=lead_f, lead_ff=lead_ff, lead_ic=lead_ic, lead_is=lead_is,
                mid_f=mid_f)


def _hy_features(seq):
    tiles = 2 * seq // (SUBLANE * DFT_N2)
    half_rows = SUBLANE * DFT_N2 // 2
    tile = jnp.arange(tiles, dtype=jnp.int32)[:, None, None]
    row = jnp.arange(half_rows, dtype=jnp.int32)[None, :, None]
    side = jnp.arange(2, dtype=jnp.int32)[None, None, :]
    t2 = side * (DFT_N2 // 2) + row // SUBLANE
    t1 = tile * SUBLANE + row % SUBLANE
    slot = t1 * DFT_N2 + t2
    pos = jnp.where(slot < seq, slot, (2 * seq - slot) % seq).astype(F32)[..., None]
    t = pos * (1.0 / (seq - 1))
    w = 2.0 * math.pi * pos / seq
    f = jnp.linspace(1e-4, HY_BANDS - 1, HY_BANDS, dtype=F32)
    z = jnp.concatenate([t, jnp.cos(f * w), -jnp.sin(f * w)], axis=-1)
    z = jnp.pad(z, ((0, 0), (0, 0), (0, 0), (0, LANE - HY_EMB)))
    return z.reshape(tiles * half_rows, 2 * LANE)


def _hy_deltas():
    d = jnp.abs(jnp.linspace(math.log(HY_TARGET) / HY_SLOW_DECAY,
                             math.log(HY_TARGET) / HY_FAST_DECAY, D_HY, dtype=F32))
    return jnp.tile(d, 2)[None, :]


def _hy_filter_body(z_ref, w1_ref, b1_ref, fr_ref, w2_ref, b2_ref, w3_ref, dl_ref, k_ref, ss_ref, *, half):
    i = pl.program_id(0)
    hi = lax.Precision.HIGHEST
    z = z_ref[...]
    fr = fr_ref[...]
    h = jnp.sin(fr * (jnp.dot(z, w1_ref[...], precision=hi, preferred_element_type=F32) + b1_ref[...]))
    h = jnp.sin(fr * (jnp.dot(h, w2_ref[...], precision=hi, preferred_element_type=F32) + b2_ref[...]))
    hb = h.astype(BF16)
    dl = dl_ref[...]
    ha = jnp.dot(hb, w3_ref[0], preferred_element_type=F32) * jnp.exp(-z[:, 0:1] * dl)
    hb = jnp.dot(hb, w3_ref[1], preferred_element_type=F32) * jnp.exp(-z[:, LANE:LANE + 1] * dl)
    row = lax.broadcasted_iota(jnp.int32, ha.shape, 0)
    ha = jnp.where(row + (i - half) * ha.shape[0] == 0, 0.0, ha)
    nc = ha.shape[1]
    h2 = DFT_N2 // 2
    for t2 in range(h2):
        k_ref[:, t2 * nc:(t2 + 1) * nc] = ha[t2 * SUBLANE:(t2 + 1) * SUBLANE, :]
        k_ref[:, (h2 + t2) * nc:(h2 + t2 + 1) * nc] = hb[t2 * SUBLANE:(t2 + 1) * SUBLANE, :]
    part = jnp.broadcast_to(jnp.sum(ha * ha, axis=0, keepdims=True)
                            + jnp.sum(hb * hb, axis=0, keepdims=True), ss_ref.shape)

    @pl.when(i == 0)
    def _():
        ss_ref[...] = part

    @pl.when(i > 0)
    def _():
        ss_ref[...] += part


def _hy_filter(seq, z, w1, b1, fr, w2, b2, w3, deltas):
    nc = 2 * D_HY
    tm = SUBLANE * DFT_N2
    steps = 2 * seq // tm
    half = steps // 2
    full = lambda shape: pl.BlockSpec(shape, lambda i: (0, 0))
    return pl.pallas_call(
        functools.partial(_hy_filter_body, half=half),
        grid=(steps,),
        in_specs=[pl.BlockSpec((tm // 2, 2 * LANE), lambda i: (i, 0)),
                  full((2 * LANE, LANE)), full((1, LANE)), full((1, LANE)),
                  full((LANE, LANE)), full((1, LANE)),
                  pl.BlockSpec((None, 2, LANE, nc), lambda i: (i // half, 0, 0, 0)), full((1, nc))],
        out_specs=[pl.BlockSpec((SUBLANE, DFT_N2 * nc), lambda i: (i, 0)), full((SUBLANE, nc))],
        out_shape=[jax.ShapeDtypeStruct((2 * seq // DFT_N2, DFT_N2 * nc), F32),
                   jax.ShapeDtypeStruct((SUBLANE, nc), F32)],
        compiler_params=_cp("arbitrary"),
        name="hyena_filter",
    )(z, w1, b1, fr, w2, b2, w3, deltas)


def _lead_fwd_body(x_ref, f_ref, re_ref, im_ref):
    n1 = re_ref.shape[0]
    r = jnp.dot(f_ref[...], x_ref[...].astype(BF16), preferred_element_type=F32)
    re_ref[...] = r[:n1].astype(BF16)
    im_ref[...] = r[n1:].astype(BF16)


def _lead_fwd(x, table, plane, nb):
    n1, rows = table.shape[0] // 2, table.shape[1]
    cols = x.shape[-1]
    tn = _tile(cols, 4096)
    return pl.pallas_call(
        _lead_fwd_body,
        grid=(nb, cols // tn),
        in_specs=[pl.BlockSpec((None, rows, tn), lambda b, j: (plane, b, j)),
                  pl.BlockSpec((2 * n1, rows), lambda b, j: (0, 0))],
        out_specs=[pl.BlockSpec((None, n1, tn), lambda b, j: (b, 0, j))] * 2,
        out_shape=[jax.ShapeDtypeStruct((nb, n1, cols), BF16)] * 2,
        compiler_params=_cp("arbitrary", "arbitrary"),
        name="hyena_lead_fwd",
    )(x, table)


def _mid_spec_body(are_ref, aim_ref, f_ref, ss_ref, kr_ref, ki_ref, *, bk):
    n2 = DFT_N2
    scale = lax.rsqrt(ss_ref[0:1, :] + EPS)

    def body(i, c):
        a = jnp.concatenate([are_ref[i], aim_ref[i]], axis=0)
        x = jnp.dot(f_ref[i], a, preferred_element_type=F32)
        kr_ref[i] = x[:n2] * scale
        ki_ref[i] = x[n2:] * scale
        return c

    lax.fori_loop(0, bk, body, 0, unroll=True)


def _mid_spec(a_re, a_im, sumsq, tb):
    n1 = tb["n1"]
    bk = _tile(n1, 8)
    a_re = a_re.reshape(n1, DFT_N2, 2 * D_HY)
    a_im = a_im.reshape(n1, DFT_N2, 2 * D_HY)
    a_spec = pl.BlockSpec((bk, DFT_N2, D_HY), lambda o, i: (i, 0, o))
    k_spec = pl.BlockSpec((None, bk, DFT_N2, D_HY), lambda o, i: (o, i, 0, 0))
    return pl.pallas_call(
        functools.partial(_mid_spec_body, bk=bk),
        grid=(2, n1 // bk),
        in_specs=[a_spec, a_spec,
                  pl.BlockSpec((bk, 2 * DFT_N2, 2 * DFT_N2), lambda o, i: (i, 0, 0)),
                  pl.BlockSpec((SUBLANE, D_HY), lambda o, i: (0, o))],
        out_specs=[k_spec, k_spec],
        out_shape=[jax.ShapeDtypeStruct((2, n1, DFT_N2, D_HY), F32)] * 2,
        compiler_params=_cp("arbitrary", "arbitrary"),
        name="hyena_mid_spectrum",
    )(a_re, a_im, tb["mid_f"], sumsq)


def _mid_conv_body(are_ref, aim_ref, f_ref, kr_ref, ki_ref, bre_ref, bim_ref, *, bk):
    n2 = DFT_N2

    def body(i, c):
        a = jnp.concatenate([are_ref[i], aim_ref[i]], axis=0)
        x = jnp.dot(f_ref[i], a, preferred_element_type=F32)
        xr = x[:n2]
        xi = x[n2:]
        kr = kr_ref[i]
        ki = ki_ref[i]
        p = jnp.concatenate([xr * kr - xi * ki, xr * ki + xi * kr], axis=0).astype(BF16)
        q = lax.dot_general(f_ref[i], p, (((0,), (0,)), ((), ())), preferred_element_type=F32)
        bre_ref[i] = q[:n2].astype(BF16)
        bim_ref[i] = q[n2:].astype(BF16)
        return c

    lax.fori_loop(0, bk, body, 0, unroll=True)


def _mid_conv(a_re, a_im, kf_re, kf_im, order, tb):
    n1 = tb["n1"]
    nb = a_re.shape[0]
    bk = _tile(n1, 8)
    a_re = a_re.reshape(nb, n1, DFT_N2, D_HY)
    a_im = a_im.reshape(nb, n1, DFT_N2, D_HY)
    a_spec = pl.BlockSpec((None, bk, DFT_N2, D_HY), lambda b, i: (b, i, 0, 0))
    t_spec = pl.BlockSpec((bk, 2 * DFT_N2, 2 * DFT_N2), lambda b, i: (i, 0, 0))
    k_spec = pl.BlockSpec((None, bk, DFT_N2, D_HY), lambda b, i: (order, i, 0, 0))
    b_re, b_im = pl.pallas_call(
        functools.partial(_mid_conv_body, bk=bk),
        grid=(nb, n1 // bk),
        in_specs=[a_spec, a_spec, t_spec, k_spec, k_spec],
        out_specs=[a_spec, a_spec],
        out_shape=[jax.ShapeDtypeStruct((nb, n1, DFT_N2, D_HY), BF16)] * 2,
        compiler_params=_cp("arbitrary", "arbitrary"),
        name="hyena_mid_conv",
    )(a_re, a_im, tb["mid_f"], kf_re, kf_im)
    return b_re.reshape(nb, n1, DFT_N2 * D_HY), b_im.reshape(nb, n1, DFT_N2 * D_HY)


def _lead_inv_body(bre_ref, bim_ref, c_ref, s_ref, z_ref, gate_ref, skip_ref, gn_ref, o_ref, *, inv_n, last):
    y = (jnp.dot(c_ref[...], bre_ref[...], preferred_element_type=F32)
         + jnp.dot(s_ref[...], bim_ref[...], preferred_element_type=F32)) * inv_n
    out = gate_ref[...].astype(F32) * (y + skip_ref[...] * z_ref[...].astype(F32))
    if not last:
        o_ref[...] = out
    else:
        gn = gn_ref[...]
        for c in range(out.shape[1] // D_HY):
            blk = out[:, c * D_HY:(c + 1) * D_HY]
            ms = jnp.mean(blk * blk, axis=-1, keepdims=True)
            o_ref[:, c * D_HY:(c + 1) * D_HY] = (blk * lax.rsqrt(ms + EPS) * gn).astype(o_ref.dtype)


def _lead_inv(b_re, b_im, z, z_plane, gates, gate_plane, skip, gn, tb, last):
    n1, n1h = tb["n1"], tb["n1h"]
    nb, _, cols = b_re.shape
    tn = _tile(cols, 4096)
    skip_t = jnp.tile(skip[None, :], (1, tn // D_HY))
    b_spec = pl.BlockSpec((None, n1, tn), lambda b, j: (b, 0, j))
    t_spec = pl.BlockSpec((n1h, n1), lambda b, j: (0, 0))
    return pl.pallas_call(
        functools.partial(_lead_inv_body, inv_n=1.0 / tb["n"], last=last),
        grid=(nb, cols // tn),
        in_specs=[b_spec, b_spec, t_spec, t_spec,
                  pl.BlockSpec((None, n1h, tn), lambda b, j: (z_plane, b, j)),
                  pl.BlockSpec((None, n1h, tn), lambda b, j: (gate_plane, b, j)),
                  pl.BlockSpec((1, tn), lambda b, j: (0, 0)),
                  pl.BlockSpec((1, D_HY), lambda b, j: (0, 0))],
        out_specs=pl.BlockSpec((None, n1h, tn), lambda b, j: (0, b, j)),
        out_shape=jax.ShapeDtypeStruct((1, nb * n1h, cols), BF16 if last else F32),
        compiler_params=_cp("arbitrary", "arbitrary"),
        name="hyena_lead_inv",
    )(b_re, b_im, tb["lead_ic"], tb["lead_is"], z, gates, skip_t, gn)


def _hyena(conv_out, nb, seq, tb, kf_re, kf_im, skip, gn):
    n1h = tb["n1h"]
    planes = conv_out.reshape(conv_out.shape[0], nb * n1h, DFT_N2 * D_HY)
    z, z_plane = planes, 0
    for order in range(2):
        a_re, a_im = _lead_fwd(z, tb["lead_f"], z_plane, nb)
        b_re, b_im = _mid_conv(a_re, a_im, kf_re, kf_im, order, tb)
        z = _lead_inv(b_re, b_im, z, z_plane, planes, 1 + order, skip[order], gn, tb, order == 1)
        z_plane = 0
    return z.reshape(nb * seq, D_HY)


def _hyena_spectra(seq, tb, w1, b1, fr, w2, b2, w3):
    w1 = jnp.pad(w1, ((0, LANE - HY_EMB), (0, 0)))
    zero1 = jnp.zeros_like(w1)
    w1 = jnp.concatenate([jnp.concatenate([w1, zero1], axis=1),
                          jnp.concatenate([zero1, w1], axis=1)], axis=0)
    zero2 = jnp.zeros_like(w2)
    w2 = jnp.concatenate([jnp.concatenate([w2, zero2], axis=1),
                          jnp.concatenate([zero2, w2], axis=1)], axis=0)
    pair = lambda v: jnp.concatenate([v, v])[None, :]
    w3 = w3.reshape(HY_FILT, 2, 2, D_HY).transpose(2, 0, 1, 3).reshape(2, HY_FILT, 2 * D_HY)
    zero3 = jnp.zeros_like(w3)
    w3 = jnp.stack([jnp.concatenate([w3, zero3], axis=1),
                    jnp.concatenate([zero3, w3], axis=1)], axis=1).astype(BF16)
    taps, sumsq = _hy_filter(seq, _hy_features(seq), w1, pair(b1), pair(fr), w2, pair(b2), w3,
                             _hy_deltas())
    a_re, a_im = _lead_fwd(taps[None], tb["lead_ff"], 0, 1)
    return _mid_spec(a_re[0], a_im[0], sumsq, tb)


def _s5_scan_body(u_ref, pm_ref, bd_ref, a_ref, cd_ref, x0_ref, o_ref, bu_ref, st_ref, y_ref,
                  *, tstep, reverse, final_state):
    tb = pl.program_id(0)
    ns = S5_STREAMS
    half = S5_STATE
    cw = S5_BLOCK_GROUPS * S5_P

    @pl.when(tb == 0)
    def _():
        st_ref[...] = x0_ref[...]

    u = u_ref[...].reshape(ns * tstep, D_S5).astype(BF16)
    u_tm = jnp.dot(pm_ref[...], u, preferred_element_type=F32).astype(BF16)

    def project_in(blk):
        bu_ref[:, 2 * blk * cw:2 * (blk + 1) * cw] = jnp.dot(
            u_tm[:, blk * LANE:(blk + 1) * LANE], bd_ref[blk], preferred_element_type=F32)

    def scan(blk):
        bre = slice(2 * blk * cw, (2 * blk + 1) * cw)
        bim = slice((2 * blk + 1) * cw, (2 * blk + 2) * cw)
        sre = slice(blk * cw, (blk + 1) * cw)
        sim = slice(half + blk * cw, half + (blk + 1) * cw)
        ar = a_ref[:, sre]
        ai = a_ref[:, sim]
        xr = st_ref[:, sre]
        xi = st_ref[:, sim]
        for i in range(tstep):
            t = tstep - 1 - i if reverse else i
            rows = slice(t * ns, (t + 1) * ns)
            xr, xi = (ar * xr - ai * xi + bu_ref[rows, bre],
                      ar * xi + ai * xr + bu_ref[rows, bim])
            if not final_state:
                bu_ref[rows, bre] = xr
                bu_ref[rows, bim] = xi
        st_ref[:, sre] = xr
        st_ref[:, sim] = xi

    def project_out(blk):
        y_ref[blk] = jnp.dot(bu_ref[:, 2 * blk * cw:2 * (blk + 1) * cw].astype(BF16), cd_ref[blk],
                             preferred_element_type=F32)

    project_in(0)
    for blk in range(S5_NBLOCKS):
        if blk + 1 < S5_NBLOCKS:
            project_in(blk + 1)
        scan(blk)
        if not final_state:
            project_out(blk)

    if final_state:
        @pl.when(tb == pl.num_programs(0) - 1)
        def _():
            o_ref[...] = st_ref[...]
    else:
        for s in range(ns):
            for blk in range(S5_NBLOCKS):
                o_ref[s, :, blk * LANE:(blk + 1) * LANE] = y_ref[blk, pl.ds(s, tstep, stride=ns), :]


def _s5_scan(proj, pmat, bd, a_b, cd, x0, tlen, reverse, final_state):
    ns = S5_STREAMS
    d = 1 if reverse else 0
    tstep = pmat.shape[0] // ns
    nt = tlen // tstep
    proj3 = proj.reshape(ns, tlen, proj.shape[1])
    blk_in = S5_BLOCK_GROUPS * S5_GROUP
    blk_state = 2 * S5_BLOCK_GROUPS * S5_P

    def window(t):
        return nt - 1 - t if reverse else t

    if final_state:
        out_spec = pl.BlockSpec((ns, 2 * S5_STATE), lambda t: (0, 0))
        out_shape = jax.ShapeDtypeStruct((ns, 2 * S5_STATE), F32)
    else:
        out_spec = pl.BlockSpec((ns, tstep, D_S5), lambda t: (0, window(t), 0))
        out_shape = jax.ShapeDtypeStruct((ns, tlen, D_S5), F32)
    out = pl.pallas_call(
        functools.partial(_s5_scan_body, tstep=tstep, reverse=reverse, final_state=final_state),
        grid=(nt,),
        in_specs=[pl.BlockSpec((ns, tstep, D_S5), lambda t: (0, window(t), COL_U)),
                  pl.BlockSpec((ns * tstep, ns * tstep), lambda t: (0, 0)),
                  pl.BlockSpec((None, S5_NBLOCKS, blk_in, blk_state), lambda t: (d, 0, 0, 0)),
                  pl.BlockSpec((None, ns, 2 * S5_STATE), lambda t: (d, 0, 0)),
                  pl.BlockSpec((None, S5_NBLOCKS, blk_state, blk_in), lambda t: (d, 0, 0, 0)),
                  pl.BlockSpec((None, ns, 2 * S5_STATE), lambda t: (d, 0, 0))],
        out_specs=out_spec,
        out_shape=out_shape,
        scratch_shapes=[pltpu.VMEM((tstep * ns, 2 * S5_STATE), F32),
                        pltpu.VMEM((ns, 2 * S5_STATE), F32),
                        pltpu.VMEM((D_S5 // LANE, tstep * ns, LANE), F32)],
        compiler_params=_cp("arbitrary"),
        name="s5_end_state" if final_state else "s5_scan",
    )(proj3, pmat, bd, a_b, cd, x0)
    return out if final_state else out.reshape(ns * tlen, D_S5)


def _s5_params(lam_re, lam_im, log_dt, b_re, b_im, c_re, c_im):
    lr = jnp.minimum(lam_re, -1e-4)
    li = lam_im
    dt = jnp.exp(log_dt)[..., None]
    er = jnp.exp(lr * dt)
    a_re = er * jnp.cos(li * dt)
    a_im = er * jnp.sin(li * dt)
    den = lr * lr + li * li
    q_re = ((a_re - 1.0) * lr + a_im * li) / den
    q_im = (a_im * lr - (a_re - 1.0) * li) / den
    bb_re = q_re[..., None] * b_re - q_im[..., None] * b_im
    bb_im = q_re[..., None] * b_im + q_im[..., None] * b_re
    nbk, gb = S5_NBLOCKS, S5_BLOCK_GROUPS
    eye = jnp.eye(gb, dtype=F32)

    def in_mat(x):
        x = x.reshape(2, nbk, gb, S5_P, S5_GROUP)
        return jnp.einsum("dbgph,gk->dbghkp", x, eye).reshape(2, nbk, gb * S5_GROUP, gb * S5_P)

    def out_mat(x):
        x = x.reshape(2, nbk, gb, S5_GROUP, S5_P)
        return jnp.einsum("dbghp,gk->dbgpkh", x, eye).reshape(2, nbk, gb * S5_P, gb * S5_GROUP)

    bd = jnp.concatenate([in_mat(bb_re), in_mat(bb_im)], axis=3).astype(BF16)
    cd = jnp.concatenate([out_mat(c_re), out_mat(-c_im)], axis=2).astype(BF16)
    a_bar = jnp.concatenate([a_re.reshape(2, S5_STATE), a_im.reshape(2, S5_STATE)], axis=1)
    return (lr * dt).reshape(2, S5_STATE), (li * dt).reshape(2, S5_STATE), a_bar, bd, cd


def _s5_row_perm(tstep):
    ns = S5_STREAMS
    r = jnp.arange(ns * tstep)
    src = (r % ns) * tstep + r // ns
    return (src[:, None] == jnp.arange(ns * tstep)[None, :]).astype(BF16)


def _s5_mixer(proj, nb, seq, s5p):
    la_re, la_im, a_bar, bd, cd = s5p
    ns = S5_STREAMS
    ncs = ns // nb
    tlen = seq // ncs
    pmat = _s5_row_perm(_tile(tlen, S5_TSTEP))
    a_b = jnp.broadcast_to(a_bar[:, None, :], (2, ns, 2 * S5_STATE))
    zeros = jnp.zeros((2, ns, 2 * S5_STATE), F32)
    mag = jnp.exp(la_re * tlen)
    at_re = (mag * jnp.cos(la_im * tlen))[:, None, :]
    at_im = (mag * jnp.sin(la_im * tlen))[:, None, :]
    ys = []
    for d in range(2):
        x_end = _s5_scan(proj, pmat, bd, a_b, cd, zeros, tlen, d == 1, True)
        xe_re = x_end[:, :S5_STATE].reshape(nb, ncs, S5_STATE)
        xe_im = x_end[:, S5_STATE:].reshape(nb, ncs, S5_STATE)
        order = range(ncs) if d == 0 else range(ncs - 1, -1, -1)
        cr = jnp.zeros((nb, S5_STATE), F32)
        ci = jnp.zeros((nb, S5_STATE), F32)
        rows = [None] * ncs
        for c in order:
            rows[c] = jnp.concatenate([cr, ci], axis=-1)
            cr, ci = (at_re[d] * cr - at_im[d] * ci + xe_re[:, c],
                      at_re[d] * ci + at_im[d] * cr + xe_im[:, c])
        x0 = jnp.stack(rows, axis=1).reshape(ns, 2 * S5_STATE)
        x0 = jnp.stack([x0, x0])
        ys.append(_s5_scan(proj, pmat, bd, a_b, cd, x0, tlen, d == 1, False))
    return ys


def _s5_post_body(yf_ref, yb_ref, u_ref, d_ref, w_ref, b_ref, gn_ref, o_ref):
    y = yf_ref[...] + yb_ref[...] + d_ref[...] * u_ref[...].astype(F32)
    g = jax.nn.gelu(y)
    r = jnp.dot(g.astype(BF16), w_ref[...], preferred_element_type=F32) + b_ref[...]
    out = r[:, :D_S5] * jax.nn.sigmoid(r[:, D_S5:])
    ms = jnp.mean(out * out, axis=-1, keepdims=True)
    o_ref[...] = (out * lax.rsqrt(ms + EPS) * gn_ref[...]).astype(o_ref.dtype)


def _s5_post(y_f, y_b, proj, d, glu_w, glu_b, gn):
    m = y_f.shape[0]
    tm = _tile(m, 512)
    full = lambda shape: pl.BlockSpec(shape, lambda i: (0, 0))
    return pl.pallas_call(
        _s5_post_body,
        grid=(m // tm,),
        in_specs=[pl.BlockSpec((tm, D_S5), lambda i: (i, 0)),
                  pl.BlockSpec((tm, D_S5), lambda i: (i, 0)),
                  pl.BlockSpec((tm, D_S5), lambda i: (i, COL_U)),
                  full((1, D_S5)), full((D_S5, 2 * D_S5)), full((1, 2 * D_S5)), full((1, D_S5))],
        out_specs=pl.BlockSpec((tm, D_S5), lambda i: (i, 0)),
        out_shape=jax.ShapeDtypeStruct((m, D_S5), BF16),
        compiler_params=_cp("arbitrary"),
        name="s5_glu_norm",
    )(y_f, y_b, proj, d, glu_w, glu_b, gn)


def _split3(x):
    hi = x.astype(BF16)
    rest = x - hi.astype(F32)
    mid = rest.astype(BF16)
    return hi, mid, (rest - mid.astype(F32)).astype(BF16)


def _mlstm_body(*refs, reverse, epilogue):
    if epilogue:
        (q_ref, k_ref, v_ref, g_ref, gt_ref, bias_ref, biast_ref, seen_ref, upto_ref, hf_ref, o_gate_ref,
         gn_ref, out_ref, c_ref, n_ref, m_ref) = refs
    else:
        (q_ref, k_ref, v_ref, g_ref, gt_ref, bias_ref, biast_ref, seen_ref, upto_ref,
         out_ref, c_ref, n_ref, m_ref) = refs
        hf_ref = o_gate_ref = gn_ref = None

    @pl.when(pl.program_id(2) == 0)
    def _():
        c_ref[...] = jnp.zeros_like(c_ref)
        n_ref[...] = jnp.zeros_like(n_ref)
        m_ref[...] = jnp.zeros_like(m_ref)

    tc = ML_CHUNK
    r = lax.broadcasted_iota(jnp.int32, (tc, tc), 0)
    s = lax.broadcasted_iota(jnp.int32, (tc, tc), 1)
    seen = (s >= r) if reverse else (s <= r)
    g = g_ref[...] + bias_ref[...]
    gt = gt_ref[...] + biast_ref[:, 0:1]
    for hh in range(ML_HEADS_PER_STEP):
        _mlstm_head(hh, g, gt, seen, q_ref, k_ref, v_ref, seen_ref, upto_ref, hf_ref, o_gate_ref,
                    gn_ref, out_ref, c_ref, n_ref, m_ref, reverse=reverse)


def _mlstm_head(hh, g, gt, seen, q_ref, k_ref, v_ref, seen_ref, upto_ref, hf_ref, o_gate_ref, gn_ref,
                out_ref, c_ref, n_ref, m_ref, *, reverse):
    head = pl.program_id(1) * ML_HEADS_PER_STEP + hh
    cols = slice(hh * ML_DH, (hh + 1) * ML_DH)
    tc = ML_CHUNK
    gate_i = 2 if reverse else 0
    idx_i = gate_i * ML_HEADS + head
    idx_f = idx_i + ML_HEADS
    tiled = lambda x, n: x if n == LANE else jnp.concatenate([x] * (n // LANE), axis=1)
    wide = lambda x: tiled(x, ML_DH)
    span = lambda x: tiled(x, tc)
    dot = functools.partial(jnp.dot, preferred_element_type=F32)

    lane = lax.broadcasted_iota(jnp.int32, g.shape, 1)
    ig = jnp.broadcast_to(jnp.sum(jnp.where(lane == idx_i, g, 0.0), axis=1, keepdims=True), g.shape)
    fg = jnp.broadcast_to(jnp.sum(jnp.where(lane == idx_f, g, 0.0), axis=1, keepdims=True), g.shape)
    sub = lax.broadcasted_iota(jnp.int32, gt.shape, 0)
    ig_row = jnp.sum(jnp.where(sub == idx_i, gt, 0.0), axis=0, keepdims=True)
    fg_row = jnp.sum(jnp.where(sub == idx_f, gt, 0.0), axis=0, keepdims=True)
    lf = jax.nn.log_sigmoid(fg)
    lf_row = jax.nn.log_sigmoid(fg_row)

    seen_b = seen_ref[...]
    upto_b = upto_ref[...]
    b_col = sum(dot(seen_b, piece) for piece in _split3(lf))
    lf_rows = jnp.broadcast_to(lf_row, (SUBLANE, tc))
    b_row = sum(dot(piece, upto_b) for piece in _split3(lf_rows))[0:1]
    b_last = jnp.sum(lf_row, axis=1, keepdims=True)

    m_prev = m_ref[hh:hh + 1, 0:1]
    src = ig_row - b_row
    dmat = jnp.where(seen, span(b_col) + src, -jnp.inf)
    g_car = b_col + m_prev
    mt = jnp.maximum(g_car, jnp.max(dmat, axis=1, keepdims=True))
    inter = jnp.exp(g_car - mt)
    qb = q_ref[:, cols]
    kb = k_ref[:, cols]
    vb = v_ref[:, cols].astype(BF16)
    ones = jnp.ones((tc, LANE), BF16)
    qk = lax.dot_general(qb, kb, (((1,), (1,)), ((), ())), preferred_element_type=F32)
    sc = (qk * jnp.exp(dmat - span(mt))).astype(BF16)
    c_state = c_ref[hh]
    n_state = n_ref[hh]
    num = wide(inter) * dot(qb, c_state.astype(BF16)) + dot(sc, vb)
    den = inter * dot(qb, n_state.astype(BF16)) + dot(sc, ones)
    h = num * wide(1.0 / jnp.maximum(jnp.abs(den), jnp.exp(-mt)))

    a_row = b_last + src
    a_col = b_last - b_col + ig
    m_new = jnp.maximum(b_last + m_prev, jnp.max(a_row, axis=1, keepdims=True))
    decay = jnp.exp(b_last + m_prev - m_new)
    kw = (kb.astype(F32) * wide(jnp.exp(a_col - m_new))).astype(BF16)
    upd = lax.dot_general(kw, jnp.concatenate([vb, ones], axis=1), (((0,), (0,)), ((), ())),
                          preferred_element_type=F32)
    c_ref[hh] = decay * c_state + upd[:, :ML_DH]
    n_ref[hh] = decay * n_state + upd[:, ML_DH:]
    m_ref[hh:hh + 1, 0:1] = m_new

    if hf_ref is not None:
        h = h + hf_ref[:, cols]
        ms = jnp.mean(h * h, axis=-1, keepdims=True)
        hn = h * lax.rsqrt(ms + EPS) * gn_ref[:, cols]
        out_ref[:, cols] = (hn * jax.nn.sigmoid(o_gate_ref[:, cols].astype(F32))).astype(out_ref.dtype)
    else:
        out_ref[:, cols] = h


def _mlstm_dir(qk_planes, proj, gates, gates_t, bias, bias_t, nb, seq, reverse, h_fwd=None, gn=None):
    tc = ML_CHUNK
    nch = seq // tc
    hps = ML_HEADS_PER_STEP
    width = hps * ML_DH
    col_v = (N_CONV + D_S5) // width
    col_o = (N_CONV + D_S5 + D_ML) // width

    def chunk(b, ci):
        return b * nch + (nch - 1 - ci if reverse else ci)

    in_specs = [
        pl.BlockSpec((None, tc, width), lambda b, h, ci: (0, chunk(b, ci), h)),
        pl.BlockSpec((None, tc, width), lambda b, h, ci: (1, chunk(b, ci), h)),
        pl.BlockSpec((tc, width), lambda b, h, ci: (chunk(b, ci), col_v + h)),
        pl.BlockSpec((tc, LANE), lambda b, h, ci: (chunk(b, ci), 0)),
        pl.BlockSpec((4 * ML_HEADS, tc), lambda b, h, ci: (0, chunk(b, ci))),
        pl.BlockSpec((1, LANE), lambda b, h, ci: (0, 0)),
        pl.BlockSpec((4 * ML_HEADS, LANE), lambda b, h, ci: (0, 0)),
        pl.BlockSpec((tc, tc), lambda b, h, ci: (0, 0)),
        pl.BlockSpec((tc, tc), lambda b, h, ci: (0, 0)),
    ]
    pos = jnp.arange(tc)
    seen = (pos[None, :] >= pos[:, None]) if reverse else (pos[None, :] <= pos[:, None])
    seen = seen.astype(BF16)
    args = [qk_planes, qk_planes, proj, gates, gates_t, bias, bias_t, seen, seen.T]
    epilogue = h_fwd is not None
    if epilogue:
        in_specs += [
            pl.BlockSpec((tc, width), lambda b, h, ci: (chunk(b, ci), h)),
            pl.BlockSpec((tc, width), lambda b, h, ci: (chunk(b, ci), col_o + h)),
            pl.BlockSpec((1, width), lambda b, h, ci: (0, h)),
        ]
        args += [h_fwd, proj, gn]
    return pl.pallas_call(
        functools.partial(_mlstm_body, reverse=reverse, epilogue=epilogue),
        grid=(nb, ML_HEADS // hps, nch),
        in_specs=in_specs,
        out_specs=pl.BlockSpec((tc, width), lambda b, h, ci: (chunk(b, ci), h)),
        out_shape=jax.ShapeDtypeStruct((nb * seq, D_ML), BF16 if epilogue else F32),
        scratch_shapes=[pltpu.VMEM((hps, ML_DH, ML_DH), F32), pltpu.VMEM((hps, ML_DH, LANE), F32),
                        pltpu.VMEM((SUBLANE, LANE), F32)],
        compiler_params=_cp("arbitrary", "arbitrary", "arbitrary"),
        name="mlstm_bwd_norm" if reverse else "mlstm_fwd",
    )(*args)


def _layer(x, nb, seq, tb, p):
    proj, gates = _proj_in(x, p["ln_g"][0:1], p["w_in"], p["w_gate"])
    gates_t = gates[:, :4 * ML_HEADS].T
    gn = p["group_norm"]
    gn_hy = gn[None, :D_HY]
    gn_s5 = gn[None, D_HY:D_HY + D_S5]
    gn_ml = gn[None, D_HY + D_S5:]

    hy_planes = _short_conv(proj, p["conv_w"], p["conv_b"][None, :], nb, seq, False)
    qk_planes = _short_conv(proj, p["conv_w"], p["conv_b"][None, :], nb, seq, True)
    kf_re, kf_im = _hyena_spectra(seq, tb, p["hy_w1"], p["hy_b1"], p["hy_freq"], p["hy_w2"],
                                  p["hy_b2"], p["hy_w3"])
    y_hy = _hyena(hy_planes, nb, seq, tb, kf_re, kf_im, p["hy_skip"], gn_hy)
    y_s5 = _s5_post(*_s5_mixer(proj, nb, seq, p["s5"]), proj, p["s5_d"][None, :],
                    p["s5_glu_w"], p["s5_glu_b"][None, :], gn_s5)
    ml_args = (qk_planes, proj, gates, gates_t, p["ml_bias"], p["ml_bias_t"], nb, seq)
    h_f = _mlstm_dir(*ml_args, False)
    y_ml = _mlstm_dir(*ml_args, True, h_f, gn_ml)

    w_out = p["w_out"]
    x = _proj_out([y_hy, y_s5, y_ml], [w_out[:D_HY], w_out[D_HY:D_HY + D_S5], w_out[D_HY + D_S5:]],
                  x, p["ln_g"][1:2])
    return _mlp(x, p["ln_g"][2:3], p["mlp_w1"], p["mlp_w2"], p["ln_g"][3:4])


_PARAM_NAMES = ("ln_g", "w_in", "conv_w", "conv_b", "hy_w1", "hy_b1", "hy_freq", "hy_w2", "hy_b2",
                "hy_w3", "hy_skip", "s5_lam_re", "s5_lam_im", "s5_log_dt", "s5_b_re", "s5_b_im",
                "s5_c_re", "s5_c_im", "s5_d", "s5_glu_w", "s5_glu_b", "ml_gate_b", "group_norm",
                "w_out", "mlp_w1", "mlp_w2")


def _prepare(p):
    p = dict(p)
    w_in = p["w_in"].astype(BF16)
    p["w_in"] = w_in[:, :P_MIX]
    p["w_gate"] = jnp.pad(w_in[:, P_MIX:], ((0, 0), (0, LANE - 4 * ML_HEADS)))
    p["w_out"] = p["w_out"].astype(BF16)
    p["mlp_w1"] = p["mlp_w1"].astype(BF16)
    p["mlp_w2"] = p["mlp_w2"].astype(BF16)
    p["s5_glu_w"] = p["s5_glu_w"].astype(BF16)
    p["ml_bias"] = jnp.pad(p["ml_gate_b"].reshape(1, -1), ((0, 0), (0, LANE - 4 * ML_HEADS)))
    p["ml_bias_t"] = jnp.broadcast_to(p["ml_gate_b"].reshape(-1, 1), (4 * ML_HEADS, LANE))
    p["s5"] = _s5_params(p["s5_lam_re"], p["s5_lam_im"], p["s5_log_dt"], p["s5_b_re"],
                         p["s5_b_im"], p["s5_c_re"], p["s5_c_im"])
    return p


def kernel(x_prompt, x_sample, ln_g, w_in, conv_w, conv_b, hy_w1, hy_b1, hy_freq, hy_w2, hy_b2, hy_w3,
           hy_skip, s5_lam_re, s5_lam_im, s5_log_dt, s5_b_re, s5_b_im, s5_c_re, s5_c_im, s5_d,
           s5_glu_w, s5_glu_b, ml_gate_b, group_norm, w_out, mlp_w1, mlp_w2):
    params = (ln_g, w_in, conv_w, conv_b, hy_w1, hy_b1, hy_freq, hy_w2, hy_b2, hy_w3, hy_skip,
              s5_lam_re, s5_lam_im, s5_log_dt, s5_b_re, s5_b_im, s5_c_re, s5_c_im, s5_d,
              s5_glu_w, s5_glu_b, ml_gate_b, group_norm, w_out, mlp_w1, mlp_w2)
    streams = []
    for x in (x_prompt, x_sample):
        nb, seq, _ = x.shape
        streams.append([x.reshape(nb * seq, D_MODEL), nb, seq, _dft_tables(seq)])
    for layer in range(DEPTH):
        p = _prepare({name: arr[layer] for name, arr in zip(_PARAM_NAMES, params)})
        for st in streams:
            st[0] = _layer(st[0], st[1], st[2], st[3], p)
    return tuple(st[0].reshape(x.shape) for st, x in zip(streams, (x_prompt, x_sample)))
```

```python
import functools
import math

import jax
import jax.numpy as jnp
from jax import lax
from jax.experimental import pallas as pl
from jax.experimental.pallas import tpu as pltpu

F32 = jnp.float32
BF16 = jnp.bfloat16

D_MODEL = 2048
DEPTH = 4
D_HY = 512
D_S5 = 512
D_ML = 1024
HY_EMB = 33
HY_BANDS = 16
HY_FILT = 64
HY_FAST_DECAY = 0.3
HY_SLOW_DECAY = 1.5
HY_TARGET = 1e-2
S5_GROUP = 16
S5_G = 32
S5_P = 64
S5_STATE = S5_G * S5_P
ML_HEADS = 4
ML_DH = 256
D_FF = 4 * D_MODEL
N_CONV = 3 * D_HY + 2 * D_ML
P_IN = N_CONV + D_S5 + 2 * D_ML + 4 * ML_HEADS
P_MIX = P_IN - 4 * ML_HEADS
EPS = 1e-6

LANE = 128
SUBLANE = 8
VMEM_LIMIT_BYTES = 48 * 1024 * 1024

DFT_N2 = LANE
S5_STREAMS = SUBLANE
S5_TSTEP = 64
S5_BLOCK_GROUPS = LANE // S5_GROUP
S5_NBLOCKS = S5_G // S5_BLOCK_GROUPS
ML_CHUNK = 256
ML_HEADS_PER_STEP = 4
COL_U = N_CONV // D_S5


def _cp(*sem):
    return pltpu.CompilerParams(dimension_semantics=sem, vmem_limit_bytes=VMEM_LIMIT_BYTES)


def _tile(n, pref):
    t = min(n, pref)
    while n % t:
        t //= 2
    return t


def _rms(x, g):
    ms = jnp.mean(x * x, axis=-1, keepdims=True)
    return x * lax.rsqrt(ms + EPS) * g


def _proj_in_body(x_ref, g_ref, w_ref, wg_ref, o_ref, og_ref, xn_ref):
    @pl.when(pl.program_id(1) == 0)
    def _():
        xn = _rms(x_ref[...], g_ref[...]).astype(BF16)
        xn_ref[...] = xn
        og_ref[...] = jnp.dot(xn, wg_ref[...], preferred_element_type=F32)

    o_ref[...] = jnp.dot(xn_ref[...], w_ref[...], preferred_element_type=F32).astype(o_ref.dtype)


def _proj_in(x, g, w, w_gate):
    m, k = x.shape
    n = w.shape[1]
    tm = _tile(m, 1024)
    tn = 1024
    return pl.pallas_call(
        _proj_in_body,
        grid=(m // tm, n // tn),
        in_specs=[pl.BlockSpec((tm, k), lambda i, j: (i, 0)),
                  pl.BlockSpec((1, k), lambda i, j: (0, 0)),
                  pl.BlockSpec((k, tn), lambda i, j: (0, j)),
                  pl.BlockSpec((k, LANE), lambda i, j: (0, 0))],
        out_specs=[pl.BlockSpec((tm, tn), lambda i, j: (i, j)),
                   pl.BlockSpec((tm, LANE), lambda i, j: (i, 0))],
        out_shape=[jax.ShapeDtypeStruct((m, n), BF16), jax.ShapeDtypeStruct((m, LANE), F32)],
        scratch_shapes=[pltpu.VMEM((tm, k), BF16)],
        compiler_params=_cp("arbitrary", "arbitrary"),
        name="norm_proj_in",
    )(x, g, w, w_gate)


def _proj_out_body(a0_ref, a1_ref, a2_ref, w0_ref, w1_ref, w2_ref, r_ref, g_ref, o_ref):
    half = o_ref.shape[0] // 2
    for rows in (slice(0, half), slice(half, 2 * half)):
        f = (jnp.dot(a0_ref[rows, :].astype(BF16), w0_ref[...], preferred_element_type=F32)
             + jnp.dot(a1_ref[rows, :].astype(BF16), w1_ref[...], preferred_element_type=F32)
             + jnp.dot(a2_ref[rows, :].astype(BF16), w2_ref[...], preferred_element_type=F32))
        o_ref[rows, :] = r_ref[rows, :] + _rms(f, g_ref[...])


def _proj_out(parts, weights, resid, g):
    m, n = resid.shape
    tm = _tile(m, 512)
    row = lambda width: pl.BlockSpec((tm, width), lambda i: (i, 0))
    full = lambda arr: pl.BlockSpec(arr.shape, lambda i: (0, 0))
    return pl.pallas_call(
        _proj_out_body,
        grid=(m // tm,),
        in_specs=[row(a.shape[1]) for a in parts] + [full(w) for w in weights] + [row(n), full(g)],
        out_specs=row(n),
        out_shape=jax.ShapeDtypeStruct((m, n), F32),
        compiler_params=_cp("arbitrary"),
        name="proj_out_resnorm",
    )(*parts, *weights, resid, g)


def _mlp_body(x_ref, g_in_ref, w1_ref, w2_ref, g_out_ref, o_ref, xn_ref, acc_ref):
    j = pl.program_id(1)

    @pl.when(j == 0)
    def _():
        xn_ref[...] = _rms(x_ref[...], g_in_ref[...]).astype(BF16)
        acc_ref[...] = jnp.zeros_like(acc_ref)

    h = jnp.dot(xn_ref[...], w1_ref[...], preferred_element_type=F32)
    h = jnp.square(jnp.maximum(h, 0.0)).astype(BF16)
    acc_ref[...] += jnp.dot(h, w2_ref[...], preferred_element_type=F32)

    @pl.when(j == pl.num_programs(1) - 1)
    def _():
        o_ref[...] = x_ref[...] + _rms(acc_ref[...], g_out_ref[...])


def _mlp(x, g_in, w1, w2, g_out):
    m, k = x.shape
    ff = w1.shape[1]
    tm = _tile(m, 512)
    tf = 1024
    return pl.pallas_call(
        _mlp_body,
        grid=(m // tm, ff // tf),
        in_specs=[pl.BlockSpec((tm, k), lambda i, j: (i, 0)),
                  pl.BlockSpec((1, k), lambda i, j: (0, 0)),
                  pl.BlockSpec((k, tf), lambda i, j: (0, j)),
                  pl.BlockSpec((tf, k), lambda i, j: (j, 0)),
                  pl.BlockSpec((1, k), lambda i, j: (0, 0))],
        out_specs=pl.BlockSpec((tm, k), lambda i, j: (i, 0)),
        out_shape=jax.ShapeDtypeStruct((m, k), F32),
        scratch_shapes=[pltpu.VMEM((tm, k), BF16), pltpu.VMEM((tm, k), F32)],
        compiler_params=_cp("arbitrary", "arbitrary"),
        name="mlp_resnorm",
    )(x, g_in, w1, w2, g_out)


def _conv_body(x_ref, p_ref, n_ref, w_ref, b_ref, o_ref, *, nblk, qk):
    i = pl.program_id(1)
    j = pl.program_id(2)
    x = x_ref[...].astype(F32)
    tm = x.shape[0]
    row = lax.broadcasted_iota(jnp.int32, x.shape, 0)
    halo = p_ref.shape[0]
    prev_row = jnp.where(i == 0, 0.0, p_ref[halo - 1:halo, :].astype(F32))
    next_row = jnp.where(i == nblk - 1, 0.0, n_ref[0:1, :].astype(F32))
    xm = jnp.where(row == 0, prev_row, pltpu.roll(x, 1, 0))
    xp = jnp.where(row == tm - 1, next_row, pltpu.roll(x, tm - 1, 0))
    w = w_ref[...]
    y = b_ref[...] + xm * w[0:1] + x * w[1:2] + xp * w[2:3]
    if qk:
        y = y * jax.nn.sigmoid(y) * jnp.where(j >= 2, ML_DH ** -0.5, 1.0)
    o_ref[...] = y.astype(o_ref.dtype)


def _short_conv(proj, conv_w, conv_b, nb, seq, qk):
    mtot = proj.shape[0]
    tm = _tile(seq, 2048)
    nblk = seq // tm
    col0, ncol = (3, 4) if qk else (0, 3)
    ppo = 2 if qk else 1
    halo = 2 * SUBLANE
    last_halo = mtot // halo - 1
    tmh = tm // halo

    def x_map(b, i, j):
        return (b * nblk + i, col0 + j)

    def p_map(b, i, j):
        return (jnp.maximum((b * nblk + i) * tmh - 1, 0), col0 + j)

    def n_map(b, i, j):
        return (jnp.minimum((b * nblk + i + 1) * tmh, last_halo), col0 + j)

    return pl.pallas_call(
        functools.partial(_conv_body, nblk=nblk, qk=qk),
        grid=(nb, nblk, ncol),
        in_specs=[pl.BlockSpec((tm, D_HY), x_map),
                  pl.BlockSpec((halo, D_HY), p_map),
                  pl.BlockSpec((halo, D_HY), n_map),
                  pl.BlockSpec((3, D_HY), lambda b, i, j: (0, col0 + j)),
                  pl.BlockSpec((1, D_HY), lambda b, i, j: (0, col0 + j))],
        out_specs=pl.BlockSpec((None, tm, D_HY), lambda b, i, j: (j // ppo, b * nblk + i, j % ppo)),
        out_shape=jax.ShapeDtypeStruct((ncol // ppo, nb * seq, ppo * D_HY), BF16),
        compiler_params=_cp("arbitrary", "arbitrary", "arbitrary"),
        name="short_conv_qk" if qk else "short_conv_hy",
    )(proj, proj, proj, conv_w, conv_b)


def _dft_tables(seq):
    n = 2 * seq
    n1 = n // DFT_N2
    n1h = n1 // 2
    n1k = n1h + SUBLANE
    k1 = jnp.arange(n1k, dtype=jnp.int32)
    t1 = jnp.arange(n1h, dtype=jnp.int32)
    kept = (k1 <= n1h).astype(F32)[:, None]
    weight = kept * jnp.where((k1 == 0) | (k1 == n1h), 1.0, 2.0)[:, None]
    ang = ((k1[:, None] * t1[None, :]) % n1).astype(F32) * (2.0 * math.pi / n1)
    lead_f = jnp.concatenate([kept * jnp.cos(ang), -kept * jnp.sin(ang)], axis=0).astype(BF16)
    lead_ic = (weight * jnp.cos(ang)).T.astype(BF16)
    lead_is = (-weight * jnp.sin(ang)).T.astype(BF16)
    t1f = jnp.arange(n1, dtype=jnp.int32)
    angf = ((k1[:, None] * t1f[None, :]) % n1).astype(F32) * (2.0 * math.pi / n1)
    lead_ff = jnp.concatenate([kept * jnp.cos(angf), -kept * jnp.sin(angf)], axis=0).astype(BF16)
    row = jnp.arange(2 * DFT_N2, dtype=jnp.int32)[None, :, None]
    col = jnp.arange(2 * DFT_N2, dtype=jnp.int32)[None, None, :]
    idx = ((col % DFT_N2) * ((row % DFT_N2) * n1 + k1[:, None, None])) % n
    phase = jnp.where(row // DFT_N2 == col // DFT_N2, 0.5 * math.pi,
                      jnp.where(row < col, 0.0, math.pi))
    mid_f = jnp.sin(idx.astype(F32) * (2.0 * math.pi / n) + phase).astype(BF16)
    return dict(n=n, n1=n1k, n1h=n1h, lead_f=lead_f, lead_ff=lead_ff, lead_ic=lead_ic, lead_is=lead_is,
                mid_f=mid_f)


def _hy_features(seq):
    tiles = 2 * seq // (SUBLANE * DFT_N2)
    half_rows = SUBLANE * DFT_N2 // 2
    tile = jnp.arange(tiles, dtype=jnp.int32)[:, None, None]
    row = jnp.arange(half_rows, dtype=jnp.int32)[None, :, None]
    side = jnp.arange(2, dtype=jnp.int32)[None, None, :]
    t2 = side * (DFT_N2 // 2) + row // SUBLANE
    t1 = tile * SUBLANE + row % SUBLANE
    slot = t1 * DFT_N2 + t2
    pos = jnp.where(slot < seq, slot, (2 * seq - slot) % seq).astype(F32)[..., None]
    t = pos * (1.0 / (seq - 1))
    w = 2.0 * math.pi * pos / seq
    f = jnp.linspace(1e-4, HY_BANDS - 1, HY_BANDS, dtype=F32)
    z = jnp.concatenate([t, jnp.cos(f * w), -jnp.sin(f * w)], axis=-1)
    z = jnp.pad(z, ((0, 0), (0, 0), (0, 0), (0, LANE - HY_EMB)))
    return z.reshape(tiles * half_rows, 2 * LANE)


def _hy_deltas():
    d = jnp.abs(jnp.linspace(math.log(HY_TARGET) / HY_SLOW_DECAY,
                             math.log(HY_TARGET) / HY_FAST_DECAY, D_HY, dtype=F32))
    return jnp.tile(d, 2)[None, :]


def _hy_filter_body(z_ref, w1_ref, b1_ref, fr_ref, w2_ref, b2_ref, w3_ref, dl_ref, k_ref, ss_ref, *, half):
    i = pl.program_id(0)
    hi = lax.Precision.HIGHEST
    z = z_ref[...]
    fr = fr_ref[...]
    h = jnp.sin(fr * (jnp.dot(z, w1_ref[...], precision=hi, preferred_element_type=F32) + b1_ref[...]))
    h = jnp.sin(fr * (jnp.dot(h, w2_ref[...], precision=hi, preferred_element_type=F32) + b2_ref[...]))
    hb = h.astype(BF16)
    dl = dl_ref[...]
    ha = jnp.dot(hb, w3_ref[0], preferred_element_type=F32) * jnp.exp(-z[:, 0:1] * dl)
    hb = jnp.dot(hb, w3_ref[1], preferred_element_type=F32) * jnp.exp(-z[:, LANE:LANE + 1] * dl)
    row = lax.broadcasted_iota(jnp.int32, ha.shape, 0)
    ha = jnp.where(row + (i - half) * ha.shape[0] == 0, 0.0, ha)
    nc = ha.shape[1]
    h2 = DFT_N2 // 2
    for t2 in range(h2):
        k_ref[:, t2 * nc:(t2 + 1) * nc] = ha[t2 * SUBLANE:(t2 + 1) * SUBLANE, :]
        k_ref[:, (h2 + t2) * nc:(h2 + t2 + 1) * nc] = hb[t2 * SUBLANE:(t2 + 1) * SUBLANE, :]
    part = jnp.broadcast_to(jnp.sum(ha * ha, axis=0, keepdims=True)
                            + jnp.sum(hb * hb, axis=0, keepdims=True), ss_ref.shape)

    @pl.when(i == 0)
    def _():
        ss_ref[...] = part

    @pl.when(i > 0)
    def _():
        ss_ref[...] += part


def _hy_filter(seq, z, w1, b1, fr, w2, b2, w3, deltas):
    nc = 2 * D_HY
    tm = SUBLANE * DFT_N2
    steps = 2 * seq // tm
    half = steps // 2
    full = lambda shape: pl.BlockSpec(shape, lambda i: (0, 0))
    return pl.pallas_call(
        functools.partial(_hy_filter_body, half=half),
        grid=(steps,),
        in_specs=[pl.BlockSpec((tm // 2, 2 * LANE), lambda i: (i, 0)),
                  full((2 * LANE, LANE)), full((1, LANE)), full((1, LANE)),
                  full((LANE, LANE)), full((1, LANE)),
                  pl.BlockSpec((None, 2, LANE, nc), lambda i: (i // half, 0, 0, 0)), full((1, nc))],
        out_specs=[pl.BlockSpec((SUBLANE, DFT_N2 * nc), lambda i: (i, 0)), full((SUBLANE, nc))],
        out_shape=[jax.ShapeDtypeStruct((2 * seq // DFT_N2, DFT_N2 * nc), F32),
                   jax.ShapeDtypeStruct((SUBLANE, nc), F32)],
        compiler_params=_cp("arbitrary"),
        name="hyena_filter",
    )(z, w1, b1, fr, w2, b2, w3, deltas)


def _lead_fwd_body(x_ref, f_ref, re_ref, im_ref):
    n1 = re_ref.shape[0]
    r = jnp.dot(f_ref[...], x_ref[...].astype(BF16), preferred_element_type=F32)
    re_ref[...] = r[:n1].astype(BF16)
    im_ref[...] = r[n1:].astype(BF16)


def _lead_fwd(x, table, plane, nb):
    n1, rows = table.shape[0] // 2, table.shape[1]
    cols = x.shape[-1]
    tn = _tile(cols, 4096)
    return pl.pallas_call(
        _lead_fwd_body,
        grid=(nb, cols // tn),
        in_specs=[pl.BlockSpec((None, rows, tn), lambda b, j: (plane, b, j)),
                  pl.BlockSpec((2 * n1, rows), lambda b, j: (0, 0))],
        out_specs=[pl.BlockSpec((None, n1, tn), lambda b, j: (b, 0, j))] * 2,
        out_shape=[jax.ShapeDtypeStruct((nb, n1, cols), BF16)] * 2,
        compiler_params=_cp("arbitrary", "arbitrary"),
        name="hyena_lead_fwd",
    )(x, table)


def _mid_spec_body(are_ref, aim_ref, f_ref, ss_ref, kr_ref, ki_ref, *, bk):
    n2 = DFT_N2
    scale = lax.rsqrt(ss_ref[0:1, :] + EPS)

    def body(i, c):
        a = jnp.concatenate([are_ref[i], aim_ref[i]], axis=0)
        x = jnp.dot(f_ref[i], a, preferred_element_type=F32)
        kr_ref[i] = x[:n2] * scale
        ki_ref[i] = x[n2:] * scale
        return c

    lax.fori_loop(0, bk, body, 0, unroll=True)


def _mid_spec(a_re, a_im, sumsq, tb):
    n1 = tb["n1"]
    bk = _tile(n1, 8)
    a_re = a_re.reshape(n1, DFT_N2, 2 * D_HY)
    a_im = a_im.reshape(n1, DFT_N2, 2 * D_HY)
    a_spec = pl.BlockSpec((bk, DFT_N2, D_HY), lambda o, i: (i, 0, o))
    k_spec = pl.BlockSpec((None, bk, DFT_N2, D_HY), lambda o, i: (o, i, 0, 0))
    return pl.pallas_call(
        functools.partial(_mid_spec_body, bk=bk),
        grid=(2, n1 // bk),
        in_specs=[a_spec, a_spec,
                  pl.BlockSpec((bk, 2 * DFT_N2, 2 * DFT_N2), lambda o, i: (i, 0, 0)),
                  pl.BlockSpec((SUBLANE, D_HY), lambda o, i: (0, o))],
        out_specs=[k_spec, k_spec],
        out_shape=[jax.ShapeDtypeStruct((2, n1, DFT_N2, D_HY), F32)] * 2,
        compiler_params=_cp("arbitrary", "arbitrary"),
        name="hyena_mid_spectrum",
    )(a_re, a_im, tb["mid_f"], sumsq)


def _mid_conv_body(are_ref, aim_ref, f_ref, kr_ref, ki_ref, bre_ref, bim_ref, *, bk):
    n2 = DFT_N2

    def body(i, c):
        a = jnp.concatenate([are_ref[i], aim_ref[i]], axis=0)
        x = jnp.dot(f_ref[i], a, preferred_element_type=F32)
        xr = x[:n2]
        xi = x[n2:]
        kr = kr_ref[i]
        ki = ki_ref[i]
        p = jnp.concatenate([xr * kr - xi * ki, xr * ki + xi * kr], axis=0).astype(BF16)
        q = lax.dot_general(f_ref[i], p, (((0,), (0,)), ((), ())), preferred_element_type=F32)
        bre_ref[i] = q[:n2].astype(BF16)
        bim_ref[i] = q[n2:].astype(BF16)
        return c

    lax.fori_loop(0, bk, body, 0, unroll=True)


def _mid_conv(a_re, a_im, kf_re, kf_im, order, tb):
    n1 = tb["n1"]
    nb = a_re.shape[0]
    bk = _tile(n1, 8)
    a_re = a_re.reshape(nb, n1, DFT_N2, D_HY)
    a_im = a_im.reshape(nb, n1, DFT_N2, D_HY)
    a_spec = pl.BlockSpec((None, bk, DFT_N2, D_HY), lambda b, i: (b, i, 0, 0))
    t_spec = pl.BlockSpec((bk, 2 * DFT_N2, 2 * DFT_N2), lambda b, i: (i, 0, 0))
    k_spec = pl.BlockSpec((None, bk, DFT_N2, D_HY), lambda b, i: (order, i, 0, 0))
    b_re, b_im = pl.pallas_call(
        functools.partial(_mid_conv_body, bk=bk),
        grid=(nb, n1 // bk),
        in_specs=[a_spec, a_spec, t_spec, k_spec, k_spec],
        out_specs=[a_spec, a_spec],
        out_shape=[jax.ShapeDtypeStruct((nb, n1, DFT_N2, D_HY), BF16)] * 2,
        compiler_params=_cp("arbitrary", "arbitrary"),
        name="hyena_mid_conv",
    )(a_re, a_im, tb["mid_f"], kf_re, kf_im)
    return b_re.reshape(nb, n1, DFT_N2 * D_HY), b_im.reshape(nb, n1, DFT_N2 * D_HY)


def _lead_inv_body(bre_ref, bim_ref, c_ref, s_ref, z_ref, gate_ref, skip_ref, gn_ref, o_ref, *, inv_n, last):
    y = (jnp.dot(c_ref[...], bre_ref[...], preferred_element_type=F32)
         + jnp.dot(s_ref[...], bim_ref[...], preferred_element_type=F32)) * inv_n
    out = gate_ref[...].astype(F32) * (y + skip_ref[...] * z_ref[...].astype(F32))
    if not last:
        o_ref[...] = out
    else:
        gn = gn_ref[...]
        for c in range(out.shape[1] // D_HY):
            blk = out[:, c * D_HY:(c + 1) * D_HY]
            ms = jnp.mean(blk * blk, axis=-1, keepdims=True)
            o_ref[:, c * D_HY:(c + 1) * D_HY] = (blk * lax.rsqrt(ms + EPS) * gn).astype(o_ref.dtype)


def _lead_inv(b_re, b_im, z, z_plane, gates, gate_plane, skip, gn, tb, last):
    n1, n1h = tb["n1"], tb["n1h"]
    nb, _, cols = b_re.shape
    tn = _tile(cols, 4096)
    skip_t = jnp.tile(skip[None, :], (1, tn // D_HY))
    b_spec = pl.BlockSpec((None, n1, tn), lambda b, j: (b, 0, j))
    t_spec = pl.BlockSpec((n1h, n1), lambda b, j: (0, 0))
    return pl.pallas_call(
        functools.partial(_lead_inv_body, inv_n=1.0 / tb["n"], last=last),
        grid=(nb, cols // tn),
        in_specs=[b_spec, b_spec, t_spec, t_spec,
                  pl.BlockSpec((None, n1h, tn), lambda b, j: (z_plane, b, j)),
                  pl.BlockSpec((None, n1h, tn), lambda b, j: (gate_plane, b, j)),
                  pl.BlockSpec((1, tn), lambda b, j: (0, 0)),
                  pl.BlockSpec((1, D_HY), lambda b, j: (0, 0))],
        out_specs=pl.BlockSpec((None, n1h, tn), lambda b, j: (0, b, j)),
        out_shape=jax.ShapeDtypeStruct((1, nb * n1h, cols), BF16 if last else F32),
        compiler_params=_cp("arbitrary", "arbitrary"),
        name="hyena_lead_inv",
    )(b_re, b_im, tb["lead_ic"], tb["lead_is"], z, gates, skip_t, gn)


def _hyena(conv_out, nb, seq, tb, kf_re, kf_im, skip, gn):
    n1h = tb["n1h"]
    planes = conv_out.reshape(conv_out.shape[0], nb * n1h, DFT_N2 * D_HY)
    z, z_plane = planes, 0
    for order in range(2):
        a_re, a_im = _lead_fwd(z, tb["lead_f"], z_plane, nb)
        b_re, b_im = _mid_conv(a_re, a_im, kf_re, kf_im, order, tb)
        z = _lead_inv(b_re, b_im, z, z_plane, planes, 1 + order, skip[order], gn, tb, order == 1)
        z_plane = 0
    return z.reshape(nb * seq, D_HY)


def _hyena_spectra(seq, tb, w1, b1, fr, w2, b2, w3):
    w1 = jnp.pad(w1, ((0, LANE - HY_EMB), (0, 0)))
    zero1 = jnp.zeros_like(w1)
    w1 = jnp.concatenate([jnp.concatenate([w1, zero1], axis=1),
                          jnp.concatenate([zero1, w1], axis=1)], axis=0)
    zero2 = jnp.zeros_like(w2)
    w2 = jnp.concatenate([jnp.concatenate([w2, zero2], axis=1),
                          jnp.concatenate([zero2, w2], axis=1)], axis=0)
    pair = lambda v: jnp.concatenate([v, v])[None, :]
    w3 = w3.reshape(HY_FILT, 2, 2, D_HY).transpose(2, 0, 1, 3).reshape(2, HY_FILT, 2 * D_HY)
    zero3 = jnp.zeros_like(w3)
    w3 = jnp.stack([jnp.concatenate([w3, zero3], axis=1),
                    jnp.concatenate([zero3, w3], axis=1)], axis=1).astype(BF16)
    taps, sumsq = _hy_filter(seq, _hy_features(seq), w1, pair(b1), pair(fr), w2, pair(b2), w3,
                             _hy_deltas())
    a_re, a_im = _lead_fwd(taps[None], tb["lead_ff"], 0, 1)
    return _mid_spec(a_re[0], a_im[0], sumsq, tb)


def _s5_scan_body(u_ref, pm_ref, bd_ref, a_ref, cd_ref, x0_ref, o_ref, bu_ref, st_ref, y_ref,
                  *, tstep, reverse, final_state):
    tb = pl.program_id(0)
    ns = S5_STREAMS
    half = S5_STATE
    cw = S5_BLOCK_GROUPS * S5_P

    @pl.when(tb == 0)
    def _():
        st_ref[...] = x0_ref[...]

    u = u_ref[...].reshape(ns * tstep, D_S5).astype(BF16)
    u_tm = jnp.dot(pm_ref[...], u, preferred_element_type=F32).astype(BF16)

    def project_in(blk):
        bu_ref[:, 2 * blk * cw:2 * (blk + 1) * cw] = jnp.dot(
            u_tm[:, blk * LANE:(blk + 1) * LANE], bd_ref[blk], preferred_element_type=F32)

    def scan(blk):
        bre = slice(2 * blk * cw, (2 * blk + 1) * cw)
        bim = slice((2 * blk + 1) * cw, (2 * blk + 2) * cw)
        sre = slice(blk * cw, (blk + 1) * cw)
        sim = slice(half + blk * cw, half + (blk + 1) * cw)
        ar = a_ref[:, sre]
        ai = a_ref[:, sim]
        xr = st_ref[:, sre]
        xi = st_ref[:, sim]
        for i in range(tstep):
            t = tstep - 1 - i if reverse else i
            rows = slice(t * ns, (t + 1) * ns)
            xr, xi = (ar * xr - ai * xi + bu_ref[rows, bre],
                      ar * xi + ai * xr + bu_ref[rows, bim])
            if not final_state:
                bu_ref[rows, bre] = xr
                bu_ref[rows, bim] = xi
        st_ref[:, sre] = xr
        st_ref[:, sim] = xi

    def project_out(blk):
        y_ref[blk] = jnp.dot(bu_ref[:, 2 * blk * cw:2 * (blk + 1) * cw].astype(BF16), cd_ref[blk],
                             preferred_element_type=F32)

    project_in(0)
    for blk in range(S5_NBLOCKS):
        if blk + 1 < S5_NBLOCKS:
            project_in(blk + 1)
        scan(blk)
        if not final_state:
            project_out(blk)

    if final_state:
        @pl.when(tb == pl.num_programs(0) - 1)
        def _():
            o_ref[...] = st_ref[...]
    else:
        for s in range(ns):
            for blk in range(S5_NBLOCKS):
                o_ref[s, :, blk * LANE:(blk + 1) * LANE] = y_ref[blk, pl.ds(s, tstep, stride=ns), :]


def _s5_scan(proj, pmat, bd, a_b, cd, x0, tlen, reverse, final_state):
    ns = S5_STREAMS
    d = 1 if reverse else 0
    tstep = pmat.shape[0] // ns
    nt = tlen // tstep
    proj3 = proj.reshape(ns, tlen, proj.shape[1])
    blk_in = S5_BLOCK_GROUPS * S5_GROUP
    blk_state = 2 * S5_BLOCK_GROUPS * S5_P

    def window(t):
        return nt - 1 - t if reverse else t

    if final_state:
        out_spec = pl.BlockSpec((ns, 2 * S5_STATE), lambda t: (0, 0))
        out_shape = jax.ShapeDtypeStruct((ns, 2 * S5_STATE), F32)
    else:
        out_spec = pl.BlockSpec((ns, tstep, D_S5), lambda t: (0, window(t), 0))
        out_shape = jax.ShapeDtypeStruct((ns, tlen, D_S5), F32)
    out = pl.pallas_call(
        functools.partial(_s5_scan_body, tstep=tstep, reverse=reverse, final_state=final_state),
        grid=(nt,),
        in_specs=[pl.BlockSpec((ns, tstep, D_S5), lambda t: (0, window(t), COL_U)),
                  pl.BlockSpec((ns * tstep, ns * tstep), lambda t: (0, 0)),
                  pl.BlockSpec((None, S5_NBLOCKS, blk_in, blk_state), lambda t: (d, 0, 0, 0)),
                  pl.BlockSpec((None, ns, 2 * S5_STATE), lambda t: (d, 0, 0)),
                  pl.BlockSpec((None, S5_NBLOCKS, blk_state, blk_in), lambda t: (d, 0, 0, 0)),
                  pl.BlockSpec((None, ns, 2 * S5_STATE), lambda t: (d, 0, 0))],
        out_specs=out_spec,
        out_shape=out_shape,
        scratch_shapes=[pltpu.VMEM((tstep * ns, 2 * S5_STATE), F32),
                        pltpu.VMEM((ns, 2 * S5_STATE), F32),
                        pltpu.VMEM((D_S5 // LANE, tstep * ns, LANE), F32)],
        compiler_params=_cp("arbitrary"),
        name="s5_end_state" if final_state else "s5_scan",
    )(proj3, pmat, bd, a_b, cd, x0)
    return out if final_state else out.reshape(ns * tlen, D_S5)


def _s5_params(lam_re, lam_im, log_dt, b_re, b_im, c_re, c_im):
    lr = jnp.minimum(lam_re, -1e-4)
    li = lam_im
    dt = jnp.exp(log_dt)[..., None]
    er = jnp.exp(lr * dt)
    a_re = er * jnp.cos(li * dt)
    a_im = er * jnp.sin(li * dt)
    den = lr * lr + li * li
    q_re = ((a_re - 1.0) * lr + a_im * li) / den
    q_im = (a_im * lr - (a_re - 1.0) * li) / den
    bb_re = q_re[..., None] * b_re - q_im[..., None] * b_im
    bb_im = q_re[..., None] * b_im + q_im[..., None] * b_re
    nbk, gb = S5_NBLOCKS, S5_BLOCK_GROUPS
    eye = jnp.eye(gb, dtype=F32)

    def in_mat(x):
        x = x.reshape(2, nbk, gb, S5_P, S5_GROUP)
        return jnp.einsum("dbgph,gk->dbghkp", x, eye).reshape(2, nbk, gb * S5_GROUP, gb * S5_P)

    def out_mat(x):
        x = x.reshape(2, nbk, gb, S5_GROUP, S5_P)
        return jnp.einsum("dbghp,gk->dbgpkh", x, eye).reshape(2, nbk, gb * S5_P, gb * S5_GROUP)

    bd = jnp.concatenate([in_mat(bb_re), in_mat(bb_im)], axis=3).astype(BF16)
    cd = jnp.concatenate([out_mat(c_re), out_mat(-c_im)], axis=2).astype(BF16)
    a_bar = jnp.concatenate([a_re.reshape(2, S5_STATE), a_im.reshape(2, S5_STATE)], axis=1)
    return (lr * dt).reshape(2, S5_STATE), (li * dt).reshape(2, S5_STATE), a_bar, bd, cd


def _s5_row_perm(tstep):
    ns = S5_STREAMS
    r = jnp.arange(ns * tstep)
    src = (r % ns) * tstep + r // ns
    return (src[:, None] == jnp.arange(ns * tstep)[None, :]).astype(BF16)


def _s5_mixer(proj, nb, seq, s5p):
    la_re, la_im, a_bar, bd, cd = s5p
    ns = S5_STREAMS
    ncs = ns // nb
    tlen = seq // ncs
    pmat = _s5_row_perm(_tile(tlen, S5_TSTEP))
    a_b = jnp.broadcast_to(a_bar[:, None, :], (2, ns, 2 * S5_STATE))
    zeros = jnp.zeros((2, ns, 2 * S5_STATE), F32)
    mag = jnp.exp(la_re * tlen)
    at_re = (mag * jnp.cos(la_im * tlen))[:, None, :]
    at_im = (mag * jnp.sin(la_im * tlen))[:, None, :]
    ys = []
    for d in range(2):
        x_end = _s5_scan(proj, pmat, bd, a_b, cd, zeros, tlen, d == 1, True)
        xe_re = x_end[:, :S5_STATE].reshape(nb, ncs, S5_STATE)
        xe_im = x_end[:, S5_STATE:].reshape(nb, ncs, S5_STATE)
        order = range(ncs) if d == 0 else range(ncs - 1, -1, -1)
        cr = jnp.zeros((nb, S5_STATE), F32)
        ci = jnp.zeros((nb, S5_STATE), F32)
        rows = [None] * ncs
        for c in order:
            rows[c] = jnp.concatenate([cr, ci], axis=-1)
            cr, ci = (at_re[d] * cr - at_im[d] * ci + xe_re[:, c],
                      at_re[d] * ci + at_im[d] * cr + xe_im[:, c])
        x0 = jnp.stack(rows, axis=1).reshape(ns, 2 * S5_STATE)
        x0 = jnp.stack([x0, x0])
        ys.append(_s5_scan(proj, pmat, bd, a_b, cd, x0, tlen, d == 1, False))
    return ys


def _s5_post_body(yf_ref, yb_ref, u_ref, d_ref, w_ref, b_ref, gn_ref, o_ref):
    y = yf_ref[...] + yb_ref[...] + d_ref[...] * u_ref[...].astype(F32)
    g = jax.nn.gelu(y)
    r = jnp.dot(g.astype(BF16), w_ref[...], preferred_element_type=F32) + b_ref[...]
    out = r[:, :D_S5] * jax.nn.sigmoid(r[:, D_S5:])
    ms = jnp.mean(out * out, axis=-1, keepdims=True)
    o_ref[...] = (out * lax.rsqrt(ms + EPS) * gn_ref[...]).astype(o_ref.dtype)


def _s5_post(y_f, y_b, proj, d, glu_w, glu_b, gn):
    m = y_f.shape[0]
    tm = _tile(m, 512)
    full = lambda shape: pl.BlockSpec(shape, lambda i: (0, 0))
    return pl.pallas_call(
        _s5_post_body,
        grid=(m // tm,),
        in_specs=[pl.BlockSpec((tm, D_S5), lambda i: (i, 0)),
                  pl.BlockSpec((tm, D_S5), lambda i: (i, 0)),
                  pl.BlockSpec((tm, D_S5), lambda i: (i, COL_U)),
                  full((1, D_S5)), full((D_S5, 2 * D_S5)), full((1, 2 * D_S5)), full((1, D_S5))],
        out_specs=pl.BlockSpec((tm, D_S5), lambda i: (i, 0)),
        out_shape=jax.ShapeDtypeStruct((m, D_S5), BF16),
        compiler_params=_cp("arbitrary"),
        name="s5_glu_norm",
    )(y_f, y_b, proj, d, glu_w, glu_b, gn)


def _split3(x):
    hi = x.astype(BF16)
    rest = x - hi.astype(F32)
    mid = rest.astype(BF16)
    return hi, mid, (rest - mid.astype(F32)).astype(BF16)


def _mlstm_body(*refs, reverse, epilogue):
    if epilogue:
        (q_ref, k_ref, v_ref, g_ref, gt_ref, bias_ref, biast_ref, seen_ref, upto_ref, hf_ref, o_gate_ref,
         gn_ref, out_ref, c_ref, n_ref, m_ref) = refs
    else:
        (q_ref, k_ref, v_ref, g_ref, gt_ref, bias_ref, biast_ref, seen_ref, upto_ref,
         out_ref, c_ref, n_ref, m_ref) = refs
        hf_ref = o_gate_ref = gn_ref = None

    @pl.when(pl.program_id(2) == 0)
    def _():
        c_ref[...] = jnp.zeros_like(c_ref)
        n_ref[...] = jnp.zeros_like(n_ref)
        m_ref[...] = jnp.zeros_like(m_ref)

    tc = ML_CHUNK
    r = lax.broadcasted_iota(jnp.int32, (tc, tc), 0)
    s = lax.broadcasted_iota(jnp.int32, (tc, tc), 1)
    seen = (s >= r) if reverse else (s <= r)
    g = g_ref[...] + bias_ref[...]
    gt = gt_ref[...] + biast_ref[:, 0:1]
    for hh in range(ML_HEADS_PER_STEP):
        _mlstm_head(hh, g, gt, seen, q_ref, k_ref, v_ref, seen_ref, upto_ref, hf_ref, o_gate_ref,
                    gn_ref, out_ref, c_ref, n_ref, m_ref, reverse=reverse)


def _mlstm_head(hh, g, gt, seen, q_ref, k_ref, v_ref, seen_ref, upto_ref, hf_ref, o_gate_ref, gn_ref,
                out_ref, c_ref, n_ref, m_ref, *, reverse):
    head = pl.program_id(1) * ML_HEADS_PER_STEP + hh
    cols = slice(hh * ML_DH, (hh + 1) * ML_DH)
    tc = ML_CHUNK
    gate_i = 2 if reverse else 0
    idx_i = gate_i * ML_HEADS + head
    idx_f = idx_i + ML_HEADS
    tiled = lambda x, n: x if n == LANE else jnp.concatenate([x] * (n // LANE), axis=1)
    wide = lambda x: tiled(x, ML_DH)
    span = lambda x: tiled(x, tc)
    dot = functools.partial(jnp.dot, preferred_element_type=F32)

    lane = lax.broadcasted_iota(jnp.int32, g.shape, 1)
    ig = jnp.broadcast_to(jnp.sum(jnp.where(lane == idx_i, g, 0.0), axis=1, keepdims=True), g.shape)
    fg = jnp.broadcast_to(jnp.sum(jnp.where(lane == idx_f, g, 0.0), axis=1, keepdims=True), g.shape)
    sub = lax.broadcasted_iota(jnp.int32, gt.shape, 0)
    ig_row = jnp.sum(jnp.where(sub == idx_i, gt, 0.0), axis=0, keepdims=True)
    fg_row = jnp.sum(jnp.where(sub == idx_f, gt, 0.0), axis=0, keepdims=True)
    lf = jax.nn.log_sigmoid(fg)
    lf_row = jax.nn.log_sigmoid(fg_row)

    seen_b = seen_ref[...]
    upto_b = upto_ref[...]
    b_col = sum(dot(seen_b, piece) for piece in _split3(lf))
    lf_rows = jnp.broadcast_to(lf_row, (SUBLANE, tc))
    b_row = sum(dot(piece, upto_b) for piece in _split3(lf_rows))[0:1]
    b_last = jnp.sum(lf_row, axis=1, keepdims=True)

    m_prev = m_ref[hh:hh + 1, 0:1]
    src = ig_row - b_row
    dmat = jnp.where(seen, span(b_col) + src, -jnp.inf)
    g_car = b_col + m_prev
    mt = jnp.maximum(g_car, jnp.max(dmat, axis=1, keepdims=True))
    inter = jnp.exp(g_car - mt)
    qb = q_ref[:, cols]
    kb = k_ref[:, cols]
    vb = v_ref[:, cols].astype(BF16)
    ones = jnp.ones((tc, LANE), BF16)
    qk = lax.dot_general(qb, kb, (((1,), (1,)), ((), ())), preferred_element_type=F32)
    sc = (qk * jnp.exp(dmat - span(mt))).astype(BF16)
    c_state = c_ref[hh]
    n_state = n_ref[hh]
    num = wide(inter) * dot(qb, c_state.astype(BF16)) + dot(sc, vb)
    den = inter * dot(qb, n_state.astype(BF16)) + dot(sc, ones)
    h = num * wide(1.0 / jnp.maximum(jnp.abs(den), jnp.exp(-mt)))

    a_row = b_last + src
    a_col = b_last - b_col + ig
    m_new = jnp.maximum(b_last + m_prev, jnp.max(a_row, axis=1, keepdims=True))
    decay = jnp.exp(b_last + m_prev - m_new)
    kw = (kb.astype(F32) * wide(jnp.exp(a_col - m_new))).astype(BF16)
    upd = lax.dot_general(kw, jnp.concatenate([vb, ones], axis=1), (((0,), (0,)), ((), ())),
                          preferred_element_type=F32)
    c_ref[hh] = decay * c_state + upd[:, :ML_DH]
    n_ref[hh] = decay * n_state + upd[:, ML_DH:]
    m_ref[hh:hh + 1, 0:1] = m_new

    if hf_ref is not None:
        h = h + hf_ref[:, cols]
        ms = jnp.mean(h * h, axis=-1, keepdims=True)
        hn = h * lax.rsqrt(ms + EPS) * gn_ref[:, cols]
        out_ref[:, cols] = (hn * jax.nn.sigmoid(o_gate_ref[:, cols].astype(F32))).astype(out_ref.dtype)
    else:
        out_ref[:, cols] = h


def _mlstm_dir(qk_planes, proj, gates, gates_t, bias, bias_t, nb, seq, reverse, h_fwd=None, gn=None):
    tc = ML_CHUNK
    nch = seq // tc
    hps = ML_HEADS_PER_STEP
    width = hps * ML_DH
    col_v = (N_CONV + D_S5) // width
    col_o = (N_CONV + D_S5 + D_ML) // width

    def chunk(b, ci):
        return b * nch + (nch - 1 - ci if reverse else ci)

    in_specs = [
        pl.BlockSpec((None, tc, width), lambda b, h, ci: (0, chunk(b, ci), h)),
        pl.BlockSpec((None, tc, width), lambda b, h, ci: (1, chunk(b, ci), h)),
        pl.BlockSpec((tc, width), lambda b, h, ci: (chunk(b, ci), col_v + h)),
        pl.BlockSpec((tc, LANE), lambda b, h, ci: (chunk(b, ci), 0)),
        pl.BlockSpec((4 * ML_HEADS, tc), lambda b, h, ci: (0, chunk(b, ci))),
        pl.BlockSpec((1, LANE), lambda b, h, ci: (0, 0)),
        pl.BlockSpec((4 * ML_HEADS, LANE), lambda b, h, ci: (0, 0)),
        pl.BlockSpec((tc, tc), lambda b, h, ci: (0, 0)),
        pl.BlockSpec((tc, tc), lambda b, h, ci: (0, 0)),
    ]
    pos = jnp.arange(tc)
    seen = (pos[None, :] >= pos[:, None]) if reverse else (pos[None, :] <= pos[:, None])
    seen = seen.astype(BF16)
    args = [qk_planes, qk_planes, proj, gates, gates_t, bias, bias_t, seen, seen.T]
    epilogue = h_fwd is not None
    if epilogue:
        in_specs += [
            pl.BlockSpec((tc, width), lambda b, h, ci: (chunk(b, ci), h)),
            pl.BlockSpec((tc, width), lambda b, h, ci: (chunk(b, ci), col_o + h)),
            pl.BlockSpec((1, width), lambda b, h, ci: (0, h)),
        ]
        args += [h_fwd, proj, gn]
    return pl.pallas_call(
        functools.partial(_mlstm_body, reverse=reverse, epilogue=epilogue),
        grid=(nb, ML_HEADS // hps, nch),
        in_specs=in_specs,
        out_specs=pl.BlockSpec((tc, width), lambda b, h, ci: (chunk(b, ci), h)),
        out_shape=jax.ShapeDtypeStruct((nb * seq, D_ML), BF16 if epilogue else F32),
        scratch_shapes=[pltpu.VMEM((hps, ML_DH, ML_DH), F32), pltpu.VMEM((hps, ML_DH, LANE), F32),
                        pltpu.VMEM((SUBLANE, LANE), F32)],
        compiler_params=_cp("arbitrary", "arbitrary", "arbitrary"),
        name="mlstm_bwd_norm" if reverse else "mlstm_fwd",
    )(*args)


def _layer(x, nb, seq, tb, p):
    proj, gates = _proj_in(x, p["ln_g"][0:1], p["w_in"], p["w_gate"])
    gates_t = gates[:, :4 * ML_HEADS].T
    gn = p["group_norm"]
    gn_hy = gn[None, :D_HY]
    gn_s5 = gn[None, D_HY:D_HY + D_S5]
    gn_ml = gn[None, D_HY + D_S5:]

    hy_planes = _short_conv(proj, p["conv_w"], p["conv_b"][None, :], nb, seq, False)
    qk_planes = _short_conv(proj, p["conv_w"], p["conv_b"][None, :], nb, seq, True)
    kf_re, kf_im = _hyena_spectra(seq, tb, p["hy_w1"], p["hy_b1"], p["hy_freq"], p["hy_w2"],
                                  p["hy_b2"], p["hy_w3"])
    y_hy = _hyena(hy_planes, nb, seq, tb, kf_re, kf_im, p["hy_skip"], gn_hy)
    y_s5 = _s5_post(*_s5_mixer(proj, nb, seq, p["s5"]), proj, p["s5_d"][None, :],
                    p["s5_glu_w"], p["s5_glu_b"][None, :], gn_s5)
    ml_args = (qk_planes, proj, gates, gates_t, p["ml_bias"], p["ml_bias_t"], nb, seq)
    h_f = _mlstm_dir(*ml_args, False)
    y_ml = _mlstm_dir(*ml_args, True, h_f, gn_ml)

    w_out = p["w_out"]
    x = _proj_out([y_hy, y_s5, y_ml], [w_out[:D_HY], w_out[D_HY:D_HY + D_S5], w_out[D_HY + D_S5:]],
                  x, p["ln_g"][1:2])
    return _mlp(x, p["ln_g"][2:3], p["mlp_w1"], p["mlp_w2"], p["ln_g"][3:4])


_PARAM_NAMES = ("ln_g", "w_in", "conv_w", "conv_b", "hy_w1", "hy_b1", "hy_freq", "hy_w2", "hy_b2",
                "hy_w3", "hy_skip", "s5_lam_re", "s5_lam_im", "s5_log_dt", "s5_b_re", "s5_b_im",
                "s5_c_re", "s5_c_im", "s5_d", "s5_glu_w", "s5_glu_b", "ml_gate_b", "group_norm",
                "w_out", "mlp_w1", "mlp_w2")


def _prepare(p):
    p = dict(p)
    w_in = p["w_in"].astype(BF16)
    p["w_in"] = w_in[:, :P_MIX]
    p["w_gate"] = jnp.pad(w_in[:, P_MIX:], ((0, 0), (0, LANE - 4 * ML_HEADS)))
    p["w_out"] = p["w_out"].astype(BF16)
    p["mlp_w1"] = p["mlp_w1"].astype(BF16)
    p["mlp_w2"] = p["mlp_w2"].astype(BF16)
    p["s5_glu_w"] = p["s5_glu_w"].astype(BF16)
    p["ml_bias"] = jnp.pad(p["ml_gate_b"].reshape(1, -1), ((0, 0), (0, LANE - 4 * ML_HEADS)))
    p["ml_bias_t"] = jnp.broadcast_to(p["ml_gate_b"].reshape(-1, 1), (4 * ML_HEADS, LANE))
    p["s5"] = _s5_params(p["s5_lam_re"], p["s5_lam_im"], p["s5_log_dt"], p["s5_b_re"],
                         p["s5_b_im"], p["s5_c_re"], p["s5_c_im"])
    return p


def kernel(x_prompt, x_sample, ln_g, w_in, conv_w, conv_b, hy_w1, hy_b1, hy_freq, hy_w2, hy_b2, hy_w3,
           hy_skip, s5_lam_re, s5_lam_im, s5_log_dt, s5_b_re, s5_b_im, s5_c_re, s5_c_im, s5_d,
           s5_glu_w, s5_glu_b, ml_gate_b, group_norm, w_out, mlp_w1, mlp_w2):
    params = (ln_g, w_in, conv_w, conv_b, hy_w1, hy_b1, hy_freq, hy_w2, hy_b2, hy_w3, hy_skip,
              s5_lam_re, s5_lam_im, s5_log_dt, s5_b_re, s5_b_im, s5_c_re, s5_c_im, s5_d,
              s5_glu_w, s5_glu_b, ml_gate_b, group_norm, w_out, mlp_w1, mlp_w2)
    streams = []
    for x in (x_prompt, x_sample):
        nb, seq, _ = x.shape
        streams.append([x.reshape(nb * seq, D_MODEL), nb, seq, _dft_tables(seq)])
    for layer in range(DEPTH):
        p = _prepare({name: arr[layer] for name, arr in zip(_PARAM_NAMES, params)})
        for st in streams:
            st[0] = _layer(st[0], st[1], st[2], st[3], p)
    return tuple(st[0].reshape(x.shape) for st, x in zip(streams, (x_prompt, x_sample)))
```

```python
import functools
import math

import jax
import jax.numpy as jnp
from jax import lax
from jax.experimental import pallas as pl
from jax.experimental.pallas import tpu as pltpu

F32 = jnp.float32
BF16 = jnp.bfloat16

D_MODEL = 2048
DEPTH = 4
D_HY = 512
D_S5 = 512
D_ML = 1024
HY_EMB = 33
HY_BANDS = 16
HY_FILT = 64
HY_FAST_DECAY = 0.3
HY_SLOW_DECAY = 1.5
HY_TARGET = 1e-2
S5_GROUP = 16
S5_G = 32
S5_P = 64
S5_STATE = S5_G * S5_P
ML_HEADS = 4
ML_DH = 256
D_FF = 4 * D_MODEL
N_CONV = 3 * D_HY + 2 * D_ML
P_IN = N_CONV + D_S5 + 2 * D_ML + 4 * ML_HEADS
P_MIX = P_IN - 4 * ML_HEADS
EPS = 1e-6

LANE = 128
SUBLANE = 8
VMEM_LIMIT_BYTES = 48 * 1024 * 1024

DFT_N2 = LANE
S5_STREAMS = SUBLANE
S5_TSTEP = 64
S5_BLOCK_GROUPS = LANE // S5_GROUP
S5_NBLOCKS = S5_G // S5_BLOCK_GROUPS
ML_CHUNK = 256
ML_HEADS_PER_STEP = 4
COL_U = N_CONV // D_S5


def _cp(*sem):
    return pltpu.CompilerParams(dimension_semantics=sem, vmem_limit_bytes=VMEM_LIMIT_BYTES)


def _tile(n, pref):
    t = min(n, pref)
    while n % t:
        t //= 2
    return t


def _rms(x, g):
    ms = jnp.mean(x * x, axis=-1, keepdims=True)
    return x * lax.rsqrt(ms + EPS) * g


def _proj_in_body(x_ref, g_ref, w_ref, wg_ref, o_ref, og_ref, xn_ref):
    @pl.when(pl.program_id(1) == 0)
    def _():
        xn = _rms(x_ref[...], g_ref[...]).astype(BF16)
        xn_ref[...] = xn
        og_ref[...] = jnp.dot(xn, wg_ref[...], preferred_element_type=F32)

    o_ref[...] = jnp.dot(xn_ref[...], w_ref[...], preferred_element_type=F32).astype(o_ref.dtype)


def _proj_in(x, g, w, w_gate):
    m, k = x.shape
    n = w.shape[1]
    tm = _tile(m, 1024)
    tn = 1024
    return pl.pallas_call(
        _proj_in_body,
        grid=(m // tm, n // tn),
        in_specs=[pl.BlockSpec((tm, k), lambda i, j: (i, 0)),
                  pl.BlockSpec((1, k), lambda i, j: (0, 0)),
                  pl.BlockSpec((k, tn), lambda i, j: (0, j)),
                  pl.BlockSpec((k, LANE), lambda i, j: (0, 0))],
        out_specs=[pl.BlockSpec((tm, tn), lambda i, j: (i, j)),
                   pl.BlockSpec((tm, LANE), lambda i, j: (i, 0))],
        out_shape=[jax.ShapeDtypeStruct((m, n), BF16), jax.ShapeDtypeStruct((m, LANE), F32)],
        scratch_shapes=[pltpu.VMEM((tm, k), BF16)],
        compiler_params=_cp("arbitrary", "arbitrary"),
        name="norm_proj_in",
    )(x, g, w, w_gate)


def _proj_out_body(a0_ref, a1_ref, a2_ref, w0_ref, w1_ref, w2_ref, r_ref, g_ref, o_ref):
    half = o_ref.shape[0] // 2
    for rows in (slice(0, half), slice(half, 2 * half)):
        f = (jnp.dot(a0_ref[rows, :].astype(BF16), w0_ref[...], preferred_element_type=F32)
             + jnp.dot(a1_ref[rows, :].astype(BF16), w1_ref[...], preferred_element_type=F32)
             + jnp.dot(a2_ref[rows, :].astype(BF16), w2_ref[...], preferred_element_type=F32))
        o_ref[rows, :] = r_ref[rows, :] + _rms(f, g_ref[...])


def _proj_out(parts, weights, resid, g):
    m, n = resid.shape
    tm = _tile(m, 512)
    row = lambda width: pl.BlockSpec((tm, width), lambda i: (i, 0))
    full = lambda arr: pl.BlockSpec(arr.shape, lambda i: (0, 0))
    return pl.pallas_call(
        _proj_out_body,
        grid=(m // tm,),
        in_specs=[row(a.shape[1]) for a in parts] + [full(w) for w in weights] + [row(n), full(g)],
        out_specs=row(n),
        out_shape=jax.ShapeDtypeStruct((m, n), F32),
        compiler_params=_cp("arbitrary"),
        name="proj_out_resnorm",
    )(*parts, *weights, resid, g)


def _mlp_body(x_ref, g_in_ref, w1_ref, w2_ref, g_out_ref, o_ref, xn_ref, acc_ref):
    j = pl.program_id(1)

    def up_down(xn):
        h = jnp.dot(xn, w1_ref[...], preferred_element_type=F32)
        h = jnp.square(jnp.maximum(h, 0.0)).astype(BF16)
        return jnp.dot(h, w2_ref[...], preferred_element_type=F32)

    @pl.when(j == 0)
    def _():
        xn = _rms(x_ref[...], g_in_ref[...]).astype(BF16)
        xn_ref[...] = xn
        acc_ref[...] = up_down(xn)

    @pl.when(j > 0)
    def _():
        acc_ref[...] += up_down(xn_ref[...])

    @pl.when(j == pl.num_programs(1) - 1)
    def _():
        o_ref[...] = x_ref[...] + _rms(acc_ref[...], g_out_ref[...])


def _mlp(x, g_in, w1, w2, g_out):
    m, k = x.shape
    ff = w1.shape[1]
    tm = _tile(m, 512)
    tf = 1024
    return pl.pallas_call(
        _mlp_body,
        grid=(m // tm, ff // tf),
        in_specs=[pl.BlockSpec((tm, k), lambda i, j: (i, 0)),
                  pl.BlockSpec((1, k), lambda i, j: (0, 0)),
                  pl.BlockSpec((k, tf), lambda i, j: (0, j)),
                  pl.BlockSpec((tf, k), lambda i, j: (j, 0)),
                  pl.BlockSpec((1, k), lambda i, j: (0, 0))],
        out_specs=pl.BlockSpec((tm, k), lambda i, j: (i, 0)),
        out_shape=jax.ShapeDtypeStruct((m, k), F32),
        scratch_shapes=[pltpu.VMEM((tm, k), BF16), pltpu.VMEM((tm, k), F32)],
        compiler_params=_cp("arbitrary", "arbitrary"),
        name="mlp_resnorm",
    )(x, g_in, w1, w2, g_out)


def _conv_body(x_ref, p_ref, n_ref, w_ref, b_ref, o_ref, *, nblk, qk):
    i = pl.program_id(1)
    j = pl.program_id(2)
    x = x_ref[...].astype(F32)
    tm = x.shape[0]
    row = lax.broadcasted_iota(jnp.int32, x.shape, 0)
    halo = p_ref.shape[0]
    prev_row = jnp.where(i == 0, 0.0, p_ref[halo - 1:halo, :].astype(F32))
    next_row = jnp.where(i == nblk - 1, 0.0, n_ref[0:1, :].astype(F32))
    xm = jnp.where(row == 0, prev_row, pltpu.roll(x, 1, 0))
    xp = jnp.where(row == tm - 1, next_row, pltpu.roll(x, tm - 1, 0))
    w = w_ref[...]
    y = b_ref[...] + xm * w[0:1] + x * w[1:2] + xp * w[2:3]
    if qk:
        y = y * jax.nn.sigmoid(y) * jnp.where(j >= 2, ML_DH ** -0.5, 1.0)
    o_ref[...] = y.astype(o_ref.dtype)


def _short_conv(proj, conv_w, conv_b, nb, seq, qk):
    mtot = proj.shape[0]
    tm = _tile(seq, 2048)
    nblk = seq // tm
    col0, ncol = (3, 4) if qk else (0, 3)
    ppo = 2 if qk else 1
    halo = 2 * SUBLANE
    last_halo = mtot // halo - 1
    tmh = tm // halo

    def x_map(b, i, j):
        return (b * nblk + i, col0 + j)

    def p_map(b, i, j):
        return (jnp.maximum((b * nblk + i) * tmh - 1, 0), col0 + j)

    def n_map(b, i, j):
        return (jnp.minimum((b * nblk + i + 1) * tmh, last_halo), col0 + j)

    return pl.pallas_call(
        functools.partial(_conv_body, nblk=nblk, qk=qk),
        grid=(nb, nblk, ncol),
        in_specs=[pl.BlockSpec((tm, D_HY), x_map),
                  pl.BlockSpec((halo, D_HY), p_map),
                  pl.BlockSpec((halo, D_HY), n_map),
                  pl.BlockSpec((3, D_HY), lambda b, i, j: (0, col0 + j)),
                  pl.BlockSpec((1, D_HY), lambda b, i, j: (0, col0 + j))],
        out_specs=pl.BlockSpec((None, tm, D_HY), lambda b, i, j: (j // ppo, b * nblk + i, j % ppo)),
        out_shape=jax.ShapeDtypeStruct((ncol // ppo, nb * seq, ppo * D_HY), BF16),
        compiler_params=_cp("arbitrary", "arbitrary", "arbitrary"),
        name="short_conv_qk" if qk else "short_conv_hy",
    )(proj, proj, proj, conv_w, conv_b)


def _dft_tables(seq):
    n = 2 * seq
    n1 = n // DFT_N2
    n1h = n1 // 2
    n1k = n1h + SUBLANE
    k1 = jnp.arange(n1k, dtype=jnp.int32)
    t1 = jnp.arange(n1h, dtype=jnp.int32)
    kept = (k1 <= n1h).astype(F32)[:, None]
    weight = kept * jnp.where((k1 == 0) | (k1 == n1h), 1.0, 2.0)[:, None]
    ang = ((k1[:, None] * t1[None, :]) % n1).astype(F32) * (2.0 * math.pi / n1)
    lead_f = jnp.concatenate([kept * jnp.cos(ang), -kept * jnp.sin(ang)], axis=0).astype(BF16)
    lead_ic = (weight * jnp.cos(ang)).T.astype(BF16)
    lead_is = (-weight * jnp.sin(ang)).T.astype(BF16)
    t1f = jnp.arange(n1, dtype=jnp.int32)
    angf = ((k1[:, None] * t1f[None, :]) % n1).astype(F32) * (2.0 * math.pi / n1)
    lead_ff = jnp.concatenate([kept * jnp.cos(angf), -kept * jnp.sin(angf)], axis=0).astype(BF16)
    row = jnp.arange(2 * DFT_N2, dtype=jnp.int32)[None, :, None]
    col = jnp.arange(2 * DFT_N2, dtype=jnp.int32)[None, None, :]
    idx = ((col % DFT_N2) * ((row % DFT_N2) * n1 + k1[:, None, None])) % n
    phase = jnp.where(row // DFT_N2 == col // DFT_N2, 0.5 * math.pi,
                      jnp.where(row < col, 0.0, math.pi))
    mid_f = jnp.sin(idx.astype(F32) * (2.0 * math.pi / n) + phase).astype(BF16)
    return dict(n=n, n1=n1k, n1h=n1h, lead_f=lead_f, lead_ff=lead_ff, lead_ic=lead_ic, lead_is=lead_is,
                mid_f=mid_f)


def _hy_features(seq):
    tiles = 2 * seq // (SUBLANE * DFT_N2)
    half_rows = SUBLANE * DFT_N2 // 2
    tile = jnp.arange(tiles, dtype=jnp.int32)[:, None, None]
    row = jnp.arange(half_rows, dtype=jnp.int32)[None, :, None]
    side = jnp.arange(2, dtype=jnp.int32)[None, None, :]
    t2 = side * (DFT_N2 // 2) + row // SUBLANE
    t1 = tile * SUBLANE + row % SUBLANE
    slot = t1 * DFT_N2 + t2
    pos = jnp.where(slot < seq, slot, (2 * seq - slot) % seq).astype(F32)[..., None]
    t = pos * (1.0 / (seq - 1))
    w = 2.0 * math.pi * pos / seq
    f = jnp.linspace(1e-4, HY_BANDS - 1, HY_BANDS, dtype=F32)
    z = jnp.concatenate([t, jnp.cos(f * w), -jnp.sin(f * w)], axis=-1)
    z = jnp.pad(z, ((0, 0), (0, 0), (0, 0), (0, LANE - HY_EMB)))
    return z.reshape(tiles * half_rows, 2 * LANE)


def _hy_deltas():
    d = jnp.abs(jnp.linspace(math.log(HY_TARGET) / HY_SLOW_DECAY,
                             math.log(HY_TARGET) / HY_FAST_DECAY, D_HY, dtype=F32))
    return jnp.tile(d, 2)[None, :]


def _hy_filter_body(z_ref, w1_ref, b1_ref, fr_ref, w2_ref, b2_ref, w3_ref, dl_ref, k_ref, ss_ref, *, half):
    i = pl.program_id(0)
    hi = lax.Precision.HIGHEST
    z = z_ref[...]
    fr = fr_ref[...]
    h = jnp.sin(fr * (jnp.dot(z, w1_ref[...], precision=hi, preferred_element_type=F32) + b1_ref[...]))
    h = jnp.sin(fr * (jnp.dot(h, w2_ref[...], precision=hi, preferred_element_type=F32) + b2_ref[...]))
    hb = h.astype(BF16)
    dl = dl_ref[...]
    ha = jnp.dot(hb, w3_ref[0], preferred_element_type=F32) * jnp.exp(-z[:, 0:1] * dl)
    hb = jnp.dot(hb, w3_ref[1], preferred_element_type=F32) * jnp.exp(-z[:, LANE:LANE + 1] * dl)
    row = lax.broadcasted_iota(jnp.int32, ha.shape, 0)
    ha = jnp.where(row + (i - half) * ha.shape[0] == 0, 0.0, ha)
    nc = ha.shape[1]
    h2 = DFT_N2 // 2
    for t2 in range(h2):
        k_ref[:, t2 * nc:(t2 + 1) * nc] = ha[t2 * SUBLANE:(t2 + 1) * SUBLANE, :]
        k_ref[:, (h2 + t2) * nc:(h2 + t2 + 1) * nc] = hb[t2 * SUBLANE:(t2 + 1) * SUBLANE, :]
    part = jnp.broadcast_to(jnp.sum(ha * ha, axis=0, keepdims=True)
                            + jnp.sum(hb * hb, axis=0, keepdims=True), ss_ref.shape)

    @pl.when(i == 0)
    def _():
        ss_ref[...] = part

    @pl.when(i > 0)
    def _():
        ss_ref[...] += part


def _hy_filter(seq, z, w1, b1, fr, w2, b2, w3, deltas):
    nc = 2 * D_HY
    tm = SUBLANE * DFT_N2
    steps = 2 * seq // tm
    half = steps // 2
    full = lambda shape: pl.BlockSpec(shape, lambda i: (0, 0))
    return pl.pallas_call(
        functools.partial(_hy_filter_body, half=half),
        grid=(steps,),
        in_specs=[pl.BlockSpec((tm // 2, 2 * LANE), lambda i: (i, 0)),
                  full((2 * LANE, LANE)), full((1, LANE)), full((1, LANE)),
                  full((LANE, LANE)), full((1, LANE)),
                  pl.BlockSpec((None, 2, LANE, nc), lambda i: (i // half, 0, 0, 0)), full((1, nc))],
        out_specs=[pl.BlockSpec((SUBLANE, DFT_N2 * nc), lambda i: (i, 0)), full((SUBLANE, nc))],
        out_shape=[jax.ShapeDtypeStruct((2 * seq // DFT_N2, DFT_N2 * nc), F32),
                   jax.ShapeDtypeStruct((SUBLANE, nc), F32)],
        compiler_params=_cp("arbitrary"),
        name="hyena_filter",
    )(z, w1, b1, fr, w2, b2, w3, deltas)


def _lead_fwd_body(x_ref, f_ref, re_ref, im_ref):
    n1 = re_ref.shape[0]
    r = jnp.dot(f_ref[...], x_ref[...].astype(BF16), preferred_element_type=F32)
    re_ref[...] = r[:n1].astype(BF16)
    im_ref[...] = r[n1:].astype(BF16)


def _lead_fwd(x, table, plane, nb):
    n1, rows = table.shape[0] // 2, table.shape[1]
    cols = x.shape[-1]
    tn = _tile(cols, 4096)
    return pl.pallas_call(
        _lead_fwd_body,
        grid=(nb, cols // tn),
        in_specs=[pl.BlockSpec((None, rows, tn), lambda b, j: (plane, b, j)),
                  pl.BlockSpec((2 * n1, rows), lambda b, j: (0, 0))],
        out_specs=[pl.BlockSpec((None, n1, tn), lambda b, j: (b, 0, j))] * 2,
        out_shape=[jax.ShapeDtypeStruct((nb, n1, cols), BF16)] * 2,
        compiler_params=_cp("arbitrary", "arbitrary"),
        name="hyena_lead_fwd",
    )(x, table)


def _mid_spec_body(are_ref, aim_ref, f_ref, ss_ref, kr_ref, ki_ref, *, bk):
    n2 = DFT_N2
    scale = lax.rsqrt(ss_ref[0:1, :] + EPS)

    def body(i, c):
        a = jnp.concatenate([are_ref[i], aim_ref[i]], axis=0)
        x = jnp.dot(f_ref[i], a, preferred_element_type=F32)
        kr_ref[i] = x[:n2] * scale
        ki_ref[i] = x[n2:] * scale
        return c

    lax.fori_loop(0, bk, body, 0, unroll=True)


def _mid_spec(a_re, a_im, sumsq, tb):
    n1 = tb["n1"]
    bk = _tile(n1, 8)
    a_re = a_re.reshape(n1, DFT_N2, 2 * D_HY)
    a_im = a_im.reshape(n1, DFT_N2, 2 * D_HY)
    a_spec = pl.BlockSpec((bk, DFT_N2, D_HY), lambda o, i: (i, 0, o))
    k_spec = pl.BlockSpec((None, bk, DFT_N2, D_HY), lambda o, i: (o, i, 0, 0))
    return pl.pallas_call(
        functools.partial(_mid_spec_body, bk=bk),
        grid=(2, n1 // bk),
        in_specs=[a_spec, a_spec,
                  pl.BlockSpec((bk, 2 * DFT_N2, 2 * DFT_N2), lambda o, i: (i, 0, 0)),
                  pl.BlockSpec((SUBLANE, D_HY), lambda o, i: (0, o))],
        out_specs=[k_spec, k_spec],
        out_shape=[jax.ShapeDtypeStruct((2, n1, DFT_N2, D_HY), F32)] * 2,
        compiler_params=_cp("arbitrary", "arbitrary"),
        name="hyena_mid_spectrum",
    )(a_re, a_im, tb["mid_f"], sumsq)


def _mid_conv_body(are_ref, aim_ref, f_ref, kr_ref, ki_ref, bre_ref, bim_ref, *, bk):
    n2 = DFT_N2

    def body(i, c):
        a = jnp.concatenate([are_ref[i], aim_ref[i]], axis=0)
        x = jnp.dot(f_ref[i], a, preferred_element_type=F32)
        xr = x[:n2]
        xi = x[n2:]
        kr = kr_ref[i]
        ki = ki_ref[i]
        p = jnp.concatenate([xr * kr - xi * ki, xr * ki + xi * kr], axis=0).astype(BF16)
        q = lax.dot_general(f_ref[i], p, (((0,), (0,)), ((), ())), preferred_element_type=F32)
        bre_ref[i] = q[:n2].astype(BF16)
        bim_ref[i] = q[n2:].astype(BF16)
        return c

    lax.fori_loop(0, bk, body, 0, unroll=True)


def _mid_conv(a_re, a_im, kf_re, kf_im, order, tb):
    n1 = tb["n1"]
    nb = a_re.shape[0]
    bk = _tile(n1, 8)
    a_re = a_re.reshape(nb, n1, DFT_N2, D_HY)
    a_im = a_im.reshape(nb, n1, DFT_N2, D_HY)
    a_spec = pl.BlockSpec((None, bk, DFT_N2, D_HY), lambda b, i: (b, i, 0, 0))
    t_spec = pl.BlockSpec((bk, 2 * DFT_N2, 2 * DFT_N2), lambda b, i: (i, 0, 0))
    k_spec = pl.BlockSpec((None, bk, DFT_N2, D_HY), lambda b, i: (order, i, 0, 0))
    b_re, b_im = pl.pallas_call(
        functools.partial(_mid_conv_body, bk=bk),
        grid=(nb, n1 // bk),
        in_specs=[a_spec, a_spec, t_spec, k_spec, k_spec],
        out_specs=[a_spec, a_spec],
        out_shape=[jax.ShapeDtypeStruct((nb, n1, DFT_N2, D_HY), BF16)] * 2,
        compiler_params=_cp("arbitrary", "arbitrary"),
        name="hyena_mid_conv",
    )(a_re, a_im, tb["mid_f"], kf_re, kf_im)
    return b_re.reshape(nb, n1, DFT_N2 * D_HY), b_im.reshape(nb, n1, DFT_N2 * D_HY)


def _lead_inv_body(bre_ref, bim_ref, c_ref, s_ref, z_ref, gate_ref, skip_ref, gn_ref, o_ref, *, inv_n, last):
    y = (jnp.dot(c_ref[...], bre_ref[...], preferred_element_type=F32)
         + jnp.dot(s_ref[...], bim_ref[...], preferred_element_type=F32)) * inv_n
    out = gate_ref[...].astype(F32) * (y + skip_ref[...] * z_ref[...].astype(F32))
    if not last:
        o_ref[...] = out
    else:
        gn = gn_ref[...]
        for c in range(out.shape[1] // D_HY):
            blk = out[:, c * D_HY:(c + 1) * D_HY]
            ms = jnp.mean(blk * blk, axis=-1, keepdims=True)
            o_ref[:, c * D_HY:(c + 1) * D_HY] = (blk * lax.rsqrt(ms + EPS) * gn).astype(o_ref.dtype)


def _lead_inv(b_re, b_im, z, z_plane, gates, gate_plane, skip, gn, tb, last):
    n1, n1h = tb["n1"], tb["n1h"]
    nb, _, cols = b_re.shape
    tn = _tile(cols, 4096)
    skip_t = jnp.tile(skip[None, :], (1, tn // D_HY))
    b_spec = pl.BlockSpec((None, n1, tn), lambda b, j: (b, 0, j))
    t_spec = pl.BlockSpec((n1h, n1), lambda b, j: (0, 0))
    return pl.pallas_call(
        functools.partial(_lead_inv_body, inv_n=1.0 / tb["n"], last=last),
        grid=(nb, cols // tn),
        in_specs=[b_spec, b_spec, t_spec, t_spec,
                  pl.BlockSpec((None, n1h, tn), lambda b, j: (z_plane, b, j)),
                  pl.BlockSpec((None, n1h, tn), lambda b, j: (gate_plane, b, j)),
                  pl.BlockSpec((1, tn), lambda b, j: (0, 0)),
                  pl.BlockSpec((1, D_HY), lambda b, j: (0, 0))],
        out_specs=pl.BlockSpec((None, n1h, tn), lambda b, j: (0, b, j)),
        out_shape=jax.ShapeDtypeStruct((1, nb * n1h, cols), BF16 if last else F32),
        compiler_params=_cp("arbitrary", "arbitrary"),
        name="hyena_lead_inv",
    )(b_re, b_im, tb["lead_ic"], tb["lead_is"], z, gates, skip_t, gn)


def _hyena(conv_out, nb, seq, tb, kf_re, kf_im, skip, gn):
    n1h = tb["n1h"]
    planes = conv_out.reshape(conv_out.shape[0], nb * n1h, DFT_N2 * D_HY)
    z, z_plane = planes, 0
    for order in range(2):
        a_re, a_im = _lead_fwd(z, tb["lead_f"], z_plane, nb)
        b_re, b_im = _mid_conv(a_re, a_im, kf_re, kf_im, order, tb)
        z = _lead_inv(b_re, b_im, z, z_plane, planes, 1 + order, skip[order], gn, tb, order == 1)
        z_plane = 0
    return z.reshape(nb * seq, D_HY)


def _hyena_spectra(seq, tb, w1, b1, fr, w2, b2, w3):
    w1 = jnp.pad(w1, ((0, LANE - HY_EMB), (0, 0)))
    zero1 = jnp.zeros_like(w1)
    w1 = jnp.concatenate([jnp.concatenate([w1, zero1], axis=1),
                          jnp.concatenate([zero1, w1], axis=1)], axis=0)
    zero2 = jnp.zeros_like(w2)
    w2 = jnp.concatenate([jnp.concatenate([w2, zero2], axis=1),
                          jnp.concatenate([zero2, w2], axis=1)], axis=0)
    pair = lambda v: jnp.concatenate([v, v])[None, :]
    w3 = w3.reshape(HY_FILT, 2, 2, D_HY).transpose(2, 0, 1, 3).reshape(2, HY_FILT, 2 * D_HY)
    zero3 = jnp.zeros_like(w3)
    w3 = jnp.stack([jnp.concatenate([w3, zero3], axis=1),
                    jnp.concatenate([zero3, w3], axis=1)], axis=1).astype(BF16)
    taps, sumsq = _hy_filter(seq, _hy_features(seq), w1, pair(b1), pair(fr), w2, pair(b2), w3,
                             _hy_deltas())
    a_re, a_im = _lead_fwd(taps[None], tb["lead_ff"], 0, 1)
    return _mid_spec(a_re[0], a_im[0], sumsq, tb)


def _s5_scan_body(u_ref, pm_ref, bd_ref, a_ref, cd_ref, x0_ref, o_ref, bu_ref, st_ref, y_ref,
                  *, tstep, reverse, final_state):
    tb = pl.program_id(0)
    ns = S5_STREAMS
    half = S5_STATE
    cw = S5_BLOCK_GROUPS * S5_P

    @pl.when(tb == 0)
    def _():
        st_ref[...] = x0_ref[...]

    u = u_ref[...].reshape(ns * tstep, D_S5).astype(BF16)
    u_tm = jnp.dot(pm_ref[...], u, preferred_element_type=F32).astype(BF16)

    def project_in(blk):
        bu_ref[:, 2 * blk * cw:2 * (blk + 1) * cw] = jnp.dot(
            u_tm[:, blk * LANE:(blk + 1) * LANE], bd_ref[blk], preferred_element_type=F32)

    def scan(blk):
        bre = slice(2 * blk * cw, (2 * blk + 1) * cw)
        bim = slice((2 * blk + 1) * cw, (2 * blk + 2) * cw)
        sre = slice(blk * cw, (blk + 1) * cw)
        sim = slice(half + blk * cw, half + (blk + 1) * cw)
        ar = a_ref[:, sre]
        ai = a_ref[:, sim]
        xr = st_ref[:, sre]
        xi = st_ref[:, sim]
        for i in range(tstep):
            t = tstep - 1 - i if reverse else i
            rows = slice(t * ns, (t + 1) * ns)
            xr, xi = (ar * xr - ai * xi + bu_ref[rows, bre],
                      ar * xi + ai * xr + bu_ref[rows, bim])
            if not final_state:
                bu_ref[rows, bre] = xr
                bu_ref[rows, bim] = xi
        st_ref[:, sre] = xr
        st_ref[:, sim] = xi

    def project_out(blk):
        y_ref[blk] = jnp.dot(bu_ref[:, 2 * blk * cw:2 * (blk + 1) * cw].astype(BF16), cd_ref[blk],
                             preferred_element_type=F32)

    project_in(0)
    for blk in range(S5_NBLOCKS):
        if blk + 1 < S5_NBLOCKS:
            project_in(blk + 1)
        scan(blk)
        if not final_state:
            project_out(blk)

    if final_state:
        @pl.when(tb == pl.num_programs(0) - 1)
        def _():
            o_ref[...] = st_ref[...]
    else:
        for s in range(ns):
            for blk in range(S5_NBLOCKS):
                o_ref[s, :, blk * LANE:(blk + 1) * LANE] = y_ref[blk, pl.ds(s, tstep, stride=ns), :]


def _s5_scan(proj, pmat, bd, a_b, cd, x0, tlen, reverse, final_state):
    ns = S5_STREAMS
    d = 1 if reverse else 0
    tstep = pmat.shape[0] // ns
    nt = tlen // tstep
    proj3 = proj.reshape(ns, tlen, proj.shape[1])
    blk_in = S5_BLOCK_GROUPS * S5_GROUP
    blk_state = 2 * S5_BLOCK_GROUPS * S5_P

    def window(t):
        return nt - 1 - t if reverse else t

    if final_state:
        out_spec = pl.BlockSpec((ns, 2 * S5_STATE), lambda t: (0, 0))
        out_shape = jax.ShapeDtypeStruct((ns, 2 * S5_STATE), F32)
    else:
        out_spec = pl.BlockSpec((ns, tstep, D_S5), lambda t: (0, window(t), 0))
        out_shape = jax.ShapeDtypeStruct((ns, tlen, D_S5), F32)
    out = pl.pallas_call(
        functools.partial(_s5_scan_body, tstep=tstep, reverse=reverse, final_state=final_state),
        grid=(nt,),
        in_specs=[pl.BlockSpec((ns, tstep, D_S5), lambda t: (0, window(t), COL_U)),
                  pl.BlockSpec((ns * tstep, ns * tstep), lambda t: (0, 0)),
                  pl.BlockSpec((None, S5_NBLOCKS, blk_in, blk_state), lambda t: (d, 0, 0, 0)),
                  pl.BlockSpec((None, ns, 2 * S5_STATE), lambda t: (d, 0, 0)),
                  pl.BlockSpec((None, S5_NBLOCKS, blk_state, blk_in), lambda t: (d, 0, 0, 0)),
                  pl.BlockSpec((None, ns, 2 * S5_STATE), lambda t: (d, 0, 0))],
        out_specs=out_spec,
        out_shape=out_shape,
        scratch_shapes=[pltpu.VMEM((tstep * ns, 2 * S5_STATE), F32),
                        pltpu.VMEM((ns, 2 * S5_STATE), F32),
                        pltpu.VMEM((D_S5 // LANE, tstep * ns, LANE), F32)],
        compiler_params=_cp("arbitrary"),
        name="s5_end_state" if final_state else "s5_scan",
    )(proj3, pmat, bd, a_b, cd, x0)
    return out if final_state else out.reshape(ns * tlen, D_S5)


def _s5_params(lam_re, lam_im, log_dt, b_re, b_im, c_re, c_im):
    lr = jnp.minimum(lam_re, -1e-4)
    li = lam_im
    dt = jnp.exp(log_dt)[..., None]
    er = jnp.exp(lr * dt)
    a_re = er * jnp.cos(li * dt)
    a_im = er * jnp.sin(li * dt)
    den = lr * lr + li * li
    q_re = ((a_re - 1.0) * lr + a_im * li) / den
    q_im = (a_im * lr - (a_re - 1.0) * li) / den
    bb_re = q_re[..., None] * b_re - q_im[..., None] * b_im
    bb_im = q_re[..., None] * b_im + q_im[..., None] * b_re
    nbk, gb = S5_NBLOCKS, S5_BLOCK_GROUPS
    eye = jnp.eye(gb, dtype=F32)

    def in_mat(x):
        x = x.reshape(2, nbk, gb, S5_P, S5_GROUP)
        return jnp.einsum("dbgph,gk->dbghkp", x, eye).reshape(2, nbk, gb * S5_GROUP, gb * S5_P)

    def out_mat(x):
        x = x.reshape(2, nbk, gb, S5_GROUP, S5_P)
        return jnp.einsum("dbghp,gk->dbgpkh", x, eye).reshape(2, nbk, gb * S5_P, gb * S5_GROUP)

    bd = jnp.concatenate([in_mat(bb_re), in_mat(bb_im)], axis=3).astype(BF16)
    cd = jnp.concatenate([out_mat(c_re), out_mat(-c_im)], axis=2).astype(BF16)
    a_bar = jnp.concatenate([a_re.reshape(2, S5_STATE), a_im.reshape(2, S5_STATE)], axis=1)
    return (lr * dt).reshape(2, S5_STATE), (li * dt).reshape(2, S5_STATE), a_bar, bd, cd


def _s5_row_perm(tstep):
    ns = S5_STREAMS
    r = jnp.arange(ns * tstep)
    src = (r % ns) * tstep + r // ns
    return (src[:, None] == jnp.arange(ns * tstep)[None, :]).astype(BF16)


def _s5_mixer(proj, nb, seq, s5p):
    la_re, la_im, a_bar, bd, cd = s5p
    ns = S5_STREAMS
    ncs = ns // nb
    tlen = seq // ncs
    pmat = _s5_row_perm(_tile(tlen, S5_TSTEP))
    a_b = jnp.broadcast_to(a_bar[:, None, :], (2, ns, 2 * S5_STATE))
    zeros = jnp.zeros((2, ns, 2 * S5_STATE), F32)
    mag = jnp.exp(la_re * tlen)
    at_re = (mag * jnp.cos(la_im * tlen))[:, None, :]
    at_im = (mag * jnp.sin(la_im * tlen))[:, None, :]
    ys = []
    for d in range(2):
        x_end = _s5_scan(proj, pmat, bd, a_b, cd, zeros, tlen, d == 1, True)
        xe_re = x_end[:, :S5_STATE].reshape(nb, ncs, S5_STATE)
        xe_im = x_end[:, S5_STATE:].reshape(nb, ncs, S5_STATE)
        order = range(ncs) if d == 0 else range(ncs - 1, -1, -1)
        cr = jnp.zeros((nb, S5_STATE), F32)
        ci = jnp.zeros((nb, S5_STATE), F32)
        rows = [None] * ncs
        for c in order:
            rows[c] = jnp.concatenate([cr, ci], axis=-1)
            cr, ci = (at_re[d] * cr - at_im[d] * ci + xe_re[:, c],
                      at_re[d] * ci + at_im[d] * cr + xe_im[:, c])
        x0 = jnp.stack(rows, axis=1).reshape(ns, 2 * S5_STATE)
        x0 = jnp.stack([x0, x0])
        ys.append(_s5_scan(proj, pmat, bd, a_b, cd, x0, tlen, d == 1, False))
    return ys


def _s5_post_body(yf_ref, yb_ref, u_ref, d_ref, w_ref, b_ref, gn_ref, o_ref):
    y = yf_ref[...] + yb_ref[...] + d_ref[...] * u_ref[...].astype(F32)
    g = jax.nn.gelu(y)
    r = jnp.dot(g.astype(BF16), w_ref[...], preferred_element_type=F32) + b_ref[...]
    out = r[:, :D_S5] * jax.nn.sigmoid(r[:, D_S5:])
    ms = jnp.mean(out * out, axis=-1, keepdims=True)
    o_ref[...] = (out * lax.rsqrt(ms + EPS) * gn_ref[...]).astype(o_ref.dtype)


def _s5_post(y_f, y_b, proj, d, glu_w, glu_b, gn):
    m = y_f.shape[0]
    tm = _tile(m, 512)
    full = lambda shape: pl.BlockSpec(shape, lambda i: (0, 0))
    return pl.pallas_call(
        _s5_post_body,
        grid=(m // tm,),
        in_specs=[pl.BlockSpec((tm, D_S5), lambda i: (i, 0)),
                  pl.BlockSpec((tm, D_S5), lambda i: (i, 0)),
                  pl.BlockSpec((tm, D_S5), lambda i: (i, COL_U)),
                  full((1, D_S5)), full((D_S5, 2 * D_S5)), full((1, 2 * D_S5)), full((1, D_S5))],
        out_specs=pl.BlockSpec((tm, D_S5), lambda i: (i, 0)),
        out_shape=jax.ShapeDtypeStruct((m, D_S5), BF16),
        compiler_params=_cp("arbitrary"),
        name="s5_glu_norm",
    )(y_f, y_b, proj, d, glu_w, glu_b, gn)


def _split3(x):
    hi = x.astype(BF16)
    rest = x - hi.astype(F32)
    mid = rest.astype(BF16)
    return hi, mid, (rest - mid.astype(F32)).astype(BF16)


def _mlstm_body(*refs, reverse, epilogue):
    if epilogue:
        (q_ref, k_ref, v_ref, g_ref, gt_ref, bias_ref, biast_ref, seen_ref, upto_ref, hf_ref, o_gate_ref,
         gn_ref, out_ref, c_ref, n_ref, m_ref) = refs
    else:
        (q_ref, k_ref, v_ref, g_ref, gt_ref, bias_ref, biast_ref, seen_ref, upto_ref,
         out_ref, c_ref, n_ref, m_ref) = refs
        hf_ref = o_gate_ref = gn_ref = None

    @pl.when(pl.program_id(2) == 0)
    def _():
        c_ref[...] = jnp.zeros_like(c_ref)
        n_ref[...] = jnp.zeros_like(n_ref)
        m_ref[...] = jnp.zeros_like(m_ref)

    tc = ML_CHUNK
    r = lax.broadcasted_iota(jnp.int32, (tc, tc), 0)
    s = lax.broadcasted_iota(jnp.int32, (tc, tc), 1)
    seen = (s >= r) if reverse else (s <= r)
    g = g_ref[...] + bias_ref[...]
    gt = gt_ref[...] + biast_ref[:, 0:1]
    for hh in range(ML_HEADS_PER_STEP):
        _mlstm_head(hh, g, gt, seen, q_ref, k_ref, v_ref, seen_ref, upto_ref, hf_ref, o_gate_ref,
                    gn_ref, out_ref, c_ref, n_ref, m_ref, reverse=reverse)


def _mlstm_head(hh, g, gt, seen, q_ref, k_ref, v_ref, seen_ref, upto_ref, hf_ref, o_gate_ref, gn_ref,
                out_ref, c_ref, n_ref, m_ref, *, reverse):
    head = pl.program_id(1) * ML_HEADS_PER_STEP + hh
    cols = slice(hh * ML_DH, (hh + 1) * ML_DH)
    tc = ML_CHUNK
    gate_i = 2 if reverse else 0
    idx_i = gate_i * ML_HEADS + head
    idx_f = idx_i + ML_HEADS
    tiled = lambda x, n: x if n == LANE else jnp.concatenate([x] * (n // LANE), axis=1)
    wide = lambda x: tiled(x, ML_DH)
    span = lambda x: tiled(x, tc)
    dot = functools.partial(jnp.dot, preferred_element_type=F32)

    lane = lax.broadcasted_iota(jnp.int32, g.shape, 1)
    ig = jnp.broadcast_to(jnp.sum(jnp.where(lane == idx_i, g, 0.0), axis=1, keepdims=True), g.shape)
    fg = jnp.broadcast_to(jnp.sum(jnp.where(lane == idx_f, g, 0.0), axis=1, keepdims=True), g.shape)
    sub = lax.broadcasted_iota(jnp.int32, gt.shape, 0)
    ig_row = jnp.sum(jnp.where(sub == idx_i, gt, 0.0), axis=0, keepdims=True)
    fg_row = jnp.sum(jnp.where(sub == idx_f, gt, 0.0), axis=0, keepdims=True)
    lf = jax.nn.log_sigmoid(fg)
    lf_row = jax.nn.log_sigmoid(fg_row)

    seen_b = seen_ref[...]
    upto_b = upto_ref[...]
    b_col = sum(dot(seen_b, piece) for piece in _split3(lf))
    lf_rows = jnp.broadcast_to(lf_row, (SUBLANE, tc))
    b_row = sum(dot(piece, upto_b) for piece in _split3(lf_rows))[0:1]
    b_last = jnp.sum(lf_row, axis=1, keepdims=True)

    m_prev = m_ref[hh:hh + 1, 0:1]
    src = ig_row - b_row
    dmat = jnp.where(seen, span(b_col) + src, -jnp.inf)
    g_car = b_col + m_prev
    mt = jnp.maximum(g_car, jnp.max(dmat, axis=1, keepdims=True))
    inter = jnp.exp(g_car - mt)
    qb = q_ref[:, cols]
    kb = k_ref[:, cols]
    vb = v_ref[:, cols].astype(BF16)
    ones = jnp.ones((tc, LANE), BF16)
    qk = lax.dot_general(qb, kb, (((1,), (1,)), ((), ())), preferred_element_type=F32)
    sc = (qk * jnp.exp(dmat - span(mt))).astype(BF16)
    c_state = c_ref[hh]
    n_state = n_ref[hh]
    num = wide(inter) * dot(qb, c_state.astype(BF16)) + dot(sc, vb)
    den = inter * dot(qb, n_state.astype(BF16)) + dot(sc, ones)
    h = num * wide(1.0 / jnp.maximum(jnp.abs(den), jnp.exp(-mt)))

    a_row = b_last + src
    a_col = b_last - b_col + ig
    m_new = jnp.maximum(b_last + m_prev, jnp.max(a_row, axis=1, keepdims=True))
    decay = jnp.exp(b_last + m_prev - m_new)
    kw = (kb.astype(F32) * wide(jnp.exp(a_col - m_new))).astype(BF16)
    upd = lax.dot_general(kw, jnp.concatenate([vb, ones], axis=1), (((0,), (0,)), ((), ())),
                          preferred_element_type=F32)
    c_ref[hh] = decay * c_state + upd[:, :ML_DH]
    n_ref[hh] = decay * n_state + upd[:, ML_DH:]
    m_ref[hh:hh + 1, 0:1] = m_new

    if hf_ref is not None:
        h = h + hf_ref[:, cols]
        ms = jnp.mean(h * h, axis=-1, keepdims=True)
        hn = h * lax.rsqrt(ms + EPS) * gn_ref[:, cols]
        out_ref[:, cols] = (hn * jax.nn.sigmoid(o_gate_ref[:, cols].astype(F32))).astype(out_ref.dtype)
    else:
        out_ref[:, cols] = h


def _mlstm_dir(qk_planes, proj, gates, gates_t, bias, bias_t, nb, seq, reverse, h_fwd=None, gn=None):
    tc = ML_CHUNK
    nch = seq // tc
    hps = ML_HEADS_PER_STEP
    width = hps * ML_DH
    col_v = (N_CONV + D_S5) // width
    col_o = (N_CONV + D_S5 + D_ML) // width

    def chunk(b, ci):
        return b * nch + (nch - 1 - ci if reverse else ci)

    in_specs = [
        pl.BlockSpec((None, tc, width), lambda b, h, ci: (0, chunk(b, ci), h)),
        pl.BlockSpec((None, tc, width), lambda b, h, ci: (1, chunk(b, ci), h)),
        pl.BlockSpec((tc, width), lambda b, h, ci: (chunk(b, ci), col_v + h)),
        pl.BlockSpec((tc, LANE), lambda b, h, ci: (chunk(b, ci), 0)),
        pl.BlockSpec((4 * ML_HEADS, tc), lambda b, h, ci: (0, chunk(b, ci))),
        pl.BlockSpec((1, LANE), lambda b, h, ci: (0, 0)),
        pl.BlockSpec((4 * ML_HEADS, LANE), lambda b, h, ci: (0, 0)),
        pl.BlockSpec((tc, tc), lambda b, h, ci: (0, 0)),
        pl.BlockSpec((tc, tc), lambda b, h, ci: (0, 0)),
    ]
    pos = jnp.arange(tc)
    seen = (pos[None, :] >= pos[:, None]) if reverse else (pos[None, :] <= pos[:, None])
    seen = seen.astype(BF16)
    args = [qk_planes, qk_planes, proj, gates, gates_t, bias, bias_t, seen, seen.T]
    epilogue = h_fwd is not None
    if epilogue:
        in_specs += [
            pl.BlockSpec((tc, width), lambda b, h, ci: (chunk(b, ci), h)),
            pl.BlockSpec((tc, width), lambda b, h, ci: (chunk(b, ci), col_o + h)),
            pl.BlockSpec((1, width), lambda b, h, ci: (0, h)),
        ]
        args += [h_fwd, proj, gn]
    return pl.pallas_call(
        functools.partial(_mlstm_body, reverse=reverse, epilogue=epilogue),
        grid=(nb, ML_HEADS // hps, nch),
        in_specs=in_specs,
        out_specs=pl.BlockSpec((tc, width), lambda b, h, ci: (chunk(b, ci), h)),
        out_shape=jax.ShapeDtypeStruct((nb * seq, D_ML), BF16 if epilogue else F32),
        scratch_shapes=[pltpu.VMEM((hps, ML_DH, ML_DH), F32), pltpu.VMEM((hps, ML_DH, LANE), F32),
                        pltpu.VMEM((SUBLANE, LANE), F32)],
        compiler_params=_cp("arbitrary", "arbitrary", "arbitrary"),
        name="mlstm_bwd_norm" if reverse else "mlstm_fwd",
    )(*args)


def _layer(x, nb, seq, tb, p):
    proj, gates = _proj_in(x, p["ln_g"][0:1], p["w_in"], p["w_gate"])
    gates_t = gates[:, :4 * ML_HEADS].T
    gn = p["group_norm"]
    gn_hy = gn[None, :D_HY]
    gn_s5 = gn[None, D_HY:D_HY + D_S5]
    gn_ml = gn[None, D_HY + D_S5:]

    hy_planes = _short_conv(proj, p["conv_w"], p["conv_b"][None, :], nb, seq, False)
    qk_planes = _short_conv(proj, p["conv_w"], p["conv_b"][None, :], nb, seq, True)
    kf_re, kf_im = _hyena_spectra(seq, tb, p["hy_w1"], p["hy_b1"], p["hy_freq"], p["hy_w2"],
                                  p["hy_b2"], p["hy_w3"])
    y_hy = _hyena(hy_planes, nb, seq, tb, kf_re, kf_im, p["hy_skip"], gn_hy)
    y_s5 = _s5_post(*_s5_mixer(proj, nb, seq, p["s5"]), proj, p["s5_d"][None, :],
                    p["s5_glu_w"], p["s5_glu_b"][None, :], gn_s5)
    ml_args = (qk_planes, proj, gates, gates_t, p["ml_bias"], p["ml_bias_t"], nb, seq)
    h_f = _mlstm_dir(*ml_args, False)
    y_ml = _mlstm_dir(*ml_args, True, h_f, gn_ml)

    w_out = p["w_out"]
    x = _proj_out([y_hy, y_s5, y_ml], [w_out[:D_HY], w_out[D_HY:D_HY + D_S5], w_out[D_HY + D_S5:]],
                  x, p["ln_g"][1:2])
    return _mlp(x, p["ln_g"][2:3], p["mlp_w1"], p["mlp_w2"], p["ln_g"][3:4])


_PARAM_NAMES = ("ln_g", "w_in", "conv_w", "conv_b", "hy_w1", "hy_b1", "hy_freq", "hy_w2", "hy_b2",
                "hy_w3", "hy_skip", "s5_lam_re", "s5_lam_im", "s5_log_dt", "s5_b_re", "s5_b_im",
                "s5_c_re", "s5_c_im", "s5_d", "s5_glu_w", "s5_glu_b", "ml_gate_b", "group_norm",
                "w_out", "mlp_w1", "mlp_w2")


def _prepare(p):
    p = dict(p)
    w_in = p["w_in"].astype(BF16)
    p["w_in"] = w_in[:, :P_MIX]
    p["w_gate"] = jnp.pad(w_in[:, P_MIX:], ((0, 0), (0, LANE - 4 * ML_HEADS)))
    p["w_out"] = p["w_out"].astype(BF16)
    p["mlp_w1"] = p["mlp_w1"].astype(BF16)
    p["mlp_w2"] = p["mlp_w2"].astype(BF16)
    p["s5_glu_w"] = p["s5_glu_w"].astype(BF16)
    p["ml_bias"] = jnp.pad(p["ml_gate_b"].reshape(1, -1), ((0, 0), (0, LANE - 4 * ML_HEADS)))
    p["ml_bias_t"] = jnp.broadcast_to(p["ml_gate_b"].reshape(-1, 1), (4 * ML_HEADS, LANE))
    p["s5"] = _s5_params(p["s5_lam_re"], p["s5_lam_im"], p["s5_log_dt"], p["s5_b_re"],
                         p["s5_b_im"], p["s5_c_re"], p["s5_c_im"])
    return p


def kernel(x_prompt, x_sample, ln_g, w_in, conv_w, conv_b, hy_w1, hy_b1, hy_freq, hy_w2, hy_b2, hy_w3,
           hy_skip, s5_lam_re, s5_lam_im, s5_log_dt, s5_b_re, s5_b_im, s5_c_re, s5_c_im, s5_d,
           s5_glu_w, s5_glu_b, ml_gate_b, group_norm, w_out, mlp_w1, mlp_w2):
    params = (ln_g, w_in, conv_w, conv_b, hy_w1, hy_b1, hy_freq, hy_w2, hy_b2, hy_w3, hy_skip,
              s5_lam_re, s5_lam_im, s5_log_dt, s5_b_re, s5_b_im, s5_c_re, s5_c_im, s5_d,
              s5_glu_w, s5_glu_b, ml_gate_b, group_norm, w_out, mlp_w1, mlp_w2)
    streams = []
    for x in (x_prompt, x_sample):
        nb, seq, _ = x.shape
        streams.append([x.reshape(nb * seq, D_MODEL), nb, seq, _dft_tables(seq)])
    for layer in range(DEPTH):
        p = _prepare({name: arr[layer] for name, arr in zip(_PARAM_NAMES, params)})
        for st in streams:
            st[0] = _layer(st[0], st[1], st[2], st[3], p)
    return tuple(st[0].reshape(x.shape) for st, x in zip(streams, (x_prompt, x_sample)))
```

```python
import functools
import math

import jax
import jax.numpy as jnp
from jax import lax
from jax.experimental import pallas as pl
from jax.experimental.pallas import tpu as pltpu

F32 = jnp.float32
BF16 = jnp.bfloat16

D_MODEL = 2048
DEPTH = 4
D_HY = 512
D_S5 = 512
D_ML = 1024
HY_EMB = 33
HY_BANDS = 16
HY_FILT = 64
HY_FAST_DECAY = 0.3
HY_SLOW_DECAY = 1.5
HY_TARGET = 1e-2
S5_GROUP = 16
S5_G = 32
S5_P = 64
S5_STATE = S5_G * S5_P
ML_HEADS = 4
ML_DH = 256
D_FF = 4 * D_MODEL
N_CONV = 3 * D_HY + 2 * D_ML
P_IN = N_CONV + D_S5 + 2 * D_ML + 4 * ML_HEADS
P_MIX = P_IN - 4 * ML_HEADS
EPS = 1e-6

LANE = 128
SUBLANE = 8
VMEM_LIMIT_BYTES = 48 * 1024 * 1024

DFT_N2 = LANE
S5_STREAMS = SUBLANE
S5_TSTEP = 64
S5_BLOCK_GROUPS = LANE // S5_GROUP
S5_NBLOCKS = S5_G // S5_BLOCK_GROUPS
ML_CHUNK = 256
ML_HEADS_PER_STEP = 4
COL_U = N_CONV // D_S5


def _cp(*sem):
    return pltpu.CompilerParams(dimension_semantics=sem, vmem_limit_bytes=VMEM_LIMIT_BYTES)


def _tile(n, pref):
    t = min(n, pref)
    while n % t:
        t //= 2
    return t


def _rms(x, g):
    ms = jnp.mean(x * x, axis=-1, keepdims=True)
    return x * lax.rsqrt(ms + EPS) * g


def _proj_in_body(x_ref, g_ref, w_ref, wg_ref, o_ref, og_ref, xn_ref):
    @pl.when(pl.program_id(1) == 0)
    def _():
        xn = _rms(x_ref[...], g_ref[...]).astype(BF16)
        xn_ref[...] = xn
        og_ref[...] = jnp.dot(xn, wg_ref[...], preferred_element_type=F32)

    o_ref[...] = jnp.dot(xn_ref[...], w_ref[...], preferred_element_type=F32).astype(o_ref.dtype)


def _proj_in(x, g, w, w_gate):
    m, k = x.shape
    n = w.shape[1]
    tm = _tile(m, 1024)
    tn = 1024
    return pl.pallas_call(
        _proj_in_body,
        grid=(m // tm, n // tn),
        in_specs=[pl.BlockSpec((tm, k), lambda i, j: (i, 0)),
                  pl.BlockSpec((1, k), lambda i, j: (0, 0)),
                  pl.BlockSpec((k, tn), lambda i, j: (0, j)),
                  pl.BlockSpec((k, LANE), lambda i, j: (0, 0))],
        out_specs=[pl.BlockSpec((tm, tn), lambda i, j: (i, j)),
                   pl.BlockSpec((tm, LANE), lambda i, j: (i, 0))],
        out_shape=[jax.ShapeDtypeStruct((m, n), BF16), jax.ShapeDtypeStruct((m, LANE), F32)],
        scratch_shapes=[pltpu.VMEM((tm, k), BF16)],
        compiler_params=_cp("parallel", "arbitrary"),
        name="norm_proj_in",
    )(x, g, w, w_gate)


def _proj_out_body(a0_ref, a1_ref, a2_ref, w0_ref, w1_ref, w2_ref, r_ref, g_ref, o_ref):
    half = o_ref.shape[0] // 2
    for rows in (slice(0, half), slice(half, 2 * half)):
        f = (jnp.dot(a0_ref[rows, :].astype(BF16), w0_ref[...], preferred_element_type=F32)
             + jnp.dot(a1_ref[rows, :].astype(BF16), w1_ref[...], preferred_element_type=F32)
             + jnp.dot(a2_ref[rows, :].astype(BF16), w2_ref[...], preferred_element_type=F32))
        o_ref[rows, :] = r_ref[rows, :] + _rms(f, g_ref[...])


def _proj_out(parts, weights, resid, g):
    m, n = resid.shape
    tm = _tile(m, 512)
    row = lambda width: pl.BlockSpec((tm, width), lambda i: (i, 0))
    full = lambda arr: pl.BlockSpec(arr.shape, lambda i: (0, 0))
    return pl.pallas_call(
        _proj_out_body,
        grid=(m // tm,),
        in_specs=[row(a.shape[1]) for a in parts] + [full(w) for w in weights] + [row(n), full(g)],
        out_specs=row(n),
        out_shape=jax.ShapeDtypeStruct((m, n), F32),
        compiler_params=_cp("parallel"),
        name="proj_out_resnorm",
    )(*parts, *weights, resid, g)


def _mlp_body(x_ref, g_in_ref, w1_ref, w2_ref, g_out_ref, o_ref, xn_ref, acc_ref):
    j = pl.program_id(1)

    def up_down(xn):
        h = jnp.dot(xn, w1_ref[...], preferred_element_type=F32)
        h = jnp.square(jnp.maximum(h, 0.0)).astype(BF16)
        return jnp.dot(h, w2_ref[...], preferred_element_type=F32)

    @pl.when(j == 0)
    def _():
        xn = _rms(x_ref[...], g_in_ref[...]).astype(BF16)
        xn_ref[...] = xn
        acc_ref[...] = up_down(xn)

    @pl.when(j > 0)
    def _():
        acc_ref[...] += up_down(xn_ref[...])

    @pl.when(j == pl.num_programs(1) - 1)
    def _():
        o_ref[...] = x_ref[...] + _rms(acc_ref[...], g_out_ref[...])


def _mlp(x, g_in, w1, w2, g_out):
    m, k = x.shape
    ff = w1.shape[1]
    tm = _tile(m, 512)
    tf = 1024
    return pl.pallas_call(
        _mlp_body,
        grid=(m // tm, ff // tf),
        in_specs=[pl.BlockSpec((tm, k), lambda i, j: (i, 0)),
                  pl.BlockSpec((1, k), lambda i, j: (0, 0)),
                  pl.BlockSpec((k, tf), lambda i, j: (0, j)),
                  pl.BlockSpec((tf, k), lambda i, j: (j, 0)),
                  pl.BlockSpec((1, k), lambda i, j: (0, 0))],
        out_specs=pl.BlockSpec((tm, k), lambda i, j: (i, 0)),
        out_shape=jax.ShapeDtypeStruct((m, k), F32),
        scratch_shapes=[pltpu.VMEM((tm, k), BF16), pltpu.VMEM((tm, k), F32)],
        compiler_params=_cp("parallel", "arbitrary"),
        name="mlp_resnorm",
    )(x, g_in, w1, w2, g_out)


def _conv_body(x_ref, p_ref, n_ref, w_ref, b_ref, o_ref, *, nblk, qk):
    i = pl.program_id(1)
    j = pl.program_id(2)
    x = x_ref[...].astype(F32)
    tm = x.shape[0]
    row = lax.broadcasted_iota(jnp.int32, x.shape, 0)
    halo = p_ref.shape[0]
    prev_row = jnp.where(i == 0, 0.0, p_ref[halo - 1:halo, :].astype(F32))
    next_row = jnp.where(i == nblk - 1, 0.0, n_ref[0:1, :].astype(F32))
    xm = jnp.where(row == 0, prev_row, pltpu.roll(x, 1, 0))
    xp = jnp.where(row == tm - 1, next_row, pltpu.roll(x, tm - 1, 0))
    w = w_ref[...]
    y = b_ref[...] + xm * w[0:1] + x * w[1:2] + xp * w[2:3]
    if qk:
        y = y * jax.nn.sigmoid(y) * jnp.where(j >= 2, ML_DH ** -0.5, 1.0)
    o_ref[...] = y.astype(o_ref.dtype)


def _short_conv(proj, conv_w, conv_b, nb, seq, qk):
    mtot = proj.shape[0]
    tm = _tile(seq, 2048)
    nblk = seq // tm
    col0, ncol = (3, 4) if qk else (0, 3)
    ppo = 2 if qk else 1
    halo = 2 * SUBLANE
    last_halo = mtot // halo - 1
    tmh = tm // halo

    def x_map(b, i, j):
        return (b * nblk + i, col0 + j)

    def p_map(b, i, j):
        return (jnp.maximum((b * nblk + i) * tmh - 1, 0), col0 + j)

    def n_map(b, i, j):
        return (jnp.minimum((b * nblk + i + 1) * tmh, last_halo), col0 + j)

    return pl.pallas_call(
        functools.partial(_conv_body, nblk=nblk, qk=qk),
        grid=(nb, nblk, ncol),
        in_specs=[pl.BlockSpec((tm, D_HY), x_map),
                  pl.BlockSpec((halo, D_HY), p_map),
                  pl.BlockSpec((halo, D_HY), n_map),
                  pl.BlockSpec((3, D_HY), lambda b, i, j: (0, col0 + j)),
                  pl.BlockSpec((1, D_HY), lambda b, i, j: (0, col0 + j))],
        out_specs=pl.BlockSpec((None, tm, D_HY), lambda b, i, j: (j // ppo, b * nblk + i, j % ppo)),
        out_shape=jax.ShapeDtypeStruct((ncol // ppo, nb * seq, ppo * D_HY), BF16),
        compiler_params=_cp("arbitrary", "arbitrary", "arbitrary"),
        name="short_conv_qk" if qk else "short_conv_hy",
    )(proj, proj, proj, conv_w, conv_b)


def _dft_tables(seq):
    n = 2 * seq
    n1 = n // DFT_N2
    n1h = n1 // 2
    n1k = n1h + SUBLANE
    k1 = jnp.arange(n1k, dtype=jnp.int32)
    t1 = jnp.arange(n1h, dtype=jnp.int32)
    kept = (k1 <= n1h).astype(F32)[:, None]
    weight = kept * jnp.where((k1 == 0) | (k1 == n1h), 1.0, 2.0)[:, None]
    ang = ((k1[:, None] * t1[None, :]) % n1).astype(F32) * (2.0 * math.pi / n1)
    lead_f = jnp.concatenate([kept * jnp.cos(ang), -kept * jnp.sin(ang)], axis=0).astype(BF16)
    lead_ic = (weight * jnp.cos(ang)).T.astype(BF16)
    lead_is = (-weight * jnp.sin(ang)).T.astype(BF16)
    t1f = jnp.arange(n1, dtype=jnp.int32)
    angf = ((k1[:, None] * t1f[None, :]) % n1).astype(F32) * (2.0 * math.pi / n1)
    lead_ff = jnp.concatenate([kept * jnp.cos(angf), -kept * jnp.sin(angf)], axis=0).astype(BF16)
    row = jnp.arange(2 * DFT_N2, dtype=jnp.int32)[None, :, None]
    col = jnp.arange(2 * DFT_N2, dtype=jnp.int32)[None, None, :]
    idx = ((col % DFT_N2) * ((row % DFT_N2) * n1 + k1[:, None, None])) % n
    phase = jnp.where(row // DFT_N2 == col // DFT_N2, 0.5 * math.pi,
                      jnp.where(row < col, 0.0, math.pi))
    mid_f = jnp.sin(idx.astype(F32) * (2.0 * math.pi / n) + phase).astype(BF16)
    return dict(n=n, n1=n1k, n1h=n1h, lead_f=lead_f, lead_ff=lead_ff, lead_ic=lead_ic, lead_is=lead_is,
                mid_f=mid_f)


def _hy_features(seq):
    tiles = 2 * seq // (SUBLANE * DFT_N2)
    half_rows = SUBLANE * DFT_N2 // 2
    tile = jnp.arange(tiles, dtype=jnp.int32)[:, None, None]
    row = jnp.arange(half_rows, dtype=jnp.int32)[None, :, None]
    side = jnp.arange(2, dtype=jnp.int32)[None, None, :]
    t2 = side * (DFT_N2 // 2) + row // SUBLANE
    t1 = tile * SUBLANE + row % SUBLANE
    slot = t1 * DFT_N2 + t2
    pos = jnp.where(slot < seq, slot, (2 * seq - slot) % seq).astype(F32)[..., None]
    t = pos * (1.0 / (seq - 1))
    w = 2.0 * math.pi * pos / seq
    f = jnp.linspace(1e-4, HY_BANDS - 1, HY_BANDS, dtype=F32)
    z = jnp.concatenate([t, jnp.cos(f * w), -jnp.sin(f * w)], axis=-1)
    z = jnp.pad(z, ((0, 0), (0, 0), (0, 0), (0, LANE - HY_EMB)))
    return z.reshape(tiles * half_rows, 2 * LANE)


def _hy_deltas():
    d = jnp.abs(jnp.linspace(math.log(HY_TARGET) / HY_SLOW_DECAY,
                             math.log(HY_TARGET) / HY_FAST_DECAY, D_HY, dtype=F32))
    return jnp.tile(d, 2)[None, :]


def _hy_filter_body(z_ref, w1_ref, b1_ref, fr_ref, w2_ref, b2_ref, w3_ref, dl_ref, k_ref, ss_ref, *, half):
    i = pl.program_id(0)
    hi = lax.Precision.HIGHEST
    z = z_ref[...]
    fr = fr_ref[...]
    h = jnp.sin(fr * (jnp.dot(z, w1_ref[...], precision=hi, preferred_element_type=F32) + b1_ref[...]))
    h = jnp.sin(fr * (jnp.dot(h, w2_ref[...], precision=hi, preferred_element_type=F32) + b2_ref[...]))
    hb = h.astype(BF16)
    dl = dl_ref[...]
    ha = jnp.dot(hb, w3_ref[0], preferred_element_type=F32) * jnp.exp(-z[:, 0:1] * dl)
    hb = jnp.dot(hb, w3_ref[1], preferred_element_type=F32) * jnp.exp(-z[:, LANE:LANE + 1] * dl)
    row = lax.broadcasted_iota(jnp.int32, ha.shape, 0)
    ha = jnp.where(row + (i - half) * ha.shape[0] == 0, 0.0, ha)
    nc = ha.shape[1]
    h2 = DFT_N2 // 2
    for t2 in range(h2):
        k_ref[:, t2 * nc:(t2 + 1) * nc] = ha[t2 * SUBLANE:(t2 + 1) * SUBLANE, :]
        k_ref[:, (h2 + t2) * nc:(h2 + t2 + 1) * nc] = hb[t2 * SUBLANE:(t2 + 1) * SUBLANE, :]
    part = jnp.broadcast_to(jnp.sum(ha * ha, axis=0, keepdims=True)
                            + jnp.sum(hb * hb, axis=0, keepdims=True), ss_ref.shape)

    @pl.when(i == 0)
    def _():
        ss_ref[...] = part

    @pl.when(i > 0)
    def _():
        ss_ref[...] += part


def _hy_filter(seq, z, w1, b1, fr, w2, b2, w3, deltas):
    nc = 2 * D_HY
    tm = SUBLANE * DFT_N2
    steps = 2 * seq // tm
    half = steps // 2
    full = lambda shape: pl.BlockSpec(shape, lambda i: (0, 0))
    return pl.pallas_call(
        functools.partial(_hy_filter_body, half=half),
        grid=(steps,),
        in_specs=[pl.BlockSpec((tm // 2, 2 * LANE), lambda i: (i, 0)),
                  full((2 * LANE, LANE)), full((1, LANE)), full((1, LANE)),
                  full((LANE, LANE)), full((1, LANE)),
                  pl.BlockSpec((None, 2, LANE, nc), lambda i: (i // half, 0, 0, 0)), full((1, nc))],
        out_specs=[pl.BlockSpec((SUBLANE, DFT_N2 * nc), lambda i: (i, 0)), full((SUBLANE, nc))],
        out_shape=[jax.ShapeDtypeStruct((2 * seq // DFT_N2, DFT_N2 * nc), F32),
                   jax.ShapeDtypeStruct((SUBLANE, nc), F32)],
        compiler_params=_cp("arbitrary"),
        name="hyena_filter",
    )(z, w1, b1, fr, w2, b2, w3, deltas)


def _lead_fwd_body(x_ref, f_ref, re_ref, im_ref):
    n1 = re_ref.shape[0]
    r = jnp.dot(f_ref[...], x_ref[...].astype(BF16), preferred_element_type=F32)
    re_ref[...] = r[:n1].astype(BF16)
    im_ref[...] = r[n1:].astype(BF16)


def _lead_fwd(x, table, plane, nb):
    n1, rows = table.shape[0] // 2, table.shape[1]
    cols = x.shape[-1]
    tn = _tile(cols, 4096)
    return pl.pallas_call(
        _lead_fwd_body,
        grid=(nb, cols // tn),
        in_specs=[pl.BlockSpec((None, rows, tn), lambda b, j: (plane, b, j)),
                  pl.BlockSpec((2 * n1, rows), lambda b, j: (0, 0))],
        out_specs=[pl.BlockSpec((None, n1, tn), lambda b, j: (b, 0, j))] * 2,
        out_shape=[jax.ShapeDtypeStruct((nb, n1, cols), BF16)] * 2,
        compiler_params=_cp("parallel", "parallel"),
        name="hyena_lead_fwd",
    )(x, table)


def _mid_spec_body(are_ref, aim_ref, f_ref, ss_ref, kr_ref, ki_ref, *, bk):
    n2 = DFT_N2
    scale = lax.rsqrt(ss_ref[0:1, :] + EPS)

    def body(i, c):
        a = jnp.concatenate([are_ref[i], aim_ref[i]], axis=0)
        x = jnp.dot(f_ref[i], a, preferred_element_type=F32)
        kr_ref[i] = x[:n2] * scale
        ki_ref[i] = x[n2:] * scale
        return c

    lax.fori_loop(0, bk, body, 0, unroll=True)


def _mid_spec(a_re, a_im, sumsq, tb):
    n1 = tb["n1"]
    bk = _tile(n1, 8)
    a_re = a_re.reshape(n1, DFT_N2, 2 * D_HY)
    a_im = a_im.reshape(n1, DFT_N2, 2 * D_HY)
    a_spec = pl.BlockSpec((bk, DFT_N2, D_HY), lambda o, i: (i, 0, o))
    k_spec = pl.BlockSpec((None, bk, DFT_N2, D_HY), lambda o, i: (o, i, 0, 0))
    return pl.pallas_call(
        functools.partial(_mid_spec_body, bk=bk),
        grid=(2, n1 // bk),
        in_specs=[a_spec, a_spec,
                  pl.BlockSpec((bk, 2 * DFT_N2, 2 * DFT_N2), lambda o, i: (i, 0, 0)),
                  pl.BlockSpec((SUBLANE, D_HY), lambda o, i: (0, o))],
        out_specs=[k_spec, k_spec],
        out_shape=[jax.ShapeDtypeStruct((2, n1, DFT_N2, D_HY), F32)] * 2,
        compiler_params=_cp("arbitrary", "arbitrary"),
        name="hyena_mid_spectrum",
    )(a_re, a_im, tb["mid_f"], sumsq)


def _mid_conv_body(are_ref, aim_ref, f_ref, kr_ref, ki_ref, bre_ref, bim_ref, *, bk):
    n2 = DFT_N2

    def body(i, c):
        a = jnp.concatenate([are_ref[i], aim_ref[i]], axis=0)
        x = jnp.dot(f_ref[i], a, preferred_element_type=F32)
        xr = x[:n2]
        xi = x[n2:]
        kr = kr_ref[i]
        ki = ki_ref[i]
        p = jnp.concatenate([xr * kr - xi * ki, xr * ki + xi * kr], axis=0).astype(BF16)
        q = lax.dot_general(f_ref[i], p, (((0,), (0,)), ((), ())), preferred_element_type=F32)
        bre_ref[i] = q[:n2].astype(BF16)
        bim_ref[i] = q[n2:].astype(BF16)
        return c

    lax.fori_loop(0, bk, body, 0, unroll=True)


def _mid_conv(a_re, a_im, kf_re, kf_im, order, tb):
    n1 = tb["n1"]
    nb = a_re.shape[0]
    bk = _tile(n1, 8)
    a_re = a_re.reshape(nb, n1, DFT_N2, D_HY)
    a_im = a_im.reshape(nb, n1, DFT_N2, D_HY)
    a_spec = pl.BlockSpec((None, bk, DFT_N2, D_HY), lambda b, i: (b, i, 0, 0))
    t_spec = pl.BlockSpec((bk, 2 * DFT_N2, 2 * DFT_N2), lambda b, i: (i, 0, 0))
    k_spec = pl.BlockSpec((None, bk, DFT_N2, D_HY), lambda b, i: (order, i, 0, 0))
    b_re, b_im = pl.pallas_call(
        functools.partial(_mid_conv_body, bk=bk),
        grid=(nb, n1 // bk),
        in_specs=[a_spec, a_spec, t_spec, k_spec, k_spec],
        out_specs=[a_spec, a_spec],
        out_shape=[jax.ShapeDtypeStruct((nb, n1, DFT_N2, D_HY), BF16)] * 2,
        compiler_params=_cp("parallel", "parallel"),
        name="hyena_mid_conv",
    )(a_re, a_im, tb["mid_f"], kf_re, kf_im)
    return b_re.reshape(nb, n1, DFT_N2 * D_HY), b_im.reshape(nb, n1, DFT_N2 * D_HY)


def _lead_inv_body(bre_ref, bim_ref, c_ref, s_ref, z_ref, gate_ref, skip_ref, gn_ref, o_ref, *, inv_n, last):
    y = (jnp.dot(c_ref[...], bre_ref[...], preferred_element_type=F32)
         + jnp.dot(s_ref[...], bim_ref[...], preferred_element_type=F32)) * inv_n
    out = gate_ref[...].astype(F32) * (y + skip_ref[...] * z_ref[...].astype(F32))
    if not last:
        o_ref[...] = out
    else:
        gn = gn_ref[...]
        for c in range(out.shape[1] // D_HY):
            blk = out[:, c * D_HY:(c + 1) * D_HY]
            ms = jnp.mean(blk * blk, axis=-1, keepdims=True)
            o_ref[:, c * D_HY:(c + 1) * D_HY] = (blk * lax.rsqrt(ms + EPS) * gn).astype(o_ref.dtype)


def _lead_inv(b_re, b_im, z, z_plane, gates, gate_plane, skip, gn, tb, last):
    n1, n1h = tb["n1"], tb["n1h"]
    nb, _, cols = b_re.shape
    tn = _tile(cols, 4096)
    skip_t = jnp.tile(skip[None, :], (1, tn // D_HY))
    b_spec = pl.BlockSpec((None, n1, tn), lambda b, j: (b, 0, j))
    t_spec = pl.BlockSpec((n1h, n1), lambda b, j: (0, 0))
    return pl.pallas_call(
        functools.partial(_lead_inv_body, inv_n=1.0 / tb["n"], last=last),
        grid=(nb, cols // tn),
        in_specs=[b_spec, b_spec, t_spec, t_spec,
                  pl.BlockSpec((None, n1h, tn), lambda b, j: (z_plane, b, j)),
                  pl.BlockSpec((None, n1h, tn), lambda b, j: (gate_plane, b, j)),
                  pl.BlockSpec((1, tn), lambda b, j: (0, 0)),
                  pl.BlockSpec((1, D_HY), lambda b, j: (0, 0))],
        out_specs=pl.BlockSpec((None, n1h, tn), lambda b, j: (0, b, j)),
        out_shape=jax.ShapeDtypeStruct((1, nb * n1h, cols), BF16 if last else F32),
        compiler_params=_cp("parallel", "parallel"),
        name="hyena_lead_inv",
    )(b_re, b_im, tb["lead_ic"], tb["lead_is"], z, gates, skip_t, gn)


def _hyena(conv_out, nb, seq, tb, kf_re, kf_im, skip, gn):
    n1h = tb["n1h"]
    planes = conv_out.reshape(conv_out.shape[0], nb * n1h, DFT_N2 * D_HY)
    z, z_plane = planes, 0
    for order in range(2):
        a_re, a_im = _lead_fwd(z, tb["lead_f"], z_plane, nb)
        b_re, b_im = _mid_conv(a_re, a_im, kf_re, kf_im, order, tb)
        z = _lead_inv(b_re, b_im, z, z_plane, planes, 1 + order, skip[order], gn, tb, order == 1)
        z_plane = 0
    return z.reshape(nb * seq, D_HY)


def _hyena_spectra(seq, tb, w1, b1, fr, w2, b2, w3):
    w1 = jnp.pad(w1, ((0, LANE - HY_EMB), (0, 0)))
    zero1 = jnp.zeros_like(w1)
    w1 = jnp.concatenate([jnp.concatenate([w1, zero1], axis=1),
                          jnp.concatenate([zero1, w1], axis=1)], axis=0)
    zero2 = jnp.zeros_like(w2)
    w2 = jnp.concatenate([jnp.concatenate([w2, zero2], axis=1),
                          jnp.concatenate([zero2, w2], axis=1)], axis=0)
    pair = lambda v: jnp.concatenate([v, v])[None, :]
    w3 = w3.reshape(HY_FILT, 2, 2, D_HY).transpose(2, 0, 1, 3).reshape(2, HY_FILT, 2 * D_HY)
    zero3 = jnp.zeros_like(w3)
    w3 = jnp.stack([jnp.concatenate([w3, zero3], axis=1),
                    jnp.concatenate([zero3, w3], axis=1)], axis=1).astype(BF16)
    taps, sumsq = _hy_filter(seq, _hy_features(seq), w1, pair(b1), pair(fr), w2, pair(b2), w3,
                             _hy_deltas())
    a_re, a_im = _lead_fwd(taps[None], tb["lead_ff"], 0, 1)
    return _mid_spec(a_re[0], a_im[0], sumsq, tb)


def _s5_scan_body(u_ref, pm_ref, bd_ref, a_ref, cd_ref, x0_ref, o_ref, bu_ref, st_ref, y_ref,
                  *, tstep, reverse, final_state):
    tb = pl.program_id(0)
    ns = S5_STREAMS
    half = S5_STATE
    cw = S5_BLOCK_GROUPS * S5_P

    @pl.when(tb == 0)
    def _():
        st_ref[...] = x0_ref[...]

    u = u_ref[...].reshape(ns * tstep, D_S5).astype(BF16)
    u_tm = jnp.dot(pm_ref[...], u, preferred_element_type=F32).astype(BF16)

    def project_in(blk):
        bu_ref[:, 2 * blk * cw:2 * (blk + 1) * cw] = jnp.dot(
            u_tm[:, blk * LANE:(blk + 1) * LANE], bd_ref[blk], preferred_element_type=F32)

    def scan(blk):
        bre = slice(2 * blk * cw, (2 * blk + 1) * cw)
        bim = slice((2 * blk + 1) * cw, (2 * blk + 2) * cw)
        sre = slice(blk * cw, (blk + 1) * cw)
        sim = slice(half + blk * cw, half + (blk + 1) * cw)
        ar = a_ref[:, sre]
        ai = a_ref[:, sim]
        xr = st_ref[:, sre]
        xi = st_ref[:, sim]
        for i in range(tstep):
            t = tstep - 1 - i if reverse else i
            rows = slice(t * ns, (t + 1) * ns)
            xr, xi = (ar * xr - ai * xi + bu_ref[rows, bre],
                      ar * xi + ai * xr + bu_ref[rows, bim])
            if not final_state:
                bu_ref[rows, bre] = xr
                bu_ref[rows, bim] = xi
        st_ref[:, sre] = xr
        st_ref[:, sim] = xi

    def project_out(blk):
        y_ref[blk] = jnp.dot(bu_ref[:, 2 * blk * cw:2 * (blk + 1) * cw].astype(BF16), cd_ref[blk],
                             preferred_element_type=F32)

    project_in(0)
    for blk in range(S5_NBLOCKS):
        if blk + 1 < S5_NBLOCKS:
            project_in(blk + 1)
        scan(blk)
        if not final_state:
            project_out(blk)

    if final_state:
        @pl.when(tb == pl.num_programs(0) - 1)
        def _():
            o_ref[...] = st_ref[...]
    else:
        for s in range(ns):
            for blk in range(S5_NBLOCKS):
                o_ref[s, :, blk * LANE:(blk + 1) * LANE] = y_ref[blk, pl.ds(s, tstep, stride=ns), :]


def _s5_scan(proj, pmat, bd, a_b, cd, x0, tlen, reverse, final_state):
    ns = S5_STREAMS
    d = 1 if reverse else 0
    tstep = pmat.shape[0] // ns
    nt = tlen // tstep
    proj3 = proj.reshape(ns, tlen, proj.shape[1])
    blk_in = S5_BLOCK_GROUPS * S5_GROUP
    blk_state = 2 * S5_BLOCK_GROUPS * S5_P

    def window(t):
        return nt - 1 - t if reverse else t

    if final_state:
        out_spec = pl.BlockSpec((ns, 2 * S5_STATE), lambda t: (0, 0))
        out_shape = jax.ShapeDtypeStruct((ns, 2 * S5_STATE), F32)
    else:
        out_spec = pl.BlockSpec((ns, tstep, D_S5), lambda t: (0, window(t), 0))
        out_shape = jax.ShapeDtypeStruct((ns, tlen, D_S5), F32)
    out = pl.pallas_call(
        functools.partial(_s5_scan_body, tstep=tstep, reverse=reverse, final_state=final_state),
        grid=(nt,),
        in_specs=[pl.BlockSpec((ns, tstep, D_S5), lambda t: (0, window(t), COL_U)),
                  pl.BlockSpec((ns * tstep, ns * tstep), lambda t: (0, 0)),
                  pl.BlockSpec((None, S5_NBLOCKS, blk_in, blk_state), lambda t: (d, 0, 0, 0)),
                  pl.BlockSpec((None, ns, 2 * S5_STATE), lambda t: (d, 0, 0)),
                  pl.BlockSpec((None, S5_NBLOCKS, blk_state, blk_in), lambda t: (d, 0, 0, 0)),
                  pl.BlockSpec((None, ns, 2 * S5_STATE), lambda t: (d, 0, 0))],
        out_specs=out_spec,
        out_shape=out_shape,
        scratch_shapes=[pltpu.VMEM((tstep * ns, 2 * S5_STATE), F32),
                        pltpu.VMEM((ns, 2 * S5_STATE), F32),
                        pltpu.VMEM((D_S5 // LANE, tstep * ns, LANE), F32)],
        compiler_params=_cp("arbitrary"),
        name="s5_end_state" if final_state else "s5_scan",
    )(proj3, pmat, bd, a_b, cd, x0)
    return out if final_state else out.reshape(ns * tlen, D_S5)


def _s5_params(lam_re, lam_im, log_dt, b_re, b_im, c_re, c_im):
    lr = jnp.minimum(lam_re, -1e-4)
    li = lam_im
    dt = jnp.exp(log_dt)[..., None]
    er = jnp.exp(lr * dt)
    a_re = er * jnp.cos(li * dt)
    a_im = er * jnp.sin(li * dt)
    den = lr * lr + li * li
    q_re = ((a_re - 1.0) * lr + a_im * li) / den
    q_im = (a_im * lr - (a_re - 1.0) * li) / den
    bb_re = q_re[..., None] * b_re - q_im[..., None] * b_im
    bb_im = q_re[..., None] * b_im + q_im[..., None] * b_re
    nbk, gb = S5_NBLOCKS, S5_BLOCK_GROUPS
    eye = jnp.eye(gb, dtype=F32)

    def in_mat(x):
        x = x.reshape(2, nbk, gb, S5_P, S5_GROUP)
        return jnp.einsum("dbgph,gk->dbghkp", x, eye).reshape(2, nbk, gb * S5_GROUP, gb * S5_P)

    def out_mat(x):
        x = x.reshape(2, nbk, gb, S5_GROUP, S5_P)
        return jnp.einsum("dbghp,gk->dbgpkh", x, eye).reshape(2, nbk, gb * S5_P, gb * S5_GROUP)

    bd = jnp.concatenate([in_mat(bb_re), in_mat(bb_im)], axis=3).astype(BF16)
    cd = jnp.concatenate([out_mat(c_re), out_mat(-c_im)], axis=2).astype(BF16)
    a_bar = jnp.concatenate([a_re.reshape(2, S5_STATE), a_im.reshape(2, S5_STATE)], axis=1)
    return (lr * dt).reshape(2, S5_STATE), (li * dt).reshape(2, S5_STATE), a_bar, bd, cd


def _s5_row_perm(tstep):
    ns = S5_STREAMS
    r = jnp.arange(ns * tstep)
    src = (r % ns) * tstep + r // ns
    return (src[:, None] == jnp.arange(ns * tstep)[None, :]).astype(BF16)


def _s5_mixer(proj, nb, seq, s5p):
    la_re, la_im, a_bar, bd, cd = s5p
    ns = S5_STREAMS
    ncs = ns // nb
    tlen = seq // ncs
    pmat = _s5_row_perm(_tile(tlen, S5_TSTEP))
    a_b = jnp.broadcast_to(a_bar[:, None, :], (2, ns, 2 * S5_STATE))
    zeros = jnp.zeros((2, ns, 2 * S5_STATE), F32)
    mag = jnp.exp(la_re * tlen)
    at_re = (mag * jnp.cos(la_im * tlen))[:, None, :]
    at_im = (mag * jnp.sin(la_im * tlen))[:, None, :]
    ys = []
    for d in range(2):
        x_end = _s5_scan(proj, pmat, bd, a_b, cd, zeros, tlen, d == 1, True)
        xe_re = x_end[:, :S5_STATE].reshape(nb, ncs, S5_STATE)
        xe_im = x_end[:, S5_STATE:].reshape(nb, ncs, S5_STATE)
        order = range(ncs) if d == 0 else range(ncs - 1, -1, -1)
        cr = jnp.zeros((nb, S5_STATE), F32)
        ci = jnp.zeros((nb, S5_STATE), F32)
        rows = [None] * ncs
        for c in order:
            rows[c] = jnp.concatenate([cr, ci], axis=-1)
            cr, ci = (at_re[d] * cr - at_im[d] * ci + xe_re[:, c],
                      at_re[d] * ci + at_im[d] * cr + xe_im[:, c])
        x0 = jnp.stack(rows, axis=1).reshape(ns, 2 * S5_STATE)
        x0 = jnp.stack([x0, x0])
        ys.append(_s5_scan(proj, pmat, bd, a_b, cd, x0, tlen, d == 1, False))
    return ys


def _s5_post_body(yf_ref, yb_ref, u_ref, d_ref, w_ref, b_ref, gn_ref, o_ref):
    y = yf_ref[...] + yb_ref[...] + d_ref[...] * u_ref[...].astype(F32)
    g = jax.nn.gelu(y)
    r = jnp.dot(g.astype(BF16), w_ref[...], preferred_element_type=F32) + b_ref[...]
    out = r[:, :D_S5] * jax.nn.sigmoid(r[:, D_S5:])
    ms = jnp.mean(out * out, axis=-1, keepdims=True)
    o_ref[...] = (out * lax.rsqrt(ms + EPS) * gn_ref[...]).astype(o_ref.dtype)


def _s5_post(y_f, y_b, proj, d, glu_w, glu_b, gn):
    m = y_f.shape[0]
    tm = _tile(m, 512)
    full = lambda shape: pl.BlockSpec(shape, lambda i: (0, 0))
    return pl.pallas_call(
        _s5_post_body,
        grid=(m // tm,),
        in_specs=[pl.BlockSpec((tm, D_S5), lambda i: (i, 0)),
                  pl.BlockSpec((tm, D_S5), lambda i: (i, 0)),
                  pl.BlockSpec((tm, D_S5), lambda i: (i, COL_U)),
                  full((1, D_S5)), full((D_S5, 2 * D_S5)), full((1, 2 * D_S5)), full((1, D_S5))],
        out_specs=pl.BlockSpec((tm, D_S5), lambda i: (i, 0)),
        out_shape=jax.ShapeDtypeStruct((m, D_S5), BF16),
        compiler_params=_cp("arbitrary"),
        name="s5_glu_norm",
    )(y_f, y_b, proj, d, glu_w, glu_b, gn)


def _split3(x):
    hi = x.astype(BF16)
    rest = x - hi.astype(F32)
    mid = rest.astype(BF16)
    return hi, mid, (rest - mid.astype(F32)).astype(BF16)


def _mlstm_body(*refs, reverse, epilogue):
    if epilogue:
        (q_ref, k_ref, v_ref, g_ref, gt_ref, bias_ref, biast_ref, seen_ref, upto_ref, hf_ref, o_gate_ref,
         gn_ref, out_ref, c_ref, n_ref, m_ref) = refs
    else:
        (q_ref, k_ref, v_ref, g_ref, gt_ref, bias_ref, biast_ref, seen_ref, upto_ref,
         out_ref, c_ref, n_ref, m_ref) = refs
        hf_ref = o_gate_ref = gn_ref = None

    @pl.when(pl.program_id(2) == 0)
    def _():
        c_ref[...] = jnp.zeros_like(c_ref)
        n_ref[...] = jnp.zeros_like(n_ref)
        m_ref[...] = jnp.zeros_like(m_ref)

    tc = ML_CHUNK
    r = lax.broadcasted_iota(jnp.int32, (tc, tc), 0)
    s = lax.broadcasted_iota(jnp.int32, (tc, tc), 1)
    seen = (s >= r) if reverse else (s <= r)
    g = g_ref[...] + bias_ref[...]
    gt = gt_ref[...] + biast_ref[:, 0:1]
    for hh in range(ML_HEADS_PER_STEP):
        _mlstm_head(hh, g, gt, seen, q_ref, k_ref, v_ref, seen_ref, upto_ref, hf_ref, o_gate_ref,
                    gn_ref, out_ref, c_ref, n_ref, m_ref, reverse=reverse)


def _mlstm_head(hh, g, gt, seen, q_ref, k_ref, v_ref, seen_ref, upto_ref, hf_ref, o_gate_ref, gn_ref,
                out_ref, c_ref, n_ref, m_ref, *, reverse):
    head = pl.program_id(1) * ML_HEADS_PER_STEP + hh
    cols = slice(hh * ML_DH, (hh + 1) * ML_DH)
    tc = ML_CHUNK
    gate_i = 2 if reverse else 0
    idx_i = gate_i * ML_HEADS + head
    idx_f = idx_i + ML_HEADS
    tiled = lambda x, n: x if n == LANE else jnp.concatenate([x] * (n // LANE), axis=1)
    wide = lambda x: tiled(x, ML_DH)
    span = lambda x: tiled(x, tc)
    dot = functools.partial(jnp.dot, preferred_element_type=F32)

    lane = lax.broadcasted_iota(jnp.int32, g.shape, 1)
    ig = jnp.broadcast_to(jnp.sum(jnp.where(lane == idx_i, g, 0.0), axis=1, keepdims=True), g.shape)
    fg = jnp.broadcast_to(jnp.sum(jnp.where(lane == idx_f, g, 0.0), axis=1, keepdims=True), g.shape)
    sub = lax.broadcasted_iota(jnp.int32, gt.shape, 0)
    ig_row = jnp.sum(jnp.where(sub == idx_i, gt, 0.0), axis=0, keepdims=True)
    fg_row = jnp.sum(jnp.where(sub == idx_f, gt, 0.0), axis=0, keepdims=True)
    lf = jax.nn.log_sigmoid(fg)
    lf_row = jax.nn.log_sigmoid(fg_row)

    seen_b = seen_ref[...]
    upto_b = upto_ref[...]
    b_col = sum(dot(seen_b, piece) for piece in _split3(lf))
    lf_rows = jnp.broadcast_to(lf_row, (SUBLANE, tc))
    b_row = sum(dot(piece, upto_b) for piece in _split3(lf_rows))[0:1]
    b_last = jnp.sum(lf_row, axis=1, keepdims=True)

    m_prev = m_ref[hh:hh + 1, 0:1]
    src = ig_row - b_row
    dmat = jnp.where(seen, span(b_col) + src, -jnp.inf)
    g_car = b_col + m_prev
    mt = jnp.maximum(g_car, jnp.max(dmat, axis=1, keepdims=True))
    inter = jnp.exp(g_car - mt)
    qb = q_ref[:, cols]
    kb = k_ref[:, cols]
    vb = v_ref[:, cols].astype(BF16)
    ones = jnp.ones((tc, LANE), BF16)
    qk = lax.dot_general(qb, kb, (((1,), (1,)), ((), ())), preferred_element_type=F32)
    sc = (qk * jnp.exp(dmat - span(mt))).astype(BF16)
    c_state = c_ref[hh]
    n_state = n_ref[hh]
    num = wide(inter) * dot(qb, c_state.astype(BF16)) + dot(sc, vb)
    den = inter * dot(qb, n_state.astype(BF16)) + dot(sc, ones)
    h = num * wide(1.0 / jnp.maximum(jnp.abs(den), jnp.exp(-mt)))

    a_row = b_last + src
    a_col = b_last - b_col + ig
    m_new = jnp.maximum(b_last + m_prev, jnp.max(a_row, axis=1, keepdims=True))
    decay = jnp.exp(b_last + m_prev - m_new)
    kw = (kb.astype(F32) * wide(jnp.exp(a_col - m_new))).astype(BF16)
    upd = lax.dot_general(kw, jnp.concatenate([vb, ones], axis=1), (((0,), (0,)), ((), ())),
                          preferred_element_type=F32)
    c_ref[hh] = decay * c_state + upd[:, :ML_DH]
    n_ref[hh] = decay * n_state + upd[:, ML_DH:]
    m_ref[hh:hh + 1, 0:1] = m_new

    if hf_ref is not None:
        h = h + hf_ref[:, cols]
        ms = jnp.mean(h * h, axis=-1, keepdims=True)
        hn = h * lax.rsqrt(ms + EPS) * gn_ref[:, cols]
        out_ref[:, cols] = (hn * jax.nn.sigmoid(o_gate_ref[:, cols].astype(F32))).astype(out_ref.dtype)
    else:
        out_ref[:, cols] = h


def _mlstm_dir(qk_planes, proj, gates, gates_t, bias, bias_t, nb, seq, reverse, h_fwd=None, gn=None):
    tc = ML_CHUNK
    nch = seq // tc
    hps = ML_HEADS_PER_STEP
    width = hps * ML_DH
    col_v = (N_CONV + D_S5) // width
    col_o = (N_CONV + D_S5 + D_ML) // width

    def chunk(b, ci):
        return b * nch + (nch - 1 - ci if reverse else ci)

    in_specs = [
        pl.BlockSpec((None, tc, width), lambda b, h, ci: (0, chunk(b, ci), h)),
        pl.BlockSpec((None, tc, width), lambda b, h, ci: (1, chunk(b, ci), h)),
        pl.BlockSpec((tc, width), lambda b, h, ci: (chunk(b, ci), col_v + h)),
        pl.BlockSpec((tc, LANE), lambda b, h, ci: (chunk(b, ci), 0)),
        pl.BlockSpec((4 * ML_HEADS, tc), lambda b, h, ci: (0, chunk(b, ci))),
        pl.BlockSpec((1, LANE), lambda b, h, ci: (0, 0)),
        pl.BlockSpec((4 * ML_HEADS, LANE), lambda b, h, ci: (0, 0)),
        pl.BlockSpec((tc, tc), lambda b, h, ci: (0, 0)),
        pl.BlockSpec((tc, tc), lambda b, h, ci: (0, 0)),
    ]
    pos = jnp.arange(tc)
    seen = (pos[None, :] >= pos[:, None]) if reverse else (pos[None, :] <= pos[:, None])
    seen = seen.astype(BF16)
    args = [qk_planes, qk_planes, proj, gates, gates_t, bias, bias_t, seen, seen.T]
    epilogue = h_fwd is not None
    if epilogue:
        in_specs += [
            pl.BlockSpec((tc, width), lambda b, h, ci: (chunk(b, ci), h)),
            pl.BlockSpec((tc, width), lambda b, h, ci: (chunk(b, ci), col_o + h)),
            pl.BlockSpec((1, width), lambda b, h, ci: (0, h)),
        ]
        args += [h_fwd, proj, gn]
    return pl.pallas_call(
        functools.partial(_mlstm_body, reverse=reverse, epilogue=epilogue),
        grid=(nb, ML_HEADS // hps, nch),
        in_specs=in_specs,
        out_specs=pl.BlockSpec((tc, width), lambda b, h, ci: (chunk(b, ci), h)),
        out_shape=jax.ShapeDtypeStruct((nb * seq, D_ML), BF16 if epilogue else F32),
        scratch_shapes=[pltpu.VMEM((hps, ML_DH, ML_DH), F32), pltpu.VMEM((hps, ML_DH, LANE), F32),
                        pltpu.VMEM((SUBLANE, LANE), F32)],
        compiler_params=_cp("arbitrary", "arbitrary", "arbitrary"),
        name="mlstm_bwd_norm" if reverse else "mlstm_fwd",
    )(*args)


def _layer(x, nb, seq, tb, p):
    proj, gates = _proj_in(x, p["ln_g"][0:1], p["w_in"], p["w_gate"])
    gates_t = gates[:, :4 * ML_HEADS].T
    gn = p["group_norm"]
    gn_hy = gn[None, :D_HY]
    gn_s5 = gn[None, D_HY:D_HY + D_S5]
    gn_ml = gn[None, D_HY + D_S5:]

    hy_planes = _short_conv(proj, p["conv_w"], p["conv_b"][None, :], nb, seq, False)
    qk_planes = _short_conv(proj, p["conv_w"], p["conv_b"][None, :], nb, seq, True)
    kf_re, kf_im = _hyena_spectra(seq, tb, p["hy_w1"], p["hy_b1"], p["hy_freq"], p["hy_w2"],
                                  p["hy_b2"], p["hy_w3"])
    y_hy = _hyena(hy_planes, nb, seq, tb, kf_re, kf_im, p["hy_skip"], gn_hy)
    y_s5 = _s5_post(*_s5_mixer(proj, nb, seq, p["s5"]), proj, p["s5_d"][None, :],
                    p["s5_glu_w"], p["s5_glu_b"][None, :], gn_s5)
    ml_args = (qk_planes, proj, gates, gates_t, p["ml_bias"], p["ml_bias_t"], nb, seq)
    h_f = _mlstm_dir(*ml_args, False)
    y_ml = _mlstm_dir(*ml_args, True, h_f, gn_ml)

    w_out = p["w_out"]
    x = _proj_out([y_hy, y_s5, y_ml], [w_out[:D_HY], w_out[D_HY:D_HY + D_S5], w_out[D_HY + D_S5:]],
                  x, p["ln_g"][1:2])
    return _mlp(x, p["ln_g"][2:3], p["mlp_w1"], p["mlp_w2"], p["ln_g"][3:4])


_PARAM_NAMES = ("ln_g", "w_in", "conv_w", "conv_b", "hy_w1", "hy_b1", "hy_freq", "hy_w2", "hy_b2",
                "hy_w3", "hy_skip", "s5_lam_re", "s5_lam_im", "s5_log_dt", "s5_b_re", "s5_b_im",
                "s5_c_re", "s5_c_im", "s5_d", "s5_glu_w", "s5_glu_b", "ml_gate_b", "group_norm",
                "w_out", "mlp_w1", "mlp_w2")


def _prepare(p):
    p = dict(p)
    w_in = p["w_in"].astype(BF16)
    p["w_in"] = w_in[:, :P_MIX]
    p["w_gate"] = jnp.pad(w_in[:, P_MIX:], ((0, 0), (0, LANE - 4 * ML_HEADS)))
    p["w_out"] = p["w_out"].astype(BF16)
    p["mlp_w1"] = p["mlp_w1"].astype(BF16)
    p["mlp_w2"] = p["mlp_w2"].astype(BF16)
    p["s5_glu_w"] = p["s5_glu_w"].astype(BF16)
    p["ml_bias"] = jnp.pad(p["ml_gate_b"].reshape(1, -1), ((0, 0), (0, LANE - 4 * ML_HEADS)))
    p["ml_bias_t"] = jnp.broadcast_to(p["ml_gate_b"].reshape(-1, 1), (4 * ML_HEADS, LANE))
    p["s5"] = _s5_params(p["s5_lam_re"], p["s5_lam_im"], p["s5_log_dt"], p["s5_b_re"],
                         p["s5_b_im"], p["s5_c_re"], p["s5_c_im"])
    return p


def kernel(x_prompt, x_sample, ln_g, w_in, conv_w, conv_b, hy_w1, hy_b1, hy_freq, hy_w2, hy_b2, hy_w3,
           hy_skip, s5_lam_re, s5_lam_im, s5_log_dt, s5_b_re, s5_b_im, s5_c_re, s5_c_im, s5_d,
           s5_glu_w, s5_glu_b, ml_gate_b, group_norm, w_out, mlp_w1, mlp_w2):
    params = (ln_g, w_in, conv_w, conv_b, hy_w1, hy_b1, hy_freq, hy_w2, hy_b2, hy_w3, hy_skip,
              s5_lam_re, s5_lam_im, s5_log_dt, s5_b_re, s5_b_im, s5_c_re, s5_c_im, s5_d,
              s5_glu_w, s5_glu_b, ml_gate_b, group_norm, w_out, mlp_w1, mlp_w2)
    streams = []
    for x in (x_prompt, x_sample):
        nb, seq, _ = x.shape
        streams.append([x.reshape(nb * seq, D_MODEL), nb, seq, _dft_tables(seq)])
    for layer in range(DEPTH):
        p = _prepare({name: arr[layer] for name, arr in zip(_PARAM_NAMES, params)})
        for st in streams:
            st[0] = _layer(st[0], st[1], st[2], st[3], p)
    return tuple(st[0].reshape(x.shape) for st, x in zip(streams, (x_prompt, x_sample)))
```
